```python
import math
import jax, jax.numpy as jnp
from jax import lax
import numpy as np

D_MODEL = 2048
BATCH = 1
SEQ = 8192
DEPTH = 1

EPS = 1e-6
D_FF = 5632
HY_WIDTH = 1024
HY_GROUPS = 8
SHORT_CONV = 3
FILTER_EMB = 33
FILTER_HIDDEN = 64
SHORT_DECAY_PCT = 0.3
LONG_DECAY_PCT = 1.5
DECAY_TARGET = 1e-2
GLA_HEADS = 4
GLA_DK = 128
GLA_DV = 256
GLA_KW = GLA_HEADS * GLA_DK
GLA_VW = GLA_HEADS * GLA_DV
GATE_RANK = 16
GATE_TEMP = 16.0
CHUNK = 64
MIX_WIDTH = HY_WIDTH + GLA_VW
IN_COLS = 3 * HY_WIDTH + 2 * GLA_KW + 2 * GLA_VW + 2 * GATE_RANK

kernel_name = 'hyena_gla_macaron_hybrid_block'

F32 = jnp.float32


def rmsnorm(x, g):
    xf = x.astype(F32)
    y = xf * lax.rsqrt(jnp.mean(xf * xf, axis=-1, keepdims=True) + EPS)
    return (y * g.astype(F32)).astype(x.dtype)


def group_rmsnorm(y, g, groups):
    shp = y.shape
    yf = y.astype(F32).reshape(shp[:-1] + (groups, shp[-1] // groups))
    yf = yf * lax.rsqrt(jnp.mean(yf * yf, axis=-1, keepdims=True) + EPS)
    return yf.reshape(shp) * g.astype(F32)


def macaron_half_ffn(x, norm_g, w_gate, w_up, w_down):
    h = rmsnorm(x, norm_g)
    return x + 0.5 * ((jax.nn.silu(h @ w_gate) * (h @ w_up)) @ w_down)


def centred_short_conv(u, w, b):
    L = u.shape[1]
    half = SHORT_CONV // 2
    up = jnp.pad(u, ((0, 0), (half, half), (0, 0)))
    out = b
    for tap in range(SHORT_CONV):
        out = out + up[:, tap:tap + L] * w[tap]
    return out


def hyena_filter(L, w1, b1, f1, w2, b2, f2, w3, b3, f3, w4):
    c = lambda a: a.astype(F32)
    t = jnp.linspace(0.0, 1.0, L, dtype=F32)[:, None]
    bands = (FILTER_EMB - 1) // 2
    freqs = jnp.linspace(1e-4, bands - 1, bands, dtype=F32)[None, :]
    ang = (2.0 * math.pi / L) * jnp.arange(L, dtype=F32)[:, None] * freqs
    z = jnp.concatenate([t, jnp.cos(ang), -jnp.sin(ang)], axis=-1)
    hid = jnp.sin(c(f1) * (z @ c(w1) + c(b1)))
    hid = jnp.sin(c(f2) * (hid @ c(w2) + c(b2)))
    hid = jnp.sin(c(f3) * (hid @ c(w3) + c(b3)))
    h = hid @ c(w4)
    min_decay = math.log(DECAY_TARGET) / LONG_DECAY_PCT
    max_decay = math.log(DECAY_TARGET) / SHORT_DECAY_PCT
    deltas = jnp.abs(jnp.linspace(min_decay, max_decay, HY_WIDTH, dtype=F32))
    window = jnp.exp(-t * deltas[None, :])
    h_fwd = h[:, :HY_WIDTH] * window
    h_bwd = h[:, HY_WIDTH:] * window
    filt = jnp.concatenate([h_fwd, jnp.zeros((1, HY_WIDTH), F32), h_bwd[1:][::-1]], axis=0)
    return filt / jnp.sum(jnp.abs(filt), axis=0, keepdims=True)


def bidir_fft_conv(u, filt):
    L = u.shape[1]
    U = jnp.fft.rfft(u, n=2 * L, axis=1)
    K = jnp.fft.rfft(filt, n=2 * L, axis=0)
    return jnp.fft.irfft(U * K[None], n=2 * L, axis=1)[:, :L]


def hyena_mixer(u, conv_w, conv_b, w1, b1, f1, w2, b2, f2, w3, b3, f3, w4, skip, out_g):
    L = u.shape[1]
    u = centred_short_conv(u, conv_w, conv_b)
    x0, x1, v = jnp.split(u, 3, axis=-1)
    filt = hyena_filter(L, w1, b1, f1, w2, b2, f2, w3, b3, f3, w4)
    z = (v * x1).astype(F32)
    z = bidir_fft_conv(z, filt) + z * skip.astype(F32)
    y = z * x0.astype(F32)
    return group_rmsnorm(y, out_g, HY_GROUPS).astype(u.dtype)


def gla_chunked(q, k, v, g, include_diag):
    B, H, L, dk = q.shape
    dv = v.shape[-1]
    n = L // CHUNK
    rs = lambda a: a.reshape(B, H, n, CHUNK, a.shape[-1])
    q, k, v, g = rs(q), rs(k), rs(v), rs(g)
    b = jnp.cumsum(g, axis=3)
    b_last = b[:, :, :, -1:]
    q_in = q * jnp.exp(b)
    k_in = k * jnp.exp(-b)
    k_st = k * jnp.exp(b_last - b)
    att = jnp.einsum('bhncd,bhnsd->bhncs', q_in, k_in)
    mask = jnp.tril(jnp.ones((CHUNK, CHUNK), dtype=bool), 0 if include_diag else -1)
    att = jnp.where(mask, att, 0.0)
    o_intra = jnp.einsum('bhncs,bhnsv->bhncv', att, v)
    kv = jnp.einsum('bhncd,bhncv->bhndv', k_st, v)
    chunk_decay = jnp.exp(b_last[:, :, :, 0])

    def step(S, inp):
        q_c, dec_c, kv_c = inp
        o = jnp.einsum('bhcd,bhdv->bhcv', q_c, S)
        S = dec_c[..., None] * S + kv_c
        return S, o

    xs = (jnp.moveaxis(q_in, 2, 0), jnp.moveaxis(chunk_decay, 2, 0), jnp.moveaxis(kv, 2, 0))
    _, o_inter = lax.scan(step, jnp.zeros((B, H, dk, dv), F32), xs)
    o = o_intra + jnp.moveaxis(o_inter, 0, 2)
    return o.reshape(B, H, L, dv)


def gla_mixer(q, k, v, r, lr_f, lr_b, w_a2_f, b_a_f, w_a2_b, b_a_b, out_g):
    B, L, _ = q.shape
    heads = lambda a, d: a.astype(F32).reshape(B, L, GLA_HEADS, d).transpose(0, 2, 1, 3)
    g_f = jax.nn.log_sigmoid((lr_f @ w_a2_f + b_a_f).astype(F32)) / GATE_TEMP
    g_b = jax.nn.log_sigmoid((lr_b @ w_a2_b + b_a_b).astype(F32)) / GATE_TEMP
    qh = heads(q, GLA_DK) * (GLA_DK ** -0.5)
    kh = heads(k, GLA_DK)
    vh = heads(v, GLA_DV)
    gf = heads(g_f, GLA_DK)
    gb = heads(g_b, GLA_DK)
    flip = lambda a: a[:, :, ::-1]
    o_f = gla_chunked(qh, kh, vh, gf, True)
    o_b = flip(gla_chunked(flip(qh), flip(kh), flip(vh), flip(gb), False))
    o = (o_f + o_b).transpose(0, 2, 1, 3).reshape(B, L, GLA_VW)
    o = group_rmsnorm(o, out_g, GLA_HEADS)
    return (o * jax.nn.silu(r.astype(F32))).astype(q.dtype)


def setup_inputs(seed: int = 0) -> dict:
    key = jax.random.key(seed)
    ks = jax.random.split(key, 40)
    ctr = [0]

    def nxt():
        ctr[0] += 1
        return ks[ctr[0] - 1]

    def nrm(shape, scale):
        return scale * jax.random.normal(nxt(), shape, F32)

    def gain(shape):
        return 1.0 + 0.01 * jax.random.normal(nxt(), shape, F32)

    Dp = DEPTH
    return {
        'x': jax.random.normal(nxt(), (BATCH, SEQ, D_MODEL), F32),
        'ffn1_norm': gain((Dp, D_MODEL)),
        'ffn1_w_gate': nrm((Dp, D_MODEL, D_FF), D_MODEL ** -0.5),
        'ffn1_w_up': nrm((Dp, D_MODEL, D_FF), D_MODEL ** -0.5),
        'ffn1_w_down': nrm((Dp, D_FF, D_MODEL), D_FF ** -0.5),
        'mix_norm': gain((Dp, D_MODEL)),
        'w_in': nrm((Dp, D_MODEL, IN_COLS), D_MODEL ** -0.5),
        'hy_conv_w': nrm((Dp, SHORT_CONV, 3 * HY_WIDTH), SHORT_CONV ** -0.5),
        'hy_conv_b': nrm((Dp, 3 * HY_WIDTH), 0.02),
        'flt_w1': nrm((Dp, FILTER_EMB, FILTER_HIDDEN), FILTER_EMB ** -0.5),
        'flt_b1': nrm((Dp, FILTER_HIDDEN), 0.02),
        'flt_f1': gain((Dp, FILTER_HIDDEN)),
        'flt_w2': nrm((Dp, FILTER_HIDDEN, FILTER_HIDDEN), FILTER_HIDDEN ** -0.5),
        'flt_b2': nrm((Dp, FILTER_HIDDEN), 0.02),
        'flt_f2': gain((Dp, FILTER_HIDDEN)),
        'flt_w3': nrm((Dp, FILTER_HIDDEN, FILTER_HIDDEN), FILTER_HIDDEN ** -0.5),
        'flt_b3': nrm((Dp, FILTER_HIDDEN), 0.02),
        'flt_f3': gain((Dp, FILTER_HIDDEN)),
        'flt_w4': nrm((Dp, FILTER_HIDDEN, 2 * HY_WIDTH), FILTER_HIDDEN ** -0.5),
        'hy_skip': nrm((Dp, HY_WIDTH), 1.0),
        'hy_out_norm': gain((Dp, HY_WIDTH)),
        'gla_w_a2_f': nrm((Dp, GATE_RANK, GLA_KW), GATE_RANK ** -0.5),
        'gla_b_a_f': nrm((Dp, GLA_KW), 0.1),
        'gla_w_a2_b': nrm((Dp, GATE_RANK, GLA_KW), GATE_RANK ** -0.5),
        'gla_b_a_b': nrm((Dp, GLA_KW), 0.1),
        'gla_out_norm': gain((Dp, GLA_VW)),
        'w_out': nrm((Dp, MIX_WIDTH, D_MODEL), MIX_WIDTH ** -0.5),
        'ffn2_norm': gain((Dp, D_MODEL)),
        'ffn2_w_gate': nrm((Dp, D_MODEL, D_FF), D_MODEL ** -0.5),
        'ffn2_w_up': nrm((Dp, D_MODEL, D_FF), D_MODEL ** -0.5),
        'ffn2_w_down': nrm((Dp, D_FF, D_MODEL), D_FF ** -0.5),
        'final_norm': gain((D_MODEL,)),
    }


def reference(x, ffn1_norm, ffn1_w_gate, ffn1_w_up, ffn1_w_down, mix_norm, w_in,
              hy_conv_w, hy_conv_b, flt_w1, flt_b1, flt_f1, flt_w2, flt_b2, flt_f2,
              flt_w3, flt_b3, flt_f3, flt_w4, hy_skip, hy_out_norm,
              gla_w_a2_f, gla_b_a_f, gla_w_a2_b, gla_b_a_b, gla_out_norm, w_out,
              ffn2_norm, ffn2_w_gate, ffn2_w_up, ffn2_w_down, final_norm):
    split_sizes = [3 * HY_WIDTH, GLA_KW, GLA_KW, GLA_VW, GLA_VW, GATE_RANK, GATE_RANK]
    split_idx = [int(i) for i in np.cumsum(split_sizes)[:-1]]
    for l in range(DEPTH):
        x = macaron_half_ffn(x, ffn1_norm[l], ffn1_w_gate[l], ffn1_w_up[l], ffn1_w_down[l])
        h = rmsnorm(x, mix_norm[l])
        p = h @ w_in[l]
        p_hy, q, k, v, r, lr_f, lr_b = jnp.split(p, split_idx, axis=-1)
        y_hy = hyena_mixer(p_hy, hy_conv_w[l], hy_conv_b[l],
                           flt_w1[l], flt_b1[l], flt_f1[l], flt_w2[l], flt_b2[l], flt_f2[l],
                           flt_w3[l], flt_b3[l], flt_f3[l], flt_w4[l], hy_skip[l], hy_out_norm[l])
        y_gla = gla_mixer(q, k, v, r, lr_f, lr_b, gla_w_a2_f[l], gla_b_a_f[l],
                          gla_w_a2_b[l], gla_b_a_b[l], gla_out_norm[l])
        x = x + jnp.concatenate([y_hy, y_gla], axis=-1) @ w_out[l]
        x = macaron_half_ffn(x, ffn2_norm[l], ffn2_w_gate[l], ffn2_w_up[l], ffn2_w_down[l])
    return rmsnorm(x, final_norm)
```

```python
import functools
import math

import numpy as np
import jax
import jax.numpy as jnp
from jax import lax
from jax.experimental import pallas as pl
from jax.experimental.pallas import tpu as pltpu

F32 = jnp.float32
BF16 = jnp.bfloat16
HIGHEST = lax.Precision.HIGHEST

EPS = 1e-6
HY_WIDTH = 1024
HY_GROUPS = 8
FILTER_EMB = 33
FILTER_HIDDEN = 64
SHORT_DECAY_PCT = 0.3
LONG_DECAY_PCT = 1.5
DECAY_TARGET = 1e-2
GLA_HEADS = 4
GLA_DK = 128
GLA_DV = 256
GLA_KW = GLA_HEADS * GLA_DK
GLA_VW = GLA_HEADS * GLA_DV
GATE_RANK = 16
GATE_TEMP = 16.0
CHUNK = 64

LANES = 128
SUBLANES = 8
VMEM_LIMIT = 60 * 1024 * 1024

RB = 128
KB = RB // 2
PITCH = RB + SUBLANES


def _cparams(sem):
    return pltpu.CompilerParams(dimension_semantics=sem, vmem_limit_bytes=VMEM_LIMIT)


def _single(block_shape, index_map):
    return pl.BlockSpec(block_shape, index_map, pipeline_mode=pl.Buffered(1))


def _ffn_kernel(x_ref, g_ref, wg_ref, wu_ref, wd_ref, fg_ref, o_ref, h_scr, acc_scr, *, final_norm):
    j = pl.program_id(1)

    @pl.when(j == 0)
    def _():
        x = x_ref[...]
        r = lax.rsqrt(jnp.mean(x * x, axis=-1, keepdims=True) + EPS)
        h_scr[...] = (x * r * g_ref[...]).astype(BF16)
        acc_scr[...] = jnp.zeros_like(acc_scr)

    h = h_scr[...]
    gate = jnp.dot(h, wg_ref[...], preferred_element_type=F32)
    up = jnp.dot(h, wu_ref[...], preferred_element_type=F32)
    a = (jax.nn.silu(gate) * up).astype(BF16)
    acc_scr[...] += jnp.dot(a, wd_ref[...], preferred_element_type=F32)

    @pl.when(j == pl.num_programs(1) - 1)
    def _():
        y = x_ref[...] + 0.5 * acc_scr[...]
        if final_norm:
            r = lax.rsqrt(jnp.mean(y * y, axis=-1, keepdims=True) + EPS)
            y = y * r * fg_ref[...]
        o_ref[...] = y


def _ffn(x, norm_g, w_gate, w_up, w_down, final_g, *, final_norm, tm=512, tf=512):
    L, D = x.shape
    DF = w_gate.shape[1]
    tm = min(tm, L)
    tf = min(tf, DF)
    return pl.pallas_call(
        functools.partial(_ffn_kernel, final_norm=final_norm),
        grid=(L // tm, DF // tf),
        in_specs=[
            pl.BlockSpec((tm, D), lambda i, j: (i, 0)),
            pl.BlockSpec((1, D), lambda i, j: (0, 0)),
            pl.BlockSpec((D, tf), lambda i, j: (0, j)),
            pl.BlockSpec((D, tf), lambda i, j: (0, j)),
            pl.BlockSpec((tf, D), lambda i, j: (j, 0)),
            pl.BlockSpec((1, D), lambda i, j: (0, 0)),
        ],
        out_specs=pl.BlockSpec((tm, D), lambda i, j: (i, 0)),
        out_shape=jax.ShapeDtypeStruct((L, D), F32),
        scratch_shapes=[pltpu.VMEM((tm, D), BF16), pltpu.VMEM((tm, D), F32)],
        compiler_params=_cparams(("parallel", "arbitrary")),
        name="ffn",
    )(x, norm_g.reshape(1, D), w_gate.astype(BF16), w_up.astype(BF16), w_down.astype(BF16),
      final_g.reshape(1, D))


def _in_proj_kernel(x_ref, g_ref, w_ref, o_ref, h_scr):
    @pl.when(pl.program_id(1) == 0)
    def _():
        x = x_ref[...]
        r = lax.rsqrt(jnp.mean(x * x, axis=-1, keepdims=True) + EPS)
        h_scr[...] = (x * r * g_ref[...]).astype(BF16)

    o_ref[...] = jnp.dot(h_scr[...], w_ref[...], preferred_element_type=F32)


def _in_proj(x, norm_g, w_pad, *, tm=1024, tn=512):
    L, D = x.shape
    NP = w_pad.shape[1]
    tm = min(tm, L)
    return pl.pallas_call(
        _in_proj_kernel,
        grid=(L // tm, NP // tn),
        in_specs=[
            pl.BlockSpec((tm, D), lambda i, j: (i, 0)),
            pl.BlockSpec((1, D), lambda i, j: (0, 0)),
            pl.BlockSpec((D, tn), lambda i, j: (0, j)),
        ],
        out_specs=pl.BlockSpec((tm, tn), lambda i, j: (i, j)),
        out_shape=jax.ShapeDtypeStruct((L, NP), F32),
        scratch_shapes=[pltpu.VMEM((tm, D), BF16)],
        compiler_params=_cparams(("parallel", "arbitrary")),
        name="in_proj",
    )(x, norm_g.reshape(1, D), w_pad)


def _out_proj_kernel(x_ref, yh_ref, yg_ref, wh_ref, wg_ref, o_ref):
    o_ref[...] = (x_ref[...]
                  + jnp.dot(yh_ref[...], wh_ref[...], preferred_element_type=F32)
                  + jnp.dot(yg_ref[...], wg_ref[...], preferred_element_type=F32))


def _out_proj(x, y_hy, y_gla, w_out, *, tm=512):
    L, D = x.shape
    WH = y_hy.shape[1]
    WG = y_gla.shape[1]
    tm = min(tm, L)
    w = w_out.astype(BF16)
    return pl.pallas_call(
        _out_proj_kernel,
        grid=(L // tm,),
        in_specs=[
            pl.BlockSpec((tm, D), lambda i: (i, 0)),
            pl.BlockSpec((tm, WH), lambda i: (i, 0)),
            pl.BlockSpec((tm, WG), lambda i: (i, 0)),
            _single((WH, D), lambda i: (0, 0)),
            _single((WG, D), lambda i: (0, 0)),
        ],
        out_specs=pl.BlockSpec((tm, D), lambda i: (i, 0)),
        out_shape=jax.ShapeDtypeStruct((L, D), F32),
        compiler_params=_cparams(("parallel",)),
        name="out_proj",
    )(x, y_hy, y_gla, w[:WH], w[WH:])


def _filt_mlp_kernel(z_ref, w1_ref, b1_ref, f1_ref, w2_ref, b2_ref, f2_ref, w3_ref, b3_ref, f3_ref, o_ref):
    dot = functools.partial(jnp.dot, precision=HIGHEST, preferred_element_type=F32)
    hid = jnp.sin(f1_ref[...] * (dot(z_ref[...], w1_ref[...]) + b1_ref[...]))
    hid = jnp.sin(f2_ref[...] * (dot(hid, w2_ref[...]) + b2_ref[...]))
    o_ref[...] = jnp.sin(f3_ref[...] * (dot(hid, w3_ref[...]) + b3_ref[...]))


def _filt_mlp(feat, w1p, b1, f1, w2, b2, f2, w3, b3, f3, *, tm=1024):
    L, FP = feat.shape
    H = w2.shape[0]
    tm = min(tm, L)
    row = lambda a: a.reshape(1, H)
    full = lambda shp: pl.BlockSpec(shp, lambda i: (0, 0))
    return pl.pallas_call(
        _filt_mlp_kernel,
        grid=(L // tm,),
        in_specs=[pl.BlockSpec((tm, FP), lambda i: (i, 0)),
                  full((FP, H)), full((1, H)), full((1, H)),
                  full((H, H)), full((1, H)), full((1, H)),
                  full((H, H)), full((1, H)), full((1, H))],
        out_specs=pl.BlockSpec((tm, H), lambda i: (i, 0)),
        out_shape=jax.ShapeDtypeStruct((L, H), F32),
        compiler_params=_cparams(("parallel",)),
        name="filt_mlp",
    )(feat, w1p, row(b1), row(f1), w2, row(b2), row(f2), w3, row(b3), row(f3))


def _dft_tables(L):
    N = 2 * L
    RA = N // RB
    NH = RA // 2
    two_pi = 2.0 * np.pi
    k1 = np.arange(RA, dtype=np.int64)
    g = np.zeros((RB // 2, 2 * RA, 2 * NH), np.float64)
    for s in range(2):
        n = RB * np.arange(NH, dtype=np.int64)[None, :] + (2 * np.arange(RB // 2, dtype=np.int64) + s)[:, None]
        ang = two_pi * ((k1[None, :, None] * n[:, None, :]) % N) / N
        g[:, :RA, s * NH:(s + 1) * NH] = np.cos(ang)
        g[:, RA:, s * NH:(s + 1) * NH] = -np.sin(ang)
    n = RB * np.arange(NH, dtype=np.int64)[None, :] + np.arange(RB, dtype=np.int64)[:, None]
    ang = two_pi * ((n[:, :, None] * k1[None, None, :]) % N) / N
    gi = np.concatenate([np.cos(ang), -np.sin(ang)], axis=2)
    phi = two_pi * ((np.arange(KB)[:, None] * np.arange(RB)[None, :]) % RB) / RB
    f3 = np.block([[np.cos(phi), np.sin(phi)], [-np.sin(phi), np.cos(phi)]])
    i1 = np.block([[np.cos(phi.T), -np.sin(phi.T)], [np.sin(phi.T), np.cos(phi.T)]])
    cast = lambda a: jnp.asarray(a.astype(np.float32)).astype(BF16)
    return cast(g), cast(gi), cast(f3), cast(i1)


def _fwd_step1(u_ref, g_ref, ab_ref, *, RA, NH, CB):
    def body(m, carry):
        ua = u_ref[pl.ds(2 * m, NH, stride=PITCH), :]
        ub = u_ref[pl.ds(2 * m + 1, NH, stride=PITCH), :]
        zz = jnp.zeros_like(ua)
        rhs = jnp.concatenate([jnp.concatenate([ua, zz], axis=1),
                               jnp.concatenate([zz, ub], axis=1)], axis=0).astype(BF16)
        res = jnp.dot(g_ref[m], rhs, preferred_element_type=F32)
        ab_ref[pl.ds(2 * m, RA, stride=PITCH), :] = res[:RA, :CB]
        ab_ref[pl.ds(RA * PITCH + 2 * m, RA, stride=PITCH), :] = res[RA:, :CB]
        ab_ref[pl.ds(2 * m + 1, RA, stride=PITCH), :] = res[:RA, CB:]
        ab_ref[pl.ds(RA * PITCH + 2 * m + 1, RA, stride=PITCH), :] = res[RA:, CB:]
        return carry
    lax.fori_loop(0, RB // 2, body, 0)


def _fwd_step3_block(ab_ref, f3, k1, *, RA):
    ar = ab_ref[pl.ds(pl.multiple_of(k1 * PITCH, SUBLANES), RB), :]
    ai = ab_ref[pl.ds(pl.multiple_of((RA + k1) * PITCH, SUBLANES), RB), :]
    rhs = jnp.concatenate([ar, ai], axis=0).astype(BF16)
    return jnp.dot(f3, rhs, preferred_element_type=F32)


def _alt_sign(shape):
    rows = lax.broadcasted_iota(jnp.int32, shape, 0)
    return jnp.where(rows % 2 == 0, 1.0, -1.0).astype(F32)


def _filt_fft_kernel(hid_ref, w4f_ref, w4b_ref, dl_ref, g_ref, f3_ref, kh_ref, kn_ref,
                     u_scr, ab_scr, xb_scr, *, L, CB):
    N = 2 * L
    RA = N // RB
    NH = RA // 2
    f3 = f3_ref[...]
    sign = _alt_sign((RB, CB))
    inv_lm1 = 1.0 / (L - 1)

    def fill(w_ref, drop_lag0):
        def body(n1, carry):
            s_abs, s_alt = carry
            r0 = pl.multiple_of(n1 * RB, RB)
            h = jnp.dot(hid_ref[pl.ds(r0, RB), :], w_ref[...], precision=HIGHEST,
                        preferred_element_type=F32)
            pos = lax.broadcasted_iota(jnp.int32, (RB, CB), 0) + n1 * RB
            t = pos.astype(F32) * inv_lm1
            h = h * jnp.exp(-t * dl_ref[...])
            if drop_lag0:
                h = jnp.where(pos == 0, 0.0, h)
            u_scr[pl.ds(pl.multiple_of(n1 * PITCH, SUBLANES), RB), :] = h
            return s_abs + jnp.abs(h), s_alt + h
        zero = jnp.zeros((RB, CB), F32)
        s_abs, s_alt = lax.fori_loop(0, NH, body, (zero, zero))
        return (jnp.sum(s_abs, axis=0, keepdims=True),
                jnp.sum(s_alt * sign, axis=0, keepdims=True))

    abs_b, alt_b = fill(w4b_ref, True)
    _fwd_step1(u_scr, g_ref, ab_scr, RA=RA, NH=NH, CB=CB)

    def park(k1, carry):
        xb_scr[pl.ds(pl.multiple_of(k1 * RB, RB), RB), :] = _fwd_step3_block(ab_scr, f3, k1, RA=RA)
        return carry
    lax.fori_loop(0, RA, park, 0)

    abs_f, alt_f = fill(w4f_ref, False)
    _fwd_step1(u_scr, g_ref, ab_scr, RA=RA, NH=NH, CB=CB)

    inv_l1 = 1.0 / (abs_f + abs_b)
    scale = inv_l1 * (2.0 / N)
    row = lax.broadcasted_iota(jnp.int32, (KB, CB), 0)

    def combine(k1, carry):
        xf = _fwd_step3_block(ab_scr, f3, k1, RA=RA)
        xb = xb_scr[pl.ds(pl.multiple_of(k1 * RB, RB), RB), :]
        kr = (xf[:KB] + xb[:KB]) * scale
        ki = (xf[KB:] - xb[KB:]) * scale
        dc = jnp.logical_and(row == 0, k1 == 0)
        kr = jnp.where(dc, 0.5 * kr, kr)
        ki = jnp.where(dc, 0.5 * ki, ki)
        kh_ref[pl.ds(pl.multiple_of(k1 * RB, RB), RB), :] = jnp.concatenate([kr, ki], axis=0).astype(kh_ref.dtype)
        return carry
    lax.fori_loop(0, RA, combine, 0)
    kn = (alt_f + alt_b) * inv_l1 * (1.0 / N)
    kn_ref[...] = jnp.broadcast_to(kn, kn_ref.shape)


def _filt_fft(hid, w4, deltas, g_tab, f3_tab, *, L, CB=LANES):
    C = deltas.shape[1]
    H = hid.shape[1]
    N = 2 * L
    RA = N // RB
    NH = RA // 2
    nblk = C // CB
    return pl.pallas_call(
        functools.partial(_filt_fft_kernel, L=L, CB=CB),
        grid=(nblk,),
        in_specs=[
            _single((L, H), lambda c: (0, 0)),
            pl.BlockSpec((H, CB), lambda c: (0, c)),
            pl.BlockSpec((H, CB), lambda c: (0, nblk + c)),
            pl.BlockSpec((1, CB), lambda c: (0, c)),
            _single(g_tab.shape, lambda c: (0, 0, 0)),
            _single(f3_tab.shape, lambda c: (0, 0)),
        ],
        out_specs=[pl.BlockSpec((RA * RB, CB), lambda c: (0, c)),
                   pl.BlockSpec((SUBLANES, CB), lambda c: (0, c))],
        out_shape=[jax.ShapeDtypeStruct((RA * RB, C), BF16),
                   jax.ShapeDtypeStruct((SUBLANES, C), F32)],
        scratch_shapes=[pltpu.VMEM((NH * PITCH, CB), F32),
                        pltpu.VMEM((2 * RA * PITCH, CB), F32),
                        pltpu.VMEM((RA * RB, CB), F32)],
        compiler_params=_cparams(("arbitrary",)),
        name="filt_fft",
    )(hid, w4, w4, deltas, g_tab, f3_tab)


def _short_conv_rows(p_ref, w_ref, b_ref, n1, *, L):
    r0 = pl.multiple_of(n1 * RB, RB)
    cur = p_ref[pl.ds(r0, RB), :]
    prev = p_ref[pl.ds(jnp.maximum(r0 - 1, 0), 1), :]
    nxt = p_ref[pl.ds(jnp.minimum(r0 + RB, L - 1), 1), :]
    prev = jnp.where(r0 > 0, prev, 0.0)
    nxt = jnp.where(r0 + RB < L, nxt, 0.0)
    rows = lax.broadcasted_iota(jnp.int32, cur.shape, 0)
    up = jnp.where(rows == 0, prev, pltpu.roll(cur, 1, axis=0))
    dn = jnp.where(rows == RB - 1, nxt, pltpu.roll(cur, RB - 1, axis=0))
    w = w_ref[...]
    return b_ref[...] + up * w[0:1] + cur * w[1:2] + dn * w[2:3]


def _hy_conv_kernel(p0_ref, p1_ref, pv_ref, w0_ref, w1_ref, wv_ref, b0_ref, b1_ref, bv_ref,
                    kh_ref, kn_ref, skip_ref, og_ref, g_ref, gi_ref, f3_ref, i1_ref,
                    o_ref, u_scr, ab_scr, *, L, CB):
    N = 2 * L
    RA = N // RB
    NH = RA // 2
    f3 = f3_ref[...]
    i1 = i1_ref[...]
    sign = _alt_sign((RB, CB))

    def fill(n1, s_alt):
        z = _short_conv_rows(pv_ref, wv_ref, bv_ref, n1, L=L) * _short_conv_rows(p1_ref, w1_ref, b1_ref, n1, L=L)
        u_scr[pl.ds(pl.multiple_of(n1 * PITCH, SUBLANES), RB), :] = z
        return s_alt + z
    s_alt = lax.fori_loop(0, NH, fill, jnp.zeros((RB, CB), F32))
    z_nyq = jnp.sum(s_alt * sign, axis=0, keepdims=True)
    y_nyq = z_nyq * kn_ref[0:1, :]

    _fwd_step1(u_scr, g_ref, ab_scr, RA=RA, NH=NH, CB=CB)

    def mid(k1, carry):
        x = _fwd_step3_block(ab_scr, f3, k1, RA=RA)
        kh = kh_ref[pl.ds(pl.multiple_of(k1 * RB, RB), RB), :].astype(F32)
        xr, xi, kr, ki = x[:KB], x[KB:], kh[:KB], kh[KB:]
        y = jnp.concatenate([xr * kr - xi * ki, xr * ki + xi * kr], axis=0).astype(BF16)
        b = jnp.dot(i1, y, preferred_element_type=F32)
        ab_scr[pl.ds(pl.multiple_of(k1 * PITCH, SUBLANES), RB), :] = b[:RB]
        ab_scr[pl.ds(pl.multiple_of((RA + k1) * PITCH, SUBLANES), RB), :] = b[RB:]
        return carry
    lax.fori_loop(0, RA, mid, 0)

    def last(n2, carry):
        br = ab_scr[pl.ds(n2, RA, stride=PITCH), :]
        bi = ab_scr[pl.ds(RA * PITCH + n2, RA, stride=PITCH), :]
        rhs = jnp.concatenate([br, bi], axis=0).astype(BF16)
        u_scr[pl.ds(n2, NH, stride=PITCH), :] = jnp.dot(gi_ref[n2], rhs, preferred_element_type=F32)
        return carry
    lax.fori_loop(0, RB, last, 0)

    def finish(n1, carry):
        z = _short_conv_rows(pv_ref, wv_ref, bv_ref, n1, L=L) * _short_conv_rows(p1_ref, w1_ref, b1_ref, n1, L=L)
        x0 = _short_conv_rows(p0_ref, w0_ref, b0_ref, n1, L=L)
        conv = u_scr[pl.ds(pl.multiple_of(n1 * PITCH, SUBLANES), RB), :] + sign * y_nyq
        y = (conv + z * skip_ref[...]) * x0
        y = y * lax.rsqrt(jnp.mean(y * y, axis=-1, keepdims=True) + EPS) * og_ref[...]
        o_ref[pl.ds(pl.multiple_of(n1 * RB, RB), RB), :] = y.astype(o_ref.dtype)
        return carry
    lax.fori_loop(0, NH, finish, 0)


def _hy_conv(p, conv_w, conv_b, khat, knyq, skip, out_g, tabs, *, L, CB=LANES):
    C = skip.shape[1]
    assert CB == C // HY_GROUPS, "one channel block must be exactly one norm group"
    N = 2 * L
    RA = N // RB
    NH = RA // 2
    nblk = C // CB
    g_tab, gi_tab, f3_tab, i1_tab = tabs
    col = lambda off: (lambda c: (0, off * nblk + c))
    pspec = lambda off: _single((L, CB), col(off))
    wspec = lambda off: pl.BlockSpec((3, CB), col(off))
    bspec = lambda off: pl.BlockSpec((1, CB), col(off))
    return pl.pallas_call(
        functools.partial(_hy_conv_kernel, L=L, CB=CB),
        grid=(nblk,),
        in_specs=[pspec(0), pspec(1), pspec(2), wspec(0), wspec(1), wspec(2), bspec(0), bspec(1), bspec(2),
                  _single((RA * RB, CB), col(0)),
                  pl.BlockSpec((SUBLANES, CB), col(0)),
                  bspec(0), bspec(0),
                  _single(g_tab.shape, lambda c: (0, 0, 0)),
                  _single(gi_tab.shape, lambda c: (0, 0, 0)),
                  _single(f3_tab.shape, lambda c: (0, 0)),
                  _single(i1_tab.shape, lambda c: (0, 0))],
        out_specs=pl.BlockSpec((L, CB), col(0)),
        out_shape=jax.ShapeDtypeStruct((L, C), BF16),
        scratch_shapes=[pltpu.VMEM((NH * PITCH, CB), F32),
                        pltpu.VMEM((2 * RA * PITCH, CB), F32)],
        compiler_params=_cparams(("arbitrary",)),
        name="hy_conv",
    )(p, p, p, conv_w, conv_w, conv_w, conv_b, conv_b, conv_b, khat, knyq, skip, out_g,
      g_tab, gi_tab, f3_tab, i1_tab)


def _log_sigmoid(x):
    return jnp.minimum(x, 0.0) - jnp.log1p(jnp.exp(-jnp.abs(x)))


def _gla_kernel(q_ref, k_ref, v_ref, r_ref, lr_ref, wg_ref, bg_ref, og_ref, o_ref,
                g_scr, s_scr, of_scr, *, TB):
    ph = pl.program_id(1)
    i = pl.program_id(2)
    nb = pl.num_programs(2)
    nch = TB // CHUNK
    blk = i + ph * (nb - 1 - 2 * i)
    fwd = ph == 0

    @pl.when(i == 0)
    def _():
        s_scr[...] = jnp.zeros_like(s_scr)

    gate_in = jnp.dot(lr_ref[...], wg_ref[...], precision=HIGHEST, preferred_element_type=F32) + bg_ref[...]
    g_scr[...] = _log_sigmoid(gate_in) * (1.0 / GATE_TEMP)

    rr = lax.broadcasted_iota(jnp.int32, (CHUNK, CHUNK), 0)
    cc = lax.broadcasted_iota(jnp.int32, (CHUNK, CHUNK), 1)
    ahead = (cc - rr) * jnp.where(fwd, 1, -1)
    csum_mat = jnp.where(ahead <= 0, 1.0, 0.0).astype(F32)
    att_mask = ahead <= jnp.where(fwd, 0, -1)
    edge = jnp.where(fwd, CHUNK - 1, 0)
    row_dk = lax.broadcasted_iota(jnp.int32, (CHUNK, GLA_DK), 0)
    scale = GLA_DK ** -0.5

    def chunk(c, carry):
        ci = c + ph * (nch - 1 - 2 * c)
        rows = pl.ds(pl.multiple_of(ci * CHUNK, CHUNK), CHUNK)
        b = jnp.dot(csum_mat, g_scr[rows, :], precision=HIGHEST, preferred_element_type=F32)
        b_edge = jnp.sum(jnp.where(row_dk == edge, b, 0.0), axis=0, keepdims=True)
        q = q_ref[rows, :] * scale
        k = k_ref[rows, :]
        v = v_ref[rows, :].astype(BF16)
        q_in = (q * jnp.exp(b)).astype(BF16)
        k_in = (k * jnp.exp(-b)).astype(BF16)
        k_st = (k * jnp.exp(b_edge - b)).astype(BF16)
        att = lax.dot_general(q_in, k_in, (((1,), (1,)), ((), ())), preferred_element_type=F32)
        att = jnp.where(att_mask, att, 0.0).astype(BF16)
        s_t = s_scr[...]
        o = (jnp.dot(att, v, preferred_element_type=F32)
             + lax.dot_general(q_in, s_t.astype(BF16), (((1,), (1,)), ((), ())), preferred_element_type=F32))
        kv_t = lax.dot_general(v, k_st, (((0,), (0,)), ((), ())), preferred_element_type=F32)
        s_scr[...] = s_t * jnp.exp(b_edge) + kv_t
        grow = pl.ds(pl.multiple_of(blk * TB + ci * CHUNK, CHUNK), CHUNK)

        @pl.when(fwd)
        def _():
            of_scr[grow, :] = o

        @pl.when(jnp.logical_not(fwd))
        def _():
            tot = of_scr[grow, :] + o
            tot = tot * lax.rsqrt(jnp.mean(tot * tot, axis=-1, keepdims=True) + EPS) * og_ref[...]
            o_ref[rows, :] = (tot * jax.nn.silu(r_ref[rows, :])).astype(o_ref.dtype)
        return carry
    lax.fori_loop(0, nch, chunk, 0)


def _gla(p, w_gate, b_gate, out_g, *, L, col0, TB=512):
    TB = min(TB, L)
    nb = L // TB
    qb = col0 // GLA_DK
    kb = qb + GLA_HEADS
    vb = (col0 + 2 * GLA_KW) // GLA_DV
    rb = vb + GLA_HEADS
    lb = (col0 + 2 * GLA_KW + 2 * GLA_VW) // LANES
    blk = lambda ph, i: i + ph * (nb - 1 - 2 * i)
    late = lambda ph, i: jnp.where(ph == 0, nb - 1, nb - 1 - i)
    return pl.pallas_call(
        functools.partial(_gla_kernel, TB=TB),
        grid=(GLA_HEADS, 2, nb),
        in_specs=[
            pl.BlockSpec((TB, GLA_DK), lambda h, ph, i: (blk(ph, i), qb + h)),
            pl.BlockSpec((TB, GLA_DK), lambda h, ph, i: (blk(ph, i), kb + h)),
            pl.BlockSpec((TB, GLA_DV), lambda h, ph, i: (blk(ph, i), vb + h)),
            pl.BlockSpec((TB, GLA_DV), lambda h, ph, i: (late(ph, i), rb + h)),
            pl.BlockSpec((TB, LANES), lambda h, ph, i: (blk(ph, i), lb)),
            pl.BlockSpec((None, LANES, GLA_DK), lambda h, ph, i: (ph, 0, h)),
            pl.BlockSpec((None, 1, GLA_DK), lambda h, ph, i: (ph, 0, h)),
            pl.BlockSpec((1, GLA_DV), lambda h, ph, i: (0, h)),
        ],
        out_specs=pl.BlockSpec((TB, GLA_DV), lambda h, ph, i: (late(ph, i), h)),
        out_shape=jax.ShapeDtypeStruct((L, GLA_VW), BF16),
        scratch_shapes=[pltpu.VMEM((TB, GLA_DK), F32),
                        pltpu.VMEM((GLA_DV, GLA_DK), F32),
                        pltpu.VMEM((L, GLA_DV), F32)],
        compiler_params=_cparams(("arbitrary", "arbitrary", "arbitrary")),
        name="gla",
    )(p, p, p, p, p, w_gate, b_gate, out_g)


def _filter_features(L):
    t = jnp.linspace(0.0, 1.0, L, dtype=F32)[:, None]
    bands = (FILTER_EMB - 1) // 2
    freqs = jnp.linspace(1e-4, bands - 1, bands, dtype=F32)[None, :]
    ang = (2.0 * math.pi / L) * jnp.arange(L, dtype=F32)[:, None] * freqs
    feat = jnp.concatenate([t, jnp.cos(ang), -jnp.sin(ang)], axis=-1)
    return jnp.pad(feat, ((0, 0), (0, LANES - FILTER_EMB)))


def _filter_deltas():
    min_decay = math.log(DECAY_TARGET) / LONG_DECAY_PCT
    max_decay = math.log(DECAY_TARGET) / SHORT_DECAY_PCT
    return jnp.abs(jnp.linspace(min_decay, max_decay, HY_WIDTH, dtype=F32)).reshape(1, HY_WIDTH)


def kernel(x, ffn1_norm, ffn1_w_gate, ffn1_w_up, ffn1_w_down, mix_norm, w_in, hy_conv_w, hy_conv_b, flt_w1, flt_b1, flt_f1, flt_w2, flt_b2, flt_f2, flt_w3, flt_b3, flt_f3, flt_w4, hy_skip, hy_out_norm, gla_w_a2_f, gla_b_a_f, gla_w_a2_b, gla_b_a_b, gla_out_norm, w_out, ffn2_norm, ffn2_w_gate, ffn2_w_up, ffn2_w_down, final_norm):
    B, L, D = x.shape
    depth = ffn1_norm.shape[0]
    tabs = _dft_tables(L)
    feat = _filter_features(L)
    deltas = _filter_deltas()
    in_cols = w_in.shape[-1]
    in_pad = -in_cols % 512
    gla_col0 = 3 * HY_WIDTH
    outs = []
    for bi in range(B):
        xb = x[bi]
        for l in range(depth):
            last = l == depth - 1
            xb = _ffn(xb, ffn1_norm[l], ffn1_w_gate[l], ffn1_w_up[l], ffn1_w_down[l], final_norm, final_norm=False)
            w_pad = jnp.pad(w_in[l], ((0, 0), (0, in_pad))).astype(BF16)
            p = _in_proj(xb, mix_norm[l], w_pad)

            w1p = jnp.pad(flt_w1[l], ((0, LANES - FILTER_EMB), (0, 0)))
            hid = _filt_mlp(feat, w1p, flt_b1[l], flt_f1[l], flt_w2[l], flt_b2[l], flt_f2[l],
                            flt_w3[l], flt_b3[l], flt_f3[l])
            khat, knyq = _filt_fft(hid, flt_w4[l], deltas, tabs[0], tabs[2], L=L)
            y_hy = _hy_conv(p, hy_conv_w[l], hy_conv_b[l].reshape(1, -1), khat, knyq,
                            hy_skip[l].reshape(1, -1), hy_out_norm[l].reshape(1, -1), tabs, L=L)

            w_gate = jnp.zeros((2, LANES, GLA_KW), F32)
            w_gate = w_gate.at[0, :GATE_RANK].set(gla_w_a2_f[l]).at[1, GATE_RANK:2 * GATE_RANK].set(gla_w_a2_b[l])
            b_gate = jnp.stack([gla_b_a_f[l], gla_b_a_b[l]]).reshape(2, 1, GLA_KW)
            y_gla = _gla(p, w_gate, b_gate, gla_out_norm[l].reshape(1, -1), L=L, col0=gla_col0)

            xb = _out_proj(xb, y_hy, y_gla, w_out[l])
            xb = _ffn(xb, ffn2_norm[l], ffn2_w_gate[l], ffn2_w_up[l], ffn2_w_down[l], final_norm,
                      final_norm=last)
        outs.append(xb)
    return jnp.stack(outs)
```

```python
import functools
import math

import numpy as np
import jax
import jax.numpy as jnp
from jax import lax
from jax.experimental import pallas as pl
from jax.experimental.pallas import tpu as pltpu

F32 = jnp.float32
BF16 = jnp.bfloat16
HIGHEST = lax.Precision.HIGHEST

EPS = 1e-6
HY_WIDTH = 1024
HY_GROUPS = 8
FILTER_EMB = 33
FILTER_HIDDEN = 64
SHORT_DECAY_PCT = 0.3
LONG_DECAY_PCT = 1.5
DECAY_TARGET = 1e-2
GLA_HEADS = 4
GLA_DK = 128
GLA_DV = 256
GLA_KW = GLA_HEADS * GLA_DK
GLA_VW = GLA_HEADS * GLA_DV
GATE_RANK = 16
GATE_TEMP = 16.0
CHUNK = 64

LANES = 128
SUBLANES = 8
VMEM_LIMIT = 60 * 1024 * 1024

RB = 128
KB = RB // 2
PITCH = RB + SUBLANES


def _cparams(sem):
    return pltpu.CompilerParams(dimension_semantics=sem, vmem_limit_bytes=VMEM_LIMIT)


def _single(block_shape, index_map):
    return pl.BlockSpec(block_shape, index_map, pipeline_mode=pl.Buffered(1))


def _ffn_kernel(x_ref, g_ref, wg_ref, wu_ref, wd_ref, fg_ref, o_ref, h_scr, acc_scr, *, final_norm):
    j = pl.program_id(1)

    @pl.when(j == 0)
    def _():
        x = x_ref[...]
        r = lax.rsqrt(jnp.mean(x * x, axis=-1, keepdims=True) + EPS)
        h_scr[...] = (x * r * g_ref[...]).astype(BF16)
        acc_scr[...] = jnp.zeros_like(acc_scr)

    h = h_scr[...]
    gate = jnp.dot(h, wg_ref[...], preferred_element_type=F32)
    up = jnp.dot(h, wu_ref[...], preferred_element_type=F32)
    a = (jax.nn.silu(gate) * up).astype(BF16)
    acc_scr[...] += jnp.dot(a, wd_ref[...], preferred_element_type=F32)

    @pl.when(j == pl.num_programs(1) - 1)
    def _():
        y = x_ref[...] + 0.5 * acc_scr[...]
        if final_norm:
            r = lax.rsqrt(jnp.mean(y * y, axis=-1, keepdims=True) + EPS)
            y = y * r * fg_ref[...]
        o_ref[...] = y


def _ffn(x, norm_g, w_gate, w_up, w_down, final_g, *, final_norm, tm=512, tf=512):
    L, D = x.shape
    DF = w_gate.shape[1]
    tm = min(tm, L)
    tf = min(tf, DF)
    return pl.pallas_call(
        functools.partial(_ffn_kernel, final_norm=final_norm),
        grid=(L // tm, DF // tf),
        in_specs=[
            pl.BlockSpec((tm, D), lambda i, j: (i, 0)),
            pl.BlockSpec((1, D), lambda i, j: (0, 0)),
            pl.BlockSpec((D, tf), lambda i, j: (0, j)),
            pl.BlockSpec((D, tf), lambda i, j: (0, j)),
            pl.BlockSpec((tf, D), lambda i, j: (j, 0)),
            pl.BlockSpec((1, D), lambda i, j: (0, 0)),
        ],
        out_specs=pl.BlockSpec((tm, D), lambda i, j: (i, 0)),
        out_shape=jax.ShapeDtypeStruct((L, D), F32),
        scratch_shapes=[pltpu.VMEM((tm, D), BF16), pltpu.VMEM((tm, D), F32)],
        compiler_params=_cparams(("parallel", "arbitrary")),
        name="ffn",
    )(x, norm_g.reshape(1, D), w_gate.astype(BF16), w_up.astype(BF16), w_down.astype(BF16),
      final_g.reshape(1, D))


def _in_proj_kernel(x_ref, g_ref, w_ref, o_ref, h_scr):
    @pl.when(pl.program_id(1) == 0)
    def _():
        x = x_ref[...]
        r = lax.rsqrt(jnp.mean(x * x, axis=-1, keepdims=True) + EPS)
        h_scr[...] = (x * r * g_ref[...]).astype(BF16)

    o_ref[...] = jnp.dot(h_scr[...], w_ref[...], preferred_element_type=F32)


def _in_proj(x, norm_g, w_pad, *, tm=1024, tn=512):
    L, D = x.shape
    NP = w_pad.shape[1]
    tm = min(tm, L)
    return pl.pallas_call(
        _in_proj_kernel,
        grid=(L // tm, NP // tn),
        in_specs=[
            pl.BlockSpec((tm, D), lambda i, j: (i, 0)),
            pl.BlockSpec((1, D), lambda i, j: (0, 0)),
            pl.BlockSpec((D, tn), lambda i, j: (0, j)),
        ],
        out_specs=pl.BlockSpec((tm, tn), lambda i, j: (i, j)),
        out_shape=jax.ShapeDtypeStruct((L, NP), F32),
        scratch_shapes=[pltpu.VMEM((tm, D), BF16)],
        compiler_params=_cparams(("parallel", "arbitrary")),
        name="in_proj",
    )(x, norm_g.reshape(1, D), w_pad)


def _out_proj_kernel(x_ref, yh_ref, yg_ref, wh_ref, wg_ref, o_ref):
    o_ref[...] = (x_ref[...]
                  + jnp.dot(yh_ref[...], wh_ref[...], preferred_element_type=F32)
                  + jnp.dot(yg_ref[...], wg_ref[...], preferred_element_type=F32))


def _out_proj(x, y_hy, y_gla, w_out, *, tm=512):
    L, D = x.shape
    WH = y_hy.shape[1]
    WG = y_gla.shape[1]
    tm = min(tm, L)
    w = w_out.astype(BF16)
    return pl.pallas_call(
        _out_proj_kernel,
        grid=(L // tm,),
        in_specs=[
            pl.BlockSpec((tm, D), lambda i: (i, 0)),
            pl.BlockSpec((tm, WH), lambda i: (i, 0)),
            pl.BlockSpec((tm, WG), lambda i: (i, 0)),
            _single((WH, D), lambda i: (0, 0)),
            _single((WG, D), lambda i: (0, 0)),
        ],
        out_specs=pl.BlockSpec((tm, D), lambda i: (i, 0)),
        out_shape=jax.ShapeDtypeStruct((L, D), F32),
        compiler_params=_cparams(("parallel",)),
        name="out_proj",
    )(x, y_hy, y_gla, w[:WH], w[WH:])


def _filt_mlp_kernel(z_ref, zr_ref, w1_ref, b1_ref, f1_ref, w2_ref, b2_ref, f2_ref, w3_ref, b3_ref, f3_ref, o_ref):
    dot = functools.partial(jnp.dot, precision=HIGHEST, preferred_element_type=F32)

    def mlp(z):
        hid = jnp.sin(f1_ref[...] * (dot(z, w1_ref[...]) + b1_ref[...]))
        hid = jnp.sin(f2_ref[...] * (dot(hid, w2_ref[...]) + b2_ref[...]))
        return jnp.sin(f3_ref[...] * (dot(hid, w3_ref[...]) + b3_ref[...]))

    H = w2_ref.shape[0]
    o_ref[:, :H] = mlp(z_ref[...])
    o_ref[:, H:] = mlp(zr_ref[...])


def _filt_mlp(feat, feat_rev, w1p, b1, f1, w2, b2, f2, w3, b3, f3, *, tm=1024):
    L, FP = feat.shape
    H = w2.shape[0]
    tm = min(tm, L)
    row = lambda a: a.reshape(1, H)
    full = lambda shp: pl.BlockSpec(shp, lambda i: (0, 0))
    return pl.pallas_call(
        _filt_mlp_kernel,
        grid=(L // tm,),
        in_specs=[pl.BlockSpec((tm, FP), lambda i: (i, 0)), pl.BlockSpec((tm, FP), lambda i: (i, 0)),
                  full((FP, H)), full((1, H)), full((1, H)),
                  full((H, H)), full((1, H)), full((1, H)),
                  full((H, H)), full((1, H)), full((1, H))],
        out_specs=pl.BlockSpec((tm, 2 * H), lambda i: (i, 0)),
        out_shape=jax.ShapeDtypeStruct((L, 2 * H), F32),
        compiler_params=_cparams(("parallel",)),
        name="filt_mlp",
    )(feat, feat_rev, w1p, row(b1), row(f1), w2, row(b2), row(f2), w3, row(b3), row(f3))


def _dft_tables(L):
    N = 2 * L
    RA = N // RB
    NH = RA // 2
    two_pi = 2.0 * np.pi
    k1 = np.arange(RA, dtype=np.int64)

    def step1_table(NR):
        g = np.zeros((RB // 2, 2 * RA, 2 * NR), np.float64)
        for s in range(2):
            n = RB * np.arange(NR, dtype=np.int64)[None, :] + (2 * np.arange(RB // 2, dtype=np.int64) + s)[:, None]
            ang = two_pi * ((k1[None, :, None] * n[:, None, :]) % N) / N
            g[:, :RA, s * NR:(s + 1) * NR] = np.cos(ang)
            g[:, RA:, s * NR:(s + 1) * NR] = -np.sin(ang)
        return g

    n = RB * np.arange(NH, dtype=np.int64)[None, :] + np.arange(RB, dtype=np.int64)[:, None]
    ang = two_pi * ((n[:, :, None] * k1[None, None, :]) % N) / N
    gi = np.concatenate([np.cos(ang), -np.sin(ang)], axis=2)
    phi = two_pi * ((np.arange(KB)[:, None] * np.arange(RB)[None, :]) % RB) / RB
    f3 = np.block([[np.cos(phi), np.sin(phi)], [-np.sin(phi), np.cos(phi)]])
    i1 = np.block([[np.cos(phi.T), -np.sin(phi.T)], [np.sin(phi.T), np.cos(phi.T)]])
    cast = lambda a: jnp.asarray(a.astype(np.float32)).astype(BF16)
    return dict(g_half=cast(step1_table(NH)), g_full=cast(step1_table(RA)), gi=cast(gi), f3=cast(f3), i1=cast(i1))


def _fwd_step1(u_ref, g_ref, ab_ref, *, RA, NR, CB, G=2):
    def body(t, carry):
        ms = [t * G + j for j in range(G)]
        rhs = []
        for m in ms:
            ua = u_ref[pl.ds(2 * m, NR, stride=PITCH), :]
            ub = u_ref[pl.ds(2 * m + 1, NR, stride=PITCH), :]
            zz = jnp.zeros_like(ua)
            rhs.append(jnp.concatenate([jnp.concatenate([ua, zz], axis=1),
                                        jnp.concatenate([zz, ub], axis=1)], axis=0).astype(BF16))
        res = [jnp.dot(g_ref[m], r, preferred_element_type=F32) for m, r in zip(ms, rhs)]
        for m, r in zip(ms, res):
            ab_ref[pl.ds(2 * m, RA, stride=PITCH), :] = r[:RA, :CB]
            ab_ref[pl.ds(RA * PITCH + 2 * m, RA, stride=PITCH), :] = r[RA:, :CB]
            ab_ref[pl.ds(2 * m + 1, RA, stride=PITCH), :] = r[:RA, CB:]
            ab_ref[pl.ds(RA * PITCH + 2 * m + 1, RA, stride=PITCH), :] = r[RA:, CB:]
        return carry
    lax.fori_loop(0, RB // 2 // G, body, 0)


def _fwd_step3_pair(ab_ref, f3, kp, *, RA):
    cols = []
    for s in range(2):
        k1 = 2 * kp + s
        ar = ab_ref[pl.ds(pl.multiple_of(k1 * PITCH, SUBLANES), RB), :]
        ai = ab_ref[pl.ds(pl.multiple_of((RA + k1) * PITCH, SUBLANES), RB), :]
        cols.append(jnp.concatenate([ar, ai], axis=0))
    rhs = jnp.concatenate(cols, axis=1).astype(BF16)
    return jnp.dot(f3, rhs, preferred_element_type=F32)


def _alt_sign(shape):
    rows = lax.broadcasted_iota(jnp.int32, shape, 0)
    return jnp.where(rows % 2 == 0, 1.0, -1.0).astype(F32)


def _filt_fft_kernel(hid_ref, w4_ref, dl_ref, g_ref, f3_ref, kh_ref, kn_ref, u_scr, ab_scr, *, L, CB, G):
    N = 2 * L
    RA = N // RB
    NH = RA // 2
    f3 = f3_ref[...]
    inv_lm1 = 1.0 / (L - 1)

    def fill(n1, carry):
        s_abs, s_alt = carry
        r0 = pl.multiple_of(n1 * RB, RB)
        h2 = jnp.dot(hid_ref[pl.ds(r0, RB), :], w4_ref[...], precision=HIGHEST, preferred_element_type=F32)
        pos = lax.broadcasted_iota(jnp.int32, (RB, CB), 0) + n1 * RB
        hf = h2[:, :CB] * jnp.exp(-(pos.astype(F32) * inv_lm1) * dl_ref[...])
        hb = h2[:, CB:] * jnp.exp(-((L - pos).astype(F32) * inv_lm1) * dl_ref[...])
        hb = jnp.where(pos == 0, 0.0, hb)
        u_scr[pl.ds(pl.multiple_of(n1 * PITCH, SUBLANES), RB), :] = hf
        u_scr[pl.ds(pl.multiple_of((NH + n1) * PITCH, SUBLANES), RB), :] = hb
        return s_abs + (jnp.abs(hf) + jnp.abs(hb)), s_alt + (hf + hb)
    zero = jnp.zeros((RB, CB), F32)
    s_abs, s_alt = lax.fori_loop(0, NH, fill, (zero, zero), unroll=2)
    inv_l1 = 1.0 / jnp.sum(s_abs, axis=0, keepdims=True)
    kn = jnp.sum(s_alt * _alt_sign((RB, CB)), axis=0, keepdims=True) * inv_l1 * (1.0 / N)
    kn_ref[...] = jnp.broadcast_to(kn, kn_ref.shape)

    _fwd_step1(u_scr, g_ref, ab_scr, RA=RA, NR=RA, CB=CB, G=G)

    scale = inv_l1 * (2.0 / N)
    scale2 = jnp.concatenate([scale, scale], axis=1)
    row2 = lax.broadcasted_iota(jnp.int32, (KB, 2 * CB), 0)
    lane2 = lax.broadcasted_iota(jnp.int32, (KB, 2 * CB), 1)
    dc_pos = jnp.logical_and(row2 == 0, lane2 < CB)

    def emit(t, carry):
        kps = [t * G + j for j in range(G)]
        xs = [_fwd_step3_pair(ab_scr, f3, kp, RA=RA) for kp in kps]
        for kp, x in zip(kps, xs):
            kr = x[:KB] * scale2
            ki = x[KB:] * scale2
            dc = jnp.logical_and(dc_pos, kp == 0)
            kr = jnp.where(dc, 0.5 * kr, kr)
            ki = jnp.where(dc, 0.5 * ki, ki)
            kh_ref[pl.ds(pl.multiple_of(kp * RB, RB), RB), :] = jnp.concatenate([kr, ki], axis=0).astype(kh_ref.dtype)
        return carry
    lax.fori_loop(0, RA // 2 // G, emit, 0)


def _filt_fft(hid2, w4bd, deltas, g_tab, f3_tab, *, L, CB=LANES, G=2):
    C = deltas.shape[1]
    H2 = hid2.shape[1]
    N = 2 * L
    RA = N // RB
    nblk = C // CB
    return pl.pallas_call(
        functools.partial(_filt_fft_kernel, L=L, CB=CB, G=G),
        grid=(nblk,),
        in_specs=[
            _single((L, H2), lambda c: (0, 0)),
            pl.BlockSpec((H2, 2 * CB), lambda c: (0, c)),
            pl.BlockSpec((1, CB), lambda c: (0, c)),
            _single(g_tab.shape, lambda c: (0, 0, 0)),
            _single(f3_tab.shape, lambda c: (0, 0)),
        ],
        out_specs=[pl.BlockSpec((RA // 2 * RB, 2 * CB), lambda c: (0, c)),
                   pl.BlockSpec((SUBLANES, CB), lambda c: (0, c))],
        out_shape=[jax.ShapeDtypeStruct((RA // 2 * RB, 2 * C), BF16),
                   jax.ShapeDtypeStruct((SUBLANES, C), F32)],
        scratch_shapes=[pltpu.VMEM((RA * PITCH, CB), F32),
                        pltpu.VMEM((2 * RA * PITCH, CB), F32)],
        compiler_params=_cparams(("arbitrary",)),
        name="filt_fft",
    )(hid2, w4bd, deltas, g_tab, f3_tab)


def _short_conv_rows(p_ref, w_ref, b_ref, n1, *, L):
    r0 = pl.multiple_of(n1 * RB, RB)
    cur = p_ref[pl.ds(r0, RB), :]
    prev = p_ref[pl.ds(jnp.maximum(r0 - 1, 0), 1), :]
    nxt = p_ref[pl.ds(jnp.minimum(r0 + RB, L - 1), 1), :]
    prev = jnp.where(r0 > 0, prev, 0.0)
    nxt = jnp.where(r0 + RB < L, nxt, 0.0)
    rows = lax.broadcasted_iota(jnp.int32, cur.shape, 0)
    up = jnp.where(rows == 0, prev, pltpu.roll(cur, 1, axis=0))
    dn = jnp.where(rows == RB - 1, nxt, pltpu.roll(cur, RB - 1, axis=0))
    w = w_ref[...]
    return b_ref[...] + up * w[0:1] + cur * w[1:2] + dn * w[2:3]


def _hy_conv_kernel(p0_ref, p1_ref, pv_ref, w0_ref, w1_ref, wv_ref, b0_ref, b1_ref, bv_ref,
                    kh_ref, kn_ref, skip_ref, og_ref, g_ref, gi_ref, f3_ref, i1_ref,
                    o_ref, u_scr, ab_scr, *, L, CB, G):
    N = 2 * L
    RA = N // RB
    NH = RA // 2
    f3 = f3_ref[...]
    i1 = i1_ref[...]
    sign = _alt_sign((RB, CB))

    def fill(n1, s_alt):
        z = _short_conv_rows(pv_ref, wv_ref, bv_ref, n1, L=L) * _short_conv_rows(p1_ref, w1_ref, b1_ref, n1, L=L)
        u_scr[pl.ds(pl.multiple_of(n1 * PITCH, SUBLANES), RB), :] = z
        return s_alt + z
    s_alt = lax.fori_loop(0, NH, fill, jnp.zeros((RB, CB), F32), unroll=2)
    z_nyq = jnp.sum(s_alt * sign, axis=0, keepdims=True)
    y_nyq = z_nyq * kn_ref[0:1, :]

    _fwd_step1(u_scr, g_ref, ab_scr, RA=RA, NR=NH, CB=CB, G=G)

    def mid(t, carry):
        kps = [t * G + j for j in range(G)]
        xs = [_fwd_step3_pair(ab_scr, f3, kp, RA=RA) for kp in kps]
        khs = [kh_ref[pl.ds(pl.multiple_of(kp * RB, RB), RB), :].astype(F32) for kp in kps]
        bs = []
        for x, kh in zip(xs, khs):
            xr, xi, kr, ki = x[:KB], x[KB:], kh[:KB], kh[KB:]
            y = jnp.concatenate([xr * kr - xi * ki, xr * ki + xi * kr], axis=0).astype(BF16)
            bs.append(jnp.dot(i1, y, preferred_element_type=F32))
        for kp, b in zip(kps, bs):
            for s in range(2):
                k1 = 2 * kp + s
                ab_scr[pl.ds(pl.multiple_of(k1 * PITCH, SUBLANES), RB), :] = b[:RB, s * CB:(s + 1) * CB]
                ab_scr[pl.ds(pl.multiple_of((RA + k1) * PITCH, SUBLANES), RB), :] = b[RB:, s * CB:(s + 1) * CB]
        return carry
    lax.fori_loop(0, RA // 2 // G, mid, 0)

    GL = 2 * G

    def last(t, carry):
        n2s = [t * GL + j for j in range(GL)]
        rhs = [jnp.concatenate([ab_scr[pl.ds(n2, RA, stride=PITCH), :],
                                ab_scr[pl.ds(RA * PITCH + n2, RA, stride=PITCH), :]], axis=0).astype(BF16)
               for n2 in n2s]
        ys = [jnp.dot(gi_ref[n2], r, preferred_element_type=F32) for n2, r in zip(n2s, rhs)]
        for n2, y in zip(n2s, ys):
            u_scr[pl.ds(n2, NH, stride=PITCH), :] = y
        return carry
    lax.fori_loop(0, RB // GL, last, 0)

    def finish(n1, carry):
        z = _short_conv_rows(pv_ref, wv_ref, bv_ref, n1, L=L) * _short_conv_rows(p1_ref, w1_ref, b1_ref, n1, L=L)
        x0 = _short_conv_rows(p0_ref, w0_ref, b0_ref, n1, L=L)
        conv = u_scr[pl.ds(pl.multiple_of(n1 * PITCH, SUBLANES), RB), :] + sign * y_nyq
        y = (conv + z * skip_ref[...]) * x0
        y = y * lax.rsqrt(jnp.mean(y * y, axis=-1, keepdims=True) + EPS) * og_ref[...]
        o_ref[pl.ds(pl.multiple_of(n1 * RB, RB), RB), :] = y.astype(o_ref.dtype)
        return carry
    lax.fori_loop(0, NH, finish, 0, unroll=2)


def _hy_conv(p, conv_w, conv_b, khat, knyq, skip, out_g, tabs, *, L, CB=LANES, G=2):
    C = skip.shape[1]
    assert CB == C // HY_GROUPS, "one channel block must be exactly one norm group"
    N = 2 * L
    RA = N // RB
    NH = RA // 2
    nblk = C // CB
    g_tab, gi_tab, f3_tab, i1_tab = tabs["g_half"], tabs["gi"], tabs["f3"], tabs["i1"]
    col = lambda off: (lambda c: (0, off * nblk + c))
    pspec = lambda off: _single((L, CB), col(off))
    wspec = lambda off: pl.BlockSpec((3, CB), col(off))
    bspec = lambda off: pl.BlockSpec((1, CB), col(off))
    return pl.pallas_call(
        functools.partial(_hy_conv_kernel, L=L, CB=CB, G=G),
        grid=(nblk,),
        in_specs=[pspec(0), pspec(1), pspec(2), wspec(0), wspec(1), wspec(2), bspec(0), bspec(1), bspec(2),
                  _single((RA // 2 * RB, 2 * CB), col(0)),
                  pl.BlockSpec((SUBLANES, CB), col(0)),
                  bspec(0), bspec(0),
                  _single(g_tab.shape, lambda c: (0, 0, 0)),
                  _single(gi_tab.shape, lambda c: (0, 0, 0)),
                  _single(f3_tab.shape, lambda c: (0, 0)),
                  _single(i1_tab.shape, lambda c: (0, 0))],
        out_specs=pl.BlockSpec((L, CB), col(0)),
        out_shape=jax.ShapeDtypeStruct((L, C), BF16),
        scratch_shapes=[pltpu.VMEM((NH * PITCH, CB), F32),
                        pltpu.VMEM((2 * RA * PITCH, CB), F32)],
        compiler_params=_cparams(("arbitrary",)),
        name="hy_conv",
    )(p, p, p, conv_w, conv_w, conv_w, conv_b, conv_b, conv_b, khat, knyq, skip, out_g,
      g_tab, gi_tab, f3_tab, i1_tab)


def _log_sigmoid(x):
    return jnp.minimum(x, 0.0) - jnp.log1p(jnp.exp(-jnp.abs(x)))


def _dot_nt(a, b):
    return lax.dot_general(a, b, (((1,), (1,)), ((), ())), preferred_element_type=F32)


def _dot_tn(a, b):
    return lax.dot_general(a, b, (((0,), (0,)), ((), ())), preferred_element_type=F32)


def _gla_scan_block(q_ref, k_ref, v_ref, lr_ref, wg, bg, s_ref, qin_ref, kv_ref, o_ref, *, TB, reverse):
    nch = TB // CHUNK
    gate_in = jnp.dot(lr_ref[...], wg, precision=HIGHEST, preferred_element_type=F32) + bg
    g = _log_sigmoid(gate_in) * (1.0 / GATE_TEMP)
    g_hi = g.astype(BF16)
    rem = g - g_hi.astype(F32)
    g_mid = rem.astype(BF16)
    g_lo = (rem - g_mid.astype(F32)).astype(BF16)
    g_parts = jnp.concatenate([g_hi, g_mid, g_lo], axis=1)

    rr = lax.broadcasted_iota(jnp.int32, (CHUNK, CHUNK), 0)
    cc = lax.broadcasted_iota(jnp.int32, (CHUNK, CHUNK), 1)
    csum_mat = jnp.where((cc >= rr) if reverse else (cc <= rr), 1.0, 0.0).astype(BF16)
    att_mask = (cc > rr) if reverse else (cc <= rr)
    edge = 0 if reverse else CHUNK - 1
    scale = GLA_DK ** -0.5

    decays = []
    for c in range(nch):
        rows = slice(c * CHUNK, (c + 1) * CHUNK)
        b3 = jnp.dot(csum_mat, g_parts[rows], preferred_element_type=F32)
        b = b3[:, :GLA_DK] + b3[:, GLA_DK:2 * GLA_DK] + b3[:, 2 * GLA_DK:]
        b_edge = b[edge:edge + 1]
        k = k_ref[rows, :]
        v = v_ref[rows, :].astype(BF16)
        q_in = (q_ref[rows, :] * scale * jnp.exp(b)).astype(BF16)
        k_in = (k * jnp.exp(-b)).astype(BF16)
        k_st = (k * jnp.exp(b_edge - b)).astype(BF16)
        att = jnp.where(att_mask, _dot_nt(q_in, k_in), 0.0).astype(BF16)
        o_ref[rows, :] = jnp.dot(att, v, preferred_element_type=F32)
        qin_ref[rows, :] = q_in
        kv_ref[c * GLA_DV:(c + 1) * GLA_DV, :] = _dot_tn(v, k_st)
        decays.append(jnp.exp(b_edge))

    s_t = s_ref[...]
    for c in (reversed(range(nch)) if reverse else range(nch)):
        rows = slice(c * CHUNK, (c + 1) * CHUNK)
        o_ref[rows, :] += _dot_nt(qin_ref[rows, :], s_t.astype(BF16))
        s_t = s_t * decays[c] + kv_ref[c * GLA_DV:(c + 1) * GLA_DV, :]
    s_ref[...] = s_t


def _gla_kernel(qf_ref, kf_ref, vf_ref, lf_ref, rf_ref, qb_ref, kb_ref, vb_ref, lb_ref, rb_ref,
                wg_ref, bg_ref, og_ref, o_ref, s_scr, qin_scr, kv_scr, ob_scr, half_scr, *, TB):
    s = pl.program_id(1)
    nb = pl.num_programs(1)

    @pl.when(s == 0)
    def _():
        s_scr[...] = jnp.zeros_like(s_scr)

    _gla_scan_block(qf_ref, kf_ref, vf_ref, lf_ref, wg_ref[0], bg_ref[0], s_scr.at[0], qin_scr.at[0],
                    kv_scr.at[0], ob_scr.at[0], TB=TB, reverse=False)
    _gla_scan_block(qb_ref, kb_ref, vb_ref, lb_ref, wg_ref[1], bg_ref[1], s_scr.at[1], qin_scr.at[1],
                    kv_scr.at[1], ob_scr.at[1], TB=TB, reverse=True)

    first_half = s < nb // 2
    for d, blk, r_ref in ((0, s, rf_ref), (1, nb - 1 - s, rb_ref)):
        rows = pl.ds(pl.multiple_of(blk * TB, TB), TB)

        @pl.when(first_half)
        def _():
            half_scr[rows, :] = ob_scr[d]

        @pl.when(jnp.logical_not(first_half))
        def _():
            tot = half_scr[rows, :] + ob_scr[d]
            tot = tot * lax.rsqrt(jnp.mean(tot * tot, axis=-1, keepdims=True) + EPS) * og_ref[...]
            o_ref[rows, :] = (tot * jax.nn.silu(r_ref[...])).astype(o_ref.dtype)


def _gla(p, w_gate, b_gate, out_g, *, L, col0, TB=512):
    TB = min(TB, L)
    nb = L // TB
    assert nb % 2 == 0, "both scan directions must meet between two blocks"
    nch = TB // CHUNK
    qb = col0 // GLA_DK
    kb = qb + GLA_HEADS
    vb = (col0 + 2 * GLA_KW) // GLA_DV
    rb = vb + GLA_HEADS
    lb = (col0 + 2 * GLA_KW + 2 * GLA_VW) // LANES
    fwd = lambda s: s
    bwd = lambda s: nb - 1 - s

    def operands(blk):
        return [pl.BlockSpec((TB, GLA_DK), lambda h, s: (blk(s), qb + h)),
                pl.BlockSpec((TB, GLA_DK), lambda h, s: (blk(s), kb + h)),
                pl.BlockSpec((TB, GLA_DV), lambda h, s: (blk(s), vb + h)),
                pl.BlockSpec((TB, LANES), lambda h, s: (blk(s), lb)),
                pl.BlockSpec((TB, GLA_DV), lambda h, s: (blk(s), rb + h))]

    return pl.pallas_call(
        functools.partial(_gla_kernel, TB=TB),
        grid=(GLA_HEADS, nb),
        in_specs=operands(fwd) + operands(bwd) + [
            pl.BlockSpec((2, LANES, GLA_DK), lambda h, s: (0, 0, h)),
            pl.BlockSpec((2, 1, GLA_DK), lambda h, s: (0, 0, h)),
            pl.BlockSpec((1, GLA_DV), lambda h, s: (0, h)),
        ],
        out_specs=pl.BlockSpec((L, GLA_DV), lambda h, s: (0, h)),
        out_shape=jax.ShapeDtypeStruct((L, GLA_VW), BF16),
        scratch_shapes=[pltpu.VMEM((2, GLA_DV, GLA_DK), F32),
                        pltpu.VMEM((2, TB, GLA_DK), BF16),
                        pltpu.VMEM((2, nch * GLA_DV, GLA_DK), F32),
                        pltpu.VMEM((2, TB, GLA_DV), F32),
                        pltpu.VMEM((L, GLA_DV), F32)],
        compiler_params=_cparams(("arbitrary", "arbitrary")),
        name="gla",
    )(*([p] * 10), w_gate, b_gate, out_g)


def _filter_features(L):
    t = jnp.linspace(0.0, 1.0, L, dtype=F32)[:, None]
    bands = (FILTER_EMB - 1) // 2
    freqs = jnp.linspace(1e-4, bands - 1, bands, dtype=F32)[None, :]
    ang = (2.0 * math.pi / L) * jnp.arange(L, dtype=F32)[:, None] * freqs
    feat = jnp.concatenate([t, jnp.cos(ang), -jnp.sin(ang)], axis=-1)
    return jnp.pad(feat, ((0, 0), (0, LANES - FILTER_EMB)))


def _filter_deltas():
    min_decay = math.log(DECAY_TARGET) / LONG_DECAY_PCT
    max_decay = math.log(DECAY_TARGET) / SHORT_DECAY_PCT
    return jnp.abs(jnp.linspace(min_decay, max_decay, HY_WIDTH, dtype=F32)).reshape(1, HY_WIDTH)


def _block_diag_w4(w4):
    H = w4.shape[0]
    nblk = HY_WIDTH // LANES
    wf = w4[:, :HY_WIDTH].reshape(H, nblk, LANES)
    wb = w4[:, HY_WIDTH:].reshape(H, nblk, LANES)
    zz = jnp.zeros_like(wf)
    top = jnp.concatenate([wf, zz], axis=2)
    bot = jnp.concatenate([zz, wb], axis=2)
    return jnp.concatenate([top, bot], axis=0).reshape(2 * H, nblk * 2 * LANES)


def kernel(x, ffn1_norm, ffn1_w_gate, ffn1_w_up, ffn1_w_down, mix_norm, w_in, hy_conv_w, hy_conv_b, flt_w1, flt_b1, flt_f1, flt_w2, flt_b2, flt_f2, flt_w3, flt_b3, flt_f3, flt_w4, hy_skip, hy_out_norm, gla_w_a2_f, gla_b_a_f, gla_w_a2_b, gla_b_a_b, gla_out_norm, w_out, ffn2_norm, ffn2_w_gate, ffn2_w_up, ffn2_w_down, final_norm):
    B, L, D = x.shape
    depth = ffn1_norm.shape[0]
    tabs = _dft_tables(L)
    feat = _filter_features(L)
    feat_rev = jnp.roll(feat[::-1], 1, axis=0)
    deltas = _filter_deltas()
    in_cols = w_in.shape[-1]
    in_pad = -in_cols % 512
    gla_col0 = 3 * HY_WIDTH
    outs = []
    for bi in range(B):
        xb = x[bi]
        for l in range(depth):
            last = l == depth - 1
            xb = _ffn(xb, ffn1_norm[l], ffn1_w_gate[l], ffn1_w_up[l], ffn1_w_down[l], final_norm, final_norm=False)
            w_pad = jnp.pad(w_in[l], ((0, 0), (0, in_pad))).astype(BF16)
            p = _in_proj(xb, mix_norm[l], w_pad)

            w1p = jnp.pad(flt_w1[l], ((0, LANES - FILTER_EMB), (0, 0)))
            hid2 = _filt_mlp(feat, feat_rev, w1p, flt_b1[l], flt_f1[l], flt_w2[l], flt_b2[l], flt_f2[l],
                             flt_w3[l], flt_b3[l], flt_f3[l])
            khat, knyq = _filt_fft(hid2, _block_diag_w4(flt_w4[l]), deltas, tabs["g_full"], tabs["f3"], L=L)
            y_hy = _hy_conv(p, hy_conv_w[l], hy_conv_b[l].reshape(1, -1), khat, knyq,
                            hy_skip[l].reshape(1, -1), hy_out_norm[l].reshape(1, -1), tabs, L=L)

            w_gate = jnp.zeros((2, LANES, GLA_KW), F32)
            w_gate = w_gate.at[0, :GATE_RANK].set(gla_w_a2_f[l]).at[1, GATE_RANK:2 * GATE_RANK].set(gla_w_a2_b[l])
            b_gate = jnp.stack([gla_b_a_f[l], gla_b_a_b[l]]).reshape(2, 1, GLA_KW)
            y_gla = _gla(p, w_gate, b_gate, gla_out_norm[l].reshape(1, -1), L=L, col0=gla_col0)

            xb = _out_proj(xb, y_hy, y_gla, w_out[l])
            xb = _ffn(xb, ffn2_norm[l], ffn2_w_gate[l], ffn2_w_up[l], ffn2_w_down[l], final_norm,
                      final_norm=last)
        outs.append(xb)
    return jnp.stack(outs)
```

```python
import functools
import math

import numpy as np
import jax
import jax.numpy as jnp
from jax import lax
from jax.experimental import pallas as pl
from jax.experimental.pallas import tpu as pltpu

F32 = jnp.float32
BF16 = jnp.bfloat16
HIGHEST = lax.Precision.HIGHEST

EPS = 1e-6
HY_WIDTH = 1024
HY_GROUPS = 8
FILTER_EMB = 33
FILTER_HIDDEN = 64
SHORT_DECAY_PCT = 0.3
LONG_DECAY_PCT = 1.5
DECAY_TARGET = 1e-2
GLA_HEADS = 4
GLA_DK = 128
GLA_DV = 256
GLA_KW = GLA_HEADS * GLA_DK
GLA_VW = GLA_HEADS * GLA_DV
GATE_RANK = 16
GATE_TEMP = 16.0
CHUNK = 64

LANES = 128
SUBLANES = 8
VMEM_LIMIT = 60 * 1024 * 1024
FFN_SLAB = 128

RB = 128
KB = RB // 2
PITCH = RB + SUBLANES


def _cparams(sem):
    return pltpu.CompilerParams(dimension_semantics=sem, vmem_limit_bytes=VMEM_LIMIT)


def _single(block_shape, index_map):
    return pl.BlockSpec(block_shape, index_map, pipeline_mode=pl.Buffered(1))


def _split_hi_lo(x):
    hi = x.astype(BF16)
    return hi, (x - hi.astype(F32)).astype(BF16)


def _three_pass_rows(w):
    w_hi, w_lo = _split_hi_lo(w)
    return jnp.concatenate([w_hi, w_lo, w_hi], axis=-2)


def _ffn_kernel(x_ref, g_ref, wg_ref, wu_ref, wd_ref, fg_ref, o_ref, h_scr, *, final_norm):
    j = pl.program_id(1)

    slab = min(FFN_SLAB, x_ref.shape[0])
    nslab = x_ref.shape[0] // slab

    @pl.when(j == 0)
    def _():
        def norm_rows(t, carry):
            rows = pl.ds(pl.multiple_of(t * slab, slab), slab)
            x = x_ref[rows, :]
            r = lax.rsqrt(jnp.mean(x * x, axis=-1, keepdims=True) + EPS)
            h_scr[rows, :] = (x * r * g_ref[...]).astype(BF16)
            o_ref[rows, :] = jnp.zeros((slab, x_ref.shape[1]), F32)
            return carry
        lax.fori_loop(0, nslab, norm_rows, 0)

    h = h_scr[...]
    gate = jnp.dot(h, wg_ref[...], preferred_element_type=F32)
    up = jnp.dot(h, wu_ref[...], preferred_element_type=F32)
    a = (jax.nn.silu(gate) * up).astype(BF16)
    o_ref[...] += jnp.dot(a, wd_ref[...], preferred_element_type=F32)

    @pl.when(j == pl.num_programs(1) - 1)
    def _():
        def finish_rows(t, carry):
            rows = pl.ds(pl.multiple_of(t * slab, slab), slab)
            y = x_ref[rows, :] + 0.5 * o_ref[rows, :]
            if final_norm:
                r = lax.rsqrt(jnp.mean(y * y, axis=-1, keepdims=True) + EPS)
                y = y * r * fg_ref[...]
            o_ref[rows, :] = y
            return carry
        lax.fori_loop(0, nslab, finish_rows, 0)


def _ffn(x, norm_g, w_gate, w_up, w_down, final_g, *, final_norm, tm=1024, tf=512):
    L, D = x.shape
    DF = w_gate.shape[1]
    tm = min(tm, L)
    tf = min(tf, DF)
    return pl.pallas_call(
        functools.partial(_ffn_kernel, final_norm=final_norm),
        grid=(L // tm, DF // tf),
        in_specs=[
            pl.BlockSpec((tm, D), lambda i, j: (i, 0)),
            pl.BlockSpec((1, D), lambda i, j: (0, 0)),
            pl.BlockSpec((D, tf), lambda i, j: (0, j)),
            pl.BlockSpec((D, tf), lambda i, j: (0, j)),
            pl.BlockSpec((tf, D), lambda i, j: (j, 0)),
            pl.BlockSpec((1, D), lambda i, j: (0, 0)),
        ],
        out_specs=pl.BlockSpec((tm, D), lambda i, j: (i, 0)),
        out_shape=jax.ShapeDtypeStruct((L, D), F32),
        scratch_shapes=[pltpu.VMEM((tm, D), BF16)],
        compiler_params=_cparams(("parallel", "arbitrary")),
        name="ffn",
    )(x, norm_g.reshape(1, D), w_gate.astype(BF16), w_up.astype(BF16), w_down.astype(BF16),
      final_g.reshape(1, D))


def _in_proj_kernel(x_ref, g_ref, w_ref, o_ref, h_scr, *, n_cols):
    j = pl.program_id(1)

    @pl.when(j == 0)
    def _():
        x = x_ref[...]
        r = lax.rsqrt(jnp.mean(x * x, axis=-1, keepdims=True) + EPS)
        h_scr[...] = (x * r * g_ref[...]).astype(BF16)

    o = jnp.dot(h_scr[...], w_ref[...].astype(BF16), preferred_element_type=F32)
    col = lax.broadcasted_iota(jnp.int32, o.shape, 1) + j * o.shape[1]
    o_ref[...] = jnp.where(col < n_cols, o, 0.0)


def _in_proj(x, norm_g, w, *, tm=1024, tn=512):
    L, D = x.shape
    n_cols = w.shape[1]
    nj = pl.cdiv(n_cols, tn)
    tm = min(tm, L)
    return pl.pallas_call(
        functools.partial(_in_proj_kernel, n_cols=n_cols),
        grid=(L // tm, nj),
        in_specs=[
            pl.BlockSpec((tm, D), lambda i, j: (i, 0)),
            pl.BlockSpec((1, D), lambda i, j: (0, 0)),
            pl.BlockSpec((D, tn), lambda i, j: (0, j)),
        ],
        out_specs=pl.BlockSpec((tm, tn), lambda i, j: (i, j)),
        out_shape=jax.ShapeDtypeStruct((L, nj * tn), F32),
        scratch_shapes=[pltpu.VMEM((tm, D), BF16)],
        compiler_params=_cparams(("parallel", "arbitrary")),
        name="in_proj",
    )(x, norm_g.reshape(1, D), w)


def _out_proj_kernel(x_ref, yh_ref, yg_ref, wh_ref, wg_ref, o_ref):
    o_ref[...] = (x_ref[...]
                  + jnp.dot(yh_ref[...], wh_ref[...], preferred_element_type=F32)
                  + jnp.dot(yg_ref[...], wg_ref[...], preferred_element_type=F32))


def _out_proj(x, y_hy, y_gla, w_out, *, tm=512):
    L, D = x.shape
    WH = y_hy.shape[1]
    WG = y_gla.shape[1]
    tm = min(tm, L)
    w = w_out.astype(BF16)
    return pl.pallas_call(
        _out_proj_kernel,
        grid=(L // tm,),
        in_specs=[
            pl.BlockSpec((tm, D), lambda i: (i, 0)),
            pl.BlockSpec((tm, WH), lambda i: (i, 0)),
            pl.BlockSpec((tm, WG), lambda i: (i, 0)),
            _single((WH, D), lambda i: (0, 0)),
            _single((WG, D), lambda i: (0, 0)),
        ],
        out_specs=pl.BlockSpec((tm, D), lambda i: (i, 0)),
        out_shape=jax.ShapeDtypeStruct((L, D), F32),
        compiler_params=_cparams(("parallel",)),
        name="out_proj",
    )(x, y_hy, y_gla, w[:WH], w[WH:])


def _filt_mlp_kernel(z_ref, w1_ref, b1_ref, f1_ref, w2_ref, b2_ref, f2_ref, w3_ref, b3_ref, f3_ref, o_ref):
    dot = functools.partial(jnp.dot, precision=HIGHEST, preferred_element_type=F32)
    hid = jnp.sin(f1_ref[...] * (dot(z_ref[...], w1_ref[...]) + b1_ref[...]))
    hid = jnp.sin(f2_ref[...] * (dot(hid, w2_ref[...]) + b2_ref[...]))
    hid = jnp.sin(f3_ref[...] * (dot(hid, w3_ref[...]) + b3_ref[...]))
    hi, lo = _split_hi_lo(hid)
    o_ref[...] = jnp.concatenate([hi, hi, lo], axis=1)


def _filt_mlp(feat2, w1, b1, f1, w2, b2, f2, w3, b3, f3, *, tm=1024):
    L, FP = feat2.shape
    H2 = w2.shape[0]
    tm = min(tm, L)
    full = lambda shp: pl.BlockSpec(shp, lambda i: (0, 0))
    return pl.pallas_call(
        _filt_mlp_kernel,
        grid=(L // tm,),
        in_specs=[pl.BlockSpec((tm, FP), lambda i: (i, 0)),
                  full((FP, H2)), full((1, H2)), full((1, H2)),
                  full((H2, H2)), full((1, H2)), full((1, H2)),
                  full((H2, H2)), full((1, H2)), full((1, H2))],
        out_specs=pl.BlockSpec((tm, 3 * H2), lambda i: (i, 0)),
        out_shape=jax.ShapeDtypeStruct((L, 3 * H2), BF16),
        compiler_params=_cparams(("parallel",)),
        name="filt_mlp",
    )(feat2, w1, b1, f1, w2, b2, f2, w3, b3, f3)


def _dft_tables(L):
    N = 2 * L
    RA = N // RB
    NH = RA // 2
    two_pi = 2.0 * np.pi
    k1 = np.arange(RA, dtype=np.int64)

    def step1_table(NR):
        g = np.zeros((RB // 2, 2 * RA, 2 * NR), np.float64)
        for s in range(2):
            n = RB * np.arange(NR, dtype=np.int64)[None, :] + (2 * np.arange(RB // 2, dtype=np.int64) + s)[:, None]
            ang = two_pi * ((k1[None, :, None] * n[:, None, :]) % N) / N
            g[:, :RA, s * NR:(s + 1) * NR] = np.cos(ang)
            g[:, RA:, s * NR:(s + 1) * NR] = -np.sin(ang)
        return g

    n = RB * np.arange(NH, dtype=np.int64)[None, :] + np.arange(RB, dtype=np.int64)[:, None]
    ang = two_pi * ((n[:, :, None] * k1[None, None, :]) % N) / N
    gi = np.concatenate([np.cos(ang), -np.sin(ang)], axis=2)
    phi = two_pi * ((np.arange(KB)[:, None] * np.arange(RB)[None, :]) % RB) / RB
    f3 = np.block([[np.cos(phi), np.sin(phi)], [-np.sin(phi), np.cos(phi)]])
    i1 = np.block([[np.cos(phi.T), -np.sin(phi.T)], [np.sin(phi.T), np.cos(phi.T)]])
    cast = lambda a: jnp.asarray(a.astype(np.float32)).astype(BF16)
    return dict(g_half=cast(step1_table(NH)), g_full=cast(step1_table(RA)), gi=cast(gi), f3=cast(f3), i1=cast(i1))


def _fwd_step1(u_ref, g_ref, ab_ref, *, RA, NR, CB, G=2):
    def body(t, carry):
        ms = [t * G + j for j in range(G)]
        rhs = []
        for m in ms:
            ua = u_ref[pl.ds(2 * m, NR, stride=PITCH), :]
            ub = u_ref[pl.ds(2 * m + 1, NR, stride=PITCH), :]
            zz = jnp.zeros_like(ua)
            rhs.append(jnp.concatenate([jnp.concatenate([ua, zz], axis=1),
                                        jnp.concatenate([zz, ub], axis=1)], axis=0).astype(BF16))
        res = [jnp.dot(g_ref[m], r, preferred_element_type=F32) for m, r in zip(ms, rhs)]
        for m, r in zip(ms, res):
            ab_ref[pl.ds(2 * m, RA, stride=PITCH), :] = r[:RA, :CB]
            ab_ref[pl.ds(RA * PITCH + 2 * m, RA, stride=PITCH), :] = r[RA:, :CB]
            ab_ref[pl.ds(2 * m + 1, RA, stride=PITCH), :] = r[:RA, CB:]
            ab_ref[pl.ds(RA * PITCH + 2 * m + 1, RA, stride=PITCH), :] = r[RA:, CB:]
        return carry
    lax.fori_loop(0, RB // 2 // G, body, 0)


def _fwd_step3_pair(ab_ref, f3, kp, *, RA):
    cols = []
    for s in range(2):
        k1 = 2 * kp + s
        ar = ab_ref[pl.ds(pl.multiple_of(k1 * PITCH, SUBLANES), RB), :]
        ai = ab_ref[pl.ds(pl.multiple_of((RA + k1) * PITCH, SUBLANES), RB), :]
        cols.append(jnp.concatenate([ar, ai], axis=0))
    rhs = jnp.concatenate(cols, axis=1).astype(BF16)
    return jnp.dot(f3, rhs, preferred_element_type=F32)


def _alt_sign(shape):
    rows = lax.broadcasted_iota(jnp.int32, shape, 0)
    return jnp.where(rows % 2 == 0, 1.0, -1.0).astype(F32)


def _filt_fft_kernel(hid_ref, w4_ref, dl_ref, g_ref, f3_ref, kh_ref, kn_ref, u_scr, ab_scr, *, L, CB, G):
    N = 2 * L
    RA = N // RB
    NH = RA // 2
    f3 = f3_ref[...]
    inv_lm1 = 1.0 / (L - 1)

    def fill(n1, carry):
        s_abs, s_alt = carry
        r0 = pl.multiple_of(n1 * RB, RB)
        h2 = jnp.dot(hid_ref[pl.ds(r0, RB), :], w4_ref[...], preferred_element_type=F32)
        pos = lax.broadcasted_iota(jnp.int32, (RB, CB), 0) + n1 * RB
        hf = h2[:, :CB] * jnp.exp(-(pos.astype(F32) * inv_lm1) * dl_ref[...])
        hb = h2[:, CB:] * jnp.exp(-((L - pos).astype(F32) * inv_lm1) * dl_ref[...])
        hb = jnp.where(pos == 0, 0.0, hb)
        u_scr[pl.ds(pl.multiple_of(n1 * PITCH, SUBLANES), RB), :] = hf
        u_scr[pl.ds(pl.multiple_of((NH + n1) * PITCH, SUBLANES), RB), :] = hb
        return s_abs + (jnp.abs(hf) + jnp.abs(hb)), s_alt + (hf + hb)
    zero = jnp.zeros((RB, CB), F32)
    s_abs, s_alt = lax.fori_loop(0, NH, fill, (zero, zero), unroll=2)
    inv_l1 = 1.0 / jnp.sum(s_abs, axis=0, keepdims=True)
    kn = jnp.sum(s_alt * _alt_sign((RB, CB)), axis=0, keepdims=True) * inv_l1 * (1.0 / N)
    kn_ref[...] = jnp.broadcast_to(kn, kn_ref.shape)

    _fwd_step1(u_scr, g_ref, ab_scr, RA=RA, NR=RA, CB=CB, G=G)

    scale = inv_l1 * (2.0 / N)
    scale2 = jnp.concatenate([scale, scale], axis=1)
    row2 = lax.broadcasted_iota(jnp.int32, (KB, 2 * CB), 0)
    lane2 = lax.broadcasted_iota(jnp.int32, (KB, 2 * CB), 1)
    dc_pos = jnp.logical_and(row2 == 0, lane2 < CB)

    def emit(t, carry):
        kps = [t * G + j for j in range(G)]
        xs = [_fwd_step3_pair(ab_scr, f3, kp, RA=RA) for kp in kps]
        for kp, x in zip(kps, xs):
            kr = x[:KB] * scale2
            ki = x[KB:] * scale2
            dc = jnp.logical_and(dc_pos, kp == 0)
            kr = jnp.where(dc, 0.5 * kr, kr)
            ki = jnp.where(dc, 0.5 * ki, ki)
            kh_ref[pl.ds(pl.multiple_of(kp * RB, RB), RB), :] = jnp.concatenate([kr, ki], axis=0).astype(kh_ref.dtype)
        return carry
    lax.fori_loop(0, RA // 2 // G, emit, 0)


def _filt_fft(hid2, w4bd, deltas, g_tab, f3_tab, *, L, CB=LANES, G=4):
    C = deltas.shape[1]
    H2 = hid2.shape[1]
    N = 2 * L
    RA = N // RB
    nblk = C // CB
    return pl.pallas_call(
        functools.partial(_filt_fft_kernel, L=L, CB=CB, G=G),
        grid=(nblk,),
        in_specs=[
            _single((L, H2), lambda c: (0, 0)),
            pl.BlockSpec((H2, 2 * CB), lambda c: (0, c)),
            pl.BlockSpec((1, CB), lambda c: (0, c)),
            _single(g_tab.shape, lambda c: (0, 0, 0)),
            _single(f3_tab.shape, lambda c: (0, 0)),
        ],
        out_specs=[pl.BlockSpec((RA // 2 * RB, 2 * CB), lambda c: (0, c)),
                   pl.BlockSpec((SUBLANES, CB), lambda c: (0, c))],
        out_shape=[jax.ShapeDtypeStruct((RA // 2 * RB, 2 * C), BF16),
                   jax.ShapeDtypeStruct((SUBLANES, C), F32)],
        scratch_shapes=[pltpu.VMEM((RA * PITCH, CB), F32),
                        pltpu.VMEM((2 * RA * PITCH, CB), F32)],
        compiler_params=_cparams(("arbitrary",)),
        name="filt_fft",
    )(hid2, w4bd, deltas, g_tab, f3_tab)


def _short_conv_rows(p_ref, w_ref, b_ref, n1, *, first, last):
    r0 = pl.multiple_of(n1 * RB, RB)
    cur = p_ref[pl.ds(r0, RB), :]
    rows = lax.broadcasted_iota(jnp.int32, cur.shape, 0)
    if first:
        up = jnp.where(rows == 0, 0.0, pltpu.roll(cur, 1, axis=0))
    else:
        up = p_ref[pl.ds(r0 - 1, RB), :]
    if last:
        dn = jnp.where(rows == RB - 1, 0.0, pltpu.roll(cur, RB - 1, axis=0))
    else:
        dn = p_ref[pl.ds(r0 + 1, RB), :]
    w = w_ref[...]
    return b_ref[...] + up * w[0:1] + cur * w[1:2] + dn * w[2:3]


def _for_row_blocks(nblocks, body, carry):
    assert nblocks >= 2
    carry = body(0, carry, True, False)
    inner = nblocks - 2
    carry = lax.fori_loop(1, nblocks - 1, lambda n1, c: body(n1, c, False, False), carry,
                          unroll=2 if inner % 2 == 0 and inner > 0 else 1)
    return body(nblocks - 1, carry, False, True)


def _hy_conv_kernel(p0_ref, p1_ref, pv_ref, w0_ref, w1_ref, wv_ref, b0_ref, b1_ref, bv_ref,
                    kh_ref, kn_ref, skip_ref, og_ref, g_ref, gi_ref, f3_ref, i1_ref,
                    o_ref, u_scr, ab_scr, *, L, CB, G):
    N = 2 * L
    RA = N // RB
    NH = RA // 2
    f3 = f3_ref[...]
    i1 = i1_ref[...]
    sign = _alt_sign((RB, CB))

    def gated(n1, first, last):
        return (_short_conv_rows(pv_ref, wv_ref, bv_ref, n1, first=first, last=last)
                * _short_conv_rows(p1_ref, w1_ref, b1_ref, n1, first=first, last=last))

    def fill(n1, s_alt, first, last):
        z = gated(n1, first, last)
        u_scr[pl.ds(pl.multiple_of(n1 * PITCH, SUBLANES), RB), :] = z
        return s_alt + z
    s_alt = _for_row_blocks(NH, fill, jnp.zeros((RB, CB), F32))
    z_nyq = jnp.sum(s_alt * sign, axis=0, keepdims=True)
    y_nyq = z_nyq * kn_ref[0:1, :]

    _fwd_step1(u_scr, g_ref, ab_scr, RA=RA, NR=NH, CB=CB, G=G)

    def mid(t, carry):
        kps = [t * G + j for j in range(G)]
        xs = [_fwd_step3_pair(ab_scr, f3, kp, RA=RA) for kp in kps]
        khs = [kh_ref[pl.ds(pl.multiple_of(kp * RB, RB), RB), :].astype(F32) for kp in kps]
        bs = []
        for x, kh in zip(xs, khs):
            xr, xi, kr, ki = x[:KB], x[KB:], kh[:KB], kh[KB:]
            y = jnp.concatenate([xr * kr - xi * ki, xr * ki + xi * kr], axis=0).astype(BF16)
            bs.append(jnp.dot(i1, y, preferred_element_type=F32))
        for kp, b in zip(kps, bs):
            for s in range(2):
                k1 = 2 * kp + s
                ab_scr[pl.ds(pl.multiple_of(k1 * PITCH, SUBLANES), RB), :] = b[:RB, s * CB:(s + 1) * CB]
                ab_scr[pl.ds(pl.multiple_of((RA + k1) * PITCH, SUBLANES), RB), :] = b[RB:, s * CB:(s + 1) * CB]
        return carry
    lax.fori_loop(0, RA // 2 // G, mid, 0)

    GL = 2 * G

    def last(t, carry):
        n2s = [t * GL + j for j in range(GL)]
        rhs = [jnp.concatenate([ab_scr[pl.ds(n2, RA, stride=PITCH), :],
                                ab_scr[pl.ds(RA * PITCH + n2, RA, stride=PITCH), :]], axis=0).astype(BF16)
               for n2 in n2s]
        ys = [jnp.dot(gi_ref[n2], r, preferred_element_type=F32) for n2, r in zip(n2s, rhs)]
        for n2, y in zip(n2s, ys):
            u_scr[pl.ds(n2, NH, stride=PITCH), :] = y
        return carry
    lax.fori_loop(0, RB // GL, last, 0)

    def finish(n1, carry, first, last):
        z = gated(n1, first, last)
        x0 = _short_conv_rows(p0_ref, w0_ref, b0_ref, n1, first=first, last=last)
        conv = u_scr[pl.ds(pl.multiple_of(n1 * PITCH, SUBLANES), RB), :] + sign * y_nyq
        y = (conv + z * skip_ref[...]) * x0
        y = y * lax.rsqrt(jnp.mean(y * y, axis=-1, keepdims=True) + EPS) * og_ref[...]
        o_ref[pl.ds(pl.multiple_of(n1 * RB, RB), RB), :] = y.astype(o_ref.dtype)
        return carry
    _for_row_blocks(NH, finish, 0)


def _hy_conv(p, conv_w, conv_b, khat, knyq, skip, out_g, tabs, *, L, CB=LANES, G=4):
    C = skip.shape[1]
    assert CB == C // HY_GROUPS, "one channel block must be exactly one norm group"
    N = 2 * L
    RA = N // RB
    NH = RA // 2
    nblk = C // CB
    g_tab, gi_tab, f3_tab, i1_tab = tabs["g_half"], tabs["gi"], tabs["f3"], tabs["i1"]
    col = lambda off: (lambda c: (0, off * nblk + c))
    pspec = lambda off: _single((L, CB), col(off))
    wspec = lambda off: pl.BlockSpec((3, CB), col(off))
    bspec = lambda off: pl.BlockSpec((1, CB), col(off))
    return pl.pallas_call(
        functools.partial(_hy_conv_kernel, L=L, CB=CB, G=G),
        grid=(nblk,),
        in_specs=[pspec(0), pspec(1), pspec(2), wspec(0), wspec(1), wspec(2), bspec(0), bspec(1), bspec(2),
                  _single((RA // 2 * RB, 2 * CB), col(0)),
                  pl.BlockSpec((SUBLANES, CB), col(0)),
                  bspec(0), bspec(0),
                  _single(g_tab.shape, lambda c: (0, 0, 0)),
                  _single(gi_tab.shape, lambda c: (0, 0, 0)),
                  _single(f3_tab.shape, lambda c: (0, 0)),
                  _single(i1_tab.shape, lambda c: (0, 0))],
        out_specs=pl.BlockSpec((L, CB), col(0)),
        out_shape=jax.ShapeDtypeStruct((L, C), BF16),
        scratch_shapes=[pltpu.VMEM((NH * PITCH, CB), F32),
                        pltpu.VMEM((2 * RA * PITCH, CB), F32)],
        compiler_params=_cparams(("arbitrary",)),
        name="hy_conv",
    )(p, p, p, conv_w, conv_w, conv_w, conv_b, conv_b, conv_b, khat, knyq, skip, out_g,
      g_tab, gi_tab, f3_tab, i1_tab)


def _log_sigmoid(x):
    return jnp.minimum(x, 0.0) - jnp.log1p(jnp.exp(-jnp.abs(x)))


def _dot_nt(a, b):
    return lax.dot_general(a, b, (((1,), (1,)), ((), ())), preferred_element_type=F32)


def _dot_tn(a, b):
    return lax.dot_general(a, b, (((0,), (0,)), ((), ())), preferred_element_type=F32)


def _gla_scan_blocks(dirs, *, TB):
    nch = TB // CHUNK
    rows = lambda c: slice(c * CHUNK, (c + 1) * CHUNK)
    units = [(d, c) for c in range(nch) for d in dirs]
    rr = lax.broadcasted_iota(jnp.int32, (CHUNK, CHUNK), 0)
    cc = lax.broadcasted_iota(jnp.int32, (CHUNK, CHUNK), 1)
    scale = GLA_DK ** -0.5

    for d in dirs:
        lr_hi, lr_lo = _split_hi_lo(d["lr"][...])
        gate_in = jnp.dot(jnp.concatenate([lr_hi, lr_hi, lr_lo], axis=1), d["wg"],
                          preferred_element_type=F32) + d["bg"]
        g = _log_sigmoid(gate_in) * (1.0 / GATE_TEMP)
        g_hi = g.astype(BF16)
        g_mid, g_lo = _split_hi_lo(g - g_hi.astype(F32))
        d["parts"][...] = jnp.concatenate([g_hi, g_mid, g_lo], axis=1)
        d["csum"] = jnp.where((cc >= rr) if d["reverse"] else (cc <= rr), 1.0, 0.0).astype(BF16)
        d["mask"] = (cc > rr) if d["reverse"] else (cc <= rr)
        d["edge"] = 0 if d["reverse"] else CHUNK - 1

    for d, c in units:
        b3 = jnp.dot(d["csum"], d["parts"][rows(c), :], preferred_element_type=F32)
        d["b"][rows(c), :] = b3[:, :GLA_DK] + b3[:, GLA_DK:2 * GLA_DK] + b3[:, 2 * GLA_DK:]

    decay = {}
    for d, c in units:
        b = d["b"][rows(c), :]
        b_edge = b[d["edge"]:d["edge"] + 1]
        k = d["k"][rows(c), :]
        d["qks"][0, rows(c), :] = (d["q"][rows(c), :] * scale * jnp.exp(b)).astype(BF16)
        d["qks"][1, rows(c), :] = (k * jnp.exp(-b)).astype(BF16)
        d["qks"][2, rows(c), :] = (k * jnp.exp(b_edge - b)).astype(BF16)
        d["vb"][rows(c), :] = d["v"][rows(c), :].astype(BF16)
        decay[(id(d), c)] = jnp.exp(b_edge)

    for d, c in units:
        att = _dot_nt(d["qks"][0, rows(c), :], d["qks"][1, rows(c), :])
        d["att"][rows(c), :] = jnp.where(d["mask"], att, 0.0).astype(BF16)

    for d, c in units:
        d["o"][rows(c), :] = jnp.dot(d["att"][rows(c), :], d["vb"][rows(c), :], preferred_element_type=F32)

    for d, c in units:
        d["kv"][c * GLA_DV:(c + 1) * GLA_DV, :] = _dot_tn(d["vb"][rows(c), :], d["qks"][2, rows(c), :])

    state = {id(d): d["s"][...] for d in dirs}
    for step in range(nch):
        for d in dirs:
            c = nch - 1 - step if d["reverse"] else step
            s_t = state[id(d)]
            d["o"][rows(c), :] += _dot_nt(d["qks"][0, rows(c), :], s_t.astype(BF16))
            state[id(d)] = s_t * decay[(id(d), c)] + d["kv"][c * GLA_DV:(c + 1) * GLA_DV, :]
    for d in dirs:
        d["s"][...] = state[id(d)]


def _gla_kernel(qf_ref, kf_ref, vf_ref, lf_ref, rf_ref, qb_ref, kb_ref, vb_ref, lb_ref, rb_ref,
                wg_ref, bg_ref, og_ref, o_ref,
                s_scr, parts_scr, b_scr, qks_scr, vb_scr, att_scr, kv_scr, ob_scr, half_scr, *, TB):
    s = pl.program_id(1)
    nb = pl.num_programs(1)

    @pl.when(s == 0)
    def _():
        s_scr[...] = jnp.zeros_like(s_scr)

    def direction(i, q, k, v, lr):
        return dict(q=q, k=k, v=v, lr=lr, wg=wg_ref[i], bg=bg_ref[i], reverse=bool(i), s=s_scr.at[i],
                    parts=parts_scr.at[i], b=b_scr.at[i], qks=qks_scr.at[i], vb=vb_scr.at[i],
                    att=att_scr.at[i], kv=kv_scr.at[i], o=ob_scr.at[i])
    _gla_scan_blocks([direction(0, qf_ref, kf_ref, vf_ref, lf_ref),
                      direction(1, qb_ref, kb_ref, vb_ref, lb_ref)], TB=TB)

    first_half = s < nb // 2
    for d, blk, r_ref in ((0, s, rf_ref), (1, nb - 1 - s, rb_ref)):
        rows = pl.ds(pl.multiple_of(blk * TB, TB), TB)

        @pl.when(first_half)
        def _():
            half_scr[rows, :] = ob_scr[d]

        @pl.when(jnp.logical_not(first_half))
        def _():
            tot = half_scr[rows, :] + ob_scr[d]
            tot = tot * lax.rsqrt(jnp.mean(tot * tot, axis=-1, keepdims=True) + EPS) * og_ref[...]
            o_ref[rows, :] = (tot * jax.nn.silu(r_ref[...])).astype(o_ref.dtype)


def _gla(p, w_gate, b_gate, out_g, *, L, col0, TB=512):
    TB = min(TB, L)
    nb = L // TB
    assert nb % 2 == 0, "both scan directions must meet between two blocks"
    nch = TB // CHUNK
    qb = col0 // GLA_DK
    kb = qb + GLA_HEADS
    vb = (col0 + 2 * GLA_KW) // GLA_DV
    rb = vb + GLA_HEADS
    lb = (col0 + 2 * GLA_KW + 2 * GLA_VW) // LANES
    fwd = lambda s: s
    bwd = lambda s: nb - 1 - s

    def operands(blk):
        return [pl.BlockSpec((TB, GLA_DK), lambda h, s: (blk(s), qb + h)),
                pl.BlockSpec((TB, GLA_DK), lambda h, s: (blk(s), kb + h)),
                pl.BlockSpec((TB, GLA_DV), lambda h, s: (blk(s), vb + h)),
                pl.BlockSpec((TB, LANES), lambda h, s: (blk(s), lb)),
                pl.BlockSpec((TB, GLA_DV), lambda h, s: (blk(s), rb + h))]

    return pl.pallas_call(
        functools.partial(_gla_kernel, TB=TB),
        grid=(GLA_HEADS, nb),
        in_specs=operands(fwd) + operands(bwd) + [
            pl.BlockSpec((2, 3 * LANES, GLA_DK), lambda h, s: (0, 0, h)),
            pl.BlockSpec((2, 1, GLA_DK), lambda h, s: (0, 0, h)),
            pl.BlockSpec((1, GLA_DV), lambda h, s: (0, h)),
        ],
        out_specs=pl.BlockSpec((L, GLA_DV), lambda h, s: (0, h)),
        out_shape=jax.ShapeDtypeStruct((L, GLA_VW), BF16),
        scratch_shapes=[pltpu.VMEM((2, GLA_DV, GLA_DK), F32),
                        pltpu.VMEM((2, TB, 3 * GLA_DK), BF16),
                        pltpu.VMEM((2, TB, GLA_DK), F32),
                        pltpu.VMEM((2, 3, TB, GLA_DK), BF16),
                        pltpu.VMEM((2, TB, GLA_DV), BF16),
                        pltpu.VMEM((2, TB, CHUNK), BF16),
                        pltpu.VMEM((2, nch * GLA_DV, GLA_DK), F32),
                        pltpu.VMEM((2, TB, GLA_DV), F32),
                        pltpu.VMEM((L, GLA_DV), F32)],
        compiler_params=_cparams(("arbitrary", "arbitrary")),
        name="gla",
    )(*([p] * 10), w_gate, b_gate, out_g)


def _filter_features(L):
    t = np.linspace(0.0, 1.0, L)[:, None]
    bands = (FILTER_EMB - 1) // 2
    freqs = np.linspace(1e-4, bands - 1, bands)[None, :]
    ang = (2.0 * np.pi / L) * np.arange(L)[:, None] * freqs
    feat = np.concatenate([t, np.cos(ang), -np.sin(ang)], axis=-1)
    feat_rev = np.roll(feat[::-1], 1, axis=0)
    both = np.zeros((L, LANES), np.float32)
    both[:, :FILTER_EMB] = feat
    both[:, FILTER_EMB:2 * FILTER_EMB] = feat_rev
    return jnp.asarray(both)


def _twice(w, rows_out):
    r, c = w.shape
    out = jnp.zeros((rows_out, 2 * c), F32)
    return out.at[:r, :c].set(w).at[r:2 * r, c:].set(w)


def _filter_deltas():
    min_decay = math.log(DECAY_TARGET) / LONG_DECAY_PCT
    max_decay = math.log(DECAY_TARGET) / SHORT_DECAY_PCT
    return jnp.abs(jnp.linspace(min_decay, max_decay, HY_WIDTH, dtype=F32)).reshape(1, HY_WIDTH)


def _block_diag_w4(w4):
    H = w4.shape[0]
    nblk = HY_WIDTH // LANES
    wf = w4[:, :HY_WIDTH].reshape(H, nblk, LANES)
    wb = w4[:, HY_WIDTH:].reshape(H, nblk, LANES)
    zz = jnp.zeros_like(wf)
    top = jnp.concatenate([wf, zz], axis=2)
    bot = jnp.concatenate([zz, wb], axis=2)
    return jnp.concatenate([top, bot], axis=0).reshape(2 * H, nblk * 2 * LANES)


def kernel(x, ffn1_norm, ffn1_w_gate, ffn1_w_up, ffn1_w_down, mix_norm, w_in, hy_conv_w, hy_conv_b, flt_w1, flt_b1, flt_f1, flt_w2, flt_b2, flt_f2, flt_w3, flt_b3, flt_f3, flt_w4, hy_skip, hy_out_norm, gla_w_a2_f, gla_b_a_f, gla_w_a2_b, gla_b_a_b, gla_out_norm, w_out, ffn2_norm, ffn2_w_gate, ffn2_w_up, ffn2_w_down, final_norm):
    B, L, D = x.shape
    depth = ffn1_norm.shape[0]
    tabs = _dft_tables(L)
    feat2 = _filter_features(L)
    deltas = _filter_deltas()
    gla_col0 = 3 * HY_WIDTH
    outs = []
    for bi in range(B):
        xb = x[bi]
        for l in range(depth):
            last = l == depth - 1
            xb = _ffn(xb, ffn1_norm[l], ffn1_w_gate[l], ffn1_w_up[l], ffn1_w_down[l], final_norm, final_norm=False)
            p = _in_proj(xb, mix_norm[l], w_in[l])

            H2 = 2 * FILTER_HIDDEN
            both = lambda a: jnp.concatenate([a, a]).reshape(1, H2)
            hid3 = _filt_mlp(feat2, _twice(flt_w1[l], LANES), both(flt_b1[l]), both(flt_f1[l]),
                             _twice(flt_w2[l], H2), both(flt_b2[l]), both(flt_f2[l]),
                             _twice(flt_w3[l], H2), both(flt_b3[l]), both(flt_f3[l]))
            w4cat = _three_pass_rows(_block_diag_w4(flt_w4[l]))
            khat, knyq = _filt_fft(hid3, w4cat, deltas, tabs["g_full"], tabs["f3"], L=L)
            y_hy = _hy_conv(p, hy_conv_w[l], hy_conv_b[l].reshape(1, -1), khat, knyq,
                            hy_skip[l].reshape(1, -1), hy_out_norm[l].reshape(1, -1), tabs, L=L)

            w_gate = jnp.zeros((2, LANES, GLA_KW), F32)
            w_gate = w_gate.at[0, :GATE_RANK].set(gla_w_a2_f[l]).at[1, GATE_RANK:2 * GATE_RANK].set(gla_w_a2_b[l])
            w_gate = _three_pass_rows(w_gate)
            b_gate = jnp.stack([gla_b_a_f[l], gla_b_a_b[l]]).reshape(2, 1, GLA_KW)
            y_gla = _gla(p, w_gate, b_gate, gla_out_norm[l].reshape(1, -1), L=L, col0=gla_col0)

            xb = _out_proj(xb, y_hy, y_gla, w_out[l])
            xb = _ffn(xb, ffn2_norm[l], ffn2_w_gate[l], ffn2_w_up[l], ffn2_w_down[l], final_norm,
                      final_norm=last)
        outs.append(xb)
    return jnp.stack(outs)
```

```python
import functools
import math

import numpy as np
import jax
import jax.numpy as jnp
from jax import lax
from jax.experimental import pallas as pl
from jax.experimental.pallas import tpu as pltpu

F32 = jnp.float32
BF16 = jnp.bfloat16
HIGHEST = lax.Precision.HIGHEST

EPS = 1e-6
HY_WIDTH = 1024
HY_GROUPS = 8
FILTER_EMB = 33
FILTER_HIDDEN = 64
SHORT_DECAY_PCT = 0.3
LONG_DECAY_PCT = 1.5
DECAY_TARGET = 1e-2
GLA_HEADS = 4
GLA_DK = 128
GLA_DV = 256
GLA_KW = GLA_HEADS * GLA_DK
GLA_VW = GLA_HEADS * GLA_DV
GATE_RANK = 16
GATE_TEMP = 16.0
CHUNK = 64

LANES = 128
SUBLANES = 8
VMEM_LIMIT = 60 * 1024 * 1024
FFN_SLAB = 128

RB = 128
KB = RB // 2
PITCH = RB + SUBLANES


def _cparams(sem):
    return pltpu.CompilerParams(dimension_semantics=sem, vmem_limit_bytes=VMEM_LIMIT)


def _single(block_shape, index_map):
    return pl.BlockSpec(block_shape, index_map, pipeline_mode=pl.Buffered(1))


def _split_hi_lo(x):
    hi = x.astype(BF16)
    return hi, (x - hi.astype(F32)).astype(BF16)


def _three_pass_rows(w):
    w_hi, w_lo = _split_hi_lo(w)
    return jnp.concatenate([w_hi, w_lo, w_hi], axis=-2)


def _dot_nt(a, b):
    return lax.dot_general(a, b, (((1,), (1,)), ((), ())), preferred_element_type=F32)


def _dot_tn(a, b):
    return lax.dot_general(a, b, (((0,), (0,)), ((), ())), preferred_element_type=F32)


def _ffn_kernel(x_ref, g_ref, wg_ref, wu_ref, wd_ref, fg_ref, o_ref, h_scr, *, final_norm):
    j = pl.program_id(1)

    slab = min(FFN_SLAB, x_ref.shape[0])
    nslab = x_ref.shape[0] // slab

    @pl.when(j == 0)
    def _():
        def norm_rows(t, carry):
            rows = pl.ds(pl.multiple_of(t * slab, slab), slab)
            x = x_ref[rows, :]
            r = lax.rsqrt(jnp.mean(x * x, axis=-1, keepdims=True) + EPS)
            h_scr[rows, :] = (x * r * g_ref[...]).astype(BF16)
            o_ref[rows, :] = jnp.zeros((slab, x_ref.shape[1]), F32)
            return carry
        lax.fori_loop(0, nslab, norm_rows, 0)

    h = h_scr[...]
    gate = jnp.dot(h, wg_ref[...], preferred_element_type=F32)
    up = jnp.dot(h, wu_ref[...], preferred_element_type=F32)
    a = (jax.nn.silu(gate) * up).astype(BF16)
    o_ref[...] += jnp.dot(a, wd_ref[...], preferred_element_type=F32)

    @pl.when(j == pl.num_programs(1) - 1)
    def _():
        def finish_rows(t, carry):
            rows = pl.ds(pl.multiple_of(t * slab, slab), slab)
            y = x_ref[rows, :] + 0.5 * o_ref[rows, :]
            if final_norm:
                r = lax.rsqrt(jnp.mean(y * y, axis=-1, keepdims=True) + EPS)
                y = y * r * fg_ref[...]
            o_ref[rows, :] = y
            return carry
        lax.fori_loop(0, nslab, finish_rows, 0)


def _ffn(x, norm_g, w_gate, w_up, w_down, final_g, *, final_norm, tm=1024, tf=512):
    L, D = x.shape
    DF = w_gate.shape[1]
    tm = min(tm, L)
    tf = min(tf, DF)
    return pl.pallas_call(
        functools.partial(_ffn_kernel, final_norm=final_norm),
        grid=(L // tm, DF // tf),
        in_specs=[
            pl.BlockSpec((tm, D), lambda i, j: (i, 0)),
            pl.BlockSpec((1, D), lambda i, j: (0, 0)),
            pl.BlockSpec((D, tf), lambda i, j: (0, j)),
            pl.BlockSpec((D, tf), lambda i, j: (0, j)),
            pl.BlockSpec((tf, D), lambda i, j: (j, 0)),
            pl.BlockSpec((1, D), lambda i, j: (0, 0)),
        ],
        out_specs=pl.BlockSpec((tm, D), lambda i, j: (i, 0)),
        out_shape=jax.ShapeDtypeStruct((L, D), F32),
        scratch_shapes=[pltpu.VMEM((tm, D), BF16)],
        compiler_params=_cparams(("parallel", "arbitrary")),
        name="ffn",
    )(x, norm_g.reshape(1, D), w_gate.astype(BF16), w_up.astype(BF16), w_down.astype(BF16),
      final_g.reshape(1, D))


def _in_proj_kernel(x_ref, g_ref, wt_ref, wtail_ref, o_ref, otail_ref, h_scr, *, n_tail):
    j = pl.program_id(1)

    @pl.when(j == 0)
    def _():
        x = x_ref[...]
        r = lax.rsqrt(jnp.mean(x * x, axis=-1, keepdims=True) + EPS)
        h = (x * r * g_ref[...]).astype(BF16)
        h_scr[...] = h
        tail = _dot_nt(h, wtail_ref[...].astype(BF16))
        col = lax.broadcasted_iota(jnp.int32, tail.shape, 1)
        otail_ref[...] = jnp.where(col < n_tail, tail, 0.0)

    o_ref[...] = _dot_nt(h_scr[...], wt_ref[...].astype(BF16))


def _in_proj(x, norm_g, w_t, *, tm=1024, tn=512):
    L, D = x.shape
    n_cols = w_t.shape[0]
    nj = n_cols // tn
    n_tail = n_cols - nj * tn
    assert 0 < n_tail <= LANES and (nj * tn) % LANES == 0
    tm = min(tm, L)
    return pl.pallas_call(
        functools.partial(_in_proj_kernel, n_tail=n_tail),
        grid=(L // tm, nj),
        in_specs=[
            pl.BlockSpec((tm, D), lambda i, j: (i, 0)),
            pl.BlockSpec((1, D), lambda i, j: (0, 0)),
            pl.BlockSpec((tn, D), lambda i, j: (j, 0)),
            pl.BlockSpec((LANES, D), lambda i, j: (nj * tn // LANES, 0)),
        ],
        out_specs=[pl.BlockSpec((tm, tn), lambda i, j: (i, j)),
                   pl.BlockSpec((tm, LANES), lambda i, j: (i, 0))],
        out_shape=[jax.ShapeDtypeStruct((L, nj * tn), F32),
                   jax.ShapeDtypeStruct((L, LANES), F32)],
        scratch_shapes=[pltpu.VMEM((tm, D), BF16)],
        compiler_params=_cparams(("parallel", "arbitrary")),
        name="in_proj",
    )(x, norm_g.reshape(1, D), w_t, w_t)


def _out_proj_kernel(x_ref, yh_ref, yg_ref, wh_ref, wg_ref, o_ref):
    o_ref[...] = (x_ref[...]
                  + jnp.dot(yh_ref[...], wh_ref[...], preferred_element_type=F32)
                  + jnp.dot(yg_ref[...], wg_ref[...], preferred_element_type=F32))


def _out_proj(x, y_hy, y_gla, w_out, *, tm=512):
    L, D = x.shape
    WH = y_hy.shape[1]
    WG = y_gla.shape[1]
    tm = min(tm, L)
    w = w_out.astype(BF16)
    return pl.pallas_call(
        _out_proj_kernel,
        grid=(L // tm,),
        in_specs=[
            pl.BlockSpec((tm, D), lambda i: (i, 0)),
            pl.BlockSpec((tm, WH), lambda i: (i, 0)),
            pl.BlockSpec((tm, WG), lambda i: (i, 0)),
            _single((WH, D), lambda i: (0, 0)),
            _single((WG, D), lambda i: (0, 0)),
        ],
        out_specs=pl.BlockSpec((tm, D), lambda i: (i, 0)),
        out_shape=jax.ShapeDtypeStruct((L, D), F32),
        compiler_params=_cparams(("parallel",)),
        name="out_proj",
    )(x, y_hy, y_gla, w[:WH], w[WH:])


def _filt_mlp_kernel(z_ref, w1_ref, b1_ref, f1_ref, w2_ref, b2_ref, f2_ref, w3_ref, b3_ref, f3_ref, o_ref):
    dot = functools.partial(jnp.dot, precision=HIGHEST, preferred_element_type=F32)
    hid = jnp.sin(f1_ref[...] * (dot(z_ref[...], w1_ref[...]) + b1_ref[...]))
    hid = jnp.sin(f2_ref[...] * (dot(hid, w2_ref[...]) + b2_ref[...]))
    hid = jnp.sin(f3_ref[...] * (dot(hid, w3_ref[...]) + b3_ref[...]))
    hi, lo = _split_hi_lo(hid)
    o_ref[...] = jnp.concatenate([hi, hi, lo], axis=1)


def _filt_mlp(feat2, w1, b1, f1, w2, b2, f2, w3, b3, f3, *, tm=1024):
    L, FP = feat2.shape
    H2 = w2.shape[0]
    tm = min(tm, L)
    full = lambda shp: pl.BlockSpec(shp, lambda i: (0, 0))
    return pl.pallas_call(
        _filt_mlp_kernel,
        grid=(L // tm,),
        in_specs=[pl.BlockSpec((tm, FP), lambda i: (i, 0)),
                  full((FP, H2)), full((1, H2)), full((1, H2)),
                  full((H2, H2)), full((1, H2)), full((1, H2)),
                  full((H2, H2)), full((1, H2)), full((1, H2))],
        out_specs=pl.BlockSpec((tm, 3 * H2), lambda i: (i, 0)),
        out_shape=jax.ShapeDtypeStruct((L, 3 * H2), BF16),
        compiler_params=_cparams(("parallel",)),
        name="filt_mlp",
    )(feat2, w1, b1, f1, w2, b2, f2, w3, b3, f3)


def _dft_tables(L):
    N = 2 * L
    RA = N // RB
    NH = RA // 2
    two_pi = 2.0 * np.pi
    k1 = np.arange(RA, dtype=np.int64)

    def step1_table(NR):
        g = np.zeros((RB // 2, 2 * RA, 2 * NR), np.float64)
        for s in range(2):
            n = RB * np.arange(NR, dtype=np.int64)[None, :] + (2 * np.arange(RB // 2, dtype=np.int64) + s)[:, None]
            ang = two_pi * ((k1[None, :, None] * n[:, None, :]) % N) / N
            g[:, :RA, s * NR:(s + 1) * NR] = np.cos(ang)
            g[:, RA:, s * NR:(s + 1) * NR] = -np.sin(ang)
        return g

    n = RB * np.arange(NH, dtype=np.int64)[None, :] + np.arange(RB, dtype=np.int64)[:, None]
    ang = two_pi * ((n[:, :, None] * k1[None, None, :]) % N) / N
    gi = np.concatenate([np.cos(ang), -np.sin(ang)], axis=2)
    phi = two_pi * ((np.arange(KB)[:, None] * np.arange(RB)[None, :]) % RB) / RB
    f3 = np.block([[np.cos(phi), np.sin(phi)], [-np.sin(phi), np.cos(phi)]])
    i1 = np.block([[np.cos(phi.T), -np.sin(phi.T)], [np.sin(phi.T), np.cos(phi.T)]])
    cast = lambda a: jnp.asarray(a.astype(np.float32)).astype(BF16)
    return dict(g_half=cast(step1_table(NH)), g_full=cast(step1_table(RA)), gi=cast(gi), f3=cast(f3), i1=cast(i1))


def _fwd_step1(u_ref, g_ref, ab_ref, *, RA, NR, CB, G=2):
    def body(t, carry):
        ms = [t * G + j for j in range(G)]
        rhs = []
        for m in ms:
            ua = u_ref[pl.ds(2 * m, NR, stride=PITCH), :]
            ub = u_ref[pl.ds(2 * m + 1, NR, stride=PITCH), :]
            zz = jnp.zeros_like(ua)
            rhs.append(jnp.concatenate([jnp.concatenate([ua, zz], axis=1),
                                        jnp.concatenate([zz, ub], axis=1)], axis=0).astype(BF16))
        res = [jnp.dot(g_ref[m], r, preferred_element_type=F32) for m, r in zip(ms, rhs)]
        for m, r in zip(ms, res):
            ab_ref[pl.ds(2 * m, RA, stride=PITCH), :] = r[:RA, :CB]
            ab_ref[pl.ds(RA * PITCH + 2 * m, RA, stride=PITCH), :] = r[RA:, :CB]
            ab_ref[pl.ds(2 * m + 1, RA, stride=PITCH), :] = r[:RA, CB:]
            ab_ref[pl.ds(RA * PITCH + 2 * m + 1, RA, stride=PITCH), :] = r[RA:, CB:]
        return carry
    lax.fori_loop(0, RB // 2 // G, body, 0)


def _fwd_step3_pair(ab_ref, f3, kp, *, RA):
    cols = []
    for s in range(2):
        k1 = 2 * kp + s
        ar = ab_ref[pl.ds(pl.multiple_of(k1 * PITCH, SUBLANES), RB), :]
        ai = ab_ref[pl.ds(pl.multiple_of((RA + k1) * PITCH, SUBLANES), RB), :]
        cols.append(jnp.concatenate([ar, ai], axis=0))
    rhs = jnp.concatenate(cols, axis=1).astype(BF16)
    return jnp.dot(f3, rhs, preferred_element_type=F32)


def _alt_sign(shape):
    rows = lax.broadcasted_iota(jnp.int32, shape, 0)
    return jnp.where(rows % 2 == 0, 1.0, -1.0).astype(F32)


def _filt_fft_kernel(hid_ref, w4_ref, dl_ref, g_ref, f3_ref, kh_ref, kn_ref, u_scr, ab_scr, *, L, CB, G):
    N = 2 * L
    RA = N // RB
    NH = RA // 2
    f3 = f3_ref[...]
    inv_lm1 = 1.0 / (L - 1)

    def fill(n1, carry):
        s_abs, s_alt = carry
        r0 = pl.multiple_of(n1 * RB, RB)
        h2 = jnp.dot(hid_ref[pl.ds(r0, RB), :], w4_ref[...], preferred_element_type=F32)
        pos = lax.broadcasted_iota(jnp.int32, (RB, CB), 0) + n1 * RB
        hf = h2[:, :CB] * jnp.exp(-(pos.astype(F32) * inv_lm1) * dl_ref[...])
        hb = h2[:, CB:] * jnp.exp(-((L - pos).astype(F32) * inv_lm1) * dl_ref[...])
        hb = jnp.where(pos == 0, 0.0, hb)
        u_scr[pl.ds(pl.multiple_of(n1 * PITCH, SUBLANES), RB), :] = hf
        u_scr[pl.ds(pl.multiple_of((NH + n1) * PITCH, SUBLANES), RB), :] = hb
        return s_abs + (jnp.abs(hf) + jnp.abs(hb)), s_alt + (hf + hb)
    zero = jnp.zeros((RB, CB), F32)
    s_abs, s_alt = lax.fori_loop(0, NH, fill, (zero, zero), unroll=2)
    inv_l1 = 1.0 / jnp.sum(s_abs, axis=0, keepdims=True)
    kn = jnp.sum(s_alt * _alt_sign((RB, CB)), axis=0, keepdims=True) * inv_l1 * (1.0 / N)
    kn_ref[...] = jnp.broadcast_to(kn, kn_ref.shape)

    _fwd_step1(u_scr, g_ref, ab_scr, RA=RA, NR=RA, CB=CB, G=G)

    scale = inv_l1 * (2.0 / N)
    scale2 = jnp.concatenate([scale, scale], axis=1)
    row2 = lax.broadcasted_iota(jnp.int32, (KB, 2 * CB), 0)
    lane2 = lax.broadcasted_iota(jnp.int32, (KB, 2 * CB), 1)
    dc_pos = jnp.logical_and(row2 == 0, lane2 < CB)

    def emit(t, carry):
        kps = [t * G + j for j in range(G)]
        xs = [_fwd_step3_pair(ab_scr, f3, kp, RA=RA) for kp in kps]
        for kp, x in zip(kps, xs):
            kr = x[:KB] * scale2
            ki = x[KB:] * scale2
            dc = jnp.logical_and(dc_pos, kp == 0)
            kr = jnp.where(dc, 0.5 * kr, kr)
            ki = jnp.where(dc, 0.5 * ki, ki)
            kh_ref[pl.ds(pl.multiple_of(kp * RB, RB), RB), :] = jnp.concatenate([kr, ki], axis=0).astype(kh_ref.dtype)
        return carry
    lax.fori_loop(0, RA // 2 // G, emit, 0)


def _filt_fft(hid2, w4bd, deltas, g_tab, f3_tab, *, L, CB=LANES, G=4):
    C = deltas.shape[1]
    H2 = hid2.shape[1]
    N = 2 * L
    RA = N // RB
    nblk = C // CB
    return pl.pallas_call(
        functools.partial(_filt_fft_kernel, L=L, CB=CB, G=G),
        grid=(nblk,),
        in_specs=[
            _single((L, H2), lambda c: (0, 0)),
            pl.BlockSpec((H2, 2 * CB), lambda c: (0, c)),
            pl.BlockSpec((1, CB), lambda c: (0, c)),
            _single(g_tab.shape, lambda c: (0, 0, 0)),
            _single(f3_tab.shape, lambda c: (0, 0)),
        ],
        out_specs=[pl.BlockSpec((RA // 2 * RB, 2 * CB), lambda c: (0, c)),
                   pl.BlockSpec((SUBLANES, CB), lambda c: (0, c))],
        out_shape=[jax.ShapeDtypeStruct((RA // 2 * RB, 2 * C), BF16),
                   jax.ShapeDtypeStruct((SUBLANES, C), F32)],
        scratch_shapes=[pltpu.VMEM((RA * PITCH, CB), F32),
                        pltpu.VMEM((2 * RA * PITCH, CB), F32)],
        compiler_params=_cparams(("arbitrary",)),
        name="filt_fft",
    )(hid2, w4bd, deltas, g_tab, f3_tab)


def _short_conv_rows(p_ref, w_ref, b_ref, n1, *, first, last):
    r0 = pl.multiple_of(n1 * RB, RB)
    cur = p_ref[pl.ds(r0, RB), :]
    rows = lax.broadcasted_iota(jnp.int32, cur.shape, 0)
    if first:
        up = jnp.where(rows == 0, 0.0, pltpu.roll(cur, 1, axis=0))
    else:
        up = p_ref[pl.ds(r0 - 1, RB), :]
    if last:
        dn = jnp.where(rows == RB - 1, 0.0, pltpu.roll(cur, RB - 1, axis=0))
    else:
        dn = p_ref[pl.ds(r0 + 1, RB), :]
    w = w_ref[...]
    return b_ref[...] + up * w[0:1] + cur * w[1:2] + dn * w[2:3]


def _for_row_blocks(nblocks, body, carry):
    assert nblocks >= 2
    carry = body(0, carry, True, False)
    inner = nblocks - 2
    carry = lax.fori_loop(1, nblocks - 1, lambda n1, c: body(n1, c, False, False), carry,
                          unroll=2 if inner % 2 == 0 and inner > 0 else 1)
    return body(nblocks - 1, carry, False, True)


def _hy_conv_kernel(p0_ref, p1_ref, pv_ref, w0_ref, w1_ref, wv_ref, b0_ref, b1_ref, bv_ref,
                    kh_ref, kn_ref, skip_ref, og_ref, g_ref, gi_ref, f3_ref, i1_ref,
                    o_ref, u_scr, ab_scr, *, L, CB, G):
    N = 2 * L
    RA = N // RB
    NH = RA // 2
    f3 = f3_ref[...]
    i1 = i1_ref[...]
    sign = _alt_sign((RB, CB))

    def gated(n1, first, last):
        return (_short_conv_rows(pv_ref, wv_ref, bv_ref, n1, first=first, last=last)
                * _short_conv_rows(p1_ref, w1_ref, b1_ref, n1, first=first, last=last))

    def fill(n1, s_alt, first, last):
        z = gated(n1, first, last)
        u_scr[pl.ds(pl.multiple_of(n1 * PITCH, SUBLANES), RB), :] = z
        return s_alt + z
    s_alt = _for_row_blocks(NH, fill, jnp.zeros((RB, CB), F32))
    z_nyq = jnp.sum(s_alt * sign, axis=0, keepdims=True)
    y_nyq = z_nyq * kn_ref[0:1, :]

    _fwd_step1(u_scr, g_ref, ab_scr, RA=RA, NR=NH, CB=CB, G=G)

    def mid(t, carry):
        kps = [t * G + j for j in range(G)]
        xs = [_fwd_step3_pair(ab_scr, f3, kp, RA=RA) for kp in kps]
        khs = [kh_ref[pl.ds(pl.multiple_of(kp * RB, RB), RB), :].astype(F32) for kp in kps]
        bs = []
        for x, kh in zip(xs, khs):
            xr, xi, kr, ki = x[:KB], x[KB:], kh[:KB], kh[KB:]
            y = jnp.concatenate([xr * kr - xi * ki, xr * ki + xi * kr], axis=0).astype(BF16)
            bs.append(jnp.dot(i1, y, preferred_element_type=F32))
        for kp, b in zip(kps, bs):
            for s in range(2):
                k1 = 2 * kp + s
                ab_scr[pl.ds(pl.multiple_of(k1 * PITCH, SUBLANES), RB), :] = b[:RB, s * CB:(s + 1) * CB]
                ab_scr[pl.ds(pl.multiple_of((RA + k1) * PITCH, SUBLANES), RB), :] = b[RB:, s * CB:(s + 1) * CB]
        return carry
    lax.fori_loop(0, RA // 2 // G, mid, 0)

    GL = 2 * G

    def last(t, carry):
        n2s = [t * GL + j for j in range(GL)]
        rhs = [jnp.concatenate([ab_scr[pl.ds(n2, RA, stride=PITCH), :],
                                ab_scr[pl.ds(RA * PITCH + n2, RA, stride=PITCH), :]], axis=0).astype(BF16)
               for n2 in n2s]
        ys = [jnp.dot(gi_ref[n2], r, preferred_element_type=F32) for n2, r in zip(n2s, rhs)]
        for n2, y in zip(n2s, ys):
            u_scr[pl.ds(n2, NH, stride=PITCH), :] = y
        return carry
    lax.fori_loop(0, RB // GL, last, 0)

    def finish(n1, carry, first, last):
        z = gated(n1, first, last)
        x0 = _short_conv_rows(p0_ref, w0_ref, b0_ref, n1, first=first, last=last)
        conv = u_scr[pl.ds(pl.multiple_of(n1 * PITCH, SUBLANES), RB), :] + sign * y_nyq
        y = (conv + z * skip_ref[...]) * x0
        y = y * lax.rsqrt(jnp.mean(y * y, axis=-1, keepdims=True) + EPS) * og_ref[...]
        o_ref[pl.ds(pl.multiple_of(n1 * RB, RB), RB), :] = y.astype(o_ref.dtype)
        return carry
    _for_row_blocks(NH, finish, 0)


def _hy_conv(p, conv_w, conv_b, khat, knyq, skip, out_g, tabs, *, L, CB=LANES, G=4):
    C = skip.shape[1]
    assert CB == C // HY_GROUPS, "one channel block must be exactly one norm group"
    N = 2 * L
    RA = N // RB
    NH = RA // 2
    nblk = C // CB
    g_tab, gi_tab, f3_tab, i1_tab = tabs["g_half"], tabs["gi"], tabs["f3"], tabs["i1"]
    col = lambda off: (lambda c: (0, off * nblk + c))
    pspec = lambda off: _single((L, CB), col(off))
    wspec = lambda off: pl.BlockSpec((3, CB), col(off))
    bspec = lambda off: pl.BlockSpec((1, CB), col(off))
    return pl.pallas_call(
        functools.partial(_hy_conv_kernel, L=L, CB=CB, G=G),
        grid=(nblk,),
        in_specs=[pspec(0), pspec(1), pspec(2), wspec(0), wspec(1), wspec(2), bspec(0), bspec(1), bspec(2),
                  _single((RA // 2 * RB, 2 * CB), col(0)),
                  pl.BlockSpec((SUBLANES, CB), col(0)),
                  bspec(0), bspec(0),
                  _single(g_tab.shape, lambda c: (0, 0, 0)),
                  _single(gi_tab.shape, lambda c: (0, 0, 0)),
                  _single(f3_tab.shape, lambda c: (0, 0)),
                  _single(i1_tab.shape, lambda c: (0, 0))],
        out_specs=pl.BlockSpec((L, CB), col(0)),
        out_shape=jax.ShapeDtypeStruct((L, C), BF16),
        scratch_shapes=[pltpu.VMEM((NH * PITCH, CB), F32),
                        pltpu.VMEM((2 * RA * PITCH, CB), F32)],
        compiler_params=_cparams(("arbitrary",)),
        name="hy_conv",
    )(p, p, p, conv_w, conv_w, conv_w, conv_b, conv_b, conv_b, khat, knyq, skip, out_g,
      g_tab, gi_tab, f3_tab, i1_tab)


def _log_sigmoid(x):
    return jnp.minimum(x, 0.0) - jnp.log1p(jnp.exp(-jnp.abs(x)))


def _gla_scan_blocks(dirs, *, TB):
    nch = TB // CHUNK
    rows = lambda c: slice(c * CHUNK, (c + 1) * CHUNK)
    units = [(d, c) for c in range(nch) for d in dirs]
    rr = lax.broadcasted_iota(jnp.int32, (CHUNK, CHUNK), 0)
    cc = lax.broadcasted_iota(jnp.int32, (CHUNK, CHUNK), 1)
    scale = GLA_DK ** -0.5

    for d in dirs:
        lr_hi, lr_lo = _split_hi_lo(d["lr"][...])
        gate_in = jnp.dot(jnp.concatenate([lr_hi, lr_hi, lr_lo], axis=1), d["wg"],
                          preferred_element_type=F32) + d["bg"]
        g = _log_sigmoid(gate_in) * (1.0 / GATE_TEMP)
        g_hi = g.astype(BF16)
        g_mid, g_lo = _split_hi_lo(g - g_hi.astype(F32))
        d["parts"][...] = jnp.concatenate([g_hi, g_mid, g_lo], axis=1)
        d["csum"] = jnp.where((cc >= rr) if d["reverse"] else (cc <= rr), 1.0, 0.0).astype(BF16)
        d["mask"] = (cc > rr) if d["reverse"] else (cc <= rr)
        d["edge"] = 0 if d["reverse"] else CHUNK - 1

    for d, c in units:
        b3 = jnp.dot(d["csum"], d["parts"][rows(c), :], preferred_element_type=F32)
        d["b"][rows(c), :] = b3[:, :GLA_DK] + b3[:, GLA_DK:2 * GLA_DK] + b3[:, 2 * GLA_DK:]

    decay = {}
    for d, c in units:
        b = d["b"][rows(c), :]
        b_edge = b[d["edge"]:d["edge"] + 1]
        k = d["k"][rows(c), :]
        d["qks"][0, rows(c), :] = (d["q"][rows(c), :] * scale * jnp.exp(b)).astype(BF16)
        d["qks"][1, rows(c), :] = (k * jnp.exp(-b)).astype(BF16)
        d["qks"][2, rows(c), :] = (k * jnp.exp(b_edge - b)).astype(BF16)
        d["vb"][rows(c), :] = d["v"][rows(c), :].astype(BF16)
        decay[(id(d), c)] = jnp.exp(b_edge)

    for d, c in units:
        att = _dot_nt(d["qks"][0, rows(c), :], d["qks"][1, rows(c), :])
        d["att"][rows(c), :] = jnp.where(d["mask"], att, 0.0).astype(BF16)

    for d, c in units:
        d["o"][rows(c), :] = jnp.dot(d["att"][rows(c), :], d["vb"][rows(c), :], preferred_element_type=F32)

    for d, c in units:
        d["kv"][c * GLA_DV:(c + 1) * GLA_DV, :] = _dot_tn(d["vb"][rows(c), :], d["qks"][2, rows(c), :])

    state = {id(d): d["s"][...] for d in dirs}
    for step in range(nch):
        for d in dirs:
            c = nch - 1 - step if d["reverse"] else step
            s_t = state[id(d)]
            d["o"][rows(c), :] += _dot_nt(d["qks"][0, rows(c), :], s_t.astype(BF16))
            state[id(d)] = s_t * decay[(id(d), c)] + d["kv"][c * GLA_DV:(c + 1) * GLA_DV, :]
    for d in dirs:
        d["s"][...] = state[id(d)]


def _gla_kernel(qf_ref, kf_ref, vf_ref, lf_ref, rf_ref, qb_ref, kb_ref, vb_ref, lb_ref, rb_ref,
                wg_ref, bg_ref, og_ref, o_ref,
                s_scr, parts_scr, b_scr, qks_scr, vb_scr, att_scr, kv_scr, ob_scr, half_scr, *, TB):
    s = pl.program_id(1)
    nb = pl.num_programs(1)

    @pl.when(s == 0)
    def _():
        s_scr[...] = jnp.zeros_like(s_scr)

    def direction(i, q, k, v, lr):
        return dict(q=q, k=k, v=v, lr=lr, wg=wg_ref[i], bg=bg_ref[i], reverse=bool(i), s=s_scr.at[i],
                    parts=parts_scr.at[i], b=b_scr.at[i], qks=qks_scr.at[i], vb=vb_scr.at[i],
                    att=att_scr.at[i], kv=kv_scr.at[i], o=ob_scr.at[i])
    _gla_scan_blocks([direction(0, qf_ref, kf_ref, vf_ref, lf_ref),
                      direction(1, qb_ref, kb_ref, vb_ref, lb_ref)], TB=TB)

    first_half = s < nb // 2
    for d, blk, r_ref in ((0, s, rf_ref), (1, nb - 1 - s, rb_ref)):
        rows = pl.ds(pl.multiple_of(blk * TB, TB), TB)

        @pl.when(first_half)
        def _():
            half_scr[rows, :] = ob_scr[d]

        @pl.when(jnp.logical_not(first_half))
        def _():
            tot = half_scr[rows, :] + ob_scr[d]
            tot = tot * lax.rsqrt(jnp.mean(tot * tot, axis=-1, keepdims=True) + EPS) * og_ref[...]
            o_ref[rows, :] = (tot * jax.nn.silu(r_ref[...])).astype(o_ref.dtype)


def _gla(p, p_lr, w_gate, b_gate, out_g, *, L, col0, TB=512):
    TB = min(TB, L)
    nb = L // TB
    assert nb % 2 == 0, "both scan directions must meet between two blocks"
    nch = TB // CHUNK
    qb = col0 // GLA_DK
    kb = qb + GLA_HEADS
    vb = (col0 + 2 * GLA_KW) // GLA_DV
    rb = vb + GLA_HEADS
    fwd = lambda s: s
    bwd = lambda s: nb - 1 - s

    def operands(blk):
        return [pl.BlockSpec((TB, GLA_DK), lambda h, s: (blk(s), qb + h)),
                pl.BlockSpec((TB, GLA_DK), lambda h, s: (blk(s), kb + h)),
                pl.BlockSpec((TB, GLA_DV), lambda h, s: (blk(s), vb + h)),
                pl.BlockSpec((TB, LANES), lambda h, s: (blk(s), 0)),
                pl.BlockSpec((TB, GLA_DV), lambda h, s: (blk(s), rb + h))]

    return pl.pallas_call(
        functools.partial(_gla_kernel, TB=TB),
        grid=(GLA_HEADS, nb),
        in_specs=operands(fwd) + operands(bwd) + [
            pl.BlockSpec((2, 3 * LANES, GLA_DK), lambda h, s: (0, 0, h)),
            pl.BlockSpec((2, 1, GLA_DK), lambda h, s: (0, 0, h)),
            pl.BlockSpec((1, GLA_DV), lambda h, s: (0, h)),
        ],
        out_specs=pl.BlockSpec((L, GLA_DV), lambda h, s: (0, h)),
        out_shape=jax.ShapeDtypeStruct((L, GLA_VW), BF16),
        scratch_shapes=[pltpu.VMEM((2, GLA_DV, GLA_DK), F32),
                        pltpu.VMEM((2, TB, 3 * GLA_DK), BF16),
                        pltpu.VMEM((2, TB, GLA_DK), F32),
                        pltpu.VMEM((2, 3, TB, GLA_DK), BF16),
                        pltpu.VMEM((2, TB, GLA_DV), BF16),
                        pltpu.VMEM((2, TB, CHUNK), BF16),
                        pltpu.VMEM((2, nch * GLA_DV, GLA_DK), F32),
                        pltpu.VMEM((2, TB, GLA_DV), F32),
                        pltpu.VMEM((L, GLA_DV), F32)],
        compiler_params=_cparams(("arbitrary", "arbitrary")),
        name="gla",
    )(*([p, p, p, p_lr, p] * 2), w_gate, b_gate, out_g)


def _filter_features(L):
    t = np.linspace(0.0, 1.0, L)[:, None]
    bands = (FILTER_EMB - 1) // 2
    freqs = np.linspace(1e-4, bands - 1, bands)[None, :]
    ang = (2.0 * np.pi / L) * np.arange(L)[:, None] * freqs
    feat = np.concatenate([t, np.cos(ang), -np.sin(ang)], axis=-1)
    feat_rev = np.roll(feat[::-1], 1, axis=0)
    both = np.zeros((L, LANES), np.float32)
    both[:, :FILTER_EMB] = feat
    both[:, FILTER_EMB:2 * FILTER_EMB] = feat_rev
    return jnp.asarray(both)


def _twice(w, rows_out):
    r, c = w.shape
    out = jnp.zeros((rows_out, 2 * c), F32)
    return out.at[:r, :c].set(w).at[r:2 * r, c:].set(w)


def _filter_deltas():
    min_decay = math.log(DECAY_TARGET) / LONG_DECAY_PCT
    max_decay = math.log(DECAY_TARGET) / SHORT_DECAY_PCT
    return jnp.abs(jnp.linspace(min_decay, max_decay, HY_WIDTH, dtype=F32)).reshape(1, HY_WIDTH)


def _block_diag_w4(w4):
    H = w4.shape[0]
    nblk = HY_WIDTH // LANES
    wf = w4[:, :HY_WIDTH].reshape(H, nblk, LANES)
    wb = w4[:, HY_WIDTH:].reshape(H, nblk, LANES)
    zz = jnp.zeros_like(wf)
    top = jnp.concatenate([wf, zz], axis=2)
    bot = jnp.concatenate([zz, wb], axis=2)
    return jnp.concatenate([top, bot], axis=0).reshape(2 * H, nblk * 2 * LANES)


def kernel(x, ffn1_norm, ffn1_w_gate, ffn1_w_up, ffn1_w_down, mix_norm, w_in, hy_conv_w, hy_conv_b, flt_w1, flt_b1, flt_f1, flt_w2, flt_b2, flt_f2, flt_w3, flt_b3, flt_f3, flt_w4, hy_skip, hy_out_norm, gla_w_a2_f, gla_b_a_f, gla_w_a2_b, gla_b_a_b, gla_out_norm, w_out, ffn2_norm, ffn2_w_gate, ffn2_w_up, ffn2_w_down, final_norm):
    B, L, D = x.shape
    depth = ffn1_norm.shape[0]
    tabs = _dft_tables(L)
    feat2 = _filter_features(L)
    deltas = _filter_deltas()
    gla_col0 = 3 * HY_WIDTH
    outs = []
    for bi in range(B):
        xb = x[bi]
        for l in range(depth):
            last = l == depth - 1
            xb = _ffn(xb, ffn1_norm[l], ffn1_w_gate[l], ffn1_w_up[l], ffn1_w_down[l], final_norm, final_norm=False)
            p, p_lr = _in_proj(xb, mix_norm[l], jnp.swapaxes(w_in[l], 0, 1))

            H2 = 2 * FILTER_HIDDEN
            both = lambda a: jnp.concatenate([a, a]).reshape(1, H2)
            hid3 = _filt_mlp(feat2, _twice(flt_w1[l], LANES), both(flt_b1[l]), both(flt_f1[l]),
                             _twice(flt_w2[l], H2), both(flt_b2[l]), both(flt_f2[l]),
                             _twice(flt_w3[l], H2), both(flt_b3[l]), both(flt_f3[l]))
            w4cat = _three_pass_rows(_block_diag_w4(flt_w4[l]))
            khat, knyq = _filt_fft(hid3, w4cat, deltas, tabs["g_full"], tabs["f3"], L=L)
            y_hy = _hy_conv(p, hy_conv_w[l], hy_conv_b[l].reshape(1, -1), khat, knyq,
                            hy_skip[l].reshape(1, -1), hy_out_norm[l].reshape(1, -1), tabs, L=L)

            w_gate = jnp.zeros((2, LANES, GLA_KW), F32)
            w_gate = w_gate.at[0, :GATE_RANK].set(gla_w_a2_f[l]).at[1, GATE_RANK:2 * GATE_RANK].set(gla_w_a2_b[l])
            w_gate = _three_pass_rows(w_gate)
            b_gate = jnp.stack([gla_b_a_f[l], gla_b_a_b[l]]).reshape(2, 1, GLA_KW)
            y_gla = _gla(p, p_lr, w_gate, b_gate, gla_out_norm[l].reshape(1, -1), L=L, col0=gla_col0)

            xb = _out_proj(xb, y_hy, y_gla, w_out[l])
            xb = _ffn(xb, ffn2_norm[l], ffn2_w_gate[l], ffn2_w_up[l], ffn2_w_down[l], final_norm,
                      final_norm=last)
        outs.append(xb)
    return jnp.stack(outs)
```

```python
import functools
import math

import numpy as np
import jax
import jax.numpy as jnp
from jax import lax
from jax.experimental import pallas as pl
from jax.experimental.pallas import tpu as pltpu

F32 = jnp.float32
BF16 = jnp.bfloat16
HIGHEST = lax.Precision.HIGHEST

EPS = 1e-6
HY_WIDTH = 1024
HY_GROUPS = 8
FILTER_EMB = 33
FILTER_HIDDEN = 64
SHORT_DECAY_PCT = 0.3
LONG_DECAY_PCT = 1.5
DECAY_TARGET = 1e-2
GLA_HEADS = 4
GLA_DK = 128
GLA_DV = 256
GLA_KW = GLA_HEADS * GLA_DK
GLA_VW = GLA_HEADS * GLA_DV
GATE_RANK = 16
GATE_TEMP = 16.0
CHUNK = 64

LANES = 128
SUBLANES = 8
VMEM_LIMIT = 60 * 1024 * 1024
FFN_SLAB = 128

RB = 128
KB = RB // 2
PITCH = RB + SUBLANES


def _cparams(sem):
    return pltpu.CompilerParams(dimension_semantics=sem, vmem_limit_bytes=VMEM_LIMIT)


def _single(block_shape, index_map):
    return pl.BlockSpec(block_shape, index_map, pipeline_mode=pl.Buffered(1))


def _split_hi_lo(x):
    hi = x.astype(BF16)
    return hi, (x - hi.astype(F32)).astype(BF16)


def _three_pass_rows(w):
    w_hi, w_lo = _split_hi_lo(w)
    return jnp.concatenate([w_hi, w_lo, w_hi], axis=-2)


def _dot_nt(a, b):
    return lax.dot_general(a, b, (((1,), (1,)), ((), ())), preferred_element_type=F32)


def _dot_tn(a, b):
    return lax.dot_general(a, b, (((0,), (0,)), ((), ())), preferred_element_type=F32)


def _ffn_kernel(x_ref, g_ref, wg_ref, wu_ref, wd_ref, fg_ref, *rest, final_norm, n_cast):
    cast_src, o_ref, cast_dst, h_scr = rest[:n_cast], rest[n_cast], rest[n_cast + 1:-1], rest[-1]
    j = pl.program_id(1)

    slab = min(FFN_SLAB, x_ref.shape[0])
    nslab = x_ref.shape[0] // slab

    @pl.when(j == 0)
    def _():
        def norm_rows(t, carry):
            rows = pl.ds(pl.multiple_of(t * slab, slab), slab)
            x = x_ref[rows, :]
            r = lax.rsqrt(jnp.mean(x * x, axis=-1, keepdims=True) + EPS)
            h_scr[rows, :] = (x * r * g_ref[...]).astype(BF16)
            o_ref[rows, :] = jnp.zeros((slab, x_ref.shape[1]), F32)
            return carry
        lax.fori_loop(0, nslab, norm_rows, 0)

    h = h_scr[...]
    gate = jnp.dot(h, wg_ref[...], preferred_element_type=F32)
    up = jnp.dot(h, wu_ref[...], preferred_element_type=F32)
    a = (jax.nn.silu(gate) * up).astype(BF16)
    o_ref[...] += jnp.dot(a, wd_ref[...], preferred_element_type=F32)

    for src, dst in zip(cast_src, cast_dst):
        dst[...] = src[...].astype(BF16)

    @pl.when(j == pl.num_programs(1) - 1)
    def _():
        def finish_rows(t, carry):
            rows = pl.ds(pl.multiple_of(t * slab, slab), slab)
            y = x_ref[rows, :] + 0.5 * o_ref[rows, :]
            if final_norm:
                r = lax.rsqrt(jnp.mean(y * y, axis=-1, keepdims=True) + EPS)
                y = y * r * fg_ref[...]
            o_ref[rows, :] = y
            return carry
        lax.fori_loop(0, nslab, finish_rows, 0)


def _cast_blocking(shape, ni, nj):
    rows, cols = shape
    assert rows % ni == 0 and (rows // ni) % SUBLANES == 0 and cols % LANES == 0
    ncol = max(d for d in range(1, nj + 1) if (cols // LANES) % d == 0)
    return (rows // ni, cols // ncol), (lambda i, j: (i, jnp.minimum(j, ncol - 1)))


def _ffn(x, norm_g, w_gate, w_up, w_down, final_g, *, final_norm, to_cast=(), tm=1024, tf=512):
    L, D = x.shape
    DF = w_gate.shape[1]
    tm = min(tm, L)
    tf = min(tf, DF)
    ni, nj = L // tm, DF // tf
    blockings = [_cast_blocking(w.shape, ni, nj) for w in to_cast]
    cast_specs = [pl.BlockSpec(blk, imap) for blk, imap in blockings]
    outs = pl.pallas_call(
        functools.partial(_ffn_kernel, final_norm=final_norm, n_cast=len(to_cast)),
        grid=(ni, nj),
        in_specs=[
            pl.BlockSpec((tm, D), lambda i, j: (i, 0)),
            pl.BlockSpec((1, D), lambda i, j: (0, 0)),
            pl.BlockSpec((D, tf), lambda i, j: (0, j)),
            pl.BlockSpec((D, tf), lambda i, j: (0, j)),
            pl.BlockSpec((tf, D), lambda i, j: (j, 0)),
            pl.BlockSpec((1, D), lambda i, j: (0, 0)),
        ] + cast_specs,
        out_specs=[pl.BlockSpec((tm, D), lambda i, j: (i, 0))] + cast_specs,
        out_shape=[jax.ShapeDtypeStruct((L, D), F32)] + [jax.ShapeDtypeStruct(w.shape, BF16) for w in to_cast],
        scratch_shapes=[pltpu.VMEM((tm, D), BF16)],
        compiler_params=_cparams(("arbitrary", "arbitrary")),
        name="ffn",
    )(x, norm_g.reshape(1, D), w_gate, w_up, w_down, final_g.reshape(1, D), *to_cast)
    return outs[0], outs[1:]


def _in_proj_kernel(x_ref, g_ref, wt_ref, wtail_ref, o_ref, otail_ref, h_scr, *, n_tail):
    j = pl.program_id(1)

    @pl.when(j == 0)
    def _():
        x = x_ref[...]
        r = lax.rsqrt(jnp.mean(x * x, axis=-1, keepdims=True) + EPS)
        h = (x * r * g_ref[...]).astype(BF16)
        h_scr[...] = h
        tail = _dot_nt(h, wtail_ref[...])
        col = lax.broadcasted_iota(jnp.int32, tail.shape, 1)
        otail_ref[...] = jnp.where(col < n_tail, tail, 0.0)

    o_ref[...] = _dot_nt(h_scr[...], wt_ref[...])


def _in_proj(x, norm_g, w_t, *, tm=1024, tn=512):
    L, D = x.shape
    n_cols = w_t.shape[0]
    nj = n_cols // tn
    n_tail = n_cols - nj * tn
    assert 0 < n_tail <= LANES and (nj * tn) % LANES == 0
    tm = min(tm, L)
    return pl.pallas_call(
        functools.partial(_in_proj_kernel, n_tail=n_tail),
        grid=(L // tm, nj),
        in_specs=[
            pl.BlockSpec((tm, D), lambda i, j: (i, 0)),
            pl.BlockSpec((1, D), lambda i, j: (0, 0)),
            pl.BlockSpec((tn, D), lambda i, j: (j, 0)),
            pl.BlockSpec((LANES, D), lambda i, j: (nj * tn // LANES, 0)),
        ],
        out_specs=[pl.BlockSpec((tm, tn), lambda i, j: (i, j)),
                   pl.BlockSpec((tm, LANES), lambda i, j: (i, 0))],
        out_shape=[jax.ShapeDtypeStruct((L, nj * tn), F32),
                   jax.ShapeDtypeStruct((L, LANES), F32)],
        scratch_shapes=[pltpu.VMEM((tm, D), BF16)],
        compiler_params=_cparams(("parallel", "arbitrary")),
        name="in_proj",
    )(x, norm_g.reshape(1, D), w_t, w_t)


def _out_proj_kernel(x_ref, yh_ref, yg_ref, wh_ref, wg_ref, o_ref):
    o_ref[...] = (x_ref[...]
                  + jnp.dot(yh_ref[...], wh_ref[...], preferred_element_type=F32)
                  + jnp.dot(yg_ref[...], wg_ref[...], preferred_element_type=F32))


def _out_proj(x, y_hy, y_gla, w_out, *, tm=512):
    L, D = x.shape
    WH = y_hy.shape[1]
    WG = y_gla.shape[1]
    tm = min(tm, L)
    w = w_out
    return pl.pallas_call(
        _out_proj_kernel,
        grid=(L // tm,),
        in_specs=[
            pl.BlockSpec((tm, D), lambda i: (i, 0)),
            pl.BlockSpec((tm, WH), lambda i: (i, 0)),
            pl.BlockSpec((tm, WG), lambda i: (i, 0)),
            _single((WH, D), lambda i: (0, 0)),
            _single((WG, D), lambda i: (0, 0)),
        ],
        out_specs=pl.BlockSpec((tm, D), lambda i: (i, 0)),
        out_shape=jax.ShapeDtypeStruct((L, D), F32),
        compiler_params=_cparams(("parallel",)),
        name="out_proj",
    )(x, y_hy, y_gla, w[:WH], w[WH:])


def _filt_mlp_kernel(z_ref, w1_ref, b1_ref, f1_ref, w2_ref, b2_ref, f2_ref, w3_ref, b3_ref, f3_ref, o_ref):
    dot = functools.partial(jnp.dot, precision=HIGHEST, preferred_element_type=F32)
    hid = jnp.sin(f1_ref[...] * (dot(z_ref[...], w1_ref[...]) + b1_ref[...]))
    hid = jnp.sin(f2_ref[...] * (dot(hid, w2_ref[...]) + b2_ref[...]))
    hid = jnp.sin(f3_ref[...] * (dot(hid, w3_ref[...]) + b3_ref[...]))
    hi, lo = _split_hi_lo(hid)
    o_ref[...] = jnp.concatenate([hi, hi, lo], axis=1)


def _filt_mlp(feat2, w1, b1, f1, w2, b2, f2, w3, b3, f3, *, tm=1024):
    L, FP = feat2.shape
    H2 = w2.shape[0]
    tm = min(tm, L)
    full = lambda shp: pl.BlockSpec(shp, lambda i: (0, 0))
    return pl.pallas_call(
        _filt_mlp_kernel,
        grid=(L // tm,),
        in_specs=[pl.BlockSpec((tm, FP), lambda i: (i, 0)),
                  full((FP, H2)), full((1, H2)), full((1, H2)),
                  full((H2, H2)), full((1, H2)), full((1, H2)),
                  full((H2, H2)), full((1, H2)), full((1, H2))],
        out_specs=pl.BlockSpec((tm, 3 * H2), lambda i: (i, 0)),
        out_shape=jax.ShapeDtypeStruct((L, 3 * H2), BF16),
        compiler_params=_cparams(("parallel",)),
        name="filt_mlp",
    )(feat2, w1, b1, f1, w2, b2, f2, w3, b3, f3)


def _dft_tables(L):
    N = 2 * L
    RA = N // RB
    NH = RA // 2
    two_pi = 2.0 * np.pi
    k1 = np.arange(RA, dtype=np.int64)

    def step1_table(NR):
        g = np.zeros((RB // 2, 2 * RA, 2 * NR), np.float64)
        for s in range(2):
            n = RB * np.arange(NR, dtype=np.int64)[None, :] + (2 * np.arange(RB // 2, dtype=np.int64) + s)[:, None]
            ang = two_pi * ((k1[None, :, None] * n[:, None, :]) % N) / N
            g[:, :RA, s * NR:(s + 1) * NR] = np.cos(ang)
            g[:, RA:, s * NR:(s + 1) * NR] = -np.sin(ang)
        return g

    n = RB * np.arange(NH, dtype=np.int64)[None, :] + np.arange(RB, dtype=np.int64)[:, None]
    ang = two_pi * ((n[:, :, None] * k1[None, None, :]) % N) / N
    gi = np.concatenate([np.cos(ang), -np.sin(ang)], axis=2)
    phi = two_pi * ((np.arange(KB)[:, None] * np.arange(RB)[None, :]) % RB) / RB
    f3 = np.block([[np.cos(phi), np.sin(phi)], [-np.sin(phi), np.cos(phi)]])
    i1 = np.block([[np.cos(phi.T), -np.sin(phi.T)], [np.sin(phi.T), np.cos(phi.T)]])
    cast = lambda a: jnp.asarray(a.astype(np.float32)).astype(BF16)
    return dict(g_half=cast(step1_table(NH)), g_full=cast(step1_table(RA)), gi=cast(gi), f3=cast(f3), i1=cast(i1))


def _fwd_step1(u_ref, g_ref, ab_ref, *, RA, NR, CB, G=2):
    def body(t, carry):
        ms = [t * G + j for j in range(G)]
        rhs = []
        for m in ms:
            ua = u_ref[pl.ds(2 * m, NR, stride=PITCH), :]
            ub = u_ref[pl.ds(2 * m + 1, NR, stride=PITCH), :]
            zz = jnp.zeros_like(ua)
            rhs.append(jnp.concatenate([jnp.concatenate([ua, zz], axis=1),
                                        jnp.concatenate([zz, ub], axis=1)], axis=0).astype(BF16))
        res = [jnp.dot(g_ref[m], r, preferred_element_type=F32) for m, r in zip(ms, rhs)]
        for m, r in zip(ms, res):
            ab_ref[pl.ds(2 * m, RA, stride=PITCH), :] = r[:RA, :CB]
            ab_ref[pl.ds(RA * PITCH + 2 * m, RA, stride=PITCH), :] = r[RA:, :CB]
            ab_ref[pl.ds(2 * m + 1, RA, stride=PITCH), :] = r[:RA, CB:]
            ab_ref[pl.ds(RA * PITCH + 2 * m + 1, RA, stride=PITCH), :] = r[RA:, CB:]
        return carry
    lax.fori_loop(0, RB // 2 // G, body, 0)


def _fwd_step3_pair(ab_ref, f3, kp, *, RA):
    cols = []
    for s in range(2):
        k1 = 2 * kp + s
        ar = ab_ref[pl.ds(pl.multiple_of(k1 * PITCH, SUBLANES), RB), :]
        ai = ab_ref[pl.ds(pl.multiple_of((RA + k1) * PITCH, SUBLANES), RB), :]
        cols.append(jnp.concatenate([ar, ai], axis=0))
    rhs = jnp.concatenate(cols, axis=1).astype(BF16)
    return jnp.dot(f3, rhs, preferred_element_type=F32)


def _alt_sign(shape):
    rows = lax.broadcasted_iota(jnp.int32, shape, 0)
    return jnp.where(rows % 2 == 0, 1.0, -1.0).astype(F32)


def _filt_fft_kernel(hid_ref, w4_ref, dl_ref, g_ref, f3_ref, kh_ref, kn_ref, u_scr, ab_scr, *, L, CB, G):
    N = 2 * L
    RA = N // RB
    NH = RA // 2
    f3 = f3_ref[...]
    inv_lm1 = 1.0 / (L - 1)

    def fill(n1, carry):
        s_abs, s_alt = carry
        r0 = pl.multiple_of(n1 * RB, RB)
        h2 = jnp.dot(hid_ref[pl.ds(r0, RB), :], w4_ref[...], preferred_element_type=F32)
        pos = lax.broadcasted_iota(jnp.int32, (RB, CB), 0) + n1 * RB
        hf = h2[:, :CB] * jnp.exp(-(pos.astype(F32) * inv_lm1) * dl_ref[...])
        hb = h2[:, CB:] * jnp.exp(-((L - pos).astype(F32) * inv_lm1) * dl_ref[...])
        hb = jnp.where(pos == 0, 0.0, hb)
        u_scr[pl.ds(pl.multiple_of(n1 * PITCH, SUBLANES), RB), :] = hf
        u_scr[pl.ds(pl.multiple_of((NH + n1) * PITCH, SUBLANES), RB), :] = hb
        return s_abs + (jnp.abs(hf) + jnp.abs(hb)), s_alt + (hf + hb)
    zero = jnp.zeros((RB, CB), F32)
    s_abs, s_alt = lax.fori_loop(0, NH, fill, (zero, zero), unroll=2)
    inv_l1 = 1.0 / jnp.sum(s_abs, axis=0, keepdims=True)
    kn = jnp.sum(s_alt * _alt_sign((RB, CB)), axis=0, keepdims=True) * inv_l1 * (1.0 / N)
    kn_ref[...] = jnp.broadcast_to(kn, kn_ref.shape)

    _fwd_step1(u_scr, g_ref, ab_scr, RA=RA, NR=RA, CB=CB, G=G)

    scale = inv_l1 * (2.0 / N)
    scale2 = jnp.concatenate([scale, scale], axis=1)
    row2 = lax.broadcasted_iota(jnp.int32, (KB, 2 * CB), 0)
    lane2 = lax.broadcasted_iota(jnp.int32, (KB, 2 * CB), 1)
    dc_pos = jnp.logical_and(row2 == 0, lane2 < CB)

    def emit(t, carry):
        kps = [t * G + j for j in range(G)]
        xs = [_fwd_step3_pair(ab_scr, f3, kp, RA=RA) for kp in kps]
        for kp, x in zip(kps, xs):
            kr = x[:KB] * scale2
            ki = x[KB:] * scale2
            dc = jnp.logical_and(dc_pos, kp == 0)
            kr = jnp.where(dc, 0.5 * kr, kr)
            ki = jnp.where(dc, 0.5 * ki, ki)
            kh_ref[pl.ds(pl.multiple_of(kp * RB, RB), RB), :] = jnp.concatenate([kr, ki], axis=0).astype(kh_ref.dtype)
        return carry
    lax.fori_loop(0, RA // 2 // G, emit, 0)


def _filt_fft(hid2, w4bd, deltas, g_tab, f3_tab, *, L, CB=LANES, G=4):
    C = deltas.shape[1]
    H2 = hid2.shape[1]
    N = 2 * L
    RA = N // RB
    nblk = C // CB
    return pl.pallas_call(
        functools.partial(_filt_fft_kernel, L=L, CB=CB, G=G),
        grid=(nblk,),
        in_specs=[
            _single((L, H2), lambda c: (0, 0)),
            pl.BlockSpec((H2, 2 * CB), lambda c: (0, c)),
            pl.BlockSpec((1, CB), lambda c: (0, c)),
            _single(g_tab.shape, lambda c: (0, 0, 0)),
            _single(f3_tab.shape, lambda c: (0, 0)),
        ],
        out_specs=[pl.BlockSpec((RA // 2 * RB, 2 * CB), lambda c: (0, c)),
                   pl.BlockSpec((SUBLANES, CB), lambda c: (0, c))],
        out_shape=[jax.ShapeDtypeStruct((RA // 2 * RB, 2 * C), BF16),
                   jax.ShapeDtypeStruct((SUBLANES, C), F32)],
        scratch_shapes=[pltpu.VMEM((RA * PITCH, CB), F32),
                        pltpu.VMEM((2 * RA * PITCH, CB), F32)],
        compiler_params=_cparams(("arbitrary",)),
        name="filt_fft",
    )(hid2, w4bd, deltas, g_tab, f3_tab)


def _short_conv_rows(p_ref, w_ref, b_ref, n1, *, first, last):
    r0 = pl.multiple_of(n1 * RB, RB)
    cur = p_ref[pl.ds(r0, RB), :]
    rows = lax.broadcasted_iota(jnp.int32, cur.shape, 0)
    if first:
        up = jnp.where(rows == 0, 0.0, pltpu.roll(cur, 1, axis=0))
    else:
        up = p_ref[pl.ds(r0 - 1, RB), :]
    if last:
        dn = jnp.where(rows == RB - 1, 0.0, pltpu.roll(cur, RB - 1, axis=0))
    else:
        dn = p_ref[pl.ds(r0 + 1, RB), :]
    w = w_ref[...]
    return b_ref[...] + up * w[0:1] + cur * w[1:2] + dn * w[2:3]


def _for_row_blocks(nblocks, body, carry):
    assert nblocks >= 2
    carry = body(0, carry, True, False)
    inner = nblocks - 2
    carry = lax.fori_loop(1, nblocks - 1, lambda n1, c: body(n1, c, False, False), carry,
                          unroll=2 if inner % 2 == 0 and inner > 0 else 1)
    return body(nblocks - 1, carry, False, True)


def _hy_conv_kernel(p0_ref, p1_ref, pv_ref, w0_ref, w1_ref, wv_ref, b0_ref, b1_ref, bv_ref,
                    kh_ref, kn_ref, skip_ref, og_ref, g_ref, gi_ref, f3_ref, i1_ref,
                    o_ref, u_scr, ab_scr, *, L, CB, G):
    N = 2 * L
    RA = N // RB
    NH = RA // 2
    f3 = f3_ref[...]
    i1 = i1_ref[...]
    sign = _alt_sign((RB, CB))

    def gated(n1, first, last):
        return (_short_conv_rows(pv_ref, wv_ref, bv_ref, n1, first=first, last=last)
                * _short_conv_rows(p1_ref, w1_ref, b1_ref, n1, first=first, last=last))

    def fill(n1, s_alt, first, last):
        z = gated(n1, first, last)
        u_scr[pl.ds(pl.multiple_of(n1 * PITCH, SUBLANES), RB), :] = z
        return s_alt + z
    s_alt = _for_row_blocks(NH, fill, jnp.zeros((RB, CB), F32))
    z_nyq = jnp.sum(s_alt * sign, axis=0, keepdims=True)
    y_nyq = z_nyq * kn_ref[0:1, :]

    _fwd_step1(u_scr, g_ref, ab_scr, RA=RA, NR=NH, CB=CB, G=G)

    def mid(t, carry):
        kps = [t * G + j for j in range(G)]
        xs = [_fwd_step3_pair(ab_scr, f3, kp, RA=RA) for kp in kps]
        khs = [kh_ref[pl.ds(pl.multiple_of(kp * RB, RB), RB), :].astype(F32) for kp in kps]
        bs = []
        for x, kh in zip(xs, khs):
            xr, xi, kr, ki = x[:KB], x[KB:], kh[:KB], kh[KB:]
            y = jnp.concatenate([xr * kr - xi * ki, xr * ki + xi * kr], axis=0).astype(BF16)
            bs.append(jnp.dot(i1, y, preferred_element_type=F32))
        for kp, b in zip(kps, bs):
            for s in range(2):
                k1 = 2 * kp + s
                ab_scr[pl.ds(pl.multiple_of(k1 * PITCH, SUBLANES), RB), :] = b[:RB, s * CB:(s + 1) * CB]
                ab_scr[pl.ds(pl.multiple_of((RA + k1) * PITCH, SUBLANES), RB), :] = b[RB:, s * CB:(s + 1) * CB]
        return carry
    lax.fori_loop(0, RA // 2 // G, mid, 0)

    GL = 2 * G

    def last(t, carry):
        n2s = [t * GL + j for j in range(GL)]
        rhs = [jnp.concatenate([ab_scr[pl.ds(n2, RA, stride=PITCH), :],
                                ab_scr[pl.ds(RA * PITCH + n2, RA, stride=PITCH), :]], axis=0).astype(BF16)
               for n2 in n2s]
        ys = [jnp.dot(gi_ref[n2], r, preferred_element_type=F32) for n2, r in zip(n2s, rhs)]
        for n2, y in zip(n2s, ys):
            u_scr[pl.ds(n2, NH, stride=PITCH), :] = y
        return carry
    lax.fori_loop(0, RB // GL, last, 0)

    def finish(n1, carry, first, last):
        z = gated(n1, first, last)
        x0 = _short_conv_rows(p0_ref, w0_ref, b0_ref, n1, first=first, last=last)
        conv = u_scr[pl.ds(pl.multiple_of(n1 * PITCH, SUBLANES), RB), :] + sign * y_nyq
        y = (conv + z * skip_ref[...]) * x0
        y = y * lax.rsqrt(jnp.mean(y * y, axis=-1, keepdims=True) + EPS) * og_ref[...]
        o_ref[pl.ds(pl.multiple_of(n1 * RB, RB), RB), :] = y.astype(o_ref.dtype)
        return carry
    _for_row_blocks(NH, finish, 0)


def _hy_conv(p, conv_w, conv_b, khat, knyq, skip, out_g, tabs, *, L, CB=LANES, G=4):
    C = skip.shape[1]
    assert CB == C // HY_GROUPS, "one channel block must be exactly one norm group"
    N = 2 * L
    RA = N // RB
    NH = RA // 2
    nblk = C // CB
    g_tab, gi_tab, f3_tab, i1_tab = tabs["g_half"], tabs["gi"], tabs["f3"], tabs["i1"]
    col = lambda off: (lambda c: (0, off * nblk + c))
    pspec = lambda off: _single((L, CB), col(off))
    wspec = lambda off: pl.BlockSpec((3, CB), col(off))
    bspec = lambda off: pl.BlockSpec((1, CB), col(off))
    return pl.pallas_call(
        functools.partial(_hy_conv_kernel, L=L, CB=CB, G=G),
        grid=(nblk,),
        in_specs=[pspec(0), pspec(1), pspec(2),
                  wspec(0), wspec(1), wspec(2), bspec(0), bspec(1), bspec(2),
                  pl.BlockSpec((RA // 2 * RB, 2 * CB), col(0)),
                  pl.BlockSpec((SUBLANES, CB), col(0)),
                  bspec(0), bspec(0),
                  _single(g_tab.shape, lambda c: (0, 0, 0)),
                  _single(gi_tab.shape, lambda c: (0, 0, 0)),
                  _single(f3_tab.shape, lambda c: (0, 0)),
                  _single(i1_tab.shape, lambda c: (0, 0))],
        out_specs=pl.BlockSpec((L, CB), col(0)),
        out_shape=jax.ShapeDtypeStruct((L, C), BF16),
        scratch_shapes=[pltpu.VMEM((NH * PITCH, CB), F32),
                        pltpu.VMEM((2 * RA * PITCH, CB), F32)],
        compiler_params=_cparams(("arbitrary",)),
        name="hy_conv",
    )(p, p, p, conv_w, conv_w, conv_w, conv_b, conv_b, conv_b, khat, knyq, skip, out_g,
      g_tab, gi_tab, f3_tab, i1_tab)


def _log_sigmoid(x):
    return jnp.minimum(x, 0.0) - jnp.log1p(jnp.exp(-jnp.abs(x)))


def _gla_scan_blocks(dirs, *, TB):
    nch = TB // CHUNK
    rows = lambda c: slice(c * CHUNK, (c + 1) * CHUNK)
    units = [(d, c) for c in range(nch) for d in dirs]
    rr = lax.broadcasted_iota(jnp.int32, (CHUNK, CHUNK), 0)
    cc = lax.broadcasted_iota(jnp.int32, (CHUNK, CHUNK), 1)
    scale = GLA_DK ** -0.5

    for d in dirs:
        lr_hi, lr_lo = _split_hi_lo(d["lr"][...])
        gate_in = jnp.dot(jnp.concatenate([lr_hi, lr_hi, lr_lo], axis=1), d["wg"],
                          preferred_element_type=F32) + d["bg"]
        g = _log_sigmoid(gate_in) * (1.0 / GATE_TEMP)
        g_hi = g.astype(BF16)
        g_mid, g_lo = _split_hi_lo(g - g_hi.astype(F32))
        d["parts"][...] = jnp.concatenate([g_hi, g_mid, g_lo], axis=1)
        d["csum"] = jnp.where((cc >= rr) if d["reverse"] else (cc <= rr), 1.0, 0.0).astype(BF16)
        d["mask"] = (cc > rr) if d["reverse"] else (cc <= rr)
        d["edge"] = 0 if d["reverse"] else CHUNK - 1

    for d, c in units:
        b3 = jnp.dot(d["csum"], d["parts"][rows(c), :], preferred_element_type=F32)
        d["b"][rows(c), :] = b3[:, :GLA_DK] + b3[:, GLA_DK:2 * GLA_DK] + b3[:, 2 * GLA_DK:]

    decay = {}
    for d, c in units:
        b = d["b"][rows(c), :]
        b_edge = b[d["edge"]:d["edge"] + 1]
        k = d["k"][rows(c), :]
        d["qks"][0, rows(c), :] = (d["q"][rows(c), :] * scale * jnp.exp(b)).astype(BF16)
        d["qks"][1, rows(c), :] = (k * jnp.exp(-b)).astype(BF16)
        d["qks"][2, rows(c), :] = (k * jnp.exp(b_edge - b)).astype(BF16)
        d["vb"][rows(c), :] = d["v"][rows(c), :].astype(BF16)
        decay[(id(d), c)] = jnp.exp(b_edge)

    for d, c in units:
        att = _dot_nt(d["qks"][0, rows(c), :], d["qks"][1, rows(c), :])
        d["att"][rows(c), :] = jnp.where(d["mask"], att, 0.0).astype(BF16)

    for d, c in units:
        d["o"][rows(c), :] = jnp.dot(d["att"][rows(c), :], d["vb"][rows(c), :], preferred_element_type=F32)

    for d, c in units:
        d["kv"][c * GLA_DV:(c + 1) * GLA_DV, :] = _dot_tn(d["vb"][rows(c), :], d["qks"][2, rows(c), :])

    state = {id(d): d["s"][...] for d in dirs}
    for step in range(nch):
        for d in dirs:
            c = nch - 1 - step if d["reverse"] else step
            s_t = state[id(d)]
            d["o"][rows(c), :] += _dot_nt(d["qks"][0, rows(c), :], s_t.astype(BF16))
            state[id(d)] = s_t * decay[(id(d), c)] + d["kv"][c * GLA_DV:(c + 1) * GLA_DV, :]
    for d in dirs:
        d["s"][...] = state[id(d)]


def _gla_kernel(qf_ref, kf_ref, vf_ref, lf_ref, rf_ref, qb_ref, kb_ref, vb_ref, lb_ref, rb_ref,
                wg_ref, bg_ref, og_ref, o_ref,
                s_scr, parts_scr, b_scr, qks_scr, vb_scr, att_scr, kv_scr, ob_scr, half_scr, *, TB):
    s = pl.program_id(1)
    nb = pl.num_programs(1)

    @pl.when(s == 0)
    def _():
        s_scr[...] = jnp.zeros_like(s_scr)

    def direction(i, q, k, v, lr):
        return dict(q=q, k=k, v=v, lr=lr, wg=wg_ref[i], bg=bg_ref[i], reverse=bool(i), s=s_scr.at[i],
                    parts=parts_scr.at[i], b=b_scr.at[i], qks=qks_scr.at[i], vb=vb_scr.at[i],
                    att=att_scr.at[i], kv=kv_scr.at[i], o=ob_scr.at[i])
    _gla_scan_blocks([direction(0, qf_ref, kf_ref, vf_ref, lf_ref),
                      direction(1, qb_ref, kb_ref, vb_ref, lb_ref)], TB=TB)

    first_half = s < nb // 2
    for d, blk, r_ref in ((0, s, rf_ref), (1, nb - 1 - s, rb_ref)):
        rows = pl.ds(pl.multiple_of(blk * TB, TB), TB)

        @pl.when(first_half)
        def _():
            half_scr[rows, :] = ob_scr[d]

        @pl.when(jnp.logical_not(first_half))
        def _():
            tot = half_scr[rows, :] + ob_scr[d]
            tot = tot * lax.rsqrt(jnp.mean(tot * tot, axis=-1, keepdims=True) + EPS) * og_ref[...]
            o_ref[rows, :] = (tot * jax.nn.silu(r_ref[...])).astype(o_ref.dtype)


def _gla(p, p_lr, w_gate, b_gate, out_g, *, L, col0, TB=512):
    TB = min(TB, L)
    nb = L // TB
    assert nb % 2 == 0, "both scan directions must meet between two blocks"
    nch = TB // CHUNK
    qb = col0 // GLA_DK
    kb = qb + GLA_HEADS
    vb = (col0 + 2 * GLA_KW) // GLA_DV
    rb = vb + GLA_HEADS
    fwd = lambda s: s
    bwd = lambda s: nb - 1 - s

    def operands(blk):
        return [pl.BlockSpec((TB, GLA_DK), lambda h, s: (blk(s), qb + h)),
                pl.BlockSpec((TB, GLA_DK), lambda h, s: (blk(s), kb + h)),
                pl.BlockSpec((TB, GLA_DV), lambda h, s: (blk(s), vb + h)),
                pl.BlockSpec((TB, LANES), lambda h, s: (blk(s), 0)),
                pl.BlockSpec((TB, GLA_DV), lambda h, s: (blk(s), rb + h))]

    return pl.pallas_call(
        functools.partial(_gla_kernel, TB=TB),
        grid=(GLA_HEADS, nb),
        in_specs=operands(fwd) + operands(bwd) + [
            pl.BlockSpec((2, 3 * LANES, GLA_DK), lambda h, s: (0, 0, h)),
            pl.BlockSpec((2, 1, GLA_DK), lambda h, s: (0, 0, h)),
            pl.BlockSpec((1, GLA_DV), lambda h, s: (0, h)),
        ],
        out_specs=pl.BlockSpec((L, GLA_DV), lambda h, s: (0, h)),
        out_shape=jax.ShapeDtypeStruct((L, GLA_VW), BF16),
        scratch_shapes=[pltpu.VMEM((2, GLA_DV, GLA_DK), F32),
                        pltpu.VMEM((2, TB, 3 * GLA_DK), BF16),
                        pltpu.VMEM((2, TB, GLA_DK), F32),
                        pltpu.VMEM((2, 3, TB, GLA_DK), BF16),
                        pltpu.VMEM((2, TB, GLA_DV), BF16),
                        pltpu.VMEM((2, TB, CHUNK), BF16),
                        pltpu.VMEM((2, nch * GLA_DV, GLA_DK), F32),
                        pltpu.VMEM((2, TB, GLA_DV), F32),
                        pltpu.VMEM((L, GLA_DV), F32)],
        compiler_params=_cparams(("arbitrary", "arbitrary")),
        name="gla",
    )(*([p, p, p, p_lr, p] * 2), w_gate, b_gate, out_g)


def _filter_features(L):
    t = np.linspace(0.0, 1.0, L)[:, None]
    bands = (FILTER_EMB - 1) // 2
    freqs = np.linspace(1e-4, bands - 1, bands)[None, :]
    ang = (2.0 * np.pi / L) * np.arange(L)[:, None] * freqs
    feat = np.concatenate([t, np.cos(ang), -np.sin(ang)], axis=-1)
    feat_rev = np.roll(feat[::-1], 1, axis=0)
    both = np.zeros((L, LANES), np.float32)
    both[:, :FILTER_EMB] = feat
    both[:, FILTER_EMB:2 * FILTER_EMB] = feat_rev
    return jnp.asarray(both)


def _twice(w, rows_out):
    r, c = w.shape
    out = jnp.zeros((rows_out, 2 * c), F32)
    return out.at[:r, :c].set(w).at[r:2 * r, c:].set(w)


def _filter_deltas():
    min_decay = math.log(DECAY_TARGET) / LONG_DECAY_PCT
    max_decay = math.log(DECAY_TARGET) / SHORT_DECAY_PCT
    return jnp.abs(jnp.linspace(min_decay, max_decay, HY_WIDTH, dtype=F32)).reshape(1, HY_WIDTH)


def _block_diag_w4(w4):
    H = w4.shape[0]
    nblk = HY_WIDTH // LANES
    wf = w4[:, :HY_WIDTH].reshape(H, nblk, LANES)
    wb = w4[:, HY_WIDTH:].reshape(H, nblk, LANES)
    zz = jnp.zeros_like(wf)
    top = jnp.concatenate([wf, zz], axis=2)
    bot = jnp.concatenate([zz, wb], axis=2)
    return jnp.concatenate([top, bot], axis=0).reshape(2 * H, nblk * 2 * LANES)


def kernel(x, ffn1_norm, ffn1_w_gate, ffn1_w_up, ffn1_w_down, mix_norm, w_in, hy_conv_w, hy_conv_b, flt_w1, flt_b1, flt_f1, flt_w2, flt_b2, flt_f2, flt_w3, flt_b3, flt_f3, flt_w4, hy_skip, hy_out_norm, gla_w_a2_f, gla_b_a_f, gla_w_a2_b, gla_b_a_b, gla_out_norm, w_out, ffn2_norm, ffn2_w_gate, ffn2_w_up, ffn2_w_down, final_norm):
    B, L, D = x.shape
    depth = ffn1_norm.shape[0]
    tabs = _dft_tables(L)
    feat2 = _filter_features(L)
    deltas = _filter_deltas()
    gla_col0 = 3 * HY_WIDTH
    outs = []
    for bi in range(B):
        xb = x[bi]
        w1 = [w[0].astype(BF16) for w in (ffn1_w_gate, ffn1_w_up, ffn1_w_down)]
        for l in range(depth):
            last = l == depth - 1
            later = [ffn2_w_gate[l], ffn2_w_up[l], ffn2_w_down[l], w_out[l]]
            xb, (w2_gate, w2_up, w2_down, w_out_bf) = _ffn(xb, ffn1_norm[l], *w1, final_norm, final_norm=False,
                                                           to_cast=later)
            p, p_lr = _in_proj(xb, mix_norm[l], jnp.swapaxes(w_in[l], 0, 1).astype(BF16))

            H2 = 2 * FILTER_HIDDEN
            both = lambda a: jnp.concatenate([a, a]).reshape(1, H2)
            hid3 = _filt_mlp(feat2, _twice(flt_w1[l], LANES), both(flt_b1[l]), both(flt_f1[l]),
                             _twice(flt_w2[l], H2), both(flt_b2[l]), both(flt_f2[l]),
                             _twice(flt_w3[l], H2), both(flt_b3[l]), both(flt_f3[l]))
            w4cat = _three_pass_rows(_block_diag_w4(flt_w4[l]))
            khat, knyq = _filt_fft(hid3, w4cat, deltas, tabs["g_full"], tabs["f3"], L=L)
            y_hy = _hy_conv(p, hy_conv_w[l], hy_conv_b[l].reshape(1, -1), khat, knyq,
                            hy_skip[l].reshape(1, -1), hy_out_norm[l].reshape(1, -1), tabs, L=L)

            w_gate = jnp.zeros((2, LANES, GLA_KW), F32)
            w_gate = w_gate.at[0, :GATE_RANK].set(gla_w_a2_f[l]).at[1, GATE_RANK:2 * GATE_RANK].set(gla_w_a2_b[l])
            w_gate = _three_pass_rows(w_gate)
            b_gate = jnp.stack([gla_b_a_f[l], gla_b_a_b[l]]).reshape(2, 1, GLA_KW)
            y_gla = _gla(p, p_lr, w_gate, b_gate, gla_out_norm[l].reshape(1, -1), L=L, col0=gla_col0)

            xb = _out_proj(xb, y_hy, y_gla, w_out_bf)
            nxt = [] if last else [ffn1_w_gate[l + 1], ffn1_w_up[l + 1], ffn1_w_down[l + 1]]
            xb, w1 = _ffn(xb, ffn2_norm[l], w2_gate, w2_up, w2_down, final_norm, final_norm=last, to_cast=nxt)
        outs.append(xb)
    return jnp.stack(outs)
```

```python
import functools
import math

import numpy as np
import jax
import jax.numpy as jnp
from jax import lax
from jax.experimental import pallas as pl
from jax.experimental.pallas import tpu as pltpu

F32 = jnp.float32
BF16 = jnp.bfloat16
HIGHEST = lax.Precision.HIGHEST

EPS = 1e-6
HY_WIDTH = 1024
HY_GROUPS = 8
FILTER_EMB = 33
FILTER_HIDDEN = 64
SHORT_DECAY_PCT = 0.3
LONG_DECAY_PCT = 1.5
DECAY_TARGET = 1e-2
GLA_HEADS = 4
GLA_DK = 128
GLA_DV = 256
GLA_KW = GLA_HEADS * GLA_DK
GLA_VW = GLA_HEADS * GLA_DV
GATE_RANK = 16
GATE_TEMP = 16.0
CHUNK = 64

LANES = 128
SUBLANES = 8
VMEM_LIMIT = 60 * 1024 * 1024
FFN_SLAB = 128

RB = 128
KB = RB // 2
PITCH = RB + SUBLANES


def _cparams(sem):
    return pltpu.CompilerParams(dimension_semantics=sem, vmem_limit_bytes=VMEM_LIMIT)


def _single(block_shape, index_map):
    return pl.BlockSpec(block_shape, index_map, pipeline_mode=pl.Buffered(1))


def _split_hi_lo(x):
    hi = x.astype(BF16)
    return hi, (x - hi.astype(F32)).astype(BF16)


def _three_pass_rows(w):
    w_hi, w_lo = _split_hi_lo(w)
    return jnp.concatenate([w_hi, w_lo, w_hi], axis=-2)


def _dot_nt(a, b):
    return lax.dot_general(a, b, (((1,), (1,)), ((), ())), preferred_element_type=F32)


def _dot_tn(a, b):
    return lax.dot_general(a, b, (((0,), (0,)), ((), ())), preferred_element_type=F32)


def _ffn_kernel(x_ref, g_ref, wg_ref, wu_ref, wd_ref, fg_ref, *rest, final_norm, n_cast):
    cast_src, o_ref, cast_dst, h_scr = rest[:n_cast], rest[n_cast], rest[n_cast + 1:-1], rest[-1]
    j = pl.program_id(1)

    slab = min(FFN_SLAB, x_ref.shape[0])
    nslab = x_ref.shape[0] // slab

    @pl.when(j == 0)
    def _():
        def norm_rows(t, carry):
            rows = pl.ds(pl.multiple_of(t * slab, slab), slab)
            x = x_ref[rows, :]
            r = lax.rsqrt(jnp.mean(x * x, axis=-1, keepdims=True) + EPS)
            h_scr[rows, :] = (x * r * g_ref[...]).astype(BF16)
            o_ref[rows, :] = jnp.zeros((slab, x_ref.shape[1]), F32)
            return carry
        lax.fori_loop(0, nslab, norm_rows, 0)

    h = h_scr[...]
    gate = jnp.dot(h, wg_ref[...], preferred_element_type=F32)
    up = jnp.dot(h, wu_ref[...], preferred_element_type=F32)
    a = (jax.nn.silu(gate) * up).astype(BF16)
    o_ref[...] += jnp.dot(a, wd_ref[...], preferred_element_type=F32)

    for src, dst in zip(cast_src, cast_dst):
        dst[...] = src[...].astype(BF16)

    @pl.when(j == pl.num_programs(1) - 1)
    def _():
        def finish_rows(t, carry):
            rows = pl.ds(pl.multiple_of(t * slab, slab), slab)
            y = x_ref[rows, :] + 0.5 * o_ref[rows, :]
            if final_norm:
                r = lax.rsqrt(jnp.mean(y * y, axis=-1, keepdims=True) + EPS)
                y = y * r * fg_ref[...]
            o_ref[rows, :] = y
            return carry
        lax.fori_loop(0, nslab, finish_rows, 0)


def _cast_blocking(shape, ni, nj):
    rows, cols = shape
    assert rows % ni == 0 and (rows // ni) % SUBLANES == 0 and cols % LANES == 0
    ncol = max(d for d in range(1, nj + 1) if (cols // LANES) % d == 0)
    return (rows // ni, cols // ncol), (lambda i, j: (i, jnp.minimum(j, ncol - 1)))


def _ffn(x, norm_g, w_gate, w_up, w_down, final_g, *, final_norm, to_cast=(), tm=1024, tf=512):
    L, D = x.shape
    DF = w_gate.shape[1]
    tm = min(tm, L)
    tf = min(tf, DF)
    ni, nj = L // tm, DF // tf
    blockings = [_cast_blocking(w.shape, ni, nj) for w in to_cast]
    cast_specs = [pl.BlockSpec(blk, imap) for blk, imap in blockings]
    outs = pl.pallas_call(
        functools.partial(_ffn_kernel, final_norm=final_norm, n_cast=len(to_cast)),
        grid=(ni, nj),
        in_specs=[
            pl.BlockSpec((tm, D), lambda i, j: (i, 0)),
            pl.BlockSpec((1, D), lambda i, j: (0, 0)),
            pl.BlockSpec((D, tf), lambda i, j: (0, j)),
            pl.BlockSpec((D, tf), lambda i, j: (0, j)),
            pl.BlockSpec((tf, D), lambda i, j: (j, 0)),
            pl.BlockSpec((1, D), lambda i, j: (0, 0)),
        ] + cast_specs,
        out_specs=[pl.BlockSpec((tm, D), lambda i, j: (i, 0))] + cast_specs,
        out_shape=[jax.ShapeDtypeStruct((L, D), F32)] + [jax.ShapeDtypeStruct(w.shape, BF16) for w in to_cast],
        scratch_shapes=[pltpu.VMEM((tm, D), BF16)],
        compiler_params=_cparams(("arbitrary", "arbitrary")),
        name="ffn",
    )(x, norm_g.reshape(1, D), w_gate, w_up, w_down, final_g.reshape(1, D), *to_cast)
    return outs[0], outs[1:]


def _in_proj_kernel(x_ref, g_ref, wt_ref, wtail_ref, o_ref, otail_ref, h_scr, *, n_tail):
    j = pl.program_id(1)

    @pl.when(j == 0)
    def _():
        x = x_ref[...]
        r = lax.rsqrt(jnp.mean(x * x, axis=-1, keepdims=True) + EPS)
        h = (x * r * g_ref[...]).astype(BF16)
        h_scr[...] = h
        tail = _dot_nt(h, wtail_ref[...])
        col = lax.broadcasted_iota(jnp.int32, tail.shape, 1)
        otail_ref[...] = jnp.where(col < n_tail, tail, 0.0)

    o_ref[...] = _dot_nt(h_scr[...], wt_ref[...])


def _in_proj(x, norm_g, w_t, *, tm=1024, tn=1024):
    L, D = x.shape
    n_cols = w_t.shape[0]
    nj = n_cols // tn
    n_tail = n_cols - nj * tn
    assert 0 < n_tail <= LANES and (nj * tn) % LANES == 0
    tm = min(tm, L)
    return pl.pallas_call(
        functools.partial(_in_proj_kernel, n_tail=n_tail),
        grid=(L // tm, nj),
        in_specs=[
            pl.BlockSpec((tm, D), lambda i, j: (i, 0)),
            pl.BlockSpec((1, D), lambda i, j: (0, 0)),
            pl.BlockSpec((tn, D), lambda i, j: (j, 0)),
            pl.BlockSpec((LANES, D), lambda i, j: (nj * tn // LANES, 0)),
        ],
        out_specs=[pl.BlockSpec((tm, tn), lambda i, j: (i, j)),
                   pl.BlockSpec((tm, LANES), lambda i, j: (i, 0))],
        out_shape=[jax.ShapeDtypeStruct((L, nj * tn), F32),
                   jax.ShapeDtypeStruct((L, LANES), F32)],
        scratch_shapes=[pltpu.VMEM((tm, D), BF16)],
        compiler_params=_cparams(("parallel", "arbitrary")),
        name="in_proj",
    )(x, norm_g.reshape(1, D), w_t, w_t)


def _out_proj_kernel(x_ref, yh_ref, yg_ref, wh_ref, wg_ref, o_ref):
    o_ref[...] = (x_ref[...]
                  + jnp.dot(yh_ref[...], wh_ref[...], preferred_element_type=F32)
                  + jnp.dot(yg_ref[...], wg_ref[...], preferred_element_type=F32))


def _out_proj(x, y_hy, y_gla, w_out, *, tm=512):
    L, D = x.shape
    WH = y_hy.shape[1]
    WG = y_gla.shape[1]
    tm = min(tm, L)
    w = w_out
    return pl.pallas_call(
        _out_proj_kernel,
        grid=(L // tm,),
        in_specs=[
            pl.BlockSpec((tm, D), lambda i: (i, 0)),
            pl.BlockSpec((tm, WH), lambda i: (i, 0)),
            pl.BlockSpec((tm, WG), lambda i: (i, 0)),
            _single((WH, D), lambda i: (0, 0)),
            _single((WG, D), lambda i: (0, 0)),
        ],
        out_specs=pl.BlockSpec((tm, D), lambda i: (i, 0)),
        out_shape=jax.ShapeDtypeStruct((L, D), F32),
        compiler_params=_cparams(("parallel",)),
        name="out_proj",
    )(x, y_hy, y_gla, w[:WH], w[WH:])


def _filt_mlp_kernel(z_ref, w1_ref, b1_ref, f1_ref, w2_ref, b2_ref, f2_ref, w3_ref, b3_ref, f3_ref, *rest, n_cast):
    cast_src, o_ref, cast_dst = rest[:n_cast], rest[n_cast], rest[n_cast + 1:]
    dot = functools.partial(jnp.dot, precision=HIGHEST, preferred_element_type=F32)
    hid = jnp.sin(f1_ref[...] * (dot(z_ref[...], w1_ref[...]) + b1_ref[...]))
    hid = jnp.sin(f2_ref[...] * (dot(hid, w2_ref[...]) + b2_ref[...]))
    hid = jnp.sin(f3_ref[...] * (dot(hid, w3_ref[...]) + b3_ref[...]))
    hi, lo = _split_hi_lo(hid)
    o_ref[...] = jnp.concatenate([hi, hi, lo], axis=1)
    for src, dst in zip(cast_src, cast_dst):
        dst[...] = src[...].astype(BF16)


def _cast_row_blocks(shape, n):
    rows, cols = shape
    tile = 2 * SUBLANES
    block_rows = pl.cdiv(pl.cdiv(rows, n), tile) * tile
    nblocks = pl.cdiv(rows, block_rows)
    return (block_rows, cols), (lambda i: (jnp.minimum(i, nblocks - 1), 0))


def _filt_mlp(feat2, w1, b1, f1, w2, b2, f2, w3, b3, f3, *, to_cast=(), tm=256):
    L, FP = feat2.shape
    H2 = w2.shape[0]
    tm = min(tm, L)
    steps = L // tm
    full = lambda shp: pl.BlockSpec(shp, lambda i: (0, 0))
    cast_specs = [pl.BlockSpec(*_cast_row_blocks(w.shape, steps)) for w in to_cast]
    outs = pl.pallas_call(
        functools.partial(_filt_mlp_kernel, n_cast=len(to_cast)),
        grid=(steps,),
        in_specs=[pl.BlockSpec((tm, FP), lambda i: (i, 0)),
                  full((FP, H2)), full((1, H2)), full((1, H2)),
                  full((H2, H2)), full((1, H2)), full((1, H2)),
                  full((H2, H2)), full((1, H2)), full((1, H2))] + cast_specs,
        out_specs=[pl.BlockSpec((tm, 3 * H2), lambda i: (i, 0))] + cast_specs,
        out_shape=[jax.ShapeDtypeStruct((L, 3 * H2), BF16)] + [jax.ShapeDtypeStruct(w.shape, BF16) for w in to_cast],
        compiler_params=_cparams(("arbitrary",)),
        name="filt_mlp",
    )(feat2, w1, b1, f1, w2, b2, f2, w3, b3, f3, *to_cast)
    return outs[0], outs[1:]


def _dft_tables(L):
    N = 2 * L
    RA = N // RB
    NH = RA // 2
    two_pi = 2.0 * np.pi
    k1 = np.arange(RA, dtype=np.int64)

    def step1_table(NR):
        g = np.zeros((RB // 2, 2 * RA, 2 * NR), np.float64)
        for s in range(2):
            n = RB * np.arange(NR, dtype=np.int64)[None, :] + (2 * np.arange(RB // 2, dtype=np.int64) + s)[:, None]
            ang = two_pi * ((k1[None, :, None] * n[:, None, :]) % N) / N
            g[:, :RA, s * NR:(s + 1) * NR] = np.cos(ang)
            g[:, RA:, s * NR:(s + 1) * NR] = -np.sin(ang)
        return g

    n = RB * np.arange(NH, dtype=np.int64)[None, :] + np.arange(RB, dtype=np.int64)[:, None]
    ang = two_pi * ((n[:, :, None] * k1[None, None, :]) % N) / N
    gi = np.concatenate([np.cos(ang), -np.sin(ang)], axis=2)
    phi = two_pi * ((np.arange(KB)[:, None] * np.arange(RB)[None, :]) % RB) / RB
    f3 = np.block([[np.cos(phi), np.sin(phi)], [-np.sin(phi), np.cos(phi)]])
    i1 = np.block([[np.cos(phi.T), -np.sin(phi.T)], [np.sin(phi.T), np.cos(phi.T)]])
    cast = lambda a: jnp.asarray(a.astype(np.float32)).astype(BF16)
    return dict(g_half=cast(step1_table(NH)), g_full=cast(step1_table(RA)), gi=cast(gi), f3=cast(f3), i1=cast(i1))


def _fwd_step1(u_ref, g_ref, ab_ref, *, RA, NR, CB, G=2):
    def body(t, carry):
        ms = [t * G + j for j in range(G)]
        rhs = []
        for m in ms:
            ua = u_ref[pl.ds(2 * m, NR, stride=PITCH), :]
            ub = u_ref[pl.ds(2 * m + 1, NR, stride=PITCH), :]
            zz = jnp.zeros_like(ua)
            rhs.append(jnp.concatenate([jnp.concatenate([ua, zz], axis=1),
                                        jnp.concatenate([zz, ub], axis=1)], axis=0).astype(BF16))
        res = [jnp.dot(g_ref[m], r, preferred_element_type=F32) for m, r in zip(ms, rhs)]
        for m, r in zip(ms, res):
            ab_ref[pl.ds(2 * m, RA, stride=PITCH), :] = r[:RA, :CB]
            ab_ref[pl.ds(RA * PITCH + 2 * m, RA, stride=PITCH), :] = r[RA:, :CB]
            ab_ref[pl.ds(2 * m + 1, RA, stride=PITCH), :] = r[:RA, CB:]
            ab_ref[pl.ds(RA * PITCH + 2 * m + 1, RA, stride=PITCH), :] = r[RA:, CB:]
        return carry
    lax.fori_loop(0, RB // 2 // G, body, 0)


def _fwd_step3_pair(ab_ref, f3, kp, *, RA):
    cols = []
    for s in range(2):
        k1 = 2 * kp + s
        ar = ab_ref[pl.ds(pl.multiple_of(k1 * PITCH, SUBLANES), RB), :]
        ai = ab_ref[pl.ds(pl.multiple_of((RA + k1) * PITCH, SUBLANES), RB), :]
        cols.append(jnp.concatenate([ar, ai], axis=0))
    rhs = jnp.concatenate(cols, axis=1).astype(BF16)
    return jnp.dot(f3, rhs, preferred_element_type=F32)


def _alt_sign(shape):
    rows = lax.broadcasted_iota(jnp.int32, shape, 0)
    return jnp.where(rows % 2 == 0, 1.0, -1.0).astype(F32)


def _filt_fft_kernel(hid_ref, w4_ref, dl_ref, g_ref, f3_ref, kh_ref, kn_ref, u_scr, ab_scr, *, L, CB, G):
    N = 2 * L
    RA = N // RB
    NH = RA // 2
    f3 = f3_ref[...]
    inv_lm1 = 1.0 / (L - 1)

    def fill(n1, carry):
        s_abs, s_alt = carry
        r0 = pl.multiple_of(n1 * RB, RB)
        h2 = jnp.dot(hid_ref[pl.ds(r0, RB), :], w4_ref[...], preferred_element_type=F32)
        pos = lax.broadcasted_iota(jnp.int32, (RB, CB), 0) + n1 * RB
        hf = h2[:, :CB] * jnp.exp(-(pos.astype(F32) * inv_lm1) * dl_ref[...])
        hb = h2[:, CB:] * jnp.exp(-((L - pos).astype(F32) * inv_lm1) * dl_ref[...])
        hb = jnp.where(pos == 0, 0.0, hb)
        u_scr[pl.ds(pl.multiple_of(n1 * PITCH, SUBLANES), RB), :] = hf
        u_scr[pl.ds(pl.multiple_of((NH + n1) * PITCH, SUBLANES), RB), :] = hb
        return s_abs + (jnp.abs(hf) + jnp.abs(hb)), s_alt + (hf + hb)
    zero = jnp.zeros((RB, CB), F32)
    s_abs, s_alt = lax.fori_loop(0, NH, fill, (zero, zero), unroll=2)
    inv_l1 = 1.0 / jnp.sum(s_abs, axis=0, keepdims=True)
    kn = jnp.sum(s_alt * _alt_sign((RB, CB)), axis=0, keepdims=True) * inv_l1 * (1.0 / N)
    kn_ref[...] = jnp.broadcast_to(kn, kn_ref.shape)

    _fwd_step1(u_scr, g_ref, ab_scr, RA=RA, NR=RA, CB=CB, G=G)

    scale = inv_l1 * (2.0 / N)
    scale2 = jnp.concatenate([scale, scale], axis=1)
    row2 = lax.broadcasted_iota(jnp.int32, (KB, 2 * CB), 0)
    lane2 = lax.broadcasted_iota(jnp.int32, (KB, 2 * CB), 1)
    dc_pos = jnp.logical_and(row2 == 0, lane2 < CB)

    def emit(t, carry):
        kps = [t * G + j for j in range(G)]
        xs = [_fwd_step3_pair(ab_scr, f3, kp, RA=RA) for kp in kps]
        for kp, x in zip(kps, xs):
            kr = x[:KB] * scale2
            ki = x[KB:] * scale2
            dc = jnp.logical_and(dc_pos, kp == 0)
            kr = jnp.where(dc, 0.5 * kr, kr)
            ki = jnp.where(dc, 0.5 * ki, ki)
            kh_ref[pl.ds(pl.multiple_of(kp * RB, RB), RB), :] = jnp.concatenate([kr, ki], axis=0).astype(kh_ref.dtype)
        return carry
    lax.fori_loop(0, RA // 2 // G, emit, 0)


def _filt_fft(hid2, w4bd, deltas, g_tab, f3_tab, *, L, CB=LANES, G=4):
    C = deltas.shape[1]
    H2 = hid2.shape[1]
    N = 2 * L
    RA = N // RB
    nblk = C // CB
    return pl.pallas_call(
        functools.partial(_filt_fft_kernel, L=L, CB=CB, G=G),
        grid=(nblk,),
        in_specs=[
            _single((L, H2), lambda c: (0, 0)),
            pl.BlockSpec((H2, 2 * CB), lambda c: (0, c)),
            pl.BlockSpec((1, CB), lambda c: (0, c)),
            _single(g_tab.shape, lambda c: (0, 0, 0)),
            _single(f3_tab.shape, lambda c: (0, 0)),
        ],
        out_specs=[pl.BlockSpec((RA // 2 * RB, 2 * CB), lambda c: (0, c)),
                   pl.BlockSpec((SUBLANES, CB), lambda c: (0, c))],
        out_shape=[jax.ShapeDtypeStruct((RA // 2 * RB, 2 * C), BF16),
                   jax.ShapeDtypeStruct((SUBLANES, C), F32)],
        scratch_shapes=[pltpu.VMEM((RA * PITCH, CB), F32),
                        pltpu.VMEM((2 * RA * PITCH, CB), F32)],
        compiler_params=_cparams(("arbitrary",)),
        name="filt_fft",
    )(hid2, w4bd, deltas, g_tab, f3_tab)


def _short_conv_rows(p_ref, w_ref, b_ref, n1, *, first, last):
    r0 = pl.multiple_of(n1 * RB, RB)
    cur = p_ref[pl.ds(r0, RB), :]
    rows = lax.broadcasted_iota(jnp.int32, cur.shape, 0)
    if first:
        up = jnp.where(rows == 0, 0.0, pltpu.roll(cur, 1, axis=0))
    else:
        up = p_ref[pl.ds(r0 - 1, RB), :]
    if last:
        dn = jnp.where(rows == RB - 1, 0.0, pltpu.roll(cur, RB - 1, axis=0))
    else:
        dn = p_ref[pl.ds(r0 + 1, RB), :]
    w = w_ref[...]
    return b_ref[...] + up * w[0:1] + cur * w[1:2] + dn * w[2:3]


def _for_row_blocks(nblocks, body, carry):
    assert nblocks >= 2
    carry = body(0, carry, True, False)
    inner = nblocks - 2
    carry = lax.fori_loop(1, nblocks - 1, lambda n1, c: body(n1, c, False, False), carry,
                          unroll=2 if inner % 2 == 0 and inner > 0 else 1)
    return body(nblocks - 1, carry, False, True)


def _hy_conv_kernel(p0_ref, p1_ref, pv_ref, w0_ref, w1_ref, wv_ref, b0_ref, b1_ref, bv_ref,
                    kh_ref, kn_ref, skip_ref, og_ref, g_ref, gi_ref, f3_ref, i1_ref,
                    o_ref, u_scr, ab_scr, *, L, CB, G):
    N = 2 * L
    RA = N // RB
    NH = RA // 2
    f3 = f3_ref[...]
    i1 = i1_ref[...]
    sign = _alt_sign((RB, CB))

    def gated(n1, first, last):
        return (_short_conv_rows(pv_ref, wv_ref, bv_ref, n1, first=first, last=last)
                * _short_conv_rows(p1_ref, w1_ref, b1_ref, n1, first=first, last=last))

    def fill(n1, s_alt, first, last):
        z = gated(n1, first, last)
        u_scr[pl.ds(pl.multiple_of(n1 * PITCH, SUBLANES), RB), :] = z
        return s_alt + z
    s_alt = _for_row_blocks(NH, fill, jnp.zeros((RB, CB), F32))
    z_nyq = jnp.sum(s_alt * sign, axis=0, keepdims=True)
    y_nyq = z_nyq * kn_ref[0:1, :]

    _fwd_step1(u_scr, g_ref, ab_scr, RA=RA, NR=NH, CB=CB, G=G)

    def mid(t, carry):
        kps = [t * G + j for j in range(G)]
        xs = [_fwd_step3_pair(ab_scr, f3, kp, RA=RA) for kp in kps]
        khs = [kh_ref[pl.ds(pl.multiple_of(kp * RB, RB), RB), :].astype(F32) for kp in kps]
        bs = []
        for x, kh in zip(xs, khs):
            xr, xi, kr, ki = x[:KB], x[KB:], kh[:KB], kh[KB:]
            y = jnp.concatenate([xr * kr - xi * ki, xr * ki + xi * kr], axis=0).astype(BF16)
            bs.append(jnp.dot(i1, y, preferred_element_type=F32))
        for kp, b in zip(kps, bs):
            for s in range(2):
                k1 = 2 * kp + s
                ab_scr[pl.ds(pl.multiple_of(k1 * PITCH, SUBLANES), RB), :] = b[:RB, s * CB:(s + 1) * CB]
                ab_scr[pl.ds(pl.multiple_of((RA + k1) * PITCH, SUBLANES), RB), :] = b[RB:, s * CB:(s + 1) * CB]
        return carry
    lax.fori_loop(0, RA // 2 // G, mid, 0)

    GL = 2 * G

    def last(t, carry):
        n2s = [t * GL + j for j in range(GL)]
        rhs = [jnp.concatenate([ab_scr[pl.ds(n2, RA, stride=PITCH), :],
                                ab_scr[pl.ds(RA * PITCH + n2, RA, stride=PITCH), :]], axis=0).astype(BF16)
               for n2 in n2s]
        ys = [jnp.dot(gi_ref[n2], r, preferred_element_type=F32) for n2, r in zip(n2s, rhs)]
        for n2, y in zip(n2s, ys):
            u_scr[pl.ds(n2, NH, stride=PITCH), :] = y
        return carry
    lax.fori_loop(0, RB // GL, last, 0)

    def finish(n1, carry, first, last):
        z = gated(n1, first, last)
        x0 = _short_conv_rows(p0_ref, w0_ref, b0_ref, n1, first=first, last=last)
        conv = u_scr[pl.ds(pl.multiple_of(n1 * PITCH, SUBLANES), RB), :] + sign * y_nyq
        y = (conv + z * skip_ref[...]) * x0
        y = y * lax.rsqrt(jnp.mean(y * y, axis=-1, keepdims=True) + EPS) * og_ref[...]
        o_ref[pl.ds(pl.multiple_of(n1 * RB, RB), RB), :] = y.astype(o_ref.dtype)
        return carry
    _for_row_blocks(NH, finish, 0)


def _hy_conv(p, conv_w, conv_b, khat, knyq, skip, out_g, tabs, *, L, CB=LANES, G=4):
    C = skip.shape[1]
    assert CB == C // HY_GROUPS, "one channel block must be exactly one norm group"
    N = 2 * L
    RA = N // RB
    NH = RA // 2
    nblk = C // CB
    g_tab, gi_tab, f3_tab, i1_tab = tabs["g_half"], tabs["gi"], tabs["f3"], tabs["i1"]
    col = lambda off: (lambda c: (0, off * nblk + c))
    pspec = lambda off: _single((L, CB), col(off))
    wspec = lambda off: pl.BlockSpec((3, CB), col(off))
    bspec = lambda off: pl.BlockSpec((1, CB), col(off))
    return pl.pallas_call(
        functools.partial(_hy_conv_kernel, L=L, CB=CB, G=G),
        grid=(nblk,),
        in_specs=[pspec(0), pspec(1), pspec(2),
                  wspec(0), wspec(1), wspec(2), bspec(0), bspec(1), bspec(2),
                  pl.BlockSpec((RA // 2 * RB, 2 * CB), col(0)),
                  pl.BlockSpec((SUBLANES, CB), col(0)),
                  bspec(0), bspec(0),
                  _single(g_tab.shape, lambda c: (0, 0, 0)),
                  _single(gi_tab.shape, lambda c: (0, 0, 0)),
                  _single(f3_tab.shape, lambda c: (0, 0)),
                  _single(i1_tab.shape, lambda c: (0, 0))],
        out_specs=pl.BlockSpec((L, CB), col(0)),
        out_shape=jax.ShapeDtypeStruct((L, C), BF16),
        scratch_shapes=[pltpu.VMEM((NH * PITCH, CB), F32),
                        pltpu.VMEM((2 * RA * PITCH, CB), F32)],
        compiler_params=_cparams(("arbitrary",)),
        name="hy_conv",
    )(p, p, p, conv_w, conv_w, conv_w, conv_b, conv_b, conv_b, khat, knyq, skip, out_g,
      g_tab, gi_tab, f3_tab, i1_tab)


def _log_sigmoid(x):
    return jnp.minimum(x, 0.0) - jnp.log1p(jnp.exp(-jnp.abs(x)))


def _gla_scan_blocks(dirs, *, TB):
    nch = TB // CHUNK
    rows = lambda c: slice(c * CHUNK, (c + 1) * CHUNK)
    units = [(d, c) for c in range(nch) for d in dirs]
    rr = lax.broadcasted_iota(jnp.int32, (CHUNK, CHUNK), 0)
    cc = lax.broadcasted_iota(jnp.int32, (CHUNK, CHUNK), 1)
    scale = GLA_DK ** -0.5

    for d in dirs:
        lr_hi, lr_lo = _split_hi_lo(d["lr"][...])
        gate_in = jnp.dot(jnp.concatenate([lr_hi, lr_hi, lr_lo], axis=1), d["wg"],
                          preferred_element_type=F32) + d["bg"]
        g = _log_sigmoid(gate_in) * (1.0 / GATE_TEMP)
        g_hi = g.astype(BF16)
        g_mid, g_lo = _split_hi_lo(g - g_hi.astype(F32))
        d["parts"][...] = jnp.concatenate([g_hi, g_mid, g_lo], axis=1)
        d["csum"] = jnp.where((cc >= rr) if d["reverse"] else (cc <= rr), 1.0, 0.0).astype(BF16)
        d["mask"] = (cc > rr) if d["reverse"] else (cc <= rr)
        d["edge"] = 0 if d["reverse"] else CHUNK - 1

    for d, c in units:
        b3 = jnp.dot(d["csum"], d["parts"][rows(c), :], preferred_element_type=F32)
        d["b"][rows(c), :] = b3[:, :GLA_DK] + b3[:, GLA_DK:2 * GLA_DK] + b3[:, 2 * GLA_DK:]

    decay = {}
    for d, c in units:
        b = d["b"][rows(c), :]
        b_edge = b[d["edge"]:d["edge"] + 1]
        k = d["k"][rows(c), :]
        d["qks"][0, rows(c), :] = (d["q"][rows(c), :] * scale * jnp.exp(b)).astype(BF16)
        d["qks"][1, rows(c), :] = (k * jnp.exp(-b)).astype(BF16)
        d["qks"][2, rows(c), :] = (k * jnp.exp(b_edge - b)).astype(BF16)
        d["vb"][rows(c), :] = d["v"][rows(c), :].astype(BF16)
        decay[(id(d), c)] = jnp.exp(b_edge)

    for d, c in units:
        att = _dot_nt(d["qks"][0, rows(c), :], d["qks"][1, rows(c), :])
        d["att"][rows(c), :] = jnp.where(d["mask"], att, 0.0).astype(BF16)

    for d, c in units:
        d["o"][rows(c), :] = jnp.dot(d["att"][rows(c), :], d["vb"][rows(c), :], preferred_element_type=F32)

    for d, c in units:
        d["kv"][c * GLA_DV:(c + 1) * GLA_DV, :] = _dot_tn(d["vb"][rows(c), :], d["qks"][2, rows(c), :])

    state = {id(d): d["s"][...] for d in dirs}
    for step in range(nch):
        for d in dirs:
            c = nch - 1 - step if d["reverse"] else step
            s_t = state[id(d)]
            d["o"][rows(c), :] += _dot_nt(d["qks"][0, rows(c), :], s_t.astype(BF16))
            state[id(d)] = s_t * decay[(id(d), c)] + d["kv"][c * GLA_DV:(c + 1) * GLA_DV, :]
    for d in dirs:
        d["s"][...] = state[id(d)]


def _gla_kernel(qf_ref, kf_ref, vf_ref, lf_ref, rf_ref, qb_ref, kb_ref, vb_ref, lb_ref, rb_ref,
                wg_ref, bg_ref, og_ref, o_ref,
                s_scr, parts_scr, b_scr, qks_scr, vb_scr, att_scr, kv_scr, ob_scr, half_scr, *, TB):
    s = pl.program_id(1)
    nb = pl.num_programs(1)

    @pl.when(s == 0)
    def _():
        s_scr[...] = jnp.zeros_like(s_scr)

    def direction(i, q, k, v, lr):
        return dict(q=q, k=k, v=v, lr=lr, wg=wg_ref[i], bg=bg_ref[i], reverse=bool(i), s=s_scr.at[i],
                    parts=parts_scr.at[i], b=b_scr.at[i], qks=qks_scr.at[i], vb=vb_scr.at[i],
                    att=att_scr.at[i], kv=kv_scr.at[i], o=ob_scr.at[i])
    _gla_scan_blocks([direction(0, qf_ref, kf_ref, vf_ref, lf_ref),
                      direction(1, qb_ref, kb_ref, vb_ref, lb_ref)], TB=TB)

    first_half = s < nb // 2
    for d, blk, r_ref in ((0, s, rf_ref), (1, nb - 1 - s, rb_ref)):
        rows = pl.ds(pl.multiple_of(blk * TB, TB), TB)

        @pl.when(first_half)
        def _():
            half_scr[rows, :] = ob_scr[d]

        @pl.when(jnp.logical_not(first_half))
        def _():
            tot = half_scr[rows, :] + ob_scr[d]
            tot = tot * lax.rsqrt(jnp.mean(tot * tot, axis=-1, keepdims=True) + EPS) * og_ref[...]
            o_ref[rows, :] = (tot * jax.nn.silu(r_ref[...])).astype(o_ref.dtype)


def _gla(p, p_lr, w_gate, b_gate, out_g, *, L, col0, TB=512):
    TB = min(TB, L)
    nb = L // TB
    assert nb % 2 == 0, "both scan directions must meet between two blocks"
    nch = TB // CHUNK
    qb = col0 // GLA_DK
    kb = qb + GLA_HEADS
    vb = (col0 + 2 * GLA_KW) // GLA_DV
    rb = vb + GLA_HEADS
    fwd = lambda s: s
    bwd = lambda s: nb - 1 - s

    def operands(blk):
        return [pl.BlockSpec((TB, GLA_DK), lambda h, s: (blk(s), qb + h)),
                pl.BlockSpec((TB, GLA_DK), lambda h, s: (blk(s), kb + h)),
                pl.BlockSpec((TB, GLA_DV), lambda h, s: (blk(s), vb + h)),
                pl.BlockSpec((TB, LANES), lambda h, s: (blk(s), 0)),
                pl.BlockSpec((TB, GLA_DV), lambda h, s: (blk(s), rb + h))]

    return pl.pallas_call(
        functools.partial(_gla_kernel, TB=TB),
        grid=(GLA_HEADS, nb),
        in_specs=operands(fwd) + operands(bwd) + [
            pl.BlockSpec((2, 3 * LANES, GLA_DK), lambda h, s: (0, 0, h)),
            pl.BlockSpec((2, 1, GLA_DK), lambda h, s: (0, 0, h)),
            pl.BlockSpec((1, GLA_DV), lambda h, s: (0, h)),
        ],
        out_specs=pl.BlockSpec((L, GLA_DV), lambda h, s: (0, h)),
        out_shape=jax.ShapeDtypeStruct((L, GLA_VW), BF16),
        scratch_shapes=[pltpu.VMEM((2, GLA_DV, GLA_DK), F32),
                        pltpu.VMEM((2, TB, 3 * GLA_DK), BF16),
                        pltpu.VMEM((2, TB, GLA_DK), F32),
                        pltpu.VMEM((2, 3, TB, GLA_DK), BF16),
                        pltpu.VMEM((2, TB, GLA_DV), BF16),
                        pltpu.VMEM((2, TB, CHUNK), BF16),
                        pltpu.VMEM((2, nch * GLA_DV, GLA_DK), F32),
                        pltpu.VMEM((2, TB, GLA_DV), F32),
                        pltpu.VMEM((L, GLA_DV), F32)],
        compiler_params=_cparams(("arbitrary", "arbitrary")),
        name="gla",
    )(*([p, p, p, p_lr, p] * 2), w_gate, b_gate, out_g)


def _filter_features(L):
    t = np.linspace(0.0, 1.0, L)[:, None]
    bands = (FILTER_EMB - 1) // 2
    freqs = np.linspace(1e-4, bands - 1, bands)[None, :]
    ang = (2.0 * np.pi / L) * np.arange(L)[:, None] * freqs
    feat = np.concatenate([t, np.cos(ang), -np.sin(ang)], axis=-1)
    feat_rev = np.roll(feat[::-1], 1, axis=0)
    both = np.zeros((L, LANES), np.float32)
    both[:, :FILTER_EMB] = feat
    both[:, FILTER_EMB:2 * FILTER_EMB] = feat_rev
    return jnp.asarray(both)


def _twice(w, rows_out):
    r, c = w.shape
    out = jnp.zeros((rows_out, 2 * c), F32)
    return out.at[:r, :c].set(w).at[r:2 * r, c:].set(w)


def _filter_deltas():
    min_decay = math.log(DECAY_TARGET) / LONG_DECAY_PCT
    max_decay = math.log(DECAY_TARGET) / SHORT_DECAY_PCT
    return jnp.abs(jnp.linspace(min_decay, max_decay, HY_WIDTH, dtype=F32)).reshape(1, HY_WIDTH)


def _block_diag_w4(w4):
    H = w4.shape[0]
    nblk = HY_WIDTH // LANES
    wf = w4[:, :HY_WIDTH].reshape(H, nblk, LANES)
    wb = w4[:, HY_WIDTH:].reshape(H, nblk, LANES)
    zz = jnp.zeros_like(wf)
    top = jnp.concatenate([wf, zz], axis=2)
    bot = jnp.concatenate([zz, wb], axis=2)
    return jnp.concatenate([top, bot], axis=0).reshape(2 * H, nblk * 2 * LANES)


def kernel(x, ffn1_norm, ffn1_w_gate, ffn1_w_up, ffn1_w_down, mix_norm, w_in, hy_conv_w, hy_conv_b, flt_w1, flt_b1, flt_f1, flt_w2, flt_b2, flt_f2, flt_w3, flt_b3, flt_f3, flt_w4, hy_skip, hy_out_norm, gla_w_a2_f, gla_b_a_f, gla_w_a2_b, gla_b_a_b, gla_out_norm, w_out, ffn2_norm, ffn2_w_gate, ffn2_w_up, ffn2_w_down, final_norm):
    B, L, D = x.shape
    depth = ffn1_norm.shape[0]
    tabs = _dft_tables(L)
    feat2 = _filter_features(L)
    deltas = _filter_deltas()
    gla_col0 = 3 * HY_WIDTH
    H2 = 2 * FILTER_HIDDEN
    both = lambda a: jnp.concatenate([a, a]).reshape(1, H2)
    filt = []
    for l in range(depth):
        early = [jnp.swapaxes(w_in[l], 0, 1)]
        if l == 0:
            early += [ffn1_w_gate[0], ffn1_w_up[0], ffn1_w_down[0]]
        hid3, cast = _filt_mlp(feat2, _twice(flt_w1[l], LANES), both(flt_b1[l]), both(flt_f1[l]),
                               _twice(flt_w2[l], H2), both(flt_b2[l]), both(flt_f2[l]),
                               _twice(flt_w3[l], H2), both(flt_b3[l]), both(flt_f3[l]), to_cast=early)
        w4cat = _three_pass_rows(_block_diag_w4(flt_w4[l]))
        khat, knyq = _filt_fft(hid3, w4cat, deltas, tabs["g_full"], tabs["f3"], L=L)
        filt.append((khat, knyq, cast))

    outs = []
    for bi in range(B):
        xb = x[bi]
        w1 = filt[0][2][1:]
        for l in range(depth):
            last = l == depth - 1
            khat, knyq, (w_in_t, *_) = filt[l]
            later = [ffn2_w_gate[l], ffn2_w_up[l], ffn2_w_down[l], w_out[l]]
            xb, (w2_gate, w2_up, w2_down, w_out_bf) = _ffn(xb, ffn1_norm[l], *w1, final_norm, final_norm=False,
                                                           to_cast=later)
            p, p_lr = _in_proj(xb, mix_norm[l], w_in_t)
            y_hy = _hy_conv(p, hy_conv_w[l], hy_conv_b[l].reshape(1, -1), khat, knyq,
                            hy_skip[l].reshape(1, -1), hy_out_norm[l].reshape(1, -1), tabs, L=L)

            w_gate = jnp.zeros((2, LANES, GLA_KW), F32)
            w_gate = w_gate.at[0, :GATE_RANK].set(gla_w_a2_f[l]).at[1, GATE_RANK:2 * GATE_RANK].set(gla_w_a2_b[l])
            w_gate = _three_pass_rows(w_gate)
            b_gate = jnp.stack([gla_b_a_f[l], gla_b_a_b[l]]).reshape(2, 1, GLA_KW)
            y_gla = _gla(p, p_lr, w_gate, b_gate, gla_out_norm[l].reshape(1, -1), L=L, col0=gla_col0)

            xb = _out_proj(xb, y_hy, y_gla, w_out_bf)
            nxt = [] if last else [ffn1_w_gate[l + 1], ffn1_w_up[l + 1], ffn1_w_down[l + 1]]
            xb, w1 = _ffn(xb, ffn2_norm[l], w2_gate, w2_up, w2_down, final_norm, final_norm=last, to_cast=nxt)
        outs.append(xb)
    return jnp.stack(outs)
```

```python
import functools
import math

import numpy as np
import jax
import jax.numpy as jnp
from jax import lax
from jax.experimental import pallas as pl
from jax.experimental.pallas import tpu as pltpu

F32 = jnp.float32
BF16 = jnp.bfloat16
HIGHEST = lax.Precision.HIGHEST

EPS = 1e-6
HY_WIDTH = 1024
HY_GROUPS = 8
FILTER_EMB = 33
FILTER_HIDDEN = 64
SHORT_DECAY_PCT = 0.3
LONG_DECAY_PCT = 1.5
DECAY_TARGET = 1e-2
GLA_HEADS = 4
GLA_DK = 128
GLA_DV = 256
GLA_KW = GLA_HEADS * GLA_DK
GLA_VW = GLA_HEADS * GLA_DV
GATE_RANK = 16
GATE_TEMP = 16.0
CHUNK = 64

LANES = 128
SUBLANES = 8
VMEM_LIMIT = 60 * 1024 * 1024
FFN_SLAB = 128
RB = 128
KB = RB // 2
PITCH = RB + SUBLANES


def _cparams(sem):
    return pltpu.CompilerParams(dimension_semantics=sem, vmem_limit_bytes=VMEM_LIMIT)


def _single(block_shape, index_map):
    return pl.BlockSpec(block_shape, index_map, pipeline_mode=pl.Buffered(1))


def _split_hi_lo(x):
    hi = x.astype(BF16)
    return hi, (x - hi.astype(F32)).astype(BF16)


def _three_pass_rows(w):
    w_hi, w_lo = _split_hi_lo(w)
    return jnp.concatenate([w_hi, w_lo, w_hi], axis=-2)


def _dot_nt(a, b):
    return lax.dot_general(a, b, (((1,), (1,)), ((), ())), preferred_element_type=F32)


def _dot_tn(a, b):
    return lax.dot_general(a, b, (((0,), (0,)), ((), ())), preferred_element_type=F32)


def _ffn_kernel(x_ref, g_ref, wg_ref, wu_ref, wd_ref, fg_ref, *rest, final_norm, n_cast):
    cast_src, o_ref, cast_dst, h_scr = rest[:n_cast], rest[n_cast], rest[n_cast + 1:-1], rest[-1]
    j = pl.program_id(1)

    slab = min(FFN_SLAB, x_ref.shape[0])
    nslab = x_ref.shape[0] // slab

    @pl.when(j == 0)
    def _():
        def norm_rows(t, carry):
            rows = pl.ds(pl.multiple_of(t * slab, slab), slab)
            x = x_ref[rows, :]
            r = lax.rsqrt(jnp.mean(x * x, axis=-1, keepdims=True) + EPS)
            h_scr[rows, :] = (x * r * g_ref[...]).astype(BF16)
            o_ref[rows, :] = jnp.zeros((slab, x_ref.shape[1]), F32)
            return carry
        lax.fori_loop(0, nslab, norm_rows, 0)

    h = h_scr[...]
    gate = jnp.dot(h, wg_ref[...], preferred_element_type=F32)
    up = jnp.dot(h, wu_ref[...], preferred_element_type=F32)
    a = (jax.nn.silu(gate) * up).astype(BF16)
    o_ref[...] += jnp.dot(a, wd_ref[...], preferred_element_type=F32)

    for src, dst in zip(cast_src, cast_dst):
        dst[...] = src[...].astype(BF16)

    @pl.when(j == pl.num_programs(1) - 1)
    def _():
        def finish_rows(t, carry):
            rows = pl.ds(pl.multiple_of(t * slab, slab), slab)
            y = x_ref[rows, :] + 0.5 * o_ref[rows, :]
            if final_norm:
                r = lax.rsqrt(jnp.mean(y * y, axis=-1, keepdims=True) + EPS)
                y = y * r * fg_ref[...]
            o_ref[rows, :] = y
            return carry
        lax.fori_loop(0, nslab, finish_rows, 0)


def _cast_blocking(shape, ni, nj):
    rows, cols = shape
    assert rows % ni == 0 and (rows // ni) % SUBLANES == 0 and cols % LANES == 0
    ncol = max(d for d in range(1, nj + 1) if (cols // LANES) % d == 0)
    return (rows // ni, cols // ncol), (lambda i, j: (i, jnp.minimum(j, ncol - 1)))


def _ffn(x, norm_g, w_gate, w_up, w_down, final_g, *, final_norm, to_cast=(), tm=1024, tf=512):
    L, D = x.shape
    DF = w_gate.shape[1]
    tm = min(tm, L)
    tf = min(tf, DF)
    ni, nj = L // tm, DF // tf
    blockings = [_cast_blocking(w.shape, ni, nj) for w in to_cast]
    cast_specs = [pl.BlockSpec(blk, imap) for blk, imap in blockings]
    outs = pl.pallas_call(
        functools.partial(_ffn_kernel, final_norm=final_norm, n_cast=len(to_cast)),
        grid=(ni, nj),
        in_specs=[
            pl.BlockSpec((tm, D), lambda i, j: (i, 0)),
            pl.BlockSpec((1, D), lambda i, j: (0, 0)),
            pl.BlockSpec((D, tf), lambda i, j: (0, j)),
            pl.BlockSpec((D, tf), lambda i, j: (0, j)),
            pl.BlockSpec((tf, D), lambda i, j: (j, 0)),
            pl.BlockSpec((1, D), lambda i, j: (0, 0)),
        ] + cast_specs,
        out_specs=[pl.BlockSpec((tm, D), lambda i, j: (i, 0))] + cast_specs,
        out_shape=[jax.ShapeDtypeStruct((L, D), F32)] + [jax.ShapeDtypeStruct(w.shape, BF16) for w in to_cast],
        scratch_shapes=[pltpu.VMEM((tm, D), BF16)],
        compiler_params=_cparams(("arbitrary", "arbitrary")),
        name="ffn",
    )(x, norm_g.reshape(1, D), w_gate, w_up, w_down, final_g.reshape(1, D), *to_cast)
    return outs[0], outs[1:]


def _in_proj_kernel(x_ref, g_ref, wt_ref, wtail_ref, o_ref, otail_ref, h_scr, *, n_tail):
    j = pl.program_id(1)

    @pl.when(j == 0)
    def _():
        x = x_ref[...]
        r = lax.rsqrt(jnp.mean(x * x, axis=-1, keepdims=True) + EPS)
        h = (x * r * g_ref[...]).astype(BF16)
        h_scr[...] = h
        tail = _dot_nt(h, wtail_ref[...])
        col = lax.broadcasted_iota(jnp.int32, tail.shape, 1)
        otail_ref[...] = jnp.where(col < n_tail, tail, 0.0)

    o_ref[...] = _dot_nt(h_scr[...], wt_ref[...])


def _in_proj(x, norm_g, w_t, *, tm=1024, tn=1024):
    L, D = x.shape
    n_cols = w_t.shape[0]
    nj = n_cols // tn
    n_tail = n_cols - nj * tn
    assert 0 < n_tail <= LANES and (nj * tn) % LANES == 0
    tm = min(tm, L)
    return pl.pallas_call(
        functools.partial(_in_proj_kernel, n_tail=n_tail),
        grid=(L // tm, nj),
        in_specs=[
            pl.BlockSpec((tm, D), lambda i, j: (i, 0)),
            pl.BlockSpec((1, D), lambda i, j: (0, 0)),
            pl.BlockSpec((tn, D), lambda i, j: (j, 0)),
            pl.BlockSpec((LANES, D), lambda i, j: (nj * tn // LANES, 0)),
        ],
        out_specs=[pl.BlockSpec((tm, tn), lambda i, j: (i, j)),
                   pl.BlockSpec((tm, LANES), lambda i, j: (i, 0))],
        out_shape=[jax.ShapeDtypeStruct((L, nj * tn), F32),
                   jax.ShapeDtypeStruct((L, LANES), F32)],
        scratch_shapes=[pltpu.VMEM((tm, D), BF16)],
        compiler_params=_cparams(("parallel", "arbitrary")),
        name="in_proj",
    )(x, norm_g.reshape(1, D), w_t, w_t)


def _out_proj_kernel(x_ref, yh_ref, yg_ref, wh_ref, wg_ref, o_ref):
    o_ref[...] = (x_ref[...]
                  + jnp.dot(yh_ref[...], wh_ref[...], preferred_element_type=F32)
                  + jnp.dot(yg_ref[...], wg_ref[...], preferred_element_type=F32))


def _out_proj(x, y_hy, y_gla, w_out, *, tm=512):
    L, D = x.shape
    WH = y_hy.shape[1]
    WG = y_gla.shape[1]
    tm = min(tm, L)
    w = w_out
    return pl.pallas_call(
        _out_proj_kernel,
        grid=(L // tm,),
        in_specs=[
            pl.BlockSpec((tm, D), lambda i: (i, 0)),
            pl.BlockSpec((tm, WH), lambda i: (i, 0)),
            pl.BlockSpec((tm, WG), lambda i: (i, 0)),
            _single((WH, D), lambda i: (0, 0)),
            _single((WG, D), lambda i: (0, 0)),
        ],
        out_specs=pl.BlockSpec((tm, D), lambda i: (i, 0)),
        out_shape=jax.ShapeDtypeStruct((L, D), F32),
        compiler_params=_cparams(("parallel",)),
        name="out_proj",
    )(x, y_hy, y_gla, w[:WH], w[WH:])


def _filt_mlp_kernel(z_ref, w1_ref, b1_ref, f1_ref, w2_ref, b2_ref, f2_ref, w3_ref, b3_ref, f3_ref, *rest, n_cast):
    cast_src, o_ref, cast_dst = rest[:n_cast], rest[n_cast], rest[n_cast + 1:]
    dot = functools.partial(jnp.dot, precision=HIGHEST, preferred_element_type=F32)
    hid = jnp.sin(f1_ref[...] * (dot(z_ref[...], w1_ref[...]) + b1_ref[...]))
    hid = jnp.sin(f2_ref[...] * (dot(hid, w2_ref[...]) + b2_ref[...]))
    hid = jnp.sin(f3_ref[...] * (dot(hid, w3_ref[...]) + b3_ref[...]))
    hi, lo = _split_hi_lo(hid)
    o_ref[...] = jnp.concatenate([hi, hi, lo], axis=1)
    for src, dst in zip(cast_src, cast_dst):
        dst[...] = src[...].astype(BF16)


def _cast_row_blocks(shape, n):
    rows, cols = shape
    tile = 2 * SUBLANES
    block_rows = pl.cdiv(pl.cdiv(rows, n), tile) * tile
    nblocks = pl.cdiv(rows, block_rows)
    return (block_rows, cols), (lambda i: (jnp.minimum(i, nblocks - 1), 0))


def _filt_mlp(feat2, w1, b1, f1, w2, b2, f2, w3, b3, f3, *, to_cast=(), tm=256):
    L, FP = feat2.shape
    H2 = w2.shape[0]
    tm = min(tm, L)
    steps = L // tm
    full = lambda shp: pl.BlockSpec(shp, lambda i: (0, 0))
    cast_specs = [pl.BlockSpec(*_cast_row_blocks(w.shape, steps)) for w in to_cast]
    outs = pl.pallas_call(
        functools.partial(_filt_mlp_kernel, n_cast=len(to_cast)),
        grid=(steps,),
        in_specs=[pl.BlockSpec((tm, FP), lambda i: (i, 0)),
                  full((FP, H2)), full((1, H2)), full((1, H2)),
                  full((H2, H2)), full((1, H2)), full((1, H2)),
                  full((H2, H2)), full((1, H2)), full((1, H2))] + cast_specs,
        out_specs=[pl.BlockSpec((tm, 3 * H2), lambda i: (i, 0))] + cast_specs,
        out_shape=[jax.ShapeDtypeStruct((L, 3 * H2), BF16)] + [jax.ShapeDtypeStruct(w.shape, BF16) for w in to_cast],
        compiler_params=_cparams(("arbitrary",)),
        name="filt_mlp",
    )(feat2, w1, b1, f1, w2, b2, f2, w3, b3, f3, *to_cast)
    return outs[0], outs[1:]


def _dft_tables(L):
    N = 2 * L
    RA = N // RB
    NH = RA // 2
    two_pi = 2.0 * np.pi
    k1 = np.arange(RA, dtype=np.int64)

    def step1_table(NR):
        g = np.zeros((RB // 2, 2 * RA, 2 * NR), np.float64)
        for s in range(2):
            n = RB * np.arange(NR, dtype=np.int64)[None, :] + (2 * np.arange(RB // 2, dtype=np.int64) + s)[:, None]
            ang = two_pi * ((k1[None, :, None] * n[:, None, :]) % N) / N
            g[:, :RA, s * NR:(s + 1) * NR] = np.cos(ang)
            g[:, RA:, s * NR:(s + 1) * NR] = -np.sin(ang)
        return g

    phi = two_pi * ((np.arange(KB)[:, None] * np.arange(RB)[None, :]) % RB) / RB
    f3 = np.block([[np.cos(phi), np.sin(phi)], [-np.sin(phi), np.cos(phi)]])
    i1 = np.block([[np.cos(phi.T), -np.sin(phi.T)], [np.sin(phi.T), np.cos(phi.T)]])
    cast = lambda a: jnp.asarray(a.astype(np.float32)).astype(BF16)
    return dict(g_half=cast(step1_table(NH)), g_full=cast(step1_table(RA)), f3=cast(f3), i1=cast(i1))


def _fwd_step1(u_ref, g_ref, ab_ref, *, RA, NR, CB, G=2):
    def body(t, carry):
        ms = [t * G + j for j in range(G)]
        rhs = []
        for m in ms:
            ua = u_ref[pl.ds(2 * m, NR, stride=PITCH), :]
            ub = u_ref[pl.ds(2 * m + 1, NR, stride=PITCH), :]
            zz = jnp.zeros_like(ua)
            rhs.append(jnp.concatenate([jnp.concatenate([ua, zz], axis=1),
                                        jnp.concatenate([zz, ub], axis=1)], axis=0).astype(BF16))
        res = [jnp.dot(g_ref[m], r, preferred_element_type=F32) for m, r in zip(ms, rhs)]
        for m, r in zip(ms, res):
            ab_ref[pl.ds(2 * m, RA, stride=PITCH), :] = r[:RA, :CB]
            ab_ref[pl.ds(RA * PITCH + 2 * m, RA, stride=PITCH), :] = r[RA:, :CB]
            ab_ref[pl.ds(2 * m + 1, RA, stride=PITCH), :] = r[:RA, CB:]
            ab_ref[pl.ds(RA * PITCH + 2 * m + 1, RA, stride=PITCH), :] = r[RA:, CB:]
        return carry
    lax.fori_loop(0, RB // 2 // G, body, 0)


def _fwd_step3_pair(ab_ref, f3, kp, *, RA):
    cols = []
    for s in range(2):
        k1 = 2 * kp + s
        ar = ab_ref[pl.ds(pl.multiple_of(k1 * PITCH, SUBLANES), RB), :]
        ai = ab_ref[pl.ds(pl.multiple_of((RA + k1) * PITCH, SUBLANES), RB), :]
        cols.append(jnp.concatenate([ar, ai], axis=0))
    rhs = jnp.concatenate(cols, axis=1).astype(BF16)
    return jnp.dot(f3, rhs, preferred_element_type=F32)


def _alt_sign(shape):
    rows = lax.broadcasted_iota(jnp.int32, shape, 0)
    return jnp.where(rows % 2 == 0, 1.0, -1.0).astype(F32)


def _filt_fft_kernel(hid_ref, w4_ref, dl_ref, g_ref, f3_ref, kh_ref, kn_ref, u_scr, ab_scr, *, L, CB, G):
    N = 2 * L
    RA = N // RB
    NH = RA // 2
    f3 = f3_ref[...]
    inv_lm1 = 1.0 / (L - 1)

    def fill(n1, carry):
        s_abs, s_alt = carry
        r0 = pl.multiple_of(n1 * RB, RB)
        h2 = jnp.dot(hid_ref[pl.ds(r0, RB), :], w4_ref[...], preferred_element_type=F32)
        pos = lax.broadcasted_iota(jnp.int32, (RB, CB), 0) + n1 * RB
        hf = h2[:, :CB] * jnp.exp(-(pos.astype(F32) * inv_lm1) * dl_ref[...])
        hb = h2[:, CB:] * jnp.exp(-((L - pos).astype(F32) * inv_lm1) * dl_ref[...])
        hb = jnp.where(pos == 0, 0.0, hb)
        u_scr[pl.ds(pl.multiple_of(n1 * PITCH, SUBLANES), RB), :] = hf
        u_scr[pl.ds(pl.multiple_of((NH + n1) * PITCH, SUBLANES), RB), :] = hb
        return s_abs + (jnp.abs(hf) + jnp.abs(hb)), s_alt + (hf + hb)
    zero = jnp.zeros((RB, CB), F32)
    s_abs, s_alt = lax.fori_loop(0, NH, fill, (zero, zero), unroll=2)
    inv_l1 = 1.0 / jnp.sum(s_abs, axis=0, keepdims=True)
    kn = jnp.sum(s_alt * _alt_sign((RB, CB)), axis=0, keepdims=True) * inv_l1 * (1.0 / N)
    kn_ref[...] = jnp.broadcast_to(kn, kn_ref.shape)

    _fwd_step1(u_scr, g_ref, ab_scr, RA=RA, NR=RA, CB=CB, G=G)

    scale = inv_l1 * (2.0 / N)
    scale2 = jnp.concatenate([scale, scale], axis=1)
    row2 = lax.broadcasted_iota(jnp.int32, (KB, 2 * CB), 0)
    lane2 = lax.broadcasted_iota(jnp.int32, (KB, 2 * CB), 1)
    dc_pos = jnp.logical_and(row2 == 0, lane2 < CB)

    def emit(t, carry):
        kps = [t * G + j for j in range(G)]
        xs = [_fwd_step3_pair(ab_scr, f3, kp, RA=RA) for kp in kps]
        for kp, x in zip(kps, xs):
            kr = x[:KB] * scale2
            ki = x[KB:] * scale2
            dc = jnp.logical_and(dc_pos, kp == 0)
            kr = jnp.where(dc, 0.5 * kr, kr)
            ki = jnp.where(dc, 0.5 * ki, ki)
            kh_ref[pl.ds(pl.multiple_of(kp * RB, RB), RB), :] = jnp.concatenate([kr, ki], axis=0).astype(kh_ref.dtype)
        return carry
    lax.fori_loop(0, RA // 2 // G, emit, 0)


def _filt_fft(hid2, w4bd, deltas, g_tab, f3_tab, *, L, CB=LANES, G=4):
    C = deltas.shape[1]
    H2 = hid2.shape[1]
    N = 2 * L
    RA = N // RB
    nblk = C // CB
    return pl.pallas_call(
        functools.partial(_filt_fft_kernel, L=L, CB=CB, G=G),
        grid=(nblk,),
        in_specs=[
            _single((L, H2), lambda c: (0, 0)),
            pl.BlockSpec((H2, 2 * CB), lambda c: (0, c)),
            pl.BlockSpec((1, CB), lambda c: (0, c)),
            _single(g_tab.shape, lambda c: (0, 0, 0)),
            _single(f3_tab.shape, lambda c: (0, 0)),
        ],
        out_specs=[pl.BlockSpec((RA // 2 * RB, 2 * CB), lambda c: (0, c)),
                   pl.BlockSpec((SUBLANES, CB), lambda c: (0, c))],
        out_shape=[jax.ShapeDtypeStruct((RA // 2 * RB, 2 * C), BF16),
                   jax.ShapeDtypeStruct((SUBLANES, C), F32)],
        scratch_shapes=[pltpu.VMEM((RA * PITCH, CB), F32),
                        pltpu.VMEM((2 * RA * PITCH, CB), F32)],
        compiler_params=_cparams(("arbitrary",)),
        name="filt_fft",
    )(hid2, w4bd, deltas, g_tab, f3_tab)


def _short_conv_rows(p_ref, w_ref, b_ref, n1, *, first, last):
    r0 = pl.multiple_of(n1 * RB, RB)
    cur = p_ref[pl.ds(r0, RB), :]
    rows = lax.broadcasted_iota(jnp.int32, cur.shape, 0)
    if first:
        up = jnp.where(rows == 0, 0.0, pltpu.roll(cur, 1, axis=0))
    else:
        up = p_ref[pl.ds(r0 - 1, RB), :]
    if last:
        dn = jnp.where(rows == RB - 1, 0.0, pltpu.roll(cur, RB - 1, axis=0))
    else:
        dn = p_ref[pl.ds(r0 + 1, RB), :]
    w = w_ref[...]
    return b_ref[...] + up * w[0:1] + cur * w[1:2] + dn * w[2:3]


def _for_row_blocks(nblocks, body, carry):
    assert nblocks >= 2
    carry = body(0, carry, True, False)
    inner = nblocks - 2
    carry = lax.fori_loop(1, nblocks - 1, lambda n1, c: body(n1, c, False, False), carry,
                          unroll=2 if inner % 2 == 0 and inner > 0 else 1)
    return body(nblocks - 1, carry, False, True)


def _hy_conv_kernel(p_hbm, w0_ref, w1_ref, wv_ref, b0_ref, b1_ref, bv_ref,
                    kh_ref, kn_ref, skip_ref, og_ref, g_ref, f3_ref, i1_ref,
                    o_ref, u_scr, ab_scr, z_scr, p_scr, p_sem, *, L, CB, G):
    N = 2 * L
    RA = N // RB
    NH = RA // 2
    f3 = f3_ref[...]
    i1 = i1_ref[...]
    sign = _alt_sign((RB, CB))
    c = pl.program_id(0)
    nblk = pl.num_programs(0)
    p0_ref, p1_ref, pv_ref = p_scr.at[0], p_scr.at[1], p_scr.at[2]

    def fetch(group, blk):
        col = pl.multiple_of((group * nblk + blk) * CB, CB)
        return pltpu.make_async_copy(p_hbm.at[:, pl.ds(col, CB)], p_scr.at[group], p_sem.at[group])

    @pl.when(c == 0)
    def _():
        fetch(1, c).start()
        fetch(2, c).start()

    fetch(0, c).start()
    fetch(1, c).wait()
    fetch(2, c).wait()

    def fill(n1, s_alt, first, last):
        z = (_short_conv_rows(pv_ref, wv_ref, bv_ref, n1, first=first, last=last)
             * _short_conv_rows(p1_ref, w1_ref, b1_ref, n1, first=first, last=last))
        u_scr[pl.ds(pl.multiple_of(n1 * PITCH, SUBLANES), RB), :] = z
        z_scr[pl.ds(pl.multiple_of(n1 * RB, RB), RB), :] = z
        return s_alt + z
    s_alt = _for_row_blocks(NH, fill, jnp.zeros((RB, CB), F32))

    @pl.when(c + 1 < nblk)
    def _():
        fetch(1, c + 1).start()
        fetch(2, c + 1).start()

    z_nyq = jnp.sum(s_alt * sign, axis=0, keepdims=True)
    y_nyq = z_nyq * kn_ref[0:1, :]

    _fwd_step1(u_scr, g_ref, ab_scr, RA=RA, NR=NH, CB=CB, G=G)

    def mid(t, carry):
        kps = [t * G + j for j in range(G)]
        xs = [_fwd_step3_pair(ab_scr, f3, kp, RA=RA) for kp in kps]
        khs = [kh_ref[pl.ds(pl.multiple_of(kp * RB, RB), RB), :].astype(F32) for kp in kps]
        bs = []
        for x, kh in zip(xs, khs):
            xr, xi, kr, ki = x[:KB], x[KB:], kh[:KB], kh[KB:]
            y = jnp.concatenate([xr * kr - xi * ki, xr * ki + xi * kr], axis=0).astype(BF16)
            bs.append(jnp.dot(i1, y, preferred_element_type=F32))
        for kp, b in zip(kps, bs):
            for s in range(2):
                k1 = 2 * kp + s
                ab_scr[pl.ds(pl.multiple_of(k1 * PITCH, SUBLANES), RB), :] = b[:RB, s * CB:(s + 1) * CB]
                ab_scr[pl.ds(pl.multiple_of((RA + k1) * PITCH, SUBLANES), RB), :] = b[RB:, s * CB:(s + 1) * CB]
        return carry
    lax.fori_loop(0, RA // 2 // G, mid, 0)

    def last(t, carry):
        ms = [t * G + j for j in range(G)]
        rhs = [jnp.concatenate(
            [jnp.concatenate([ab_scr[pl.ds(2 * m + s, RA, stride=PITCH), :],
                              ab_scr[pl.ds(RA * PITCH + 2 * m + s, RA, stride=PITCH), :]], axis=0)
             for s in range(2)], axis=1).astype(BF16) for m in ms]
        ys = [_dot_tn(g_ref[m], r) for m, r in zip(ms, rhs)]
        for m, y in zip(ms, ys):
            u_scr[pl.ds(2 * m, NH, stride=PITCH), :] = y[:NH, :CB]
            u_scr[pl.ds(2 * m + 1, NH, stride=PITCH), :] = y[NH:, CB:]
        return carry
    lax.fori_loop(0, RB // 2 // G, last, 0)

    nyq_rows = sign * y_nyq
    fetch(0, c).wait()

    def finish(n1, carry, first, last):
        z = z_scr[pl.ds(pl.multiple_of(n1 * RB, RB), RB), :]
        x0 = _short_conv_rows(p0_ref, w0_ref, b0_ref, n1, first=first, last=last)
        conv = u_scr[pl.ds(pl.multiple_of(n1 * PITCH, SUBLANES), RB), :] + nyq_rows
        y = (conv + z * skip_ref[...]) * x0
        y = y * lax.rsqrt(jnp.mean(y * y, axis=-1, keepdims=True) + EPS) * og_ref[...]
        o_ref[pl.ds(pl.multiple_of(n1 * RB, RB), RB), :] = y.astype(o_ref.dtype)
        return carry
    _for_row_blocks(NH, finish, 0)


def _hy_conv(p, conv_w, conv_b, khat, knyq, skip, out_g, tabs, *, L, CB=LANES, G=4):
    C = skip.shape[1]
    assert CB == C // HY_GROUPS, "one channel block must be exactly one norm group"
    N = 2 * L
    RA = N // RB
    NH = RA // 2
    nblk = C // CB
    g_tab, f3_tab, i1_tab = tabs["g_half"], tabs["f3"], tabs["i1"]
    col = lambda off: (lambda c: (0, off * nblk + c))
    wspec = lambda off: pl.BlockSpec((3, CB), col(off))
    bspec = lambda off: pl.BlockSpec((1, CB), col(off))
    return pl.pallas_call(
        functools.partial(_hy_conv_kernel, L=L, CB=CB, G=G),
        grid=(nblk,),
        in_specs=[pl.BlockSpec(memory_space=pl.ANY),
                  wspec(0), wspec(1), wspec(2), bspec(0), bspec(1), bspec(2),
                  pl.BlockSpec((RA // 2 * RB, 2 * CB), col(0)),
                  pl.BlockSpec((SUBLANES, CB), col(0)),
                  bspec(0), bspec(0),
                  _single(g_tab.shape, lambda c: (0, 0, 0)),
                  _single(f3_tab.shape, lambda c: (0, 0)),
                  _single(i1_tab.shape, lambda c: (0, 0))],
        out_specs=pl.BlockSpec((L, CB), col(0)),
        out_shape=jax.ShapeDtypeStruct((L, C), BF16),
        scratch_shapes=[pltpu.VMEM((NH * PITCH, CB), F32),
                        pltpu.VMEM((2 * RA * PITCH, CB), F32),
                        pltpu.VMEM((L, CB), F32),
                        pltpu.VMEM((3, L, CB), F32),
                        pltpu.SemaphoreType.DMA((3,))],
        compiler_params=_cparams(("arbitrary",)),
        name="hy_conv",
    )(p, conv_w, conv_w, conv_w, conv_b, conv_b, conv_b, khat, knyq, skip, out_g,
      g_tab, f3_tab, i1_tab)


def _log_sigmoid(x):
    return jnp.minimum(x, 0.0) - jnp.log1p(jnp.exp(-jnp.abs(x)))


def _gla_scan_blocks(dirs, *, TB):
    nch = TB // CHUNK
    rows = lambda c: slice(c * CHUNK, (c + 1) * CHUNK)
    units = [(d, c) for c in range(nch) for d in dirs]
    rr = lax.broadcasted_iota(jnp.int32, (CHUNK, CHUNK), 0)
    cc = lax.broadcasted_iota(jnp.int32, (CHUNK, CHUNK), 1)
    scale = GLA_DK ** -0.5

    for d in dirs:
        lr_hi, lr_lo = _split_hi_lo(d["lr"][...])
        gate_in = jnp.dot(jnp.concatenate([lr_hi, lr_hi, lr_lo], axis=1), d["wg"],
                          preferred_element_type=F32) + d["bg"]
        g = _log_sigmoid(gate_in) * (1.0 / GATE_TEMP)
        g_hi = g.astype(BF16)
        g_mid, g_lo = _split_hi_lo(g - g_hi.astype(F32))
        d["parts"][...] = jnp.concatenate([g_hi, g_mid, g_lo], axis=1)
        d["csum"] = jnp.where((cc >= rr) if d["reverse"] else (cc <= rr), 1.0, 0.0).astype(BF16)
        d["mask"] = (cc > rr) if d["reverse"] else (cc <= rr)
        d["edge"] = 0 if d["reverse"] else CHUNK - 1

    for d, c in units:
        b3 = jnp.dot(d["csum"], d["parts"][rows(c), :], preferred_element_type=F32)
        d["b"][rows(c), :] = b3[:, :GLA_DK] + b3[:, GLA_DK:2 * GLA_DK] + b3[:, 2 * GLA_DK:]

    decay = {}
    for d, c in units:
        b = d["b"][rows(c), :]
        b_edge = b[d["edge"]:d["edge"] + 1]
        k = d["k"][rows(c), :]
        d["qks"][0, rows(c), :] = (d["q"][rows(c), :] * scale * jnp.exp(b)).astype(BF16)
        d["qks"][1, rows(c), :] = (k * jnp.exp(-b)).astype(BF16)
        d["qks"][2, rows(c), :] = (k * jnp.exp(b_edge - b)).astype(BF16)
        d["vb"][rows(c), :] = d["v"][rows(c), :].astype(BF16)
        decay[(id(d), c)] = jnp.exp(b_edge)

    for d, c in units:
        att = _dot_nt(d["qks"][0, rows(c), :], d["qks"][1, rows(c), :])
        d["att"][rows(c), :] = jnp.where(d["mask"], att, 0.0).astype(BF16)

    for d, c in units:
        d["o"][rows(c), :] = jnp.dot(d["att"][rows(c), :], d["vb"][rows(c), :], preferred_element_type=F32)

    for d, c in units:
        d["kv"][c * GLA_DV:(c + 1) * GLA_DV, :] = _dot_tn(d["vb"][rows(c), :], d["qks"][2, rows(c), :])

    state = {id(d): d["s"][...] for d in dirs}
    for step in range(nch):
        for d in dirs:
            c = nch - 1 - step if d["reverse"] else step
            s_t = state[id(d)]
            d["o"][rows(c), :] += _dot_nt(d["qks"][0, rows(c), :], s_t.astype(BF16))
            state[id(d)] = s_t * decay[(id(d), c)] + d["kv"][c * GLA_DV:(c + 1) * GLA_DV, :]
    for d in dirs:
        d["s"][...] = state[id(d)]


def _gla_kernel(qf_ref, kf_ref, vf_ref, lf_ref, rf_ref, qb_ref, kb_ref, vb_ref, lb_ref, rb_ref,
                wg_ref, bg_ref, og_ref, o_ref,
                s_scr, parts_scr, b_scr, qks_scr, vb_scr, att_scr, kv_scr, ob_scr, half_scr, *, TB):
    s = pl.program_id(1)
    nb = pl.num_programs(1)

    @pl.when(s == 0)
    def _():
        s_scr[...] = jnp.zeros_like(s_scr)

    def direction(i, q, k, v, lr):
        return dict(q=q, k=k, v=v, lr=lr, wg=wg_ref[i], bg=bg_ref[i], reverse=bool(i), s=s_scr.at[i],
                    parts=parts_scr.at[i], b=b_scr.at[i], qks=qks_scr.at[i], vb=vb_scr.at[i],
                    att=att_scr.at[i], kv=kv_scr.at[i], o=ob_scr.at[i])
    _gla_scan_blocks([direction(0, qf_ref, kf_ref, vf_ref, lf_ref),
                      direction(1, qb_ref, kb_ref, vb_ref, lb_ref)], TB=TB)

    first_half = s < nb // 2
    for d, blk, r_ref in ((0, s, rf_ref), (1, nb - 1 - s, rb_ref)):
        rows = pl.ds(pl.multiple_of(blk * TB, TB), TB)

        @pl.when(first_half)
        def _():
            half_scr[rows, :] = ob_scr[d]

        @pl.when(jnp.logical_not(first_half))
        def _():
            tot = half_scr[rows, :] + ob_scr[d]
            tot = tot * lax.rsqrt(jnp.mean(tot * tot, axis=-1, keepdims=True) + EPS) * og_ref[...]
            o_ref[rows, :] = (tot * jax.nn.silu(r_ref[...])).astype(o_ref.dtype)


def _gla(p, p_lr, w_gate, b_gate, out_g, *, L, col0, TB=512):
    TB = min(TB, L)
    nb = L // TB
    assert nb % 2 == 0, "both scan directions must meet between two blocks"
    nch = TB // CHUNK
    qb = col0 // GLA_DK
    kb = qb + GLA_HEADS
    vb = (col0 + 2 * GLA_KW) // GLA_DV
    rb = vb + GLA_HEADS
    fwd = lambda s: s
    bwd = lambda s: nb - 1 - s

    def operands(blk):
        return [pl.BlockSpec((TB, GLA_DK), lambda h, s: (blk(s), qb + h)),
                pl.BlockSpec((TB, GLA_DK), lambda h, s: (blk(s), kb + h)),
                pl.BlockSpec((TB, GLA_DV), lambda h, s: (blk(s), vb + h)),
                pl.BlockSpec((TB, LANES), lambda h, s: (blk(s), 0)),
                pl.BlockSpec((TB, GLA_DV), lambda h, s: (blk(s), rb + h))]

    return pl.pallas_call(
        functools.partial(_gla_kernel, TB=TB),
        grid=(GLA_HEADS, nb),
        in_specs=operands(fwd) + operands(bwd) + [
            pl.BlockSpec((2, 3 * LANES, GLA_DK), lambda h, s: (0, 0, h)),
            pl.BlockSpec((2, 1, GLA_DK), lambda h, s: (0, 0, h)),
            pl.BlockSpec((1, GLA_DV), lambda h, s: (0, h)),
        ],
        out_specs=pl.BlockSpec((L, GLA_DV), lambda h, s: (0, h)),
        out_shape=jax.ShapeDtypeStruct((L, GLA_VW), BF16),
        scratch_shapes=[pltpu.VMEM((2, GLA_DV, GLA_DK), F32),
                        pltpu.VMEM((2, TB, 3 * GLA_DK), BF16),
                        pltpu.VMEM((2, TB, GLA_DK), F32),
                        pltpu.VMEM((2, 3, TB, GLA_DK), BF16),
                        pltpu.VMEM((2, TB, GLA_DV), BF16),
                        pltpu.VMEM((2, TB, CHUNK), BF16),
                        pltpu.VMEM((2, nch * GLA_DV, GLA_DK), F32),
                        pltpu.VMEM((2, TB, GLA_DV), F32),
                        pltpu.VMEM((L, GLA_DV), F32)],
        compiler_params=_cparams(("arbitrary", "arbitrary")),
        name="gla",
    )(*([p, p, p, p_lr, p] * 2), w_gate, b_gate, out_g)


def _filter_features(L):
    t = np.linspace(0.0, 1.0, L)[:, None]
    bands = (FILTER_EMB - 1) // 2
    freqs = np.linspace(1e-4, bands - 1, bands)[None, :]
    ang = (2.0 * np.pi / L) * np.arange(L)[:, None] * freqs
    feat = np.concatenate([t, np.cos(ang), -np.sin(ang)], axis=-1)
    feat_rev = np.roll(feat[::-1], 1, axis=0)
    both = np.zeros((L, LANES), np.float32)
    both[:, :FILTER_EMB] = feat
    both[:, FILTER_EMB:2 * FILTER_EMB] = feat_rev
    return jnp.asarray(both)


def _twice(w, rows_out):
    r, c = w.shape
    out = jnp.zeros((rows_out, 2 * c), F32)
    return out.at[:r, :c].set(w).at[r:2 * r, c:].set(w)


def _filter_deltas():
    min_decay = math.log(DECAY_TARGET) / LONG_DECAY_PCT
    max_decay = math.log(DECAY_TARGET) / SHORT_DECAY_PCT
    return jnp.abs(jnp.linspace(min_decay, max_decay, HY_WIDTH, dtype=F32)).reshape(1, HY_WIDTH)


def _block_diag_w4(w4):
    H = w4.shape[0]
    nblk = HY_WIDTH // LANES
    wf = w4[:, :HY_WIDTH].reshape(H, nblk, LANES)
    wb = w4[:, HY_WIDTH:].reshape(H, nblk, LANES)
    zz = jnp.zeros_like(wf)
    top = jnp.concatenate([wf, zz], axis=2)
    bot = jnp.concatenate([zz, wb], axis=2)
    return jnp.concatenate([top, bot], axis=0).reshape(2 * H, nblk * 2 * LANES)


def kernel(x, ffn1_norm, ffn1_w_gate, ffn1_w_up, ffn1_w_down, mix_norm, w_in, hy_conv_w, hy_conv_b, flt_w1, flt_b1, flt_f1, flt_w2, flt_b2, flt_f2, flt_w3, flt_b3, flt_f3, flt_w4, hy_skip, hy_out_norm, gla_w_a2_f, gla_b_a_f, gla_w_a2_b, gla_b_a_b, gla_out_norm, w_out, ffn2_norm, ffn2_w_gate, ffn2_w_up, ffn2_w_down, final_norm):
    B, L, D = x.shape
    depth = ffn1_norm.shape[0]
    tabs = _dft_tables(L)
    feat2 = _filter_features(L)
    deltas = _filter_deltas()
    gla_col0 = 3 * HY_WIDTH
    H2 = 2 * FILTER_HIDDEN
    both = lambda a: jnp.concatenate([a, a]).reshape(1, H2)
    filt = []
    for l in range(depth):
        early = [jnp.swapaxes(w_in[l], 0, 1)]
        if l == 0:
            early += [ffn1_w_gate[0], ffn1_w_up[0], ffn1_w_down[0]]
        hid3, cast = _filt_mlp(feat2, _twice(flt_w1[l], LANES), both(flt_b1[l]), both(flt_f1[l]),
                               _twice(flt_w2[l], H2), both(flt_b2[l]), both(flt_f2[l]),
                               _twice(flt_w3[l], H2), both(flt_b3[l]), both(flt_f3[l]), to_cast=early)
        w4cat = _three_pass_rows(_block_diag_w4(flt_w4[l]))
        khat, knyq = _filt_fft(hid3, w4cat, deltas, tabs["g_full"], tabs["f3"], L=L)
        filt.append((khat, knyq, cast))

    outs = []
    for bi in range(B):
        xb = x[bi]
        w1 = filt[0][2][1:]
        for l in range(depth):
            last = l == depth - 1
            khat, knyq, (w_in_t, *_) = filt[l]
            later = [ffn2_w_gate[l], ffn2_w_up[l], ffn2_w_down[l], w_out[l]]
            xb, (w2_gate, w2_up, w2_down, w_out_bf) = _ffn(xb, ffn1_norm[l], *w1, final_norm, final_norm=False,
                                                           to_cast=later)
            p, p_lr = _in_proj(xb, mix_norm[l], w_in_t)
            y_hy = _hy_conv(p, hy_conv_w[l], hy_conv_b[l].reshape(1, -1), khat, knyq,
                            hy_skip[l].reshape(1, -1), hy_out_norm[l].reshape(1, -1), tabs, L=L)

            w_gate = jnp.zeros((2, LANES, GLA_KW), F32)
            w_gate = w_gate.at[0, :GATE_RANK].set(gla_w_a2_f[l]).at[1, GATE_RANK:2 * GATE_RANK].set(gla_w_a2_b[l])
            w_gate = _three_pass_rows(w_gate)
            b_gate = jnp.stack([gla_b_a_f[l], gla_b_a_b[l]]).reshape(2, 1, GLA_KW)
            y_gla = _gla(p, p_lr, w_gate, b_gate, gla_out_norm[l].reshape(1, -1), L=L, col0=gla_col0)

            xb = _out_proj(xb, y_hy, y_gla, w_out_bf)
            nxt = [] if last else [ffn1_w_gate[l + 1], ffn1_w_up[l + 1], ffn1_w_down[l + 1]]
            xb, w1 = _ffn(xb, ffn2_norm[l], w2_gate, w2_up, w2_down, final_norm, final_norm=last, to_cast=nxt)
        outs.append(xb)
    return jnp.stack(outs)
```

```python
import functools
import math

import numpy as np
import jax
import jax.numpy as jnp
from jax import lax
from jax.experimental import pallas as pl
from jax.experimental.pallas import tpu as pltpu

F32 = jnp.float32
BF16 = jnp.bfloat16
HIGHEST = lax.Precision.HIGHEST

EPS = 1e-6
HY_WIDTH = 1024
HY_GROUPS = 8
FILTER_EMB = 33
FILTER_HIDDEN = 64
SHORT_DECAY_PCT = 0.3
LONG_DECAY_PCT = 1.5
DECAY_TARGET = 1e-2
GLA_HEADS = 4
GLA_DK = 128
GLA_DV = 256
GLA_KW = GLA_HEADS * GLA_DK
GLA_VW = GLA_HEADS * GLA_DV
GATE_RANK = 16
GATE_TEMP = 16.0
CHUNK = 64

LANES = 128
SUBLANES = 8
VMEM_LIMIT = 60 * 1024 * 1024
FFN_SLAB = 128
RB = 128
KB = RB // 2
PITCH = RB + SUBLANES


def _cparams(sem):
    return pltpu.CompilerParams(dimension_semantics=sem, vmem_limit_bytes=VMEM_LIMIT)


def _single(block_shape, index_map):
    return pl.BlockSpec(block_shape, index_map, pipeline_mode=pl.Buffered(1))


def _split_hi_lo(x):
    hi = x.astype(BF16)
    return hi, (x - hi.astype(F32)).astype(BF16)


def _three_pass_rows(w):
    w_hi, w_lo = _split_hi_lo(w)
    return jnp.concatenate([w_hi, w_lo, w_hi], axis=-2)


def _dot_nt(a, b):
    return lax.dot_general(a, b, (((1,), (1,)), ((), ())), preferred_element_type=F32)


def _dot_tn(a, b):
    return lax.dot_general(a, b, (((0,), (0,)), ((), ())), preferred_element_type=F32)


def _ffn_kernel(x_ref, g_ref, wg_ref, wu_ref, wd_ref, fg_ref, *rest, final_norm, n_cast):
    cast_src, o_ref, cast_dst, h_scr = rest[:n_cast], rest[n_cast], rest[n_cast + 1:-1], rest[-1]
    j = pl.program_id(1)

    slab = min(FFN_SLAB, x_ref.shape[0])
    nslab = x_ref.shape[0] // slab

    @pl.when(j == 0)
    def _():
        def norm_rows(t, carry):
            rows = pl.ds(pl.multiple_of(t * slab, slab), slab)
            x = x_ref[rows, :]
            r = lax.rsqrt(jnp.mean(x * x, axis=-1, keepdims=True) + EPS)
            h_scr[rows, :] = (x * r * g_ref[...]).astype(BF16)
            o_ref[rows, :] = jnp.zeros((slab, x_ref.shape[1]), F32)
            return carry
        lax.fori_loop(0, nslab, norm_rows, 0)

    h = h_scr[...]
    gate = jnp.dot(h, wg_ref[...], preferred_element_type=F32)
    up = jnp.dot(h, wu_ref[...], preferred_element_type=F32)
    a = (jax.nn.silu(gate) * up).astype(BF16)
    o_ref[...] += jnp.dot(a, wd_ref[...], preferred_element_type=F32)

    for src, dst in zip(cast_src, cast_dst):
        dst[...] = src[...].astype(BF16)

    @pl.when(j == pl.num_programs(1) - 1)
    def _():
        def finish_rows(t, carry):
            rows = pl.ds(pl.multiple_of(t * slab, slab), slab)
            y = x_ref[rows, :] + 0.5 * o_ref[rows, :]
            if final_norm:
                r = lax.rsqrt(jnp.mean(y * y, axis=-1, keepdims=True) + EPS)
                y = y * r * fg_ref[...]
            o_ref[rows, :] = y
            return carry
        lax.fori_loop(0, nslab, finish_rows, 0)


def _cast_blocking(shape, ni, nj):
    rows, cols = shape
    assert rows % ni == 0 and (rows // ni) % SUBLANES == 0 and cols % LANES == 0
    ncol = max(d for d in range(1, nj + 1) if (cols // LANES) % d == 0)
    return (rows // ni, cols // ncol), (lambda i, j: (i, jnp.minimum(j, ncol - 1)))


def _ffn(x, norm_g, w_gate, w_up, w_down, final_g, *, final_norm, to_cast=(), tm=1024, tf=512):
    L, D = x.shape
    DF = w_gate.shape[1]
    tm = min(tm, L)
    tf = min(tf, DF)
    ni, nj = L // tm, DF // tf
    blockings = [_cast_blocking(w.shape, ni, nj) for w in to_cast]
    cast_specs = [pl.BlockSpec(blk, imap) for blk, imap in blockings]
    outs = pl.pallas_call(
        functools.partial(_ffn_kernel, final_norm=final_norm, n_cast=len(to_cast)),
        grid=(ni, nj),
        in_specs=[
            pl.BlockSpec((tm, D), lambda i, j: (i, 0)),
            pl.BlockSpec((1, D), lambda i, j: (0, 0)),
            pl.BlockSpec((D, tf), lambda i, j: (0, j)),
            pl.BlockSpec((D, tf), lambda i, j: (0, j)),
            pl.BlockSpec((tf, D), lambda i, j: (j, 0)),
            pl.BlockSpec((1, D), lambda i, j: (0, 0)),
        ] + cast_specs,
        out_specs=[pl.BlockSpec((tm, D), lambda i, j: (i, 0))] + cast_specs,
        out_shape=[jax.ShapeDtypeStruct((L, D), F32)] + [jax.ShapeDtypeStruct(w.shape, BF16) for w in to_cast],
        scratch_shapes=[pltpu.VMEM((tm, D), BF16)],
        compiler_params=_cparams(("arbitrary", "arbitrary")),
        name="ffn",
    )(x, norm_g.reshape(1, D), w_gate, w_up, w_down, final_g.reshape(1, D), *to_cast)
    return outs[0], outs[1:]


def _in_proj_kernel(x_ref, g_ref, wt_ref, wtail_ref, o_ref, otail_ref, h_scr, *, n_tail):
    j = pl.program_id(1)

    @pl.when(j == 0)
    def _():
        x = x_ref[...]
        r = lax.rsqrt(jnp.mean(x * x, axis=-1, keepdims=True) + EPS)
        h = (x * r * g_ref[...]).astype(BF16)
        h_scr[...] = h
        tail = _dot_nt(h, wtail_ref[...])
        col = lax.broadcasted_iota(jnp.int32, tail.shape, 1)
        otail_ref[...] = jnp.where(col < n_tail, tail, 0.0)

    o_ref[...] = _dot_nt(h_scr[...], wt_ref[...])


def _in_proj(x, norm_g, w_t, *, tm=1024, tn=1024):
    L, D = x.shape
    n_cols = w_t.shape[0]
    nj = n_cols // tn
    n_tail = n_cols - nj * tn
    assert 0 < n_tail <= LANES and (nj * tn) % LANES == 0
    tm = min(tm, L)
    return pl.pallas_call(
        functools.partial(_in_proj_kernel, n_tail=n_tail),
        grid=(L // tm, nj),
        in_specs=[
            pl.BlockSpec((tm, D), lambda i, j: (i, 0)),
            pl.BlockSpec((1, D), lambda i, j: (0, 0)),
            pl.BlockSpec((tn, D), lambda i, j: (j, 0)),
            pl.BlockSpec((LANES, D), lambda i, j: (nj * tn // LANES, 0)),
        ],
        out_specs=[pl.BlockSpec((tm, tn), lambda i, j: (i, j)),
                   pl.BlockSpec((tm, LANES), lambda i, j: (i, 0))],
        out_shape=[jax.ShapeDtypeStruct((L, nj * tn), F32),
                   jax.ShapeDtypeStruct((L, LANES), F32)],
        scratch_shapes=[pltpu.VMEM((tm, D), BF16)],
        compiler_params=_cparams(("parallel", "arbitrary")),
        name="in_proj",
    )(x, norm_g.reshape(1, D), w_t, w_t)


def _out_proj_kernel(x_ref, yh_ref, yg_ref, wh_ref, wg_ref, o_ref):
    o_ref[...] = (x_ref[...]
                  + jnp.dot(yh_ref[...], wh_ref[...], preferred_element_type=F32)
                  + jnp.dot(yg_ref[...], wg_ref[...], preferred_element_type=F32))


def _out_proj(x, y_hy, y_gla, w_out, *, tm=512):
    L, D = x.shape
    WH = y_hy.shape[1]
    WG = y_gla.shape[1]
    tm = min(tm, L)
    w = w_out
    return pl.pallas_call(
        _out_proj_kernel,
        grid=(L // tm,),
        in_specs=[
            pl.BlockSpec((tm, D), lambda i: (i, 0)),
            pl.BlockSpec((tm, WH), lambda i: (i, 0)),
            pl.BlockSpec((tm, WG), lambda i: (i, 0)),
            _single((WH, D), lambda i: (0, 0)),
            _single((WG, D), lambda i: (0, 0)),
        ],
        out_specs=pl.BlockSpec((tm, D), lambda i: (i, 0)),
        out_shape=jax.ShapeDtypeStruct((L, D), F32),
        compiler_params=_cparams(("parallel",)),
        name="out_proj",
    )(x, y_hy, y_gla, w[:WH], w[WH:])


def _filt_mlp_kernel(z_ref, w1_ref, b1_ref, f1_ref, w2_ref, b2_ref, f2_ref, w3_ref, b3_ref, f3_ref, *rest, n_cast):
    cast_src, o_ref, cast_dst = rest[:n_cast], rest[n_cast], rest[n_cast + 1:]
    dot = functools.partial(jnp.dot, precision=HIGHEST, preferred_element_type=F32)
    hid = jnp.sin(f1_ref[...] * (dot(z_ref[...], w1_ref[...]) + b1_ref[...]))
    hid = jnp.sin(f2_ref[...] * (dot(hid, w2_ref[...]) + b2_ref[...]))
    hid = jnp.sin(f3_ref[...] * (dot(hid, w3_ref[...]) + b3_ref[...]))
    hi, lo = _split_hi_lo(hid)
    o_ref[...] = jnp.concatenate([hi, hi, lo], axis=1)
    for src, dst in zip(cast_src, cast_dst):
        dst[...] = src[...].astype(BF16)


def _cast_row_blocks(shape, n):
    rows, cols = shape
    tile = 2 * SUBLANES
    block_rows = pl.cdiv(pl.cdiv(rows, n), tile) * tile
    nblocks = pl.cdiv(rows, block_rows)
    return (block_rows, cols), (lambda i: (jnp.minimum(i, nblocks - 1), 0))


def _filt_mlp(feat2, w1, b1, f1, w2, b2, f2, w3, b3, f3, *, to_cast=(), tm=512):
    L, FP = feat2.shape
    H2 = w2.shape[0]
    tm = min(tm, L)
    steps = L // tm
    full = lambda shp: pl.BlockSpec(shp, lambda i: (0, 0))
    cast_specs = [pl.BlockSpec(*_cast_row_blocks(w.shape, steps)) for w in to_cast]
    outs = pl.pallas_call(
        functools.partial(_filt_mlp_kernel, n_cast=len(to_cast)),
        grid=(steps,),
        in_specs=[pl.BlockSpec((tm, FP), lambda i: (i, 0)),
                  full((FP, H2)), full((1, H2)), full((1, H2)),
                  full((H2, H2)), full((1, H2)), full((1, H2)),
                  full((H2, H2)), full((1, H2)), full((1, H2))] + cast_specs,
        out_specs=[pl.BlockSpec((tm, 3 * H2), lambda i: (i, 0))] + cast_specs,
        out_shape=[jax.ShapeDtypeStruct((L, 3 * H2), BF16)] + [jax.ShapeDtypeStruct(w.shape, BF16) for w in to_cast],
        compiler_params=_cparams(("arbitrary",)),
        name="filt_mlp",
    )(feat2, w1, b1, f1, w2, b2, f2, w3, b3, f3, *to_cast)
    return outs[0], outs[1:]


def _dft_tables(L):
    N = 2 * L
    RA = N // RB
    NH = RA // 2
    two_pi = 2.0 * np.pi
    k1 = np.arange(RA, dtype=np.int64)

    def step1_table(NR):
        g = np.zeros((RB // 2, 2 * RA, 2 * NR), np.float64)
        for s in range(2):
            n = RB * np.arange(NR, dtype=np.int64)[None, :] + (2 * np.arange(RB // 2, dtype=np.int64) + s)[:, None]
            ang = two_pi * ((k1[None, :, None] * n[:, None, :]) % N) / N
            g[:, :RA, s * NR:(s + 1) * NR] = np.cos(ang)
            g[:, RA:, s * NR:(s + 1) * NR] = -np.sin(ang)
        return g

    phi = two_pi * ((np.arange(KB)[:, None] * np.arange(RB)[None, :]) % RB) / RB
    f3 = np.block([[np.cos(phi), np.sin(phi)], [-np.sin(phi), np.cos(phi)]])
    i1 = np.block([[np.cos(phi.T), -np.sin(phi.T)], [np.sin(phi.T), np.cos(phi.T)]])
    cast = lambda a: jnp.asarray(a.astype(np.float32)).astype(BF16)
    return dict(g_half=cast(step1_table(NH)), g_full=cast(step1_table(RA)), f3=cast(f3), i1=cast(i1))


def _fwd_step1(u_ref, g_ref, ab_ref, *, RA, NR, CB, G=2):
    def body(t, carry):
        ms = [t * G + j for j in range(G)]
        rhs = []
        for m in ms:
            ua = u_ref[pl.ds(2 * m, NR, stride=PITCH), :]
            ub = u_ref[pl.ds(2 * m + 1, NR, stride=PITCH), :]
            zz = jnp.zeros_like(ua)
            rhs.append(jnp.concatenate([jnp.concatenate([ua, zz], axis=1),
                                        jnp.concatenate([zz, ub], axis=1)], axis=0).astype(BF16))
        res = [jnp.dot(g_ref[m], r, preferred_element_type=F32) for m, r in zip(ms, rhs)]
        for m, r in zip(ms, res):
            ab_ref[pl.ds(2 * m, RA, stride=PITCH), :] = r[:RA, :CB]
            ab_ref[pl.ds(RA * PITCH + 2 * m, RA, stride=PITCH), :] = r[RA:, :CB]
            ab_ref[pl.ds(2 * m + 1, RA, stride=PITCH), :] = r[:RA, CB:]
            ab_ref[pl.ds(RA * PITCH + 2 * m + 1, RA, stride=PITCH), :] = r[RA:, CB:]
        return carry
    lax.fori_loop(0, RB // 2 // G, body, 0)


def _fwd_step3_pair(ab_ref, f3, kp, *, RA):
    cols = []
    for s in range(2):
        k1 = 2 * kp + s
        ar = ab_ref[pl.ds(pl.multiple_of(k1 * PITCH, SUBLANES), RB), :]
        ai = ab_ref[pl.ds(pl.multiple_of((RA + k1) * PITCH, SUBLANES), RB), :]
        cols.append(jnp.concatenate([ar, ai], axis=0))
    rhs = jnp.concatenate(cols, axis=1).astype(BF16)
    return jnp.dot(f3, rhs, preferred_element_type=F32)


def _alt_sign(shape):
    rows = lax.broadcasted_iota(jnp.int32, shape, 0)
    return jnp.where(rows % 2 == 0, 1.0, -1.0).astype(F32)


def _filt_fft_kernel(hid_ref, w4_ref, dl_ref, g_ref, f3_ref, kh_ref, kn_ref, u_scr, ab_scr, *, L, CB, G):
    N = 2 * L
    RA = N // RB
    NH = RA // 2
    f3 = f3_ref[...]
    inv_lm1 = 1.0 / (L - 1)

    def fill(n1, carry):
        s_abs, s_alt = carry
        r0 = pl.multiple_of(n1 * RB, RB)
        h2 = jnp.dot(hid_ref[pl.ds(r0, RB), :], w4_ref[...], preferred_element_type=F32)
        pos = lax.broadcasted_iota(jnp.int32, (RB, CB), 0) + n1 * RB
        hf = h2[:, :CB] * jnp.exp(-(pos.astype(F32) * inv_lm1) * dl_ref[...])
        hb = h2[:, CB:] * jnp.exp(-((L - pos).astype(F32) * inv_lm1) * dl_ref[...])
        hb = jnp.where(pos == 0, 0.0, hb)
        u_scr[pl.ds(pl.multiple_of(n1 * PITCH, SUBLANES), RB), :] = hf
        u_scr[pl.ds(pl.multiple_of((NH + n1) * PITCH, SUBLANES), RB), :] = hb
        return s_abs + (jnp.abs(hf) + jnp.abs(hb)), s_alt + (hf + hb)
    zero = jnp.zeros((RB, CB), F32)
    s_abs, s_alt = lax.fori_loop(0, NH, fill, (zero, zero), unroll=2)
    inv_l1 = 1.0 / jnp.sum(s_abs, axis=0, keepdims=True)
    kn = jnp.sum(s_alt * _alt_sign((RB, CB)), axis=0, keepdims=True) * inv_l1 * (1.0 / N)
    kn_ref[...] = jnp.broadcast_to(kn, kn_ref.shape)

    _fwd_step1(u_scr, g_ref, ab_scr, RA=RA, NR=RA, CB=CB, G=G)

    scale = inv_l1 * (2.0 / N)
    scale2 = jnp.concatenate([scale, scale], axis=1)
    row2 = lax.broadcasted_iota(jnp.int32, (KB, 2 * CB), 0)
    lane2 = lax.broadcasted_iota(jnp.int32, (KB, 2 * CB), 1)
    dc_pos = jnp.logical_and(row2 == 0, lane2 < CB)

    def emit(t, carry):
        kps = [t * G + j for j in range(G)]
        xs = [_fwd_step3_pair(ab_scr, f3, kp, RA=RA) for kp in kps]
        for kp, x in zip(kps, xs):
            kr = x[:KB] * scale2
            ki = x[KB:] * scale2
            dc = jnp.logical_and(dc_pos, kp == 0)
            kr = jnp.where(dc, 0.5 * kr, kr)
            ki = jnp.where(dc, 0.5 * ki, ki)
            kh_ref[pl.ds(pl.multiple_of(kp * RB, RB), RB), :] = jnp.concatenate([kr, ki], axis=0).astype(kh_ref.dtype)
        return carry
    lax.fori_loop(0, RA // 2 // G, emit, 0)


def _filt_fft(hid2, w4bd, deltas, g_tab, f3_tab, *, L, CB=LANES, G=4):
    C = deltas.shape[1]
    H2 = hid2.shape[1]
    N = 2 * L
    RA = N // RB
    nblk = C // CB
    return pl.pallas_call(
        functools.partial(_filt_fft_kernel, L=L, CB=CB, G=G),
        grid=(nblk,),
        in_specs=[
            _single((L, H2), lambda c: (0, 0)),
            pl.BlockSpec((H2, 2 * CB), lambda c: (0, c)),
            pl.BlockSpec((1, CB), lambda c: (0, c)),
            _single(g_tab.shape, lambda c: (0, 0, 0)),
            _single(f3_tab.shape, lambda c: (0, 0)),
        ],
        out_specs=[pl.BlockSpec((RA // 2 * RB, 2 * CB), lambda c: (0, c)),
                   pl.BlockSpec((SUBLANES, CB), lambda c: (0, c))],
        out_shape=[jax.ShapeDtypeStruct((RA // 2 * RB, 2 * C), BF16),
                   jax.ShapeDtypeStruct((SUBLANES, C), F32)],
        scratch_shapes=[pltpu.VMEM((RA * PITCH, CB), F32),
                        pltpu.VMEM((2 * RA * PITCH, CB), F32)],
        compiler_params=_cparams(("arbitrary",)),
        name="filt_fft",
    )(hid2, w4bd, deltas, g_tab, f3_tab)


def _short_conv_rows(p_ref, w_ref, b_ref, n1, *, first, last):
    r0 = pl.multiple_of(n1 * RB, RB)
    cur = p_ref[pl.ds(r0, RB), :]
    rows = lax.broadcasted_iota(jnp.int32, cur.shape, 0)
    if first:
        up = jnp.where(rows == 0, 0.0, pltpu.roll(cur, 1, axis=0))
    else:
        up = p_ref[pl.ds(r0 - 1, RB), :]
    if last:
        dn = jnp.where(rows == RB - 1, 0.0, pltpu.roll(cur, RB - 1, axis=0))
    else:
        dn = p_ref[pl.ds(r0 + 1, RB), :]
    w = w_ref[...]
    return b_ref[...] + up * w[0:1] + cur * w[1:2] + dn * w[2:3]


def _for_row_blocks(nblocks, body, carry):
    assert nblocks >= 2
    carry = body(0, carry, True, False)
    inner = nblocks - 2
    carry = lax.fori_loop(1, nblocks - 1, lambda n1, c: body(n1, c, False, False), carry,
                          unroll=2 if inner % 2 == 0 and inner > 0 else 1)
    return body(nblocks - 1, carry, False, True)


def _hy_conv_kernel(p_hbm, w0_ref, w1_ref, wv_ref, b0_ref, b1_ref, bv_ref,
                    kh_ref, kn_ref, skip_ref, og_ref, g_ref, f3_ref, i1_ref,
                    o_ref, u_scr, ab_scr, z_scr, p_scr, p_sem, *, L, CB, G):
    N = 2 * L
    RA = N // RB
    NH = RA // 2
    f3 = f3_ref[...]
    i1 = i1_ref[...]
    sign = _alt_sign((RB, CB))
    c = pl.program_id(0)
    nblk = pl.num_programs(0)
    p0_ref, p1_ref, pv_ref = p_scr.at[0], p_scr.at[1], p_scr.at[2]

    def fetch(group, blk):
        col = pl.multiple_of((group * nblk + blk) * CB, CB)
        return pltpu.make_async_copy(p_hbm.at[:, pl.ds(col, CB)], p_scr.at[group], p_sem.at[group])

    @pl.when(c == 0)
    def _():
        fetch(1, c).start()
        fetch(2, c).start()

    fetch(0, c).start()
    fetch(1, c).wait()
    fetch(2, c).wait()

    def fill(n1, s_alt, first, last):
        z = (_short_conv_rows(pv_ref, wv_ref, bv_ref, n1, first=first, last=last)
             * _short_conv_rows(p1_ref, w1_ref, b1_ref, n1, first=first, last=last))
        u_scr[pl.ds(pl.multiple_of(n1 * PITCH, SUBLANES), RB), :] = z
        z_scr[pl.ds(pl.multiple_of(n1 * RB, RB), RB), :] = z
        return s_alt + z
    s_alt = _for_row_blocks(NH, fill, jnp.zeros((RB, CB), F32))

    @pl.when(c + 1 < nblk)
    def _():
        fetch(1, c + 1).start()
        fetch(2, c + 1).start()

    z_nyq = jnp.sum(s_alt * sign, axis=0, keepdims=True)
    y_nyq = z_nyq * kn_ref[0:1, :]

    _fwd_step1(u_scr, g_ref, ab_scr, RA=RA, NR=NH, CB=CB, G=G)

    def mid(t, carry):
        kps = [t * G + j for j in range(G)]
        xs = [_fwd_step3_pair(ab_scr, f3, kp, RA=RA) for kp in kps]
        khs = [kh_ref[pl.ds(pl.multiple_of(kp * RB, RB), RB), :].astype(F32) for kp in kps]
        bs = []
        for x, kh in zip(xs, khs):
            xr, xi, kr, ki = x[:KB], x[KB:], kh[:KB], kh[KB:]
            y = jnp.concatenate([xr * kr - xi * ki, xr * ki + xi * kr], axis=0).astype(BF16)
            bs.append(jnp.dot(i1, y, preferred_element_type=F32))
        for kp, b in zip(kps, bs):
            for s in range(2):
                k1 = 2 * kp + s
                ab_scr[pl.ds(pl.multiple_of(k1 * PITCH, SUBLANES), RB), :] = b[:RB, s * CB:(s + 1) * CB]
                ab_scr[pl.ds(pl.multiple_of((RA + k1) * PITCH, SUBLANES), RB), :] = b[RB:, s * CB:(s + 1) * CB]
        return carry
    lax.fori_loop(0, RA // 2 // G, mid, 0)

    def last(t, carry):
        ms = [t * G + j for j in range(G)]
        rhs = [jnp.concatenate(
            [jnp.concatenate([ab_scr[pl.ds(2 * m + s, RA, stride=PITCH), :],
                              ab_scr[pl.ds(RA * PITCH + 2 * m + s, RA, stride=PITCH), :]], axis=0)
             for s in range(2)], axis=1).astype(BF16) for m in ms]
        ys = [_dot_tn(g_ref[m], r) for m, r in zip(ms, rhs)]
        for m, y in zip(ms, ys):
            u_scr[pl.ds(2 * m, NH, stride=PITCH), :] = y[:NH, :CB]
            u_scr[pl.ds(2 * m + 1, NH, stride=PITCH), :] = y[NH:, CB:]
        return carry
    lax.fori_loop(0, RB // 2 // G, last, 0)

    nyq_rows = sign * y_nyq
    fetch(0, c).wait()

    def finish(n1, carry, first, last):
        z = z_scr[pl.ds(pl.multiple_of(n1 * RB, RB), RB), :]
        x0 = _short_conv_rows(p0_ref, w0_ref, b0_ref, n1, first=first, last=last)
        conv = u_scr[pl.ds(pl.multiple_of(n1 * PITCH, SUBLANES), RB), :] + nyq_rows
        y = (conv + z * skip_ref[...]) * x0
        y = y * lax.rsqrt(jnp.mean(y * y, axis=-1, keepdims=True) + EPS) * og_ref[...]
        o_ref[pl.ds(pl.multiple_of(n1 * RB, RB), RB), :] = y.astype(o_ref.dtype)
        return carry
    _for_row_blocks(NH, finish, 0)


def _hy_conv(p, conv_w, conv_b, khat, knyq, skip, out_g, tabs, *, L, CB=LANES, G=4):
    C = skip.shape[1]
    assert CB == C // HY_GROUPS, "one channel block must be exactly one norm group"
    N = 2 * L
    RA = N // RB
    NH = RA // 2
    nblk = C // CB
    g_tab, f3_tab, i1_tab = tabs["g_half"], tabs["f3"], tabs["i1"]
    col = lambda off: (lambda c: (0, off * nblk + c))
    wspec = lambda off: pl.BlockSpec((3, CB), col(off))
    bspec = lambda off: pl.BlockSpec((1, CB), col(off))
    return pl.pallas_call(
        functools.partial(_hy_conv_kernel, L=L, CB=CB, G=G),
        grid=(nblk,),
        in_specs=[pl.BlockSpec(memory_space=pl.ANY),
                  wspec(0), wspec(1), wspec(2), bspec(0), bspec(1), bspec(2),
                  pl.BlockSpec((RA // 2 * RB, 2 * CB), col(0)),
                  pl.BlockSpec((SUBLANES, CB), col(0)),
                  bspec(0), bspec(0),
                  _single(g_tab.shape, lambda c: (0, 0, 0)),
                  _single(f3_tab.shape, lambda c: (0, 0)),
                  _single(i1_tab.shape, lambda c: (0, 0))],
        out_specs=pl.BlockSpec((L, CB), col(0)),
        out_shape=jax.ShapeDtypeStruct((L, C), BF16),
        scratch_shapes=[pltpu.VMEM((NH * PITCH, CB), F32),
                        pltpu.VMEM((2 * RA * PITCH, CB), F32),
                        pltpu.VMEM((L, CB), F32),
                        pltpu.VMEM((3, L, CB), F32),
                        pltpu.SemaphoreType.DMA((3,))],
        compiler_params=_cparams(("arbitrary",)),
        name="hy_conv",
    )(p, conv_w, conv_w, conv_w, conv_b, conv_b, conv_b, khat, knyq, skip, out_g,
      g_tab, f3_tab, i1_tab)


def _log_sigmoid(x):
    return jnp.minimum(x, 0.0) - jnp.log1p(jnp.exp(-jnp.abs(x)))


def _gla_scan_blocks(dirs, *, TB):
    nch = TB // CHUNK
    rows = lambda c: slice(c * CHUNK, (c + 1) * CHUNK)
    units = [(d, c) for c in range(nch) for d in dirs]
    rr = lax.broadcasted_iota(jnp.int32, (CHUNK, CHUNK), 0)
    cc = lax.broadcasted_iota(jnp.int32, (CHUNK, CHUNK), 1)
    scale = GLA_DK ** -0.5

    for d in dirs:
        lr_hi, lr_lo = _split_hi_lo(d["lr"][...])
        gate_in = jnp.dot(jnp.concatenate([lr_hi, lr_hi, lr_lo], axis=1), d["wg"],
                          preferred_element_type=F32) + d["bg"]
        g = _log_sigmoid(gate_in) * (1.0 / GATE_TEMP)
        g_hi = g.astype(BF16)
        g_mid, g_lo = _split_hi_lo(g - g_hi.astype(F32))
        d["parts"][...] = jnp.concatenate([g_hi, g_mid, g_lo], axis=1)
        d["csum"] = jnp.where((cc >= rr) if d["reverse"] else (cc <= rr), 1.0, 0.0).astype(BF16)
        d["mask"] = (cc > rr) if d["reverse"] else (cc <= rr)
        d["edge"] = 0 if d["reverse"] else CHUNK - 1

    for d, c in units:
        b3 = jnp.dot(d["csum"], d["parts"][rows(c), :], preferred_element_type=F32)
        d["b"][rows(c), :] = b3[:, :GLA_DK] + b3[:, GLA_DK:2 * GLA_DK] + b3[:, 2 * GLA_DK:]

    decay = {}
    for d, c in units:
        b = d["b"][rows(c), :]
        b_edge = b[d["edge"]:d["edge"] + 1]
        k = d["k"][rows(c), :]
        d["qks"][0, rows(c), :] = (d["q"][rows(c), :] * scale * jnp.exp(b)).astype(BF16)
        d["qks"][1, rows(c), :] = (k * jnp.exp(-b)).astype(BF16)
        d["qks"][2, rows(c), :] = (k * jnp.exp(b_edge - b)).astype(BF16)
        d["vb"][rows(c), :] = d["v"][rows(c), :].astype(BF16)
        decay[(id(d), c)] = jnp.exp(b_edge)

    for d, c in units:
        att = _dot_nt(d["qks"][0, rows(c), :], d["qks"][1, rows(c), :])
        d["att"][rows(c), :] = jnp.where(d["mask"], att, 0.0).astype(BF16)

    for d, c in units:
        d["o"][rows(c), :] = jnp.dot(d["att"][rows(c), :], d["vb"][rows(c), :], preferred_element_type=F32)

    for d, c in units:
        d["kv"][c * GLA_DV:(c + 1) * GLA_DV, :] = _dot_tn(d["vb"][rows(c), :], d["qks"][2, rows(c), :])

    state = {id(d): d["s"][...] for d in dirs}
    for step in range(nch):
        for d in dirs:
            c = nch - 1 - step if d["reverse"] else step
            s_t = state[id(d)]
            d["o"][rows(c), :] += _dot_nt(d["qks"][0, rows(c), :], s_t.astype(BF16))
            state[id(d)] = s_t * decay[(id(d), c)] + d["kv"][c * GLA_DV:(c + 1) * GLA_DV, :]
    for d in dirs:
        d["s"][...] = state[id(d)]


def _gla_kernel(qf_ref, kf_ref, vf_ref, lf_ref, rf_ref, qb_ref, kb_ref, vb_ref, lb_ref, rb_ref,
                wg_ref, bg_ref, og_ref, o_ref,
                s_scr, parts_scr, b_scr, qks_scr, vb_scr, att_scr, kv_scr, ob_scr, half_scr, *, TB):
    s = pl.program_id(1)
    nb = pl.num_programs(1)

    @pl.when(s == 0)
    def _():
        s_scr[...] = jnp.zeros_like(s_scr)

    def direction(i, q, k, v, lr):
        return dict(q=q, k=k, v=v, lr=lr, wg=wg_ref[i], bg=bg_ref[i], reverse=bool(i), s=s_scr.at[i],
                    parts=parts_scr.at[i], b=b_scr.at[i], qks=qks_scr.at[i], vb=vb_scr.at[i],
                    att=att_scr.at[i], kv=kv_scr.at[i], o=ob_scr.at[i])
    _gla_scan_blocks([direction(0, qf_ref, kf_ref, vf_ref, lf_ref),
                      direction(1, qb_ref, kb_ref, vb_ref, lb_ref)], TB=TB)

    first_half = s < nb // 2
    for d, blk, r_ref in ((0, s, rf_ref), (1, nb - 1 - s, rb_ref)):
        rows = pl.ds(pl.multiple_of(blk * TB, TB), TB)

        @pl.when(first_half)
        def _():
            half_scr[rows, :] = ob_scr[d]

        @pl.when(jnp.logical_not(first_half))
        def _():
            tot = half_scr[rows, :] + ob_scr[d]
            tot = tot * lax.rsqrt(jnp.mean(tot * tot, axis=-1, keepdims=True) + EPS) * og_ref[...]
            o_ref[rows, :] = (tot * jax.nn.silu(r_ref[...])).astype(o_ref.dtype)


def _gla(p, p_lr, w_gate, b_gate, out_g, *, L, col0, TB=1024):
    TB = min(TB, L)
    nb = L // TB
    assert nb % 2 == 0, "both scan directions must meet between two blocks"
    nch = TB // CHUNK
    qb = col0 // GLA_DK
    kb = qb + GLA_HEADS
    vb = (col0 + 2 * GLA_KW) // GLA_DV
    rb = vb + GLA_HEADS
    fwd = lambda s: s
    bwd = lambda s: nb - 1 - s

    def operands(blk):
        return [pl.BlockSpec((TB, GLA_DK), lambda h, s: (blk(s), qb + h)),
                pl.BlockSpec((TB, GLA_DK), lambda h, s: (blk(s), kb + h)),
                pl.BlockSpec((TB, GLA_DV), lambda h, s: (blk(s), vb + h)),
                pl.BlockSpec((TB, LANES), lambda h, s: (blk(s), 0)),
                pl.BlockSpec((TB, GLA_DV), lambda h, s: (blk(s), rb + h))]

    return pl.pallas_call(
        functools.partial(_gla_kernel, TB=TB),
        grid=(GLA_HEADS, nb),
        in_specs=operands(fwd) + operands(bwd) + [
            pl.BlockSpec((2, 3 * LANES, GLA_DK), lambda h, s: (0, 0, h)),
            pl.BlockSpec((2, 1, GLA_DK), lambda h, s: (0, 0, h)),
            pl.BlockSpec((1, GLA_DV), lambda h, s: (0, h)),
        ],
        out_specs=pl.BlockSpec((L, GLA_DV), lambda h, s: (0, h)),
        out_shape=jax.ShapeDtypeStruct((L, GLA_VW), BF16),
        scratch_shapes=[pltpu.VMEM((2, GLA_DV, GLA_DK), F32),
                        pltpu.VMEM((2, TB, 3 * GLA_DK), BF16),
                        pltpu.VMEM((2, TB, GLA_DK), F32),
                        pltpu.VMEM((2, 3, TB, GLA_DK), BF16),
                        pltpu.VMEM((2, TB, GLA_DV), BF16),
                        pltpu.VMEM((2, TB, CHUNK), BF16),
                        pltpu.VMEM((2, nch * GLA_DV, GLA_DK), F32),
                        pltpu.VMEM((2, TB, GLA_DV), F32),
                        pltpu.VMEM((L, GLA_DV), F32)],
        compiler_params=_cparams(("arbitrary", "arbitrary")),
        name="gla",
    )(*([p, p, p, p_lr, p] * 2), w_gate, b_gate, out_g)


def _filter_features(L):
    t = np.linspace(0.0, 1.0, L)[:, None]
    bands = (FILTER_EMB - 1) // 2
    freqs = np.linspace(1e-4, bands - 1, bands)[None, :]
    ang = (2.0 * np.pi / L) * np.arange(L)[:, None] * freqs
    feat = np.concatenate([t, np.cos(ang), -np.sin(ang)], axis=-1)
    feat_rev = np.roll(feat[::-1], 1, axis=0)
    both = np.zeros((L, LANES), np.float32)
    both[:, :FILTER_EMB] = feat
    both[:, FILTER_EMB:2 * FILTER_EMB] = feat_rev
    return jnp.asarray(both)


def _twice(w, rows_out):
    r, c = w.shape
    out = jnp.zeros((rows_out, 2 * c), F32)
    return out.at[:r, :c].set(w).at[r:2 * r, c:].set(w)


def _filter_deltas():
    min_decay = math.log(DECAY_TARGET) / LONG_DECAY_PCT
    max_decay = math.log(DECAY_TARGET) / SHORT_DECAY_PCT
    return jnp.abs(jnp.linspace(min_decay, max_decay, HY_WIDTH, dtype=F32)).reshape(1, HY_WIDTH)


def _block_diag_w4(w4):
    H = w4.shape[0]
    nblk = HY_WIDTH // LANES
    wf = w4[:, :HY_WIDTH].reshape(H, nblk, LANES)
    wb = w4[:, HY_WIDTH:].reshape(H, nblk, LANES)
    zz = jnp.zeros_like(wf)
    top = jnp.concatenate([wf, zz], axis=2)
    bot = jnp.concatenate([zz, wb], axis=2)
    return jnp.concatenate([top, bot], axis=0).reshape(2 * H, nblk * 2 * LANES)


def kernel(x, ffn1_norm, ffn1_w_gate, ffn1_w_up, ffn1_w_down, mix_norm, w_in, hy_conv_w, hy_conv_b, flt_w1, flt_b1, flt_f1, flt_w2, flt_b2, flt_f2, flt_w3, flt_b3, flt_f3, flt_w4, hy_skip, hy_out_norm, gla_w_a2_f, gla_b_a_f, gla_w_a2_b, gla_b_a_b, gla_out_norm, w_out, ffn2_norm, ffn2_w_gate, ffn2_w_up, ffn2_w_down, final_norm):
    B, L, D = x.shape
    depth = ffn1_norm.shape[0]
    tabs = _dft_tables(L)
    feat2 = _filter_features(L)
    deltas = _filter_deltas()
    gla_col0 = 3 * HY_WIDTH
    H2 = 2 * FILTER_HIDDEN
    both = lambda a: jnp.concatenate([a, a]).reshape(1, H2)
    filt = []
    for l in range(depth):
        early = [jnp.swapaxes(w_in[l], 0, 1)]
        if l == 0:
            early += [ffn1_w_gate[0], ffn1_w_up[0], ffn1_w_down[0]]
        hid3, cast = _filt_mlp(feat2, _twice(flt_w1[l], LANES), both(flt_b1[l]), both(flt_f1[l]),
                               _twice(flt_w2[l], H2), both(flt_b2[l]), both(flt_f2[l]),
                               _twice(flt_w3[l], H2), both(flt_b3[l]), both(flt_f3[l]), to_cast=early)
        w4cat = _three_pass_rows(_block_diag_w4(flt_w4[l]))
        khat, knyq = _filt_fft(hid3, w4cat, deltas, tabs["g_full"], tabs["f3"], L=L)
        filt.append((khat, knyq, cast))

    outs = []
    for bi in range(B):
        xb = x[bi]
        w1 = filt[0][2][1:]
        for l in range(depth):
            last = l == depth - 1
            khat, knyq, (w_in_t, *_) = filt[l]
            later = [ffn2_w_gate[l], ffn2_w_up[l], ffn2_w_down[l], w_out[l]]
            xb, (w2_gate, w2_up, w2_down, w_out_bf) = _ffn(xb, ffn1_norm[l], *w1, final_norm, final_norm=False,
                                                           to_cast=later)
            p, p_lr = _in_proj(xb, mix_norm[l], w_in_t)
            y_hy = _hy_conv(p, hy_conv_w[l], hy_conv_b[l].reshape(1, -1), khat, knyq,
                            hy_skip[l].reshape(1, -1), hy_out_norm[l].reshape(1, -1), tabs, L=L)

            w_gate = jnp.zeros((2, LANES, GLA_KW), F32)
            w_gate = w_gate.at[0, :GATE_RANK].set(gla_w_a2_f[l]).at[1, GATE_RANK:2 * GATE_RANK].set(gla_w_a2_b[l])
            w_gate = _three_pass_rows(w_gate)
            b_gate = jnp.stack([gla_b_a_f[l], gla_b_a_b[l]]).reshape(2, 1, GLA_KW)
            y_gla = _gla(p, p_lr, w_gate, b_gate, gla_out_norm[l].reshape(1, -1), L=L, col0=gla_col0)

            xb = _out_proj(xb, y_hy, y_gla, w_out_bf)
            nxt = [] if last else [ffn1_w_gate[l + 1], ffn1_w_up[l + 1], ffn1_w_down[l + 1]]
            xb, w1 = _ffn(xb, ffn2_norm[l], w2_gate, w2_up, w2_down, final_norm, final_norm=last, to_cast=nxt)
        outs.append(xb)
    return jnp.stack(outs)
```

```python
import functools
import math

import numpy as np
import jax
import jax.numpy as jnp
from jax import lax
from jax.experimental import pallas as pl
from jax.experimental.pallas import tpu as pltpu

F32 = jnp.float32
BF16 = jnp.bfloat16
HIGHEST = lax.Precision.HIGHEST

EPS = 1e-6
HY_WIDTH = 1024
HY_GROUPS = 8
FILTER_EMB = 33
FILTER_HIDDEN = 64
SHORT_DECAY_PCT = 0.3
LONG_DECAY_PCT = 1.5
DECAY_TARGET = 1e-2
GLA_HEADS = 4
GLA_DK = 128
GLA_DV = 256
GLA_KW = GLA_HEADS * GLA_DK
GLA_VW = GLA_HEADS * GLA_DV
GATE_RANK = 16
GATE_TEMP = 16.0
CHUNK = 64

LANES = 128
SUBLANES = 8
VMEM_LIMIT = 60 * 1024 * 1024
FFN_SLAB = 128
RB = 128
KB = RB // 2
PITCH = RB + SUBLANES


def _cparams(sem):
    return pltpu.CompilerParams(dimension_semantics=sem, vmem_limit_bytes=VMEM_LIMIT)


def _single(block_shape, index_map):
    return pl.BlockSpec(block_shape, index_map, pipeline_mode=pl.Buffered(1))


def _split_hi_lo(x):
    hi = x.astype(BF16)
    return hi, (x - hi.astype(F32)).astype(BF16)


def _three_pass_rows(w):
    w_hi, w_lo = _split_hi_lo(w)
    return jnp.concatenate([w_hi, w_lo, w_hi], axis=-2)


def _dot_nt(a, b):
    return lax.dot_general(a, b, (((1,), (1,)), ((), ())), preferred_element_type=F32)


def _dot_tn(a, b):
    return lax.dot_general(a, b, (((0,), (0,)), ((), ())), preferred_element_type=F32)


def _ffn_kernel(x_ref, g_ref, wg_ref, wu_ref, wd_ref, fg_ref, *rest, final_norm, n_cast):
    cast_src, o_ref, cast_dst, h_scr = rest[:n_cast], rest[n_cast], rest[n_cast + 1:-1], rest[-1]
    j = pl.program_id(1)

    slab = min(FFN_SLAB, x_ref.shape[0])
    nslab = x_ref.shape[0] // slab

    @pl.when(j == 0)
    def _():
        def norm_rows(t, carry):
            rows = pl.ds(pl.multiple_of(t * slab, slab), slab)
            x = x_ref[rows, :]
            r = lax.rsqrt(jnp.mean(x * x, axis=-1, keepdims=True) + EPS)
            h_scr[rows, :] = (x * r * g_ref[...]).astype(BF16)
            o_ref[rows, :] = jnp.zeros((slab, x_ref.shape[1]), F32)
            return carry
        lax.fori_loop(0, nslab, norm_rows, 0)

    h = h_scr[...]
    gate = jnp.dot(h, wg_ref[...], preferred_element_type=F32)
    up = jnp.dot(h, wu_ref[...], preferred_element_type=F32)
    a = (jax.nn.silu(gate) * up).astype(BF16)
    o_ref[...] += jnp.dot(a, wd_ref[...], preferred_element_type=F32)

    for src, dst in zip(cast_src, cast_dst):
        dst[...] = src[...].astype(BF16)

    @pl.when(j == pl.num_programs(1) - 1)
    def _():
        def finish_rows(t, carry):
            rows = pl.ds(pl.multiple_of(t * slab, slab), slab)
            y = x_ref[rows, :] + 0.5 * o_ref[rows, :]
            if final_norm:
                r = lax.rsqrt(jnp.mean(y * y, axis=-1, keepdims=True) + EPS)
                y = y * r * fg_ref[...]
            o_ref[rows, :] = y
            return carry
        lax.fori_loop(0, nslab, finish_rows, 0)


def _cast_blocking(shape, ni, nj):
    rows, cols = shape
    assert rows % ni == 0 and (rows // ni) % SUBLANES == 0 and cols % LANES == 0
    ncol = max(d for d in range(1, nj + 1) if (cols // LANES) % d == 0)
    return (rows // ni, cols // ncol), (lambda i, j: (i, jnp.minimum(j, ncol - 1)))


def _ffn(x, norm_g, w_gate, w_up, w_down, final_g, *, final_norm, to_cast=(), tm=1024, tf=512):
    L, D = x.shape
    DF = w_gate.shape[1]
    tm = min(tm, L)
    tf = min(tf, DF)
    ni, nj = L // tm, DF // tf
    blockings = [_cast_blocking(w.shape, ni, nj) for w in to_cast]
    cast_specs = [pl.BlockSpec(blk, imap) for blk, imap in blockings]
    outs = pl.pallas_call(
        functools.partial(_ffn_kernel, final_norm=final_norm, n_cast=len(to_cast)),
        grid=(ni, nj),
        in_specs=[
            pl.BlockSpec((tm, D), lambda i, j: (i, 0)),
            pl.BlockSpec((1, D), lambda i, j: (0, 0)),
            pl.BlockSpec((D, tf), lambda i, j: (0, j)),
            pl.BlockSpec((D, tf), lambda i, j: (0, j)),
            pl.BlockSpec((tf, D), lambda i, j: (j, 0)),
            pl.BlockSpec((1, D), lambda i, j: (0, 0)),
        ] + cast_specs,
        out_specs=[pl.BlockSpec((tm, D), lambda i, j: (i, 0))] + cast_specs,
        out_shape=[jax.ShapeDtypeStruct((L, D), F32)] + [jax.ShapeDtypeStruct(w.shape, BF16) for w in to_cast],
        scratch_shapes=[pltpu.VMEM((tm, D), BF16)],
        compiler_params=_cparams(("arbitrary", "arbitrary")),
        name="ffn",
    )(x, norm_g.reshape(1, D), w_gate, w_up, w_down, final_g.reshape(1, D), *to_cast)
    return outs[0], outs[1:]


def _in_proj_kernel(x_ref, g_ref, wt_ref, wtail_ref, o_ref, otail_ref, h_scr, *, n_tail):
    j = pl.program_id(1)

    @pl.when(j == 0)
    def _():
        x = x_ref[...]
        r = lax.rsqrt(jnp.mean(x * x, axis=-1, keepdims=True) + EPS)
        h = (x * r * g_ref[...]).astype(BF16)
        h_scr[...] = h
        tail = _dot_nt(h, wtail_ref[...])
        col = lax.broadcasted_iota(jnp.int32, tail.shape, 1)
        otail_ref[...] = jnp.where(col < n_tail, tail, 0.0)

    o_ref[...] = _dot_nt(h_scr[...], wt_ref[...])


def _in_proj(x, norm_g, w_t, *, tm=1024, tn=1024):
    L, D = x.shape
    n_cols = w_t.shape[0]
    nj = n_cols // tn
    n_tail = n_cols - nj * tn
    assert 0 < n_tail <= LANES and (nj * tn) % LANES == 0
    tm = min(tm, L)
    return pl.pallas_call(
        functools.partial(_in_proj_kernel, n_tail=n_tail),
        grid=(L // tm, nj),
        in_specs=[
            pl.BlockSpec((tm, D), lambda i, j: (i, 0)),
            pl.BlockSpec((1, D), lambda i, j: (0, 0)),
            pl.BlockSpec((tn, D), lambda i, j: (j, 0)),
            pl.BlockSpec((LANES, D), lambda i, j: (nj * tn // LANES, 0)),
        ],
        out_specs=[pl.BlockSpec((tm, tn), lambda i, j: (i, j)),
                   pl.BlockSpec((tm, LANES), lambda i, j: (i, 0))],
        out_shape=[jax.ShapeDtypeStruct((L, nj * tn), F32),
                   jax.ShapeDtypeStruct((L, LANES), F32)],
        scratch_shapes=[pltpu.VMEM((tm, D), BF16)],
        compiler_params=_cparams(("parallel", "arbitrary")),
        name="in_proj",
    )(x, norm_g.reshape(1, D), w_t, w_t)


def _out_proj_kernel(x_ref, yh_ref, yg_ref, wh_ref, wg_ref, o_ref):
    o_ref[...] = (x_ref[...]
                  + jnp.dot(yh_ref[...], wh_ref[...], preferred_element_type=F32)
                  + jnp.dot(yg_ref[...], wg_ref[...], preferred_element_type=F32))


def _out_proj(x, y_hy, y_gla, w_out, *, tm=512):
    L, D = x.shape
    WH = y_hy.shape[1]
    WG = y_gla.shape[1]
    tm = min(tm, L)
    w = w_out
    return pl.pallas_call(
        _out_proj_kernel,
        grid=(L // tm,),
        in_specs=[
            pl.BlockSpec((tm, D), lambda i: (i, 0)),
            pl.BlockSpec((tm, WH), lambda i: (i, 0)),
            pl.BlockSpec((tm, WG), lambda i: (i, 0)),
            _single((WH, D), lambda i: (0, 0)),
            _single((WG, D), lambda i: (0, 0)),
        ],
        out_specs=pl.BlockSpec((tm, D), lambda i: (i, 0)),
        out_shape=jax.ShapeDtypeStruct((L, D), F32),
        compiler_params=_cparams(("parallel",)),
        name="out_proj",
    )(x, y_hy, y_gla, w[:WH], w[WH:])


def _filt_mlp_kernel(z_ref, w1_ref, b1_ref, f1_ref, w2_ref, b2_ref, f2_ref, w3_ref, b3_ref, f3_ref, *rest, n_cast):
    cast_src, o_ref, cast_dst = rest[:n_cast], rest[n_cast], rest[n_cast + 1:]
    dot = functools.partial(jnp.dot, precision=HIGHEST, preferred_element_type=F32)
    hid = jnp.sin(f1_ref[...] * (dot(z_ref[...], w1_ref[...]) + b1_ref[...]))
    hid = jnp.sin(f2_ref[...] * (dot(hid, w2_ref[...]) + b2_ref[...]))
    hid = jnp.sin(f3_ref[...] * (dot(hid, w3_ref[...]) + b3_ref[...]))
    hi, lo = _split_hi_lo(hid)
    o_ref[...] = jnp.concatenate([hi, hi, lo], axis=1)
    for src, dst in zip(cast_src, cast_dst):
        dst[...] = src[...].astype(BF16)


def _cast_row_blocks(shape, n):
    rows, cols = shape
    tile = 2 * SUBLANES
    block_rows = pl.cdiv(pl.cdiv(rows, n), tile) * tile
    nblocks = pl.cdiv(rows, block_rows)
    return (block_rows, cols), (lambda i: (jnp.minimum(i, nblocks - 1), 0))


def _filt_mlp(feat2, w1, b1, f1, w2, b2, f2, w3, b3, f3, *, to_cast=(), tm=512):
    L, FP = feat2.shape
    H2 = w2.shape[0]
    tm = min(tm, L)
    steps = L // tm
    full = lambda shp: pl.BlockSpec(shp, lambda i: (0, 0))
    cast_specs = [pl.BlockSpec(*_cast_row_blocks(w.shape, steps)) for w in to_cast]
    outs = pl.pallas_call(
        functools.partial(_filt_mlp_kernel, n_cast=len(to_cast)),
        grid=(steps,),
        in_specs=[pl.BlockSpec((tm, FP), lambda i: (i, 0)),
                  full((FP, H2)), full((1, H2)), full((1, H2)),
                  full((H2, H2)), full((1, H2)), full((1, H2)),
                  full((H2, H2)), full((1, H2)), full((1, H2))] + cast_specs,
        out_specs=[pl.BlockSpec((tm, 3 * H2), lambda i: (i, 0))] + cast_specs,
        out_shape=[jax.ShapeDtypeStruct((L, 3 * H2), BF16)] + [jax.ShapeDtypeStruct(w.shape, BF16) for w in to_cast],
        compiler_params=_cparams(("arbitrary",)),
        name="filt_mlp",
    )(feat2, w1, b1, f1, w2, b2, f2, w3, b3, f3, *to_cast)
    return outs[0], outs[1:]


def _dft_tables(L):
    N = 2 * L
    RA = N // RB
    NH = RA // 2
    two_pi = 2.0 * np.pi
    k1 = np.arange(RA, dtype=np.int64)

    def step1_table(NR):
        g = np.zeros((RB // 2, 2 * RA, 2 * NR), np.float64)
        for s in range(2):
            n = RB * np.arange(NR, dtype=np.int64)[None, :] + (2 * np.arange(RB // 2, dtype=np.int64) + s)[:, None]
            ang = two_pi * ((k1[None, :, None] * n[:, None, :]) % N) / N
            g[:, :RA, s * NR:(s + 1) * NR] = np.cos(ang)
            g[:, RA:, s * NR:(s + 1) * NR] = -np.sin(ang)
        return g

    phi = two_pi * ((np.arange(KB)[:, None] * np.arange(RB)[None, :]) % RB) / RB
    f3 = np.block([[np.cos(phi), np.sin(phi)], [-np.sin(phi), np.cos(phi)]])
    i1 = np.block([[np.cos(phi.T), -np.sin(phi.T)], [np.sin(phi.T), np.cos(phi.T)]])
    cast = lambda a: jnp.asarray(a.astype(np.float32)).astype(BF16)
    return dict(g_half=cast(step1_table(NH)), g_full=cast(step1_table(RA)), f3=cast(f3), i1=cast(i1))


def _fwd_step1(u_ref, g_ref, ab_ref, *, RA, NR, CB, G=2):
    def body(t, carry):
        ms = [t * G + j for j in range(G)]
        rhs = []
        for m in ms:
            ua = u_ref[pl.ds(2 * m, NR, stride=PITCH), :]
            ub = u_ref[pl.ds(2 * m + 1, NR, stride=PITCH), :]
            zz = jnp.zeros_like(ua)
            rhs.append(jnp.concatenate([jnp.concatenate([ua, zz], axis=1),
                                        jnp.concatenate([zz, ub], axis=1)], axis=0).astype(BF16))
        res = [jnp.dot(g_ref[m], r, preferred_element_type=F32) for m, r in zip(ms, rhs)]
        for m, r in zip(ms, res):
            ab_ref[pl.ds(2 * m, RA, stride=PITCH), :] = r[:RA, :CB]
            ab_ref[pl.ds(RA * PITCH + 2 * m, RA, stride=PITCH), :] = r[RA:, :CB]
            ab_ref[pl.ds(2 * m + 1, RA, stride=PITCH), :] = r[:RA, CB:]
            ab_ref[pl.ds(RA * PITCH + 2 * m + 1, RA, stride=PITCH), :] = r[RA:, CB:]
        return carry
    lax.fori_loop(0, RB // 2 // G, body, 0)


def _fwd_step3_pair(ab_ref, f3, kp, *, RA):
    cols = []
    for s in range(2):
        k1 = 2 * kp + s
        ar = ab_ref[pl.ds(pl.multiple_of(k1 * PITCH, SUBLANES), RB), :]
        ai = ab_ref[pl.ds(pl.multiple_of((RA + k1) * PITCH, SUBLANES), RB), :]
        cols.append(jnp.concatenate([ar, ai], axis=0))
    rhs = jnp.concatenate(cols, axis=1).astype(BF16)
    return jnp.dot(f3, rhs, preferred_element_type=F32)


def _for_row_blocks(nblocks, body, carry, unroll=2):
    assert nblocks >= 2
    carry = body(0, carry, True, False)
    carry = lax.fori_loop(1, nblocks - 1, lambda n1, c: body(n1, c, False, False), carry,
                          unroll=max(1, min(unroll, nblocks - 2)))
    return body(nblocks - 1, carry, False, True)


def _alt_sign(shape):
    rows = lax.broadcasted_iota(jnp.int32, shape, 0)
    return jnp.where(rows % 2 == 0, 1.0, -1.0).astype(F32)


def _filt_fft_kernel(hid_ref, w4_ref, dl_ref, g_ref, f3_ref, kh_ref, kn_ref, u_scr, ab_scr, *, L, CB, G):
    N = 2 * L
    RA = N // RB
    NH = RA // 2
    f3 = f3_ref[...]
    inv_lm1 = 1.0 / (L - 1)

    row = lax.broadcasted_iota(jnp.int32, (RB, CB), 0)
    arg_base = -(row.astype(F32) * inv_lm1) * dl_ref[...]
    arg_step = -(RB * inv_lm1) * dl_ref[...]
    arg_end = -(L * inv_lm1) * dl_ref[...]

    def fill(n1, carry, first, last):
        s_abs, s_alt = carry
        r0 = pl.multiple_of(n1 * RB, RB)
        h2 = jnp.dot(hid_ref[pl.ds(r0, RB), :], w4_ref[...], preferred_element_type=F32)
        arg = arg_base + jnp.asarray(n1, F32) * arg_step
        hf = h2[:, :CB] * jnp.exp(arg)
        hb = h2[:, CB:] * jnp.exp(arg_end - arg)
        if first:
            hb = jnp.where(row == 0, 0.0, hb)
        u_scr[pl.ds(pl.multiple_of(n1 * PITCH, SUBLANES), RB), :] = hf
        u_scr[pl.ds(pl.multiple_of((NH + n1) * PITCH, SUBLANES), RB), :] = hb
        return s_abs + (jnp.abs(hf) + jnp.abs(hb)), s_alt + (hf + hb)
    zero = jnp.zeros((RB, CB), F32)
    s_abs, s_alt = _for_row_blocks(NH, fill, (zero, zero), unroll=8)
    inv_l1 = 1.0 / jnp.sum(s_abs, axis=0, keepdims=True)
    kn = jnp.sum(s_alt * _alt_sign((RB, CB)), axis=0, keepdims=True) * inv_l1 * (1.0 / N)
    kn_ref[...] = jnp.broadcast_to(kn, kn_ref.shape)

    _fwd_step1(u_scr, g_ref, ab_scr, RA=RA, NR=RA, CB=CB, G=G)

    scale = inv_l1 * (2.0 / N)
    scale2 = jnp.concatenate([scale, scale], axis=1)
    row2 = lax.broadcasted_iota(jnp.int32, (KB, 2 * CB), 0)
    lane2 = lax.broadcasted_iota(jnp.int32, (KB, 2 * CB), 1)
    dc_pos = jnp.logical_and(row2 == 0, lane2 < CB)

    def emit(t, carry):
        kps = [t * G + j for j in range(G)]
        xs = [_fwd_step3_pair(ab_scr, f3, kp, RA=RA) for kp in kps]
        for kp, x in zip(kps, xs):
            kr = x[:KB] * scale2
            ki = x[KB:] * scale2
            dc = jnp.logical_and(dc_pos, kp == 0)
            kr = jnp.where(dc, 0.5 * kr, kr)
            ki = jnp.where(dc, 0.5 * ki, ki)
            kh_ref[pl.ds(pl.multiple_of(kp * RB, RB), RB), :] = jnp.concatenate([kr, ki], axis=0).astype(kh_ref.dtype)
        return carry
    lax.fori_loop(0, RA // 2 // G, emit, 0)


def _filt_fft(hid2, w4bd, deltas, g_tab, f3_tab, *, L, CB=LANES, G=16):
    C = deltas.shape[1]
    H2 = hid2.shape[1]
    N = 2 * L
    RA = N // RB
    nblk = C // CB
    G = min(G, RA // 2)
    return pl.pallas_call(
        functools.partial(_filt_fft_kernel, L=L, CB=CB, G=G),
        grid=(nblk,),
        in_specs=[
            _single((L, H2), lambda c: (0, 0)),
            pl.BlockSpec((H2, 2 * CB), lambda c: (0, c)),
            pl.BlockSpec((1, CB), lambda c: (0, c)),
            _single(g_tab.shape, lambda c: (0, 0, 0)),
            _single(f3_tab.shape, lambda c: (0, 0)),
        ],
        out_specs=[pl.BlockSpec((RA // 2 * RB, 2 * CB), lambda c: (0, c)),
                   pl.BlockSpec((SUBLANES, CB), lambda c: (0, c))],
        out_shape=[jax.ShapeDtypeStruct((RA // 2 * RB, 2 * C), BF16),
                   jax.ShapeDtypeStruct((SUBLANES, C), F32)],
        scratch_shapes=[pltpu.VMEM((RA * PITCH, CB), F32),
                        pltpu.VMEM((2 * RA * PITCH, CB), F32)],
        compiler_params=_cparams(("arbitrary",)),
        name="filt_fft",
    )(hid2, w4bd, deltas, g_tab, f3_tab)


def _short_conv_rows(p_ref, w_ref, b_ref, n1, *, first, last):
    r0 = pl.multiple_of(n1 * RB, RB)
    cur = p_ref[pl.ds(r0, RB), :]
    rows = lax.broadcasted_iota(jnp.int32, cur.shape, 0)
    if first:
        up = jnp.where(rows == 0, 0.0, pltpu.roll(cur, 1, axis=0))
    else:
        up = p_ref[pl.ds(r0 - 1, RB), :]
    if last:
        dn = jnp.where(rows == RB - 1, 0.0, pltpu.roll(cur, RB - 1, axis=0))
    else:
        dn = p_ref[pl.ds(r0 + 1, RB), :]
    w = w_ref[...]
    return b_ref[...] + up * w[0:1] + cur * w[1:2] + dn * w[2:3]


def _hy_conv_kernel(p_hbm, w0_ref, w1_ref, wv_ref, b0_ref, b1_ref, bv_ref,
                    kh_ref, kn_ref, skip_ref, og_ref, g_ref, f3_ref, i1_ref,
                    o_ref, u_scr, ab_scr, z_scr, p_scr, p_sem, *, L, CB, G):
    N = 2 * L
    RA = N // RB
    NH = RA // 2
    f3 = f3_ref[...]
    i1 = i1_ref[...]
    sign = _alt_sign((RB, CB))
    c = pl.program_id(0)
    nblk = pl.num_programs(0)
    p0_ref, p1_ref, pv_ref = p_scr.at[0], p_scr.at[1], p_scr.at[2]

    def fetch(group, blk):
        col = pl.multiple_of((group * nblk + blk) * CB, CB)
        return pltpu.make_async_copy(p_hbm.at[:, pl.ds(col, CB)], p_scr.at[group], p_sem.at[group])

    @pl.when(c == 0)
    def _():
        fetch(1, c).start()
        fetch(2, c).start()

    fetch(0, c).start()
    fetch(1, c).wait()
    fetch(2, c).wait()

    def fill(n1, s_alt, first, last):
        z = (_short_conv_rows(pv_ref, wv_ref, bv_ref, n1, first=first, last=last)
             * _short_conv_rows(p1_ref, w1_ref, b1_ref, n1, first=first, last=last))
        u_scr[pl.ds(pl.multiple_of(n1 * PITCH, SUBLANES), RB), :] = z
        z_scr[pl.ds(pl.multiple_of(n1 * RB, RB), RB), :] = z
        return s_alt + z
    s_alt = _for_row_blocks(NH, fill, jnp.zeros((RB, CB), F32))

    @pl.when(c + 1 < nblk)
    def _():
        fetch(1, c + 1).start()
        fetch(2, c + 1).start()

    z_nyq = jnp.sum(s_alt * sign, axis=0, keepdims=True)
    y_nyq = z_nyq * kn_ref[0:1, :]

    _fwd_step1(u_scr, g_ref, ab_scr, RA=RA, NR=NH, CB=CB, G=G)

    GM = max(G // 2, 1)

    def mid(t, carry):
        kps = [t * GM + j for j in range(GM)]
        xs = [_fwd_step3_pair(ab_scr, f3, kp, RA=RA) for kp in kps]
        khs = [kh_ref[pl.ds(pl.multiple_of(kp * RB, RB), RB), :].astype(F32) for kp in kps]
        bs = []
        for x, kh in zip(xs, khs):
            xr, xi, kr, ki = x[:KB], x[KB:], kh[:KB], kh[KB:]
            y = jnp.concatenate([xr * kr - xi * ki, xr * ki + xi * kr], axis=0).astype(BF16)
            bs.append(jnp.dot(i1, y, preferred_element_type=F32))
        for kp, b in zip(kps, bs):
            for s in range(2):
                k1 = 2 * kp + s
                ab_scr[pl.ds(pl.multiple_of(k1 * PITCH, SUBLANES), RB), :] = b[:RB, s * CB:(s + 1) * CB]
                ab_scr[pl.ds(pl.multiple_of((RA + k1) * PITCH, SUBLANES), RB), :] = b[RB:, s * CB:(s + 1) * CB]
        return carry
    lax.fori_loop(0, RA // 2 // GM, mid, 0)

    def last(t, carry):
        ms = [t * G + j for j in range(G)]
        rhs = [jnp.concatenate(
            [jnp.concatenate([ab_scr[pl.ds(2 * m + s, RA, stride=PITCH), :],
                              ab_scr[pl.ds(RA * PITCH + 2 * m + s, RA, stride=PITCH), :]], axis=0)
             for s in range(2)], axis=1).astype(BF16) for m in ms]
        ys = [_dot_tn(g_ref[m], r) for m, r in zip(ms, rhs)]
        for m, y in zip(ms, ys):
            u_scr[pl.ds(2 * m, NH, stride=PITCH), :] = y[:NH, :CB]
            u_scr[pl.ds(2 * m + 1, NH, stride=PITCH), :] = y[NH:, CB:]
        return carry
    lax.fori_loop(0, RB // 2 // G, last, 0)

    nyq_rows = sign * y_nyq
    fetch(0, c).wait()

    def finish(n1, carry, first, last):
        z = z_scr[pl.ds(pl.multiple_of(n1 * RB, RB), RB), :]
        x0 = _short_conv_rows(p0_ref, w0_ref, b0_ref, n1, first=first, last=last)
        conv = u_scr[pl.ds(pl.multiple_of(n1 * PITCH, SUBLANES), RB), :] + nyq_rows
        y = (conv + z * skip_ref[...]) * x0
        y = y * lax.rsqrt(jnp.mean(y * y, axis=-1, keepdims=True) + EPS) * og_ref[...]
        o_ref[pl.ds(pl.multiple_of(n1 * RB, RB), RB), :] = y.astype(o_ref.dtype)
        return carry
    _for_row_blocks(NH, finish, 0, unroll=4)


def _hy_conv(p, conv_w, conv_b, khat, knyq, skip, out_g, tabs, *, L, CB=LANES, G=16):
    C = skip.shape[1]
    assert CB == C // HY_GROUPS, "one channel block must be exactly one norm group"
    N = 2 * L
    RA = N // RB
    NH = RA // 2
    nblk = C // CB
    G = min(G, RA // 2)
    g_tab, f3_tab, i1_tab = tabs["g_half"], tabs["f3"], tabs["i1"]
    col = lambda off: (lambda c: (0, off * nblk + c))
    wspec = lambda off: pl.BlockSpec((3, CB), col(off))
    bspec = lambda off: pl.BlockSpec((1, CB), col(off))
    return pl.pallas_call(
        functools.partial(_hy_conv_kernel, L=L, CB=CB, G=G),
        grid=(nblk,),
        in_specs=[pl.BlockSpec(memory_space=pl.ANY),
                  wspec(0), wspec(1), wspec(2), bspec(0), bspec(1), bspec(2),
                  pl.BlockSpec((RA // 2 * RB, 2 * CB), col(0)),
                  pl.BlockSpec((SUBLANES, CB), col(0)),
                  bspec(0), bspec(0),
                  _single(g_tab.shape, lambda c: (0, 0, 0)),
                  _single(f3_tab.shape, lambda c: (0, 0)),
                  _single(i1_tab.shape, lambda c: (0, 0))],
        out_specs=pl.BlockSpec((L, CB), col(0)),
        out_shape=jax.ShapeDtypeStruct((L, C), BF16),
        scratch_shapes=[pltpu.VMEM((NH * PITCH, CB), F32),
                        pltpu.VMEM((2 * RA * PITCH, CB), F32),
                        pltpu.VMEM((L, CB), F32),
                        pltpu.VMEM((3, L, CB), F32),
                        pltpu.SemaphoreType.DMA((3,))],
        compiler_params=_cparams(("arbitrary",)),
        name="hy_conv",
    )(p, conv_w, conv_w, conv_w, conv_b, conv_b, conv_b, khat, knyq, skip, out_g,
      g_tab, f3_tab, i1_tab)


def _log_sigmoid(x):
    return jnp.minimum(x, 0.0) - jnp.log1p(jnp.exp(-jnp.abs(x)))


def _gla_scan_blocks(dirs, *, TB):
    nch = TB // CHUNK
    rows = lambda c: slice(c * CHUNK, (c + 1) * CHUNK)
    units = [(d, c) for c in range(nch) for d in dirs]
    rr = lax.broadcasted_iota(jnp.int32, (CHUNK, CHUNK), 0)
    cc = lax.broadcasted_iota(jnp.int32, (CHUNK, CHUNK), 1)
    scale = GLA_DK ** -0.5

    for d in dirs:
        lr_hi, lr_lo = _split_hi_lo(d["lr"][...])
        gate_in = jnp.dot(jnp.concatenate([lr_hi, lr_hi, lr_lo], axis=1), d["wg"],
                          preferred_element_type=F32) + d["bg"]
        g = _log_sigmoid(gate_in) * (1.0 / GATE_TEMP)
        g_hi = g.astype(BF16)
        g_mid, g_lo = _split_hi_lo(g - g_hi.astype(F32))
        d["parts"][...] = jnp.concatenate([g_hi, g_mid, g_lo], axis=1)
        d["csum"] = jnp.where((cc >= rr) if d["reverse"] else (cc <= rr), 1.0, 0.0).astype(BF16)
        d["mask"] = (cc > rr) if d["reverse"] else (cc <= rr)
        d["edge"] = 0 if d["reverse"] else CHUNK - 1

    for d, c in units:
        b3 = jnp.dot(d["csum"], d["parts"][rows(c), :], preferred_element_type=F32)
        d["b"][rows(c), :] = b3[:, :GLA_DK] + b3[:, GLA_DK:2 * GLA_DK] + b3[:, 2 * GLA_DK:]

    decay = {}
    for d, c in units:
        b = d["b"][rows(c), :]
        b_edge = b[d["edge"]:d["edge"] + 1]
        k = d["k"][rows(c), :]
        d["qks"][0, rows(c), :] = (d["q"][rows(c), :] * scale * jnp.exp(b)).astype(BF16)
        d["qks"][1, rows(c), :] = (k * jnp.exp(-b)).astype(BF16)
        d["qks"][2, rows(c), :] = (k * jnp.exp(b_edge - b)).astype(BF16)
        d["vb"][rows(c), :] = d["v"][rows(c), :].astype(BF16)
        decay[(id(d), c)] = jnp.exp(b_edge)

    for d, c in units:
        att = _dot_nt(d["qks"][0, rows(c), :], d["qks"][1, rows(c), :])
        d["att"][rows(c), :] = jnp.where(d["mask"], att, 0.0).astype(BF16)

    for d, c in units:
        d["o"][rows(c), :] = jnp.dot(d["att"][rows(c), :], d["vb"][rows(c), :], preferred_element_type=F32)

    for d, c in units:
        d["kv"][c * GLA_DV:(c + 1) * GLA_DV, :] = _dot_tn(d["vb"][rows(c), :], d["qks"][2, rows(c), :])

    state = {id(d): d["s"][...] for d in dirs}
    for step in range(nch):
        for d in dirs:
            c = nch - 1 - step if d["reverse"] else step
            s_t = state[id(d)]
            d["o"][rows(c), :] += _dot_nt(d["qks"][0, rows(c), :], s_t.astype(BF16))
            state[id(d)] = s_t * decay[(id(d), c)] + d["kv"][c * GLA_DV:(c + 1) * GLA_DV, :]
    for d in dirs:
        d["s"][...] = state[id(d)]


def _gla_kernel(qf_ref, kf_ref, vf_ref, lf_ref, rf_ref, qb_ref, kb_ref, vb_ref, lb_ref, rb_ref,
                wg_ref, bg_ref, og_ref, o_ref,
                s_scr, parts_scr, b_scr, qks_scr, vb_scr, att_scr, kv_scr, ob_scr, half_scr, *, TB):
    s = pl.program_id(1)
    nb = pl.num_programs(1)

    @pl.when(s == 0)
    def _():
        s_scr[...] = jnp.zeros_like(s_scr)

    def direction(i, q, k, v, lr):
        return dict(q=q, k=k, v=v, lr=lr, wg=wg_ref[i], bg=bg_ref[i], reverse=bool(i), s=s_scr.at[i],
                    parts=parts_scr.at[i], b=b_scr.at[i], qks=qks_scr.at[i], vb=vb_scr.at[i],
                    att=att_scr.at[i], kv=kv_scr.at[i], o=ob_scr.at[i])
    _gla_scan_blocks([direction(0, qf_ref, kf_ref, vf_ref, lf_ref),
                      direction(1, qb_ref, kb_ref, vb_ref, lb_ref)], TB=TB)

    first_half = s < nb // 2
    for d, blk, r_ref in ((0, s, rf_ref), (1, nb - 1 - s, rb_ref)):
        rows = pl.ds(pl.multiple_of(blk * TB, TB), TB)

        @pl.when(first_half)
        def _():
            half_scr[rows, :] = ob_scr[d]

        @pl.when(jnp.logical_not(first_half))
        def _():
            tot = half_scr[rows, :] + ob_scr[d]
            tot = tot * lax.rsqrt(jnp.mean(tot * tot, axis=-1, keepdims=True) + EPS) * og_ref[...]
            o_ref[rows, :] = (tot * jax.nn.silu(r_ref[...])).astype(o_ref.dtype)


def _gla(p, p_lr, w_gate, b_gate, out_g, *, L, col0, TB=1024):
    TB = min(TB, L)
    nb = L // TB
    assert nb % 2 == 0, "both scan directions must meet between two blocks"
    nch = TB // CHUNK
    qb = col0 // GLA_DK
    kb = qb + GLA_HEADS
    vb = (col0 + 2 * GLA_KW) // GLA_DV
    rb = vb + GLA_HEADS
    fwd = lambda s: s
    bwd = lambda s: nb - 1 - s

    def operands(blk):
        return [pl.BlockSpec((TB, GLA_DK), lambda h, s: (blk(s), qb + h)),
                pl.BlockSpec((TB, GLA_DK), lambda h, s: (blk(s), kb + h)),
                pl.BlockSpec((TB, GLA_DV), lambda h, s: (blk(s), vb + h)),
                pl.BlockSpec((TB, LANES), lambda h, s: (blk(s), 0)),
                pl.BlockSpec((TB, GLA_DV), lambda h, s: (blk(s), rb + h))]

    return pl.pallas_call(
        functools.partial(_gla_kernel, TB=TB),
        grid=(GLA_HEADS, nb),
        in_specs=operands(fwd) + operands(bwd) + [
            pl.BlockSpec((2, 3 * LANES, GLA_DK), lambda h, s: (0, 0, h)),
            pl.BlockSpec((2, 1, GLA_DK), lambda h, s: (0, 0, h)),
            pl.BlockSpec((1, GLA_DV), lambda h, s: (0, h)),
        ],
        out_specs=pl.BlockSpec((L, GLA_DV), lambda h, s: (0, h)),
        out_shape=jax.ShapeDtypeStruct((L, GLA_VW), BF16),
        scratch_shapes=[pltpu.VMEM((2, GLA_DV, GLA_DK), F32),
                        pltpu.VMEM((2, TB, 3 * GLA_DK), BF16),
                        pltpu.VMEM((2, TB, GLA_DK), F32),
                        pltpu.VMEM((2, 3, TB, GLA_DK), BF16),
                        pltpu.VMEM((2, TB, GLA_DV), BF16),
                        pltpu.VMEM((2, TB, CHUNK), BF16),
                        pltpu.VMEM((2, nch * GLA_DV, GLA_DK), F32),
                        pltpu.VMEM((2, TB, GLA_DV), F32),
                        pltpu.VMEM((L, GLA_DV), F32)],
        compiler_params=_cparams(("arbitrary", "arbitrary")),
        name="gla",
    )(*([p, p, p, p_lr, p] * 2), w_gate, b_gate, out_g)


def _filter_features(L):
    t = np.linspace(0.0, 1.0, L)[:, None]
    bands = (FILTER_EMB - 1) // 2
    freqs = np.linspace(1e-4, bands - 1, bands)[None, :]
    ang = (2.0 * np.pi / L) * np.arange(L)[:, None] * freqs
    feat = np.concatenate([t, np.cos(ang), -np.sin(ang)], axis=-1)
    feat_rev = np.roll(feat[::-1], 1, axis=0)
    both = np.zeros((L, LANES), np.float32)
    both[:, :FILTER_EMB] = feat
    both[:, FILTER_EMB:2 * FILTER_EMB] = feat_rev
    return jnp.asarray(both)


def _twice(w, rows_out):
    r, c = w.shape
    out = jnp.zeros((rows_out, 2 * c), F32)
    return out.at[:r, :c].set(w).at[r:2 * r, c:].set(w)


def _filter_deltas():
    min_decay = math.log(DECAY_TARGET) / LONG_DECAY_PCT
    max_decay = math.log(DECAY_TARGET) / SHORT_DECAY_PCT
    return jnp.abs(jnp.linspace(min_decay, max_decay, HY_WIDTH, dtype=F32)).reshape(1, HY_WIDTH)


def _block_diag_w4(w4):
    H = w4.shape[0]
    nblk = HY_WIDTH // LANES
    wf = w4[:, :HY_WIDTH].reshape(H, nblk, LANES)
    wb = w4[:, HY_WIDTH:].reshape(H, nblk, LANES)
    zz = jnp.zeros_like(wf)
    top = jnp.concatenate([wf, zz], axis=2)
    bot = jnp.concatenate([zz, wb], axis=2)
    return jnp.concatenate([top, bot], axis=0).reshape(2 * H, nblk * 2 * LANES)


def kernel(x, ffn1_norm, ffn1_w_gate, ffn1_w_up, ffn1_w_down, mix_norm, w_in, hy_conv_w, hy_conv_b, flt_w1, flt_b1, flt_f1, flt_w2, flt_b2, flt_f2, flt_w3, flt_b3, flt_f3, flt_w4, hy_skip, hy_out_norm, gla_w_a2_f, gla_b_a_f, gla_w_a2_b, gla_b_a_b, gla_out_norm, w_out, ffn2_norm, ffn2_w_gate, ffn2_w_up, ffn2_w_down, final_norm):
    B, L, D = x.shape
    depth = ffn1_norm.shape[0]
    tabs = _dft_tables(L)
    feat2 = _filter_features(L)
    deltas = _filter_deltas()
    gla_col0 = 3 * HY_WIDTH
    H2 = 2 * FILTER_HIDDEN
    both = lambda a: jnp.concatenate([a, a]).reshape(1, H2)
    filt = []
    for l in range(depth):
        early = [jnp.swapaxes(w_in[l], 0, 1)]
        if l == 0:
            early += [ffn1_w_gate[0], ffn1_w_up[0], ffn1_w_down[0]]
        hid3, cast = _filt_mlp(feat2, _twice(flt_w1[l], LANES), both(flt_b1[l]), both(flt_f1[l]),
                               _twice(flt_w2[l], H2), both(flt_b2[l]), both(flt_f2[l]),
                               _twice(flt_w3[l], H2), both(flt_b3[l]), both(flt_f3[l]), to_cast=early)
        w4cat = _three_pass_rows(_block_diag_w4(flt_w4[l]))
        khat, knyq = _filt_fft(hid3, w4cat, deltas, tabs["g_full"], tabs["f3"], L=L)
        filt.append((khat, knyq, cast))

    outs = []
    for bi in range(B):
        xb = x[bi]
        w1 = filt[0][2][1:]
        for l in range(depth):
            last = l == depth - 1
            khat, knyq, (w_in_t, *_) = filt[l]
            later = [ffn2_w_gate[l], ffn2_w_up[l], ffn2_w_down[l], w_out[l]]
            xb, (w2_gate, w2_up, w2_down, w_out_bf) = _ffn(xb, ffn1_norm[l], *w1, final_norm, final_norm=False,
                                                           to_cast=later)
            p, p_lr = _in_proj(xb, mix_norm[l], w_in_t)
            y_hy = _hy_conv(p, hy_conv_w[l], hy_conv_b[l].reshape(1, -1), khat, knyq,
                            hy_skip[l].reshape(1, -1), hy_out_norm[l].reshape(1, -1), tabs, L=L)

            w_gate = jnp.zeros((2, LANES, GLA_KW), F32)
            w_gate = w_gate.at[0, :GATE_RANK].set(gla_w_a2_f[l]).at[1, GATE_RANK:2 * GATE_RANK].set(gla_w_a2_b[l])
            w_gate = _three_pass_rows(w_gate)
            b_gate = jnp.stack([gla_b_a_f[l], gla_b_a_b[l]]).reshape(2, 1, GLA_KW)
            y_gla = _gla(p, p_lr, w_gate, b_gate, gla_out_norm[l].reshape(1, -1), L=L, col0=gla_col0)

            xb = _out_proj(xb, y_hy, y_gla, w_out_bf)
            nxt = [] if last else [ffn1_w_gate[l + 1], ffn1_w_up[l + 1], ffn1_w_down[l + 1]]
            xb, w1 = _ffn(xb, ffn2_norm[l], w2_gate, w2_up, w2_down, final_norm, final_norm=last, to_cast=nxt)
        outs.append(xb)
    return jnp.stack(outs)
```

```python
import functools
import math

import numpy as np
import jax
import jax.numpy as jnp
from jax import lax
from jax.experimental import pallas as pl
from jax.experimental.pallas import tpu as pltpu

F32 = jnp.float32
BF16 = jnp.bfloat16
HIGHEST = lax.Precision.HIGHEST

EPS = 1e-6
HY_WIDTH = 1024
HY_GROUPS = 8
FILTER_EMB = 33
FILTER_HIDDEN = 64
SHORT_DECAY_PCT = 0.3
LONG_DECAY_PCT = 1.5
DECAY_TARGET = 1e-2
GLA_HEADS = 4
GLA_DK = 128
GLA_DV = 256
GLA_KW = GLA_HEADS * GLA_DK
GLA_VW = GLA_HEADS * GLA_DV
GATE_RANK = 16
GATE_TEMP = 16.0
CHUNK = 64

LANES = 128
SUBLANES = 8
VMEM_LIMIT = 60 * 1024 * 1024
FFN_SLAB = 128
RB = 128
KB = RB // 2
PITCH = RB + SUBLANES


def _cparams(sem):
    return pltpu.CompilerParams(dimension_semantics=sem, vmem_limit_bytes=VMEM_LIMIT)


def _single(block_shape, index_map):
    return pl.BlockSpec(block_shape, index_map, pipeline_mode=pl.Buffered(1))


def _split_hi_lo(x):
    hi = x.astype(BF16)
    return hi, (x - hi.astype(F32)).astype(BF16)


def _three_pass_rows(w):
    w_hi, w_lo = _split_hi_lo(w)
    return jnp.concatenate([w_hi, w_lo, w_hi], axis=-2)


def _dot_nt(a, b):
    return lax.dot_general(a, b, (((1,), (1,)), ((), ())), preferred_element_type=F32)


def _dot_tn(a, b):
    return lax.dot_general(a, b, (((0,), (0,)), ((), ())), preferred_element_type=F32)


def _ffn_kernel(x_ref, g_ref, wg_ref, wu_ref, wd_ref, fg_ref, *rest, final_norm, n_cast):
    cast_src, o_ref, cast_dst, h_scr = rest[:n_cast], rest[n_cast], rest[n_cast + 1:-1], rest[-1]
    j = pl.program_id(1)

    slab = min(FFN_SLAB, x_ref.shape[0])
    nslab = x_ref.shape[0] // slab

    @pl.when(j == 0)
    def _():
        def norm_rows(t, carry):
            rows = pl.ds(pl.multiple_of(t * slab, slab), slab)
            x = x_ref[rows, :]
            r = lax.rsqrt(jnp.mean(x * x, axis=-1, keepdims=True) + EPS)
            h_scr[rows, :] = (x * r * g_ref[...]).astype(BF16)
            o_ref[rows, :] = jnp.zeros((slab, x_ref.shape[1]), F32)
            return carry
        lax.fori_loop(0, nslab, norm_rows, 0)

    h = h_scr[...]
    gate = jnp.dot(h, wg_ref[...], preferred_element_type=F32)
    up = jnp.dot(h, wu_ref[...], preferred_element_type=F32)
    a = (jax.nn.silu(gate) * up).astype(BF16)
    o_ref[...] += jnp.dot(a, wd_ref[...], preferred_element_type=F32)

    for src, dst in zip(cast_src, cast_dst):
        dst[...] = src[...].astype(BF16)

    @pl.when(j == pl.num_programs(1) - 1)
    def _():
        def finish_rows(t, carry):
            rows = pl.ds(pl.multiple_of(t * slab, slab), slab)
            y = x_ref[rows, :] + 0.5 * o_ref[rows, :]
            if final_norm:
                r = lax.rsqrt(jnp.mean(y * y, axis=-1, keepdims=True) + EPS)
                y = y * r * fg_ref[...]
            o_ref[rows, :] = y
            return carry
        lax.fori_loop(0, nslab, finish_rows, 0)


def _cast_blocking(shape, ni, nj):
    rows, cols = shape
    assert rows % ni == 0 and (rows // ni) % SUBLANES == 0 and cols % LANES == 0
    ncol = max(d for d in range(1, nj + 1) if (cols // LANES) % d == 0)
    return (rows // ni, cols // ncol), (lambda i, j: (i, jnp.minimum(j, ncol - 1)))


def _ffn(x, norm_g, w_gate, w_up, w_down, final_g, *, final_norm, to_cast=(), tm=1024, tf=512):
    L, D = x.shape
    DF = w_gate.shape[1]
    tm = min(tm, L)
    tf = min(tf, DF)
    ni, nj = L // tm, DF // tf
    blockings = [_cast_blocking(w.shape, ni, nj) for w in to_cast]
    cast_specs = [pl.BlockSpec(blk, imap) for blk, imap in blockings]
    outs = pl.pallas_call(
        functools.partial(_ffn_kernel, final_norm=final_norm, n_cast=len(to_cast)),
        grid=(ni, nj),
        in_specs=[
            pl.BlockSpec((tm, D), lambda i, j: (i, 0)),
            pl.BlockSpec((1, D), lambda i, j: (0, 0)),
            pl.BlockSpec((D, tf), lambda i, j: (0, j)),
            pl.BlockSpec((D, tf), lambda i, j: (0, j)),
            pl.BlockSpec((tf, D), lambda i, j: (j, 0)),
            pl.BlockSpec((1, D), lambda i, j: (0, 0)),
        ] + cast_specs,
        out_specs=[pl.BlockSpec((tm, D), lambda i, j: (i, 0))] + cast_specs,
        out_shape=[jax.ShapeDtypeStruct((L, D), F32)] + [jax.ShapeDtypeStruct(w.shape, BF16) for w in to_cast],
        scratch_shapes=[pltpu.VMEM((tm, D), BF16)],
        compiler_params=_cparams(("arbitrary", "arbitrary")),
        name="ffn",
    )(x, norm_g.reshape(1, D), w_gate, w_up, w_down, final_g.reshape(1, D), *to_cast)
    return outs[0], outs[1:]


def _in_proj_kernel(x_ref, g_ref, wt_ref, wtail_ref, o_ref, otail_ref, h_scr, *, n_tail):
    j = pl.program_id(1)

    @pl.when(j == 0)
    def _():
        x = x_ref[...]
        r = lax.rsqrt(jnp.mean(x * x, axis=-1, keepdims=True) + EPS)
        h = (x * r * g_ref[...]).astype(BF16)
        h_scr[...] = h
        tail = _dot_nt(h, wtail_ref[...])
        col = lax.broadcasted_iota(jnp.int32, tail.shape, 1)
        otail_ref[...] = jnp.where(col < n_tail, tail, 0.0)

    o_ref[...] = _dot_nt(h_scr[...], wt_ref[...])


def _in_proj(x, norm_g, w_t, *, tm=1024, tn=2048):
    L, D = x.shape
    n_cols = w_t.shape[0]
    nj = n_cols // tn
    n_tail = n_cols - nj * tn
    assert 0 < n_tail <= LANES and (nj * tn) % LANES == 0
    tm = min(tm, L)
    return pl.pallas_call(
        functools.partial(_in_proj_kernel, n_tail=n_tail),
        grid=(L // tm, nj),
        in_specs=[
            pl.BlockSpec((tm, D), lambda i, j: (i, 0)),
            pl.BlockSpec((1, D), lambda i, j: (0, 0)),
            pl.BlockSpec((tn, D), lambda i, j: (j, 0)),
            pl.BlockSpec((LANES, D), lambda i, j: (nj * tn // LANES, 0)),
        ],
        out_specs=[pl.BlockSpec((tm, tn), lambda i, j: (i, j)),
                   pl.BlockSpec((tm, LANES), lambda i, j: (i, 0))],
        out_shape=[jax.ShapeDtypeStruct((L, nj * tn), F32),
                   jax.ShapeDtypeStruct((L, LANES), F32)],
        scratch_shapes=[pltpu.VMEM((tm, D), BF16)],
        compiler_params=_cparams(("parallel", "arbitrary")),
        name="in_proj",
    )(x, norm_g.reshape(1, D), w_t, w_t)


def _out_proj_kernel(x_ref, yh_ref, yg_ref, wh_ref, wg_ref, o_ref):
    o_ref[...] = (x_ref[...]
                  + jnp.dot(yh_ref[...], wh_ref[...], preferred_element_type=F32)
                  + jnp.dot(yg_ref[...], wg_ref[...], preferred_element_type=F32))


def _out_proj(x, y_hy, y_gla, w_out, *, tm=512):
    L, D = x.shape
    WH = y_hy.shape[1]
    WG = y_gla.shape[1]
    tm = min(tm, L)
    w = w_out
    return pl.pallas_call(
        _out_proj_kernel,
        grid=(L // tm,),
        in_specs=[
            pl.BlockSpec((tm, D), lambda i: (i, 0)),
            pl.BlockSpec((tm, WH), lambda i: (i, 0)),
            pl.BlockSpec((tm, WG), lambda i: (i, 0)),
            _single((WH, D), lambda i: (0, 0)),
            _single((WG, D), lambda i: (0, 0)),
        ],
        out_specs=pl.BlockSpec((tm, D), lambda i: (i, 0)),
        out_shape=jax.ShapeDtypeStruct((L, D), F32),
        compiler_params=_cparams(("parallel",)),
        name="out_proj",
    )(x, y_hy, y_gla, w[:WH], w[WH:])


def _filt_mlp_kernel(z_ref, w1_ref, b1_ref, f1_ref, w2_ref, b2_ref, f2_ref, w3_ref, b3_ref, f3_ref, *rest, n_cast):
    cast_src, o_ref, cast_dst = rest[:n_cast], rest[n_cast], rest[n_cast + 1:]
    dot = functools.partial(jnp.dot, precision=HIGHEST, preferred_element_type=F32)
    hid = jnp.sin(f1_ref[...] * (dot(z_ref[...], w1_ref[...]) + b1_ref[...]))
    hid = jnp.sin(f2_ref[...] * (dot(hid, w2_ref[...]) + b2_ref[...]))
    hid = jnp.sin(f3_ref[...] * (dot(hid, w3_ref[...]) + b3_ref[...]))
    hi, lo = _split_hi_lo(hid)
    o_ref[...] = jnp.concatenate([hi, hi, lo], axis=1)
    for src, dst in zip(cast_src, cast_dst):
        dst[...] = src[...].astype(BF16)


def _cast_row_blocks(shape, n):
    rows, cols = shape
    tile = 2 * SUBLANES
    block_rows = pl.cdiv(pl.cdiv(rows, n), tile) * tile
    nblocks = pl.cdiv(rows, block_rows)
    return (block_rows, cols), (lambda i: (jnp.minimum(i, nblocks - 1), 0))


def _filt_mlp(feat2, w1, b1, f1, w2, b2, f2, w3, b3, f3, *, to_cast=(), tm=512):
    L, FP = feat2.shape
    H2 = w2.shape[0]
    tm = min(tm, L)
    steps = L // tm
    full = lambda shp: pl.BlockSpec(shp, lambda i: (0, 0))
    cast_specs = [pl.BlockSpec(*_cast_row_blocks(w.shape, steps)) for w in to_cast]
    outs = pl.pallas_call(
        functools.partial(_filt_mlp_kernel, n_cast=len(to_cast)),
        grid=(steps,),
        in_specs=[pl.BlockSpec((tm, FP), lambda i: (i, 0)),
                  full((FP, H2)), full((1, H2)), full((1, H2)),
                  full((H2, H2)), full((1, H2)), full((1, H2)),
                  full((H2, H2)), full((1, H2)), full((1, H2))] + cast_specs,
        out_specs=[pl.BlockSpec((tm, 3 * H2), lambda i: (i, 0))] + cast_specs,
        out_shape=[jax.ShapeDtypeStruct((L, 3 * H2), BF16)] + [jax.ShapeDtypeStruct(w.shape, BF16) for w in to_cast],
        compiler_params=_cparams(("arbitrary",)),
        name="filt_mlp",
    )(feat2, w1, b1, f1, w2, b2, f2, w3, b3, f3, *to_cast)
    return outs[0], outs[1:]


def _dft_tables(L):
    N = 2 * L
    RA = N // RB
    NH = RA // 2
    two_pi = 2.0 * np.pi
    k1 = np.arange(RA, dtype=np.int64)

    def step1_table(NR):
        g = np.zeros((RB // 2, 2 * RA, 2 * NR), np.float64)
        for s in range(2):
            n = RB * np.arange(NR, dtype=np.int64)[None, :] + (2 * np.arange(RB // 2, dtype=np.int64) + s)[:, None]
            ang = two_pi * ((k1[None, :, None] * n[:, None, :]) % N) / N
            g[:, :RA, s * NR:(s + 1) * NR] = np.cos(ang)
            g[:, RA:, s * NR:(s + 1) * NR] = -np.sin(ang)
        return g

    phi = two_pi * ((np.arange(KB)[:, None] * np.arange(RB)[None, :]) % RB) / RB
    f3 = np.block([[np.cos(phi), np.sin(phi)], [-np.sin(phi), np.cos(phi)]])
    i1 = np.block([[np.cos(phi.T), -np.sin(phi.T)], [np.sin(phi.T), np.cos(phi.T)]])
    cast = lambda a: jnp.asarray(a.astype(np.float32)).astype(BF16)
    return dict(g_half=cast(step1_table(NH)), g_full=cast(step1_table(RA)), f3=cast(f3), i1=cast(i1))


def _fwd_step1(u_ref, g_ref, ab_ref, *, RA, NR, CB, G=2):
    def body(t, carry):
        ms = [t * G + j for j in range(G)]
        rhs = []
        for m in ms:
            ua = u_ref[pl.ds(2 * m, NR, stride=PITCH), :]
            ub = u_ref[pl.ds(2 * m + 1, NR, stride=PITCH), :]
            zz = jnp.zeros_like(ua)
            rhs.append(jnp.concatenate([jnp.concatenate([ua, zz], axis=1),
                                        jnp.concatenate([zz, ub], axis=1)], axis=0).astype(BF16))
        res = [jnp.dot(g_ref[m], r, preferred_element_type=F32) for m, r in zip(ms, rhs)]
        for m, r in zip(ms, res):
            ab_ref[pl.ds(2 * m, RA, stride=PITCH), :] = r[:RA, :CB]
            ab_ref[pl.ds(RA * PITCH + 2 * m, RA, stride=PITCH), :] = r[RA:, :CB]
            ab_ref[pl.ds(2 * m + 1, RA, stride=PITCH), :] = r[:RA, CB:]
            ab_ref[pl.ds(RA * PITCH + 2 * m + 1, RA, stride=PITCH), :] = r[RA:, CB:]
        return carry
    lax.fori_loop(0, RB // 2 // G, body, 0)


def _fwd_step3_pair(ab_ref, f3, kp, *, RA):
    cols = []
    for s in range(2):
        k1 = 2 * kp + s
        ar = ab_ref[pl.ds(pl.multiple_of(k1 * PITCH, SUBLANES), RB), :]
        ai = ab_ref[pl.ds(pl.multiple_of((RA + k1) * PITCH, SUBLANES), RB), :]
        cols.append(jnp.concatenate([ar, ai], axis=0))
    rhs = jnp.concatenate(cols, axis=1).astype(BF16)
    return jnp.dot(f3, rhs, preferred_element_type=F32)


def _for_row_blocks(nblocks, body, carry, unroll=2):
    assert nblocks >= 2
    carry = body(0, carry, True, False)
    carry = lax.fori_loop(1, nblocks - 1, lambda n1, c: body(n1, c, False, False), carry,
                          unroll=max(1, min(unroll, nblocks - 2)))
    return body(nblocks - 1, carry, False, True)


def _alt_sign(shape):
    rows = lax.broadcasted_iota(jnp.int32, shape, 0)
    return jnp.where(rows % 2 == 0, 1.0, -1.0).astype(F32)


def _filt_fft_kernel(hid_ref, w4_ref, dl_ref, g_ref, f3_ref, kh_ref, kn_ref, u_scr, ab_scr, *, L, CB, G):
    N = 2 * L
    RA = N // RB
    NH = RA // 2
    f3 = f3_ref[...]
    inv_lm1 = 1.0 / (L - 1)

    row = lax.broadcasted_iota(jnp.int32, (RB, CB), 0)
    arg_base = -(row.astype(F32) * inv_lm1) * dl_ref[...]
    arg_step = -(RB * inv_lm1) * dl_ref[...]
    arg_end = -(L * inv_lm1) * dl_ref[...]

    def fill(n1, carry, first, last):
        s_abs, s_alt = carry
        r0 = pl.multiple_of(n1 * RB, RB)
        h2 = jnp.dot(hid_ref[pl.ds(r0, RB), :], w4_ref[...], preferred_element_type=F32)
        arg = arg_base + jnp.asarray(n1, F32) * arg_step
        hf = h2[:, :CB] * jnp.exp(arg)
        hb = h2[:, CB:] * jnp.exp(arg_end - arg)
        if first:
            hb = jnp.where(row == 0, 0.0, hb)
        u_scr[pl.ds(pl.multiple_of(n1 * PITCH, SUBLANES), RB), :] = hf
        u_scr[pl.ds(pl.multiple_of((NH + n1) * PITCH, SUBLANES), RB), :] = hb
        return s_abs + (jnp.abs(hf) + jnp.abs(hb)), s_alt + (hf + hb)
    zero = jnp.zeros((RB, CB), F32)
    s_abs, s_alt = _for_row_blocks(NH, fill, (zero, zero), unroll=8)
    inv_l1 = 1.0 / jnp.sum(s_abs, axis=0, keepdims=True)
    kn = jnp.sum(s_alt * _alt_sign((RB, CB)), axis=0, keepdims=True) * inv_l1 * (1.0 / N)
    kn_ref[...] = jnp.broadcast_to(kn, kn_ref.shape)

    _fwd_step1(u_scr, g_ref, ab_scr, RA=RA, NR=RA, CB=CB, G=G)

    scale = inv_l1 * (2.0 / N)
    scale2 = jnp.concatenate([scale, scale], axis=1)
    row2 = lax.broadcasted_iota(jnp.int32, (KB, 2 * CB), 0)
    lane2 = lax.broadcasted_iota(jnp.int32, (KB, 2 * CB), 1)
    dc_pos = jnp.logical_and(row2 == 0, lane2 < CB)

    def emit(t, carry):
        kps = [t * G + j for j in range(G)]
        xs = [_fwd_step3_pair(ab_scr, f3, kp, RA=RA) for kp in kps]
        for kp, x in zip(kps, xs):
            kr = x[:KB] * scale2
            ki = x[KB:] * scale2
            dc = jnp.logical_and(dc_pos, kp == 0)
            kr = jnp.where(dc, 0.5 * kr, kr)
            ki = jnp.where(dc, 0.5 * ki, ki)
            kh_ref[pl.ds(pl.multiple_of(kp * RB, RB), RB), :] = jnp.concatenate([kr, ki], axis=0).astype(kh_ref.dtype)
        return carry
    lax.fori_loop(0, RA // 2 // G, emit, 0)


def _filt_fft(hid2, w4bd, deltas, g_tab, f3_tab, *, L, CB=LANES, G=16):
    C = deltas.shape[1]
    H2 = hid2.shape[1]
    N = 2 * L
    RA = N // RB
    nblk = C // CB
    G = min(G, RA // 2)
    return pl.pallas_call(
        functools.partial(_filt_fft_kernel, L=L, CB=CB, G=G),
        grid=(nblk,),
        in_specs=[
            _single((L, H2), lambda c: (0, 0)),
            pl.BlockSpec((H2, 2 * CB), lambda c: (0, c)),
            pl.BlockSpec((1, CB), lambda c: (0, c)),
            _single(g_tab.shape, lambda c: (0, 0, 0)),
            _single(f3_tab.shape, lambda c: (0, 0)),
        ],
        out_specs=[pl.BlockSpec((RA // 2 * RB, 2 * CB), lambda c: (0, c)),
                   pl.BlockSpec((SUBLANES, CB), lambda c: (0, c))],
        out_shape=[jax.ShapeDtypeStruct((RA // 2 * RB, 2 * C), BF16),
                   jax.ShapeDtypeStruct((SUBLANES, C), F32)],
        scratch_shapes=[pltpu.VMEM((RA * PITCH, CB), F32),
                        pltpu.VMEM((2 * RA * PITCH, CB), F32)],
        compiler_params=_cparams(("arbitrary",)),
        name="filt_fft",
    )(hid2, w4bd, deltas, g_tab, f3_tab)


def _short_conv_rows(p_ref, w_ref, b_ref, n1, *, first, last):
    r0 = pl.multiple_of(n1 * RB, RB)
    cur = p_ref[pl.ds(r0, RB), :]
    rows = lax.broadcasted_iota(jnp.int32, cur.shape, 0)
    if first:
        up = jnp.where(rows == 0, 0.0, pltpu.roll(cur, 1, axis=0))
    else:
        up = p_ref[pl.ds(r0 - 1, RB), :]
    if last:
        dn = jnp.where(rows == RB - 1, 0.0, pltpu.roll(cur, RB - 1, axis=0))
    else:
        dn = p_ref[pl.ds(r0 + 1, RB), :]
    w = w_ref[...]
    return b_ref[...] + up * w[0:1] + cur * w[1:2] + dn * w[2:3]


def _hy_conv_kernel(p_hbm, w0_ref, w1_ref, wv_ref, b0_ref, b1_ref, bv_ref,
                    kh_ref, kn_ref, skip_ref, og_ref, g_ref, f3_ref, i1_ref,
                    o_ref, u_scr, ab_scr, z_scr, p_scr, p_sem, *, L, CB, G):
    N = 2 * L
    RA = N // RB
    NH = RA // 2
    f3 = f3_ref[...]
    i1 = i1_ref[...]
    sign = _alt_sign((RB, CB))
    c = pl.program_id(0)
    nblk = pl.num_programs(0)
    p0_ref, p1_ref, pv_ref = p_scr.at[0], p_scr.at[1], p_scr.at[2]

    def fetch(group, blk):
        col = pl.multiple_of((group * nblk + blk) * CB, CB)
        return pltpu.make_async_copy(p_hbm.at[:, pl.ds(col, CB)], p_scr.at[group], p_sem.at[group])

    @pl.when(c == 0)
    def _():
        fetch(1, c).start()
        fetch(2, c).start()

    fetch(0, c).start()
    fetch(1, c).wait()
    fetch(2, c).wait()

    def fill(n1, s_alt, first, last):
        z = (_short_conv_rows(pv_ref, wv_ref, bv_ref, n1, first=first, last=last)
             * _short_conv_rows(p1_ref, w1_ref, b1_ref, n1, first=first, last=last))
        u_scr[pl.ds(pl.multiple_of(n1 * PITCH, SUBLANES), RB), :] = z
        z_scr[pl.ds(pl.multiple_of(n1 * RB, RB), RB), :] = z
        return s_alt + z
    s_alt = _for_row_blocks(NH, fill, jnp.zeros((RB, CB), F32))

    @pl.when(c + 1 < nblk)
    def _():
        fetch(1, c + 1).start()
        fetch(2, c + 1).start()

    z_nyq = jnp.sum(s_alt * sign, axis=0, keepdims=True)
    y_nyq = z_nyq * kn_ref[0:1, :]

    _fwd_step1(u_scr, g_ref, ab_scr, RA=RA, NR=NH, CB=CB, G=G)

    GM = max(G // 2, 1)

    def mid(t, carry):
        kps = [t * GM + j for j in range(GM)]
        xs = [_fwd_step3_pair(ab_scr, f3, kp, RA=RA) for kp in kps]
        khs = [kh_ref[pl.ds(pl.multiple_of(kp * RB, RB), RB), :].astype(F32) for kp in kps]
        bs = []
        for x, kh in zip(xs, khs):
            xr, xi, kr, ki = x[:KB], x[KB:], kh[:KB], kh[KB:]
            y = jnp.concatenate([xr * kr - xi * ki, xr * ki + xi * kr], axis=0).astype(BF16)
            bs.append(jnp.dot(i1, y, preferred_element_type=F32))
        for kp, b in zip(kps, bs):
            for s in range(2):
                k1 = 2 * kp + s
                ab_scr[pl.ds(pl.multiple_of(k1 * PITCH, SUBLANES), RB), :] = b[:RB, s * CB:(s + 1) * CB]
                ab_scr[pl.ds(pl.multiple_of((RA + k1) * PITCH, SUBLANES), RB), :] = b[RB:, s * CB:(s + 1) * CB]
        return carry
    lax.fori_loop(0, RA // 2 // GM, mid, 0)

    def last(t, carry):
        ms = [t * G + j for j in range(G)]
        rhs = [jnp.concatenate(
            [jnp.concatenate([ab_scr[pl.ds(2 * m + s, RA, stride=PITCH), :],
                              ab_scr[pl.ds(RA * PITCH + 2 * m + s, RA, stride=PITCH), :]], axis=0)
             for s in range(2)], axis=1).astype(BF16) for m in ms]
        ys = [_dot_tn(g_ref[m], r) for m, r in zip(ms, rhs)]
        for m, y in zip(ms, ys):
            u_scr[pl.ds(2 * m, NH, stride=PITCH), :] = y[:NH, :CB]
            u_scr[pl.ds(2 * m + 1, NH, stride=PITCH), :] = y[NH:, CB:]
        return carry
    lax.fori_loop(0, RB // 2 // G, last, 0)

    nyq_rows = sign * y_nyq
    fetch(0, c).wait()

    def finish(n1, carry, first, last):
        z = z_scr[pl.ds(pl.multiple_of(n1 * RB, RB), RB), :]
        x0 = _short_conv_rows(p0_ref, w0_ref, b0_ref, n1, first=first, last=last)
        conv = u_scr[pl.ds(pl.multiple_of(n1 * PITCH, SUBLANES), RB), :] + nyq_rows
        y = (conv + z * skip_ref[...]) * x0
        y = y * lax.rsqrt(jnp.mean(y * y, axis=-1, keepdims=True) + EPS) * og_ref[...]
        o_ref[pl.ds(pl.multiple_of(n1 * RB, RB), RB), :] = y.astype(o_ref.dtype)
        return carry
    _for_row_blocks(NH, finish, 0, unroll=4)


def _hy_conv(p, conv_w, conv_b, khat, knyq, skip, out_g, tabs, *, L, CB=LANES, G=16):
    C = skip.shape[1]
    assert CB == C // HY_GROUPS, "one channel block must be exactly one norm group"
    N = 2 * L
    RA = N // RB
    NH = RA // 2
    nblk = C // CB
    G = min(G, RA // 2)
    g_tab, f3_tab, i1_tab = tabs["g_half"], tabs["f3"], tabs["i1"]
    col = lambda off: (lambda c: (0, off * nblk + c))
    wspec = lambda off: pl.BlockSpec((3, CB), col(off))
    bspec = lambda off: pl.BlockSpec((1, CB), col(off))
    return pl.pallas_call(
        functools.partial(_hy_conv_kernel, L=L, CB=CB, G=G),
        grid=(nblk,),
        in_specs=[pl.BlockSpec(memory_space=pl.ANY),
                  wspec(0), wspec(1), wspec(2), bspec(0), bspec(1), bspec(2),
                  pl.BlockSpec((RA // 2 * RB, 2 * CB), col(0)),
                  pl.BlockSpec((SUBLANES, CB), col(0)),
                  bspec(0), bspec(0),
                  _single(g_tab.shape, lambda c: (0, 0, 0)),
                  _single(f3_tab.shape, lambda c: (0, 0)),
                  _single(i1_tab.shape, lambda c: (0, 0))],
        out_specs=pl.BlockSpec((L, CB), col(0)),
        out_shape=jax.ShapeDtypeStruct((L, C), BF16),
        scratch_shapes=[pltpu.VMEM((NH * PITCH, CB), F32),
                        pltpu.VMEM((2 * RA * PITCH, CB), F32),
                        pltpu.VMEM((L, CB), F32),
                        pltpu.VMEM((3, L, CB), F32),
                        pltpu.SemaphoreType.DMA((3,))],
        compiler_params=_cparams(("arbitrary",)),
        name="hy_conv",
    )(p, conv_w, conv_w, conv_w, conv_b, conv_b, conv_b, khat, knyq, skip, out_g,
      g_tab, f3_tab, i1_tab)


def _log_sigmoid(x):
    return jnp.minimum(x, 0.0) - jnp.log1p(jnp.exp(-jnp.abs(x)))


def _gla_scan_blocks(dirs, *, TB):
    nch = TB // CHUNK
    rows = lambda c: slice(c * CHUNK, (c + 1) * CHUNK)
    units = [(d, c) for c in range(nch) for d in dirs]
    rr = lax.broadcasted_iota(jnp.int32, (CHUNK, CHUNK), 0)
    cc = lax.broadcasted_iota(jnp.int32, (CHUNK, CHUNK), 1)
    scale = GLA_DK ** -0.5

    for d in dirs:
        lr_hi, lr_lo = _split_hi_lo(d["lr"][...])
        gate_in = jnp.dot(jnp.concatenate([lr_hi, lr_hi, lr_lo], axis=1), d["wg"],
                          preferred_element_type=F32) + d["bg"]
        g = _log_sigmoid(gate_in) * (1.0 / GATE_TEMP)
        g_hi = g.astype(BF16)
        g_mid, g_lo = _split_hi_lo(g - g_hi.astype(F32))
        d["parts"][...] = jnp.concatenate([g_hi, g_mid, g_lo], axis=1)
        d["csum"] = jnp.where((cc >= rr) if d["reverse"] else (cc <= rr), 1.0, 0.0).astype(BF16)
        d["mask"] = (cc > rr) if d["reverse"] else (cc <= rr)
        d["edge"] = 0 if d["reverse"] else CHUNK - 1

    for d, c in units:
        b3 = jnp.dot(d["csum"], d["parts"][rows(c), :], preferred_element_type=F32)
        d["b"][rows(c), :] = b3[:, :GLA_DK] + b3[:, GLA_DK:2 * GLA_DK] + b3[:, 2 * GLA_DK:]

    decay = {}
    for d, c in units:
        b = d["b"][rows(c), :]
        b_edge = b[d["edge"]:d["edge"] + 1]
        k = d["k"][rows(c), :]
        d["qks"][0, rows(c), :] = (d["q"][rows(c), :] * scale * jnp.exp(b)).astype(BF16)
        d["qks"][1, rows(c), :] = (k * jnp.exp(-b)).astype(BF16)
        d["qks"][2, rows(c), :] = (k * jnp.exp(b_edge - b)).astype(BF16)
        d["vb"][rows(c), :] = d["v"][rows(c), :].astype(BF16)
        decay[(id(d), c)] = jnp.exp(b_edge)

    for d, c in units:
        att = _dot_nt(d["qks"][0, rows(c), :], d["qks"][1, rows(c), :])
        d["att"][rows(c), :] = jnp.where(d["mask"], att, 0.0).astype(BF16)

    for d, c in units:
        d["o"][rows(c), :] = jnp.dot(d["att"][rows(c), :], d["vb"][rows(c), :], preferred_element_type=F32)

    for d, c in units:
        d["kv"][c * GLA_DV:(c + 1) * GLA_DV, :] = _dot_tn(d["vb"][rows(c), :], d["qks"][2, rows(c), :])

    state = {id(d): d["s"][...] for d in dirs}
    for step in range(nch):
        for d in dirs:
            c = nch - 1 - step if d["reverse"] else step
            s_t = state[id(d)]
            d["o"][rows(c), :] += _dot_nt(d["qks"][0, rows(c), :], s_t.astype(BF16))
            state[id(d)] = s_t * decay[(id(d), c)] + d["kv"][c * GLA_DV:(c + 1) * GLA_DV, :]
    for d in dirs:
        d["s"][...] = state[id(d)]


def _gla_kernel(qf_ref, kf_ref, vf_ref, lf_ref, rf_ref, qb_ref, kb_ref, vb_ref, lb_ref, rb_ref,
                wg_ref, bg_ref, og_ref, o_ref,
                s_scr, parts_scr, b_scr, qks_scr, vb_scr, att_scr, kv_scr, ob_scr, half_scr, *, TB):
    s = pl.program_id(1)
    nb = pl.num_programs(1)

    @pl.when(s == 0)
    def _():
        s_scr[...] = jnp.zeros_like(s_scr)

    def direction(i, q, k, v, lr):
        return dict(q=q, k=k, v=v, lr=lr, wg=wg_ref[i], bg=bg_ref[i], reverse=bool(i), s=s_scr.at[i],
                    parts=parts_scr.at[i], b=b_scr.at[i], qks=qks_scr.at[i], vb=vb_scr.at[i],
                    att=att_scr.at[i], kv=kv_scr.at[i], o=ob_scr.at[i])
    _gla_scan_blocks([direction(0, qf_ref, kf_ref, vf_ref, lf_ref),
                      direction(1, qb_ref, kb_ref, vb_ref, lb_ref)], TB=TB)

    first_half = s < nb // 2
    for d, blk, r_ref in ((0, s, rf_ref), (1, nb - 1 - s, rb_ref)):
        rows = pl.ds(pl.multiple_of(blk * TB, TB), TB)

        @pl.when(first_half)
        def _():
            half_scr[rows, :] = ob_scr[d]

        @pl.when(jnp.logical_not(first_half))
        def _():
            tot = half_scr[rows, :] + ob_scr[d]
            tot = tot * lax.rsqrt(jnp.mean(tot * tot, axis=-1, keepdims=True) + EPS) * og_ref[...]
            o_ref[rows, :] = (tot * jax.nn.silu(r_ref[...])).astype(o_ref.dtype)


def _gla(p, p_lr, w_gate, b_gate, out_g, *, L, col0, TB=1024):
    TB = min(TB, L)
    nb = L // TB
    assert nb % 2 == 0, "both scan directions must meet between two blocks"
    nch = TB // CHUNK
    qb = col0 // GLA_DK
    kb = qb + GLA_HEADS
    vb = (col0 + 2 * GLA_KW) // GLA_DV
    rb = vb + GLA_HEADS
    fwd = lambda s: s
    bwd = lambda s: nb - 1 - s

    def operands(blk):
        return [pl.BlockSpec((TB, GLA_DK), lambda h, s: (blk(s), qb + h)),
                pl.BlockSpec((TB, GLA_DK), lambda h, s: (blk(s), kb + h)),
                pl.BlockSpec((TB, GLA_DV), lambda h, s: (blk(s), vb + h)),
                pl.BlockSpec((TB, LANES), lambda h, s: (blk(s), 0)),
                pl.BlockSpec((TB, GLA_DV), lambda h, s: (blk(s), rb + h))]

    return pl.pallas_call(
        functools.partial(_gla_kernel, TB=TB),
        grid=(GLA_HEADS, nb),
        in_specs=operands(fwd) + operands(bwd) + [
            pl.BlockSpec((2, 3 * LANES, GLA_DK), lambda h, s: (0, 0, h)),
            pl.BlockSpec((2, 1, GLA_DK), lambda h, s: (0, 0, h)),
            pl.BlockSpec((1, GLA_DV), lambda h, s: (0, h)),
        ],
        out_specs=pl.BlockSpec((L, GLA_DV), lambda h, s: (0, h)),
        out_shape=jax.ShapeDtypeStruct((L, GLA_VW), BF16),
        scratch_shapes=[pltpu.VMEM((2, GLA_DV, GLA_DK), F32),
                        pltpu.VMEM((2, TB, 3 * GLA_DK), BF16),
                        pltpu.VMEM((2, TB, GLA_DK), F32),
                        pltpu.VMEM((2, 3, TB, GLA_DK), BF16),
                        pltpu.VMEM((2, TB, GLA_DV), BF16),
                        pltpu.VMEM((2, TB, CHUNK), BF16),
                        pltpu.VMEM((2, nch * GLA_DV, GLA_DK), F32),
                        pltpu.VMEM((2, TB, GLA_DV), F32),
                        pltpu.VMEM((L, GLA_DV), F32)],
        compiler_params=_cparams(("arbitrary", "arbitrary")),
        name="gla",
    )(*([p, p, p, p_lr, p] * 2), w_gate, b_gate, out_g)


def _filter_features(L):
    t = np.linspace(0.0, 1.0, L)[:, None]
    bands = (FILTER_EMB - 1) // 2
    freqs = np.linspace(1e-4, bands - 1, bands)[None, :]
    ang = (2.0 * np.pi / L) * np.arange(L)[:, None] * freqs
    feat = np.concatenate([t, np.cos(ang), -np.sin(ang)], axis=-1)
    feat_rev = np.roll(feat[::-1], 1, axis=0)
    both = np.zeros((L, LANES), np.float32)
    both[:, :FILTER_EMB] = feat
    both[:, FILTER_EMB:2 * FILTER_EMB] = feat_rev
    return jnp.asarray(both)


def _twice(w, rows_out):
    r, c = w.shape
    out = jnp.zeros((rows_out, 2 * c), F32)
    return out.at[:r, :c].set(w).at[r:2 * r, c:].set(w)


def _filter_deltas():
    min_decay = math.log(DECAY_TARGET) / LONG_DECAY_PCT
    max_decay = math.log(DECAY_TARGET) / SHORT_DECAY_PCT
    return jnp.abs(jnp.linspace(min_decay, max_decay, HY_WIDTH, dtype=F32)).reshape(1, HY_WIDTH)


def _block_diag_w4(w4):
    H = w4.shape[0]
    nblk = HY_WIDTH // LANES
    wf = w4[:, :HY_WIDTH].reshape(H, nblk, LANES)
    wb = w4[:, HY_WIDTH:].reshape(H, nblk, LANES)
    zz = jnp.zeros_like(wf)
    top = jnp.concatenate([wf, zz], axis=2)
    bot = jnp.concatenate([zz, wb], axis=2)
    return jnp.concatenate([top, bot], axis=0).reshape(2 * H, nblk * 2 * LANES)


def kernel(x, ffn1_norm, ffn1_w_gate, ffn1_w_up, ffn1_w_down, mix_norm, w_in, hy_conv_w, hy_conv_b, flt_w1, flt_b1, flt_f1, flt_w2, flt_b2, flt_f2, flt_w3, flt_b3, flt_f3, flt_w4, hy_skip, hy_out_norm, gla_w_a2_f, gla_b_a_f, gla_w_a2_b, gla_b_a_b, gla_out_norm, w_out, ffn2_norm, ffn2_w_gate, ffn2_w_up, ffn2_w_down, final_norm):
    B, L, D = x.shape
    depth = ffn1_norm.shape[0]
    tabs = _dft_tables(L)
    feat2 = _filter_features(L)
    deltas = _filter_deltas()
    gla_col0 = 3 * HY_WIDTH
    H2 = 2 * FILTER_HIDDEN
    both = lambda a: jnp.concatenate([a, a]).reshape(1, H2)
    filt = []
    for l in range(depth):
        early = [jnp.swapaxes(w_in[l], 0, 1)]
        if l == 0:
            early += [ffn1_w_gate[0], ffn1_w_up[0], ffn1_w_down[0]]
        hid3, cast = _filt_mlp(feat2, _twice(flt_w1[l], LANES), both(flt_b1[l]), both(flt_f1[l]),
                               _twice(flt_w2[l], H2), both(flt_b2[l]), both(flt_f2[l]),
                               _twice(flt_w3[l], H2), both(flt_b3[l]), both(flt_f3[l]), to_cast=early)
        w4cat = _three_pass_rows(_block_diag_w4(flt_w4[l]))
        khat, knyq = _filt_fft(hid3, w4cat, deltas, tabs["g_full"], tabs["f3"], L=L)
        filt.append((khat, knyq, cast))

    outs = []
    for bi in range(B):
        xb = x[bi]
        w1 = filt[0][2][1:]
        for l in range(depth):
            last = l == depth - 1
            khat, knyq, (w_in_t, *_) = filt[l]
            later = [ffn2_w_gate[l], ffn2_w_up[l], ffn2_w_down[l], w_out[l]]
            xb, (w2_gate, w2_up, w2_down, w_out_bf) = _ffn(xb, ffn1_norm[l], *w1, final_norm, final_norm=False,
                                                           to_cast=later)
            p, p_lr = _in_proj(xb, mix_norm[l], w_in_t)
            y_hy = _hy_conv(p, hy_conv_w[l], hy_conv_b[l].reshape(1, -1), khat, knyq,
                            hy_skip[l].reshape(1, -1), hy_out_norm[l].reshape(1, -1), tabs, L=L)

            w_gate = jnp.zeros((2, LANES, GLA_KW), F32)
            w_gate = w_gate.at[0, :GATE_RANK].set(gla_w_a2_f[l]).at[1, GATE_RANK:2 * GATE_RANK].set(gla_w_a2_b[l])
            w_gate = _three_pass_rows(w_gate)
            b_gate = jnp.stack([gla_b_a_f[l], gla_b_a_b[l]]).reshape(2, 1, GLA_KW)
            y_gla = _gla(p, p_lr, w_gate, b_gate, gla_out_norm[l].reshape(1, -1), L=L, col0=gla_col0)

            xb = _out_proj(xb, y_hy, y_gla, w_out_bf)
            nxt = [] if last else [ffn1_w_gate[l + 1], ffn1_w_up[l + 1], ffn1_w_down[l + 1]]
            xb, w1 = _ffn(xb, ffn2_norm[l], w2_gate, w2_up, w2_down, final_norm, final_norm=last, to_cast=nxt)
        outs.append(xb)
    return jnp.stack(outs)
```

```python
import functools
import math

import numpy as np
import jax
import jax.numpy as jnp
from jax import lax
from jax.experimental import pallas as pl
from jax.experimental.pallas import tpu as pltpu

F32 = jnp.float32
BF16 = jnp.bfloat16
HIGHEST = lax.Precision.HIGHEST

EPS = 1e-6
HY_WIDTH = 1024
HY_GROUPS = 8
FILTER_EMB = 33
FILTER_HIDDEN = 64
SHORT_DECAY_PCT = 0.3
LONG_DECAY_PCT = 1.5
DECAY_TARGET = 1e-2
GLA_HEADS = 4
GLA_DK = 128
GLA_DV = 256
GLA_KW = GLA_HEADS * GLA_DK
GLA_VW = GLA_HEADS * GLA_DV
GATE_RANK = 16
GATE_TEMP = 16.0
CHUNK = 64

LANES = 128
SUBLANES = 8
VMEM_LIMIT = 60 * 1024 * 1024
FFN_SLAB = 128
RB = 128
KB = RB // 2
PITCH = RB + SUBLANES


def _cparams(sem):
    return pltpu.CompilerParams(dimension_semantics=sem, vmem_limit_bytes=VMEM_LIMIT)


def _single(block_shape, index_map):
    return pl.BlockSpec(block_shape, index_map, pipeline_mode=pl.Buffered(1))


def _split_hi_lo(x):
    hi = x.astype(BF16)
    return hi, (x - hi.astype(F32)).astype(BF16)


def _three_pass_rows(w):
    w_hi, w_lo = _split_hi_lo(w)
    return jnp.concatenate([w_hi, w_lo, w_hi], axis=-2)


def _dot_nt(a, b):
    return lax.dot_general(a, b, (((1,), (1,)), ((), ())), preferred_element_type=F32)


def _dot_tn(a, b):
    return lax.dot_general(a, b, (((0,), (0,)), ((), ())), preferred_element_type=F32)


def _ffn_kernel(x_ref, g_ref, wg_ref, wu_ref, wd_ref, fg_ref, *rest, final_norm, n_cast):
    cast_src, o_ref, cast_dst, h_scr = rest[:n_cast], rest[n_cast], rest[n_cast + 1:-1], rest[-1]
    j = pl.program_id(1)

    slab = min(FFN_SLAB, x_ref.shape[0])
    nslab = x_ref.shape[0] // slab

    @pl.when(j == 0)
    def _():
        def norm_rows(t, carry):
            rows = pl.ds(pl.multiple_of(t * slab, slab), slab)
            x = x_ref[rows, :]
            r = lax.rsqrt(jnp.mean(x * x, axis=-1, keepdims=True) + EPS)
            h_scr[rows, :] = (x * r * g_ref[...]).astype(BF16)
            o_ref[rows, :] = jnp.zeros((slab, x_ref.shape[1]), F32)
            return carry
        lax.fori_loop(0, nslab, norm_rows, 0)

    h = h_scr[...]
    gate = jnp.dot(h, wg_ref[...], preferred_element_type=F32)
    up = jnp.dot(h, wu_ref[...], preferred_element_type=F32)
    a = (jax.nn.silu(gate) * up).astype(BF16)
    o_ref[...] += jnp.dot(a, wd_ref[...], preferred_element_type=F32)

    for src, dst in zip(cast_src, cast_dst):
        dst[...] = src[...].astype(BF16)

    @pl.when(j == pl.num_programs(1) - 1)
    def _():
        def finish_rows(t, carry):
            rows = pl.ds(pl.multiple_of(t * slab, slab), slab)
            y = x_ref[rows, :] + 0.5 * o_ref[rows, :]
            if final_norm:
                r = lax.rsqrt(jnp.mean(y * y, axis=-1, keepdims=True) + EPS)
                y = y * r * fg_ref[...]
            o_ref[rows, :] = y
            return carry
        lax.fori_loop(0, nslab, finish_rows, 0)


def _cast_blocking(shape, ni, nj):
    rows, cols = shape
    assert rows % ni == 0 and (rows // ni) % SUBLANES == 0 and cols % LANES == 0
    ncol = max(d for d in range(1, nj + 1) if (cols // LANES) % d == 0)
    return (rows // ni, cols // ncol), (lambda i, j: (i, jnp.minimum(j, ncol - 1)))


def _ffn(x, norm_g, w_gate, w_up, w_down, final_g, *, final_norm, to_cast=(), tm=1024, tf=512):
    L, D = x.shape
    DF = w_gate.shape[1]
    tm = min(tm, L)
    tf = min(tf, DF)
    ni, nj = L // tm, DF // tf
    blockings = [_cast_blocking(w.shape, ni, nj) for w in to_cast]
    cast_specs = [pl.BlockSpec(blk, imap) for blk, imap in blockings]
    outs = pl.pallas_call(
        functools.partial(_ffn_kernel, final_norm=final_norm, n_cast=len(to_cast)),
        grid=(ni, nj),
        in_specs=[
            pl.BlockSpec((tm, D), lambda i, j: (i, 0)),
            pl.BlockSpec((1, D), lambda i, j: (0, 0)),
            pl.BlockSpec((D, tf), lambda i, j: (0, j)),
            pl.BlockSpec((D, tf), lambda i, j: (0, j)),
            pl.BlockSpec((tf, D), lambda i, j: (j, 0)),
            pl.BlockSpec((1, D), lambda i, j: (0, 0)),
        ] + cast_specs,
        out_specs=[pl.BlockSpec((tm, D), lambda i, j: (i, 0))] + cast_specs,
        out_shape=[jax.ShapeDtypeStruct((L, D), F32)] + [jax.ShapeDtypeStruct(w.shape, BF16) for w in to_cast],
        scratch_shapes=[pltpu.VMEM((tm, D), BF16)],
        compiler_params=_cparams(("arbitrary", "arbitrary")),
        name="ffn",
    )(x, norm_g.reshape(1, D), w_gate, w_up, w_down, final_g.reshape(1, D), *to_cast)
    return outs[0], outs[1:]


def _in_proj_kernel(x_ref, g_ref, wt_ref, wtail_ref, o_ref, otail_ref, h_scr, *, n_tail):
    j = pl.program_id(1)

    @pl.when(j == 0)
    def _():
        x = x_ref[...]
        r = lax.rsqrt(jnp.mean(x * x, axis=-1, keepdims=True) + EPS)
        h = (x * r * g_ref[...]).astype(BF16)
        h_scr[...] = h
        tail = _dot_nt(h, wtail_ref[...])
        col = lax.broadcasted_iota(jnp.int32, tail.shape, 1)
        otail_ref[...] = jnp.where(col < n_tail, tail, 0.0)

    o_ref[...] = _dot_nt(h_scr[...], wt_ref[...])


def _in_proj(x, norm_g, w_t, *, tm=1024, tn=2048):
    L, D = x.shape
    n_cols = w_t.shape[0]
    nj = n_cols // tn
    n_tail = n_cols - nj * tn
    assert 0 < n_tail <= LANES and (nj * tn) % LANES == 0
    tm = min(tm, L)
    return pl.pallas_call(
        functools.partial(_in_proj_kernel, n_tail=n_tail),
        grid=(L // tm, nj),
        in_specs=[
            pl.BlockSpec((tm, D), lambda i, j: (i, 0)),
            pl.BlockSpec((1, D), lambda i, j: (0, 0)),
            pl.BlockSpec((tn, D), lambda i, j: (j, 0)),
            pl.BlockSpec((LANES, D), lambda i, j: (nj * tn // LANES, 0)),
        ],
        out_specs=[pl.BlockSpec((tm, tn), lambda i, j: (i, j)),
                   pl.BlockSpec((tm, LANES), lambda i, j: (i, 0))],
        out_shape=[jax.ShapeDtypeStruct((L, nj * tn), F32),
                   jax.ShapeDtypeStruct((L, LANES), F32)],
        scratch_shapes=[pltpu.VMEM((tm, D), BF16)],
        compiler_params=_cparams(("parallel", "arbitrary")),
        name="in_proj",
    )(x, norm_g.reshape(1, D), w_t, w_t)


def _out_proj_kernel(x_ref, yh_ref, yg_ref, wh_ref, wg_ref, o_ref):
    o_ref[...] = (x_ref[...]
                  + jnp.dot(yh_ref[...], wh_ref[...], preferred_element_type=F32)
                  + jnp.dot(yg_ref[...], wg_ref[...], preferred_element_type=F32))


def _out_proj(x, y_hy, y_gla, w_out, *, tm=512):
    L, D = x.shape
    WH = y_hy.shape[1]
    WG = y_gla.shape[1]
    tm = min(tm, L)
    assert WH == WG and w_out.shape[0] == WH + WG
    return pl.pallas_call(
        _out_proj_kernel,
        grid=(L // tm,),
        in_specs=[
            pl.BlockSpec((tm, D), lambda i: (i, 0)),
            pl.BlockSpec((tm, WH), lambda i: (i, 0)),
            pl.BlockSpec((tm, WG), lambda i: (i, 0)),
            _single((WH, D), lambda i: (0, 0)),
            _single((WG, D), lambda i: (1, 0)),
        ],
        out_specs=pl.BlockSpec((tm, D), lambda i: (i, 0)),
        out_shape=jax.ShapeDtypeStruct((L, D), F32),
        compiler_params=_cparams(("parallel",)),
        name="out_proj",
    )(x, y_hy, y_gla, w_out, w_out)


def _filt_mlp_kernel(z_ref, w1_ref, b1_ref, f1_ref, w2_ref, b2_ref, f2_ref, w3_ref, b3_ref, f3_ref, *rest, n_cast):
    cast_src, o_ref, cast_dst = rest[:n_cast], rest[n_cast], rest[n_cast + 1:]
    dot = functools.partial(jnp.dot, precision=HIGHEST, preferred_element_type=F32)
    hid = jnp.sin(f1_ref[...] * (dot(z_ref[...], w1_ref[...]) + b1_ref[...]))
    hid = jnp.sin(f2_ref[...] * (dot(hid, w2_ref[...]) + b2_ref[...]))
    hid = jnp.sin(f3_ref[...] * (dot(hid, w3_ref[...]) + b3_ref[...]))
    hi, lo = _split_hi_lo(hid)
    o_ref[...] = jnp.concatenate([hi, hi, lo], axis=1)
    for src, dst in zip(cast_src, cast_dst):
        dst[...] = src[...].astype(BF16)


def _cast_row_blocks(shape, n):
    rows, cols = shape
    tile = 2 * SUBLANES
    block_rows = pl.cdiv(pl.cdiv(rows, n), tile) * tile
    nblocks = pl.cdiv(rows, block_rows)
    return (block_rows, cols), (lambda i: (jnp.minimum(i, nblocks - 1), 0))


def _filt_mlp(feat2, w1, b1, f1, w2, b2, f2, w3, b3, f3, *, to_cast=(), tm=512):
    L, FP = feat2.shape
    H2 = w2.shape[0]
    tm = min(tm, L)
    steps = L // tm
    full = lambda shp: pl.BlockSpec(shp, lambda i: (0, 0))
    cast_specs = [pl.BlockSpec(*_cast_row_blocks(w.shape, steps)) for w in to_cast]
    outs = pl.pallas_call(
        functools.partial(_filt_mlp_kernel, n_cast=len(to_cast)),
        grid=(steps,),
        in_specs=[pl.BlockSpec((tm, FP), lambda i: (i, 0)),
                  full((FP, H2)), full((1, H2)), full((1, H2)),
                  full((H2, H2)), full((1, H2)), full((1, H2)),
                  full((H2, H2)), full((1, H2)), full((1, H2))] + cast_specs,
        out_specs=[pl.BlockSpec((tm, 3 * H2), lambda i: (i, 0))] + cast_specs,
        out_shape=[jax.ShapeDtypeStruct((L, 3 * H2), BF16)] + [jax.ShapeDtypeStruct(w.shape, BF16) for w in to_cast],
        compiler_params=_cparams(("arbitrary",)),
        name="filt_mlp",
    )(feat2, w1, b1, f1, w2, b2, f2, w3, b3, f3, *to_cast)
    return outs[0], outs[1:]


def _dft_tables(L):
    N = 2 * L
    RA = N // RB
    NH = RA // 2
    two_pi = 2.0 * np.pi
    k1 = np.arange(RA, dtype=np.int64)

    def step1_table(NR):
        g = np.zeros((RB // 2, 2 * RA, 2 * NR), np.float64)
        for s in range(2):
            n = RB * np.arange(NR, dtype=np.int64)[None, :] + (2 * np.arange(RB // 2, dtype=np.int64) + s)[:, None]
            ang = two_pi * ((k1[None, :, None] * n[:, None, :]) % N) / N
            g[:, :RA, s * NR:(s + 1) * NR] = np.cos(ang)
            g[:, RA:, s * NR:(s + 1) * NR] = -np.sin(ang)
        return g

    phi = two_pi * ((np.arange(KB)[:, None] * np.arange(RB)[None, :]) % RB) / RB
    f3 = np.block([[np.cos(phi), np.sin(phi)], [-np.sin(phi), np.cos(phi)]])
    i1 = np.block([[np.cos(phi.T), -np.sin(phi.T)], [np.sin(phi.T), np.cos(phi.T)]])
    cast = lambda a: jnp.asarray(a.astype(np.float32)).astype(BF16)
    return dict(g_half=cast(step1_table(NH)), g_full=cast(step1_table(RA)), f3=cast(f3), i1=cast(i1))


def _fwd_step1(u_ref, g_ref, ab_ref, *, RA, NR, CB, G=2):
    def body(t, carry):
        ms = [t * G + j for j in range(G)]
        rhs = []
        for m in ms:
            ua = u_ref[pl.ds(2 * m, NR, stride=PITCH), :]
            ub = u_ref[pl.ds(2 * m + 1, NR, stride=PITCH), :]
            zz = jnp.zeros_like(ua)
            rhs.append(jnp.concatenate([jnp.concatenate([ua, zz], axis=1),
                                        jnp.concatenate([zz, ub], axis=1)], axis=0).astype(BF16))
        res = [jnp.dot(g_ref[m], r, preferred_element_type=F32) for m, r in zip(ms, rhs)]
        for m, r in zip(ms, res):
            ab_ref[pl.ds(2 * m, RA, stride=PITCH), :] = r[:RA, :CB]
            ab_ref[pl.ds(RA * PITCH + 2 * m, RA, stride=PITCH), :] = r[RA:, :CB]
            ab_ref[pl.ds(2 * m + 1, RA, stride=PITCH), :] = r[:RA, CB:]
            ab_ref[pl.ds(RA * PITCH + 2 * m + 1, RA, stride=PITCH), :] = r[RA:, CB:]
        return carry
    lax.fori_loop(0, RB // 2 // G, body, 0)


def _fwd_step3_pair(ab_ref, f3, kp, *, RA):
    cols = []
    for s in range(2):
        k1 = 2 * kp + s
        ar = ab_ref[pl.ds(pl.multiple_of(k1 * PITCH, SUBLANES), RB), :]
        ai = ab_ref[pl.ds(pl.multiple_of((RA + k1) * PITCH, SUBLANES), RB), :]
        cols.append(jnp.concatenate([ar, ai], axis=0))
    rhs = jnp.concatenate(cols, axis=1).astype(BF16)
    return jnp.dot(f3, rhs, preferred_element_type=F32)


def _for_row_blocks(nblocks, body, carry, unroll=2):
    assert nblocks >= 2
    carry = body(0, carry, True, False)
    carry = lax.fori_loop(1, nblocks - 1, lambda n1, c: body(n1, c, False, False), carry,
                          unroll=max(1, min(unroll, nblocks - 2)))
    return body(nblocks - 1, carry, False, True)


def _alt_sign(shape):
    rows = lax.broadcasted_iota(jnp.int32, shape, 0)
    return jnp.where(rows % 2 == 0, 1.0, -1.0).astype(F32)


def _filt_fft_kernel(hid_ref, w4_ref, dl_ref, g_ref, f3_ref, kh_ref, kn_ref, u_scr, ab_scr, *, L, CB, G):
    N = 2 * L
    RA = N // RB
    NH = RA // 2
    f3 = f3_ref[...]
    inv_lm1 = 1.0 / (L - 1)

    row = lax.broadcasted_iota(jnp.int32, (RB, CB), 0)
    arg_base = -(row.astype(F32) * inv_lm1) * dl_ref[...]
    arg_step = -(RB * inv_lm1) * dl_ref[...]
    arg_end = -(L * inv_lm1) * dl_ref[...]

    def fill(n1, carry, first, last):
        s_abs, s_alt = carry
        r0 = pl.multiple_of(n1 * RB, RB)
        h2 = jnp.dot(hid_ref[pl.ds(r0, RB), :], w4_ref[...], preferred_element_type=F32)
        arg = arg_base + jnp.asarray(n1, F32) * arg_step
        hf = h2[:, :CB] * jnp.exp(arg)
        hb = h2[:, CB:] * jnp.exp(arg_end - arg)
        if first:
            hb = jnp.where(row == 0, 0.0, hb)
        u_scr[pl.ds(pl.multiple_of(n1 * PITCH, SUBLANES), RB), :] = hf
        u_scr[pl.ds(pl.multiple_of((NH + n1) * PITCH, SUBLANES), RB), :] = hb
        return s_abs + (jnp.abs(hf) + jnp.abs(hb)), s_alt + (hf + hb)
    zero = jnp.zeros((RB, CB), F32)
    s_abs, s_alt = _for_row_blocks(NH, fill, (zero, zero), unroll=8)
    inv_l1 = 1.0 / jnp.sum(s_abs, axis=0, keepdims=True)
    kn = jnp.sum(s_alt * _alt_sign((RB, CB)), axis=0, keepdims=True) * inv_l1 * (1.0 / N)
    kn_ref[...] = jnp.broadcast_to(kn, kn_ref.shape)

    _fwd_step1(u_scr, g_ref, ab_scr, RA=RA, NR=RA, CB=CB, G=G)

    scale = inv_l1 * (2.0 / N)
    scale2 = jnp.concatenate([scale, scale], axis=1)
    row2 = lax.broadcasted_iota(jnp.int32, (KB, 2 * CB), 0)
    lane2 = lax.broadcasted_iota(jnp.int32, (KB, 2 * CB), 1)
    dc_pos = jnp.logical_and(row2 == 0, lane2 < CB)

    def emit(t, carry):
        kps = [t * G + j for j in range(G)]
        xs = [_fwd_step3_pair(ab_scr, f3, kp, RA=RA) for kp in kps]
        for kp, x in zip(kps, xs):
            kr = x[:KB] * scale2
            ki = x[KB:] * scale2
            dc = jnp.logical_and(dc_pos, kp == 0)
            kr = jnp.where(dc, 0.5 * kr, kr)
            ki = jnp.where(dc, 0.5 * ki, ki)
            kh_ref[pl.ds(pl.multiple_of(kp * RB, RB), RB), :] = jnp.concatenate([kr, ki], axis=0).astype(kh_ref.dtype)
        return carry
    lax.fori_loop(0, RA // 2 // G, emit, 0)


def _filt_fft(hid2, w4bd, deltas, g_tab, f3_tab, *, L, CB=LANES, G=16):
    C = deltas.shape[1]
    H2 = hid2.shape[1]
    N = 2 * L
    RA = N // RB
    nblk = C // CB
    G = min(G, RA // 2)
    return pl.pallas_call(
        functools.partial(_filt_fft_kernel, L=L, CB=CB, G=G),
        grid=(nblk,),
        in_specs=[
            _single((L, H2), lambda c: (0, 0)),
            pl.BlockSpec((H2, 2 * CB), lambda c: (0, c)),
            pl.BlockSpec((1, CB), lambda c: (0, c)),
            _single(g_tab.shape, lambda c: (0, 0, 0)),
            _single(f3_tab.shape, lambda c: (0, 0)),
        ],
        out_specs=[pl.BlockSpec((RA // 2 * RB, 2 * CB), lambda c: (0, c)),
                   pl.BlockSpec((SUBLANES, CB), lambda c: (0, c))],
        out_shape=[jax.ShapeDtypeStruct((RA // 2 * RB, 2 * C), BF16),
                   jax.ShapeDtypeStruct((SUBLANES, C), F32)],
        scratch_shapes=[pltpu.VMEM((RA * PITCH, CB), F32),
                        pltpu.VMEM((2 * RA * PITCH, CB), F32)],
        compiler_params=_cparams(("arbitrary",)),
        name="filt_fft",
    )(hid2, w4bd, deltas, g_tab, f3_tab)


def _short_conv_rows(p_ref, w_ref, b_ref, n1, *, first, last):
    r0 = pl.multiple_of(n1 * RB, RB)
    cur = p_ref[pl.ds(r0, RB), :]
    rows = lax.broadcasted_iota(jnp.int32, cur.shape, 0)
    if first:
        up = jnp.where(rows == 0, 0.0, pltpu.roll(cur, 1, axis=0))
    else:
        up = p_ref[pl.ds(r0 - 1, RB), :]
    if last:
        dn = jnp.where(rows == RB - 1, 0.0, pltpu.roll(cur, RB - 1, axis=0))
    else:
        dn = p_ref[pl.ds(r0 + 1, RB), :]
    w = w_ref[...]
    return b_ref[...] + up * w[0:1] + cur * w[1:2] + dn * w[2:3]


def _hy_conv_kernel(p_hbm, w0_ref, w1_ref, wv_ref, b0_ref, b1_ref, bv_ref,
                    kh_ref, kn_ref, skip_ref, og_ref, g_ref, f3_ref, i1_ref,
                    o_ref, u_scr, ab_scr, z_scr, p_scr, p_sem, *, L, CB, G):
    N = 2 * L
    RA = N // RB
    NH = RA // 2
    f3 = f3_ref[...]
    i1 = i1_ref[...]
    sign = _alt_sign((RB, CB))
    c = pl.program_id(0)
    nblk = pl.num_programs(0)
    p0_ref, p1_ref, pv_ref = p_scr.at[0], p_scr.at[1], p_scr.at[2]

    def fetch(group, blk):
        col = pl.multiple_of((group * nblk + blk) * CB, CB)
        return pltpu.make_async_copy(p_hbm.at[:, pl.ds(col, CB)], p_scr.at[group], p_sem.at[group])

    @pl.when(c == 0)
    def _():
        fetch(1, c).start()
        fetch(2, c).start()

    fetch(0, c).start()
    fetch(1, c).wait()
    fetch(2, c).wait()

    def fill(n1, s_alt, first, last):
        z = (_short_conv_rows(pv_ref, wv_ref, bv_ref, n1, first=first, last=last)
             * _short_conv_rows(p1_ref, w1_ref, b1_ref, n1, first=first, last=last))
        u_scr[pl.ds(pl.multiple_of(n1 * PITCH, SUBLANES), RB), :] = z
        z_scr[pl.ds(pl.multiple_of(n1 * RB, RB), RB), :] = z
        return s_alt + z
    s_alt = _for_row_blocks(NH, fill, jnp.zeros((RB, CB), F32))

    @pl.when(c + 1 < nblk)
    def _():
        fetch(1, c + 1).start()
        fetch(2, c + 1).start()

    z_nyq = jnp.sum(s_alt * sign, axis=0, keepdims=True)
    y_nyq = z_nyq * kn_ref[0:1, :]

    _fwd_step1(u_scr, g_ref, ab_scr, RA=RA, NR=NH, CB=CB, G=G)

    GM = max(G // 2, 1)

    def mid(t, carry):
        kps = [t * GM + j for j in range(GM)]
        xs = [_fwd_step3_pair(ab_scr, f3, kp, RA=RA) for kp in kps]
        khs = [kh_ref[pl.ds(pl.multiple_of(kp * RB, RB), RB), :].astype(F32) for kp in kps]
        bs = []
        for x, kh in zip(xs, khs):
            xr, xi, kr, ki = x[:KB], x[KB:], kh[:KB], kh[KB:]
            y = jnp.concatenate([xr * kr - xi * ki, xr * ki + xi * kr], axis=0).astype(BF16)
            bs.append(jnp.dot(i1, y, preferred_element_type=F32))
        for kp, b in zip(kps, bs):
            for s in range(2):
                k1 = 2 * kp + s
                ab_scr[pl.ds(pl.multiple_of(k1 * PITCH, SUBLANES), RB), :] = b[:RB, s * CB:(s + 1) * CB]
                ab_scr[pl.ds(pl.multiple_of((RA + k1) * PITCH, SUBLANES), RB), :] = b[RB:, s * CB:(s + 1) * CB]
        return carry
    lax.fori_loop(0, RA // 2 // GM, mid, 0)

    def last(t, carry):
        ms = [t * G + j for j in range(G)]
        rhs = [jnp.concatenate(
            [jnp.concatenate([ab_scr[pl.ds(2 * m + s, RA, stride=PITCH), :],
                              ab_scr[pl.ds(RA * PITCH + 2 * m + s, RA, stride=PITCH), :]], axis=0)
             for s in range(2)], axis=1).astype(BF16) for m in ms]
        ys = [_dot_tn(g_ref[m], r) for m, r in zip(ms, rhs)]
        for m, y in zip(ms, ys):
            u_scr[pl.ds(2 * m, NH, stride=PITCH), :] = y[:NH, :CB]
            u_scr[pl.ds(2 * m + 1, NH, stride=PITCH), :] = y[NH:, CB:]
        return carry
    lax.fori_loop(0, RB // 2 // G, last, 0)

    nyq_rows = sign * y_nyq
    fetch(0, c).wait()

    def finish(n1, carry, first, last):
        z = z_scr[pl.ds(pl.multiple_of(n1 * RB, RB), RB), :]
        x0 = _short_conv_rows(p0_ref, w0_ref, b0_ref, n1, first=first, last=last)
        conv = u_scr[pl.ds(pl.multiple_of(n1 * PITCH, SUBLANES), RB), :] + nyq_rows
        y = (conv + z * skip_ref[...]) * x0
        y = y * lax.rsqrt(jnp.mean(y * y, axis=-1, keepdims=True) + EPS) * og_ref[...]
        o_ref[pl.ds(pl.multiple_of(n1 * RB, RB), RB), :] = y.astype(o_ref.dtype)
        return carry
    _for_row_blocks(NH, finish, 0, unroll=4)


def _hy_conv(p, conv_w, conv_b, khat, knyq, skip, out_g, tabs, *, L, CB=LANES, G=16):
    C = skip.shape[1]
    assert CB == C // HY_GROUPS, "one channel block must be exactly one norm group"
    N = 2 * L
    RA = N // RB
    NH = RA // 2
    nblk = C // CB
    G = min(G, RA // 2)
    g_tab, f3_tab, i1_tab = tabs["g_half"], tabs["f3"], tabs["i1"]
    col = lambda off: (lambda c: (0, off * nblk + c))
    wspec = lambda off: pl.BlockSpec((3, CB), col(off))
    bspec = lambda off: pl.BlockSpec((1, CB), col(off))
    return pl.pallas_call(
        functools.partial(_hy_conv_kernel, L=L, CB=CB, G=G),
        grid=(nblk,),
        in_specs=[pl.BlockSpec(memory_space=pl.ANY),
                  wspec(0), wspec(1), wspec(2), bspec(0), bspec(1), bspec(2),
                  pl.BlockSpec((RA // 2 * RB, 2 * CB), col(0)),
                  pl.BlockSpec((SUBLANES, CB), col(0)),
                  bspec(0), bspec(0),
                  _single(g_tab.shape, lambda c: (0, 0, 0)),
                  _single(f3_tab.shape, lambda c: (0, 0)),
                  _single(i1_tab.shape, lambda c: (0, 0))],
        out_specs=pl.BlockSpec((L, CB), col(0)),
        out_shape=jax.ShapeDtypeStruct((L, C), BF16),
        scratch_shapes=[pltpu.VMEM((NH * PITCH, CB), F32),
                        pltpu.VMEM((2 * RA * PITCH, CB), F32),
                        pltpu.VMEM((L, CB), F32),
                        pltpu.VMEM((3, L, CB), F32),
                        pltpu.SemaphoreType.DMA((3,))],
        compiler_params=_cparams(("arbitrary",)),
        name="hy_conv",
    )(p, conv_w, conv_w, conv_w, conv_b, conv_b, conv_b, khat, knyq, skip, out_g,
      g_tab, f3_tab, i1_tab)


def _log_sigmoid(x):
    return jnp.minimum(x, 0.0) - jnp.log1p(jnp.exp(-jnp.abs(x)))


def _gla_scan_blocks(dirs, *, TB):
    nch = TB // CHUNK
    rows = lambda c: slice(c * CHUNK, (c + 1) * CHUNK)
    units = [(d, c) for c in range(nch) for d in dirs]
    rr = lax.broadcasted_iota(jnp.int32, (CHUNK, CHUNK), 0)
    cc = lax.broadcasted_iota(jnp.int32, (CHUNK, CHUNK), 1)
    scale = GLA_DK ** -0.5

    for d in dirs:
        lr_hi, lr_lo = _split_hi_lo(d["lr"][...])
        gate_in = jnp.dot(jnp.concatenate([lr_hi, lr_hi, lr_lo], axis=1), d["wg"],
                          preferred_element_type=F32) + d["bg"]
        g = _log_sigmoid(gate_in) * (1.0 / GATE_TEMP)
        g_hi = g.astype(BF16)
        g_mid, g_lo = _split_hi_lo(g - g_hi.astype(F32))
        d["parts"][...] = jnp.concatenate([g_hi, g_mid, g_lo], axis=1)
        d["csum"] = jnp.where((cc >= rr) if d["reverse"] else (cc <= rr), 1.0, 0.0).astype(BF16)
        d["mask"] = (cc > rr) if d["reverse"] else (cc <= rr)
        d["edge"] = 0 if d["reverse"] else CHUNK - 1

    for d, c in units:
        b3 = jnp.dot(d["csum"], d["parts"][rows(c), :], preferred_element_type=F32)
        d["b"][rows(c), :] = b3[:, :GLA_DK] + b3[:, GLA_DK:2 * GLA_DK] + b3[:, 2 * GLA_DK:]

    decay = {}
    for d, c in units:
        b = d["b"][rows(c), :]
        b_edge = b[d["edge"]:d["edge"] + 1]
        k = d["k"][rows(c), :]
        d["qks"][0, rows(c), :] = (d["q"][rows(c), :] * scale * jnp.exp(b)).astype(BF16)
        d["qks"][1, rows(c), :] = (k * jnp.exp(-b)).astype(BF16)
        d["qks"][2, rows(c), :] = (k * jnp.exp(b_edge - b)).astype(BF16)
        d["vb"][rows(c), :] = d["v"][rows(c), :].astype(BF16)
        decay[(id(d), c)] = jnp.exp(b_edge)

    for d, c in units:
        att = _dot_nt(d["qks"][0, rows(c), :], d["qks"][1, rows(c), :])
        d["att"][rows(c), :] = jnp.where(d["mask"], att, 0.0).astype(BF16)

    for d, c in units:
        d["o"][rows(c), :] = jnp.dot(d["att"][rows(c), :], d["vb"][rows(c), :], preferred_element_type=F32)

    for d, c in units:
        d["kv"][c * GLA_DV:(c + 1) * GLA_DV, :] = _dot_tn(d["vb"][rows(c), :], d["qks"][2, rows(c), :])

    state = {id(d): d["s"][...] for d in dirs}
    for step in range(nch):
        for d in dirs:
            c = nch - 1 - step if d["reverse"] else step
            s_t = state[id(d)]
            d["o"][rows(c), :] += _dot_nt(d["qks"][0, rows(c), :], s_t.astype(BF16))
            state[id(d)] = s_t * decay[(id(d), c)] + d["kv"][c * GLA_DV:(c + 1) * GLA_DV, :]
    for d in dirs:
        d["s"][...] = state[id(d)]


def _gla_kernel(qf_ref, kf_ref, vf_ref, lf_ref, rf_ref, qb_ref, kb_ref, vb_ref, lb_ref, rb_ref,
                wg_ref, bg_ref, og_ref, o_ref,
                s_scr, parts_scr, b_scr, qks_scr, vb_scr, att_scr, kv_scr, ob_scr, half_scr, *, TB):
    s = pl.program_id(1)
    nb = pl.num_programs(1)

    @pl.when(s == 0)
    def _():
        s_scr[...] = jnp.zeros_like(s_scr)

    def direction(i, q, k, v, lr):
        return dict(q=q, k=k, v=v, lr=lr, wg=wg_ref[i], bg=bg_ref[i], reverse=bool(i), s=s_scr.at[i],
                    parts=parts_scr.at[i], b=b_scr.at[i], qks=qks_scr.at[i], vb=vb_scr.at[i],
                    att=att_scr.at[i], kv=kv_scr.at[i], o=ob_scr.at[i])
    _gla_scan_blocks([direction(0, qf_ref, kf_ref, vf_ref, lf_ref),
                      direction(1, qb_ref, kb_ref, vb_ref, lb_ref)], TB=TB)

    first_half = s < nb // 2
    for d, blk, r_ref in ((0, s, rf_ref), (1, nb - 1 - s, rb_ref)):
        rows = pl.ds(pl.multiple_of(blk * TB, TB), TB)

        @pl.when(first_half)
        def _():
            half_scr[rows, :] = ob_scr[d]

        @pl.when(jnp.logical_not(first_half))
        def _():
            tot = half_scr[rows, :] + ob_scr[d]
            tot = tot * lax.rsqrt(jnp.mean(tot * tot, axis=-1, keepdims=True) + EPS) * og_ref[...]
            o_ref[rows, :] = (tot * jax.nn.silu(r_ref[...])).astype(o_ref.dtype)


def _gla(p, p_lr, w_gate, b_gate, out_g, *, L, col0, TB=1024):
    TB = min(TB, L)
    nb = L // TB
    assert nb % 2 == 0, "both scan directions must meet between two blocks"
    nch = TB // CHUNK
    qb = col0 // GLA_DK
    kb = qb + GLA_HEADS
    vb = (col0 + 2 * GLA_KW) // GLA_DV
    rb = vb + GLA_HEADS
    fwd = lambda s: s
    bwd = lambda s: nb - 1 - s

    def operands(blk):
        return [pl.BlockSpec((TB, GLA_DK), lambda h, s: (blk(s), qb + h)),
                pl.BlockSpec((TB, GLA_DK), lambda h, s: (blk(s), kb + h)),
                pl.BlockSpec((TB, GLA_DV), lambda h, s: (blk(s), vb + h)),
                pl.BlockSpec((TB, LANES), lambda h, s: (blk(s), 0)),
                pl.BlockSpec((TB, GLA_DV), lambda h, s: (blk(s), rb + h))]

    return pl.pallas_call(
        functools.partial(_gla_kernel, TB=TB),
        grid=(GLA_HEADS, nb),
        in_specs=operands(fwd) + operands(bwd) + [
            pl.BlockSpec((2, 3 * LANES, GLA_DK), lambda h, s: (0, 0, h)),
            pl.BlockSpec((2, 1, GLA_DK), lambda h, s: (0, 0, h)),
            pl.BlockSpec((1, GLA_DV), lambda h, s: (0, h)),
        ],
        out_specs=pl.BlockSpec((L, GLA_DV), lambda h, s: (0, h)),
        out_shape=jax.ShapeDtypeStruct((L, GLA_VW), BF16),
        scratch_shapes=[pltpu.VMEM((2, GLA_DV, GLA_DK), F32),
                        pltpu.VMEM((2, TB, 3 * GLA_DK), BF16),
                        pltpu.VMEM((2, TB, GLA_DK), F32),
                        pltpu.VMEM((2, 3, TB, GLA_DK), BF16),
                        pltpu.VMEM((2, TB, GLA_DV), BF16),
                        pltpu.VMEM((2, TB, CHUNK), BF16),
                        pltpu.VMEM((2, nch * GLA_DV, GLA_DK), F32),
                        pltpu.VMEM((2, TB, GLA_DV), F32),
                        pltpu.VMEM((L, GLA_DV), F32)],
        compiler_params=_cparams(("arbitrary", "arbitrary")),
        name="gla",
    )(*([p, p, p, p_lr, p] * 2), w_gate, b_gate, out_g)


def _filter_features(L):
    t = np.linspace(0.0, 1.0, L)[:, None]
    bands = (FILTER_EMB - 1) // 2
    freqs = np.linspace(1e-4, bands - 1, bands)[None, :]
    ang = (2.0 * np.pi / L) * np.arange(L)[:, None] * freqs
    feat = np.concatenate([t, np.cos(ang), -np.sin(ang)], axis=-1)
    feat_rev = np.roll(feat[::-1], 1, axis=0)
    both = np.zeros((L, LANES), np.float32)
    both[:, :FILTER_EMB] = feat
    both[:, FILTER_EMB:2 * FILTER_EMB] = feat_rev
    return jnp.asarray(both)


def _twice(w, rows_out):
    r, c = w.shape
    zz = jnp.zeros_like(w)
    both = jnp.concatenate([jnp.concatenate([w, zz], axis=1), jnp.concatenate([zz, w], axis=1)], axis=0)
    return jnp.pad(both, ((0, rows_out - 2 * r), (0, 0)))


def _filter_deltas():
    min_decay = math.log(DECAY_TARGET) / LONG_DECAY_PCT
    max_decay = math.log(DECAY_TARGET) / SHORT_DECAY_PCT
    return jnp.abs(jnp.linspace(min_decay, max_decay, HY_WIDTH, dtype=F32)).reshape(1, HY_WIDTH)


def _block_diag_w4(w4):
    H = w4.shape[0]
    nblk = HY_WIDTH // LANES
    wf = w4[:, :HY_WIDTH].reshape(H, nblk, LANES)
    wb = w4[:, HY_WIDTH:].reshape(H, nblk, LANES)
    zz = jnp.zeros_like(wf)
    top = jnp.concatenate([wf, zz], axis=2)
    bot = jnp.concatenate([zz, wb], axis=2)
    return jnp.concatenate([top, bot], axis=0).reshape(2 * H, nblk * 2 * LANES)


def kernel(x, ffn1_norm, ffn1_w_gate, ffn1_w_up, ffn1_w_down, mix_norm, w_in, hy_conv_w, hy_conv_b, flt_w1, flt_b1, flt_f1, flt_w2, flt_b2, flt_f2, flt_w3, flt_b3, flt_f3, flt_w4, hy_skip, hy_out_norm, gla_w_a2_f, gla_b_a_f, gla_w_a2_b, gla_b_a_b, gla_out_norm, w_out, ffn2_norm, ffn2_w_gate, ffn2_w_up, ffn2_w_down, final_norm):
    B, L, D = x.shape
    depth = ffn1_norm.shape[0]
    tabs = _dft_tables(L)
    feat2 = _filter_features(L)
    deltas = _filter_deltas()
    gla_col0 = 3 * HY_WIDTH
    H2 = 2 * FILTER_HIDDEN
    both = lambda a: jnp.concatenate([a, a]).reshape(1, H2)
    filt = []
    for l in range(depth):
        early = [jnp.swapaxes(w_in[l], 0, 1)]
        if l == 0:
            early += [ffn1_w_gate[0], ffn1_w_up[0], ffn1_w_down[0]]
        hid3, cast = _filt_mlp(feat2, _twice(flt_w1[l], LANES), both(flt_b1[l]), both(flt_f1[l]),
                               _twice(flt_w2[l], H2), both(flt_b2[l]), both(flt_f2[l]),
                               _twice(flt_w3[l], H2), both(flt_b3[l]), both(flt_f3[l]), to_cast=early)
        w4cat = _three_pass_rows(_block_diag_w4(flt_w4[l]))
        khat, knyq = _filt_fft(hid3, w4cat, deltas, tabs["g_full"], tabs["f3"], L=L)
        filt.append((khat, knyq, cast))

    outs = []
    for bi in range(B):
        xb = x[bi]
        w1 = filt[0][2][1:]
        for l in range(depth):
            last = l == depth - 1
            khat, knyq, (w_in_t, *_) = filt[l]
            later = [ffn2_w_gate[l], ffn2_w_up[l], ffn2_w_down[l], w_out[l]]
            xb, (w2_gate, w2_up, w2_down, w_out_bf) = _ffn(xb, ffn1_norm[l], *w1, final_norm, final_norm=False,
                                                           to_cast=later)
            p, p_lr = _in_proj(xb, mix_norm[l], w_in_t)
            y_hy = _hy_conv(p, hy_conv_w[l], hy_conv_b[l].reshape(1, -1), khat, knyq,
                            hy_skip[l].reshape(1, -1), hy_out_norm[l].reshape(1, -1), tabs, L=L)

            w_gate = jnp.stack([jnp.pad(gla_w_a2_f[l], ((0, LANES - GATE_RANK), (0, 0))),
                                jnp.pad(gla_w_a2_b[l], ((GATE_RANK, LANES - 2 * GATE_RANK), (0, 0)))])
            w_gate = _three_pass_rows(w_gate)
            b_gate = jnp.stack([gla_b_a_f[l], gla_b_a_b[l]]).reshape(2, 1, GLA_KW)
            y_gla = _gla(p, p_lr, w_gate, b_gate, gla_out_norm[l].reshape(1, -1), L=L, col0=gla_col0)

            xb = _out_proj(xb, y_hy, y_gla, w_out_bf)
            nxt = [] if last else [ffn1_w_gate[l + 1], ffn1_w_up[l + 1], ffn1_w_down[l + 1]]
            xb, w1 = _ffn(xb, ffn2_norm[l], w2_gate, w2_up, w2_down, final_norm, final_norm=last, to_cast=nxt)
        outs.append(xb)
    return jnp.stack(outs)
```

```python
import functools
import math

import numpy as np
import jax
import jax.numpy as jnp
from jax import lax
from jax.experimental import pallas as pl
from jax.experimental.pallas import tpu as pltpu

F32 = jnp.float32
BF16 = jnp.bfloat16
HIGHEST = lax.Precision.HIGHEST

EPS = 1e-6
HY_WIDTH = 1024
HY_GROUPS = 8
FILTER_EMB = 33
FILTER_HIDDEN = 64
SHORT_DECAY_PCT = 0.3
LONG_DECAY_PCT = 1.5
DECAY_TARGET = 1e-2
GLA_HEADS = 4
GLA_DK = 128
GLA_DV = 256
GLA_KW = GLA_HEADS * GLA_DK
GLA_VW = GLA_HEADS * GLA_DV
GATE_RANK = 16
GATE_TEMP = 16.0
CHUNK = 64

LANES = 128
SUBLANES = 8
VMEM_LIMIT = 60 * 1024 * 1024
FFN_SLAB = 128
RB = 128
KB = RB // 2
PITCH = RB + SUBLANES


def _cparams(sem):
    return pltpu.CompilerParams(dimension_semantics=sem, vmem_limit_bytes=VMEM_LIMIT)


def _single(block_shape, index_map):
    return pl.BlockSpec(block_shape, index_map, pipeline_mode=pl.Buffered(1))


def _split_hi_lo(x):
    hi = x.astype(BF16)
    return hi, (x - hi.astype(F32)).astype(BF16)


def _three_pass_rows(w):
    w_hi, w_lo = _split_hi_lo(w)
    return jnp.concatenate([w_hi, w_lo, w_hi], axis=-2)


def _dot_nt(a, b):
    return lax.dot_general(a, b, (((1,), (1,)), ((), ())), preferred_element_type=F32)


def _dot_tn(a, b):
    return lax.dot_general(a, b, (((0,), (0,)), ((), ())), preferred_element_type=F32)


def _ffn_kernel(x_ref, g_ref, wg_ref, wu_ref, wd_ref, fg_ref, *rest, final_norm, n_cast):
    cast_src, o_ref, cast_dst, h_scr = rest[:n_cast], rest[n_cast], rest[n_cast + 1:-1], rest[-1]
    j = pl.program_id(1)

    slab = min(FFN_SLAB, x_ref.shape[0])
    nslab = x_ref.shape[0] // slab

    @pl.when(j == 0)
    def _():
        def norm_rows(t, carry):
            rows = pl.ds(pl.multiple_of(t * slab, slab), slab)
            x = x_ref[rows, :]
            r = lax.rsqrt(jnp.mean(x * x, axis=-1, keepdims=True) + EPS)
            h_scr[rows, :] = (x * r * g_ref[...]).astype(BF16)
            o_ref[rows, :] = jnp.zeros((slab, x_ref.shape[1]), F32)
            return carry
        lax.fori_loop(0, nslab, norm_rows, 0)

    h = h_scr[...]
    gate = jnp.dot(h, wg_ref[...], preferred_element_type=F32)
    up = jnp.dot(h, wu_ref[...], preferred_element_type=F32)
    a = (jax.nn.silu(gate) * up).astype(BF16)
    o_ref[...] += jnp.dot(a, wd_ref[...], preferred_element_type=F32)

    for src, dst in zip(cast_src, cast_dst):
        dst[...] = src[...].astype(BF16)

    @pl.when(j == pl.num_programs(1) - 1)
    def _():
        def finish_rows(t, carry):
            rows = pl.ds(pl.multiple_of(t * slab, slab), slab)
            y = x_ref[rows, :] + 0.5 * o_ref[rows, :]
            if final_norm:
                r = lax.rsqrt(jnp.mean(y * y, axis=-1, keepdims=True) + EPS)
                y = y * r * fg_ref[...]
            o_ref[rows, :] = y
            return carry
        lax.fori_loop(0, nslab, finish_rows, 0)


def _cast_blocking(shape, ni, nj):
    rows, cols = shape
    assert rows % ni == 0 and (rows // ni) % SUBLANES == 0 and cols % LANES == 0
    ncol = max(d for d in range(1, nj + 1) if (cols // LANES) % d == 0)
    return (rows // ni, cols // ncol), (lambda i, j: (i, jnp.minimum(j, ncol - 1)))


def _ffn(x, norm_g, w_gate, w_up, w_down, final_g, *, final_norm, to_cast=(), tm=1024, tf=512):
    L, D = x.shape
    DF = w_gate.shape[1]
    tm = min(tm, L)
    tf = min(tf, DF)
    ni, nj = L // tm, DF // tf
    blockings = [_cast_blocking(w.shape, ni, nj) for w in to_cast]
    cast_specs = [pl.BlockSpec(blk, imap) for blk, imap in blockings]
    outs = pl.pallas_call(
        functools.partial(_ffn_kernel, final_norm=final_norm, n_cast=len(to_cast)),
        grid=(ni, nj),
        in_specs=[
            pl.BlockSpec((tm, D), lambda i, j: (i, 0)),
            pl.BlockSpec((1, D), lambda i, j: (0, 0)),
            pl.BlockSpec((D, tf), lambda i, j: (0, j)),
            pl.BlockSpec((D, tf), lambda i, j: (0, j)),
            pl.BlockSpec((tf, D), lambda i, j: (j, 0)),
            pl.BlockSpec((1, D), lambda i, j: (0, 0)),
        ] + cast_specs,
        out_specs=[pl.BlockSpec((tm, D), lambda i, j: (i, 0))] + cast_specs,
        out_shape=[jax.ShapeDtypeStruct((L, D), F32)] + [jax.ShapeDtypeStruct(w.shape, BF16) for w in to_cast],
        scratch_shapes=[pltpu.VMEM((tm, D), BF16)],
        compiler_params=_cparams(("arbitrary", "arbitrary")),
        name="ffn",
    )(x, norm_g.reshape(1, D), w_gate, w_up, w_down, final_g.reshape(1, D), *to_cast)
    return outs[0], outs[1:]


def _in_proj_kernel(x_ref, g_ref, wt_ref, wtail_ref, o_ref, otail_ref, h_scr, *, n_tail):
    j = pl.program_id(1)

    @pl.when(j == 0)
    def _():
        x = x_ref[...]
        r = lax.rsqrt(jnp.mean(x * x, axis=-1, keepdims=True) + EPS)
        h = (x * r * g_ref[...]).astype(BF16)
        h_scr[...] = h
        tail = _dot_nt(h, wtail_ref[...])
        col = lax.broadcasted_iota(jnp.int32, tail.shape, 1)
        otail_ref[...] = jnp.where(col < n_tail, tail, 0.0)

    o_ref[...] = _dot_nt(h_scr[...], wt_ref[...])


def _in_proj(x, norm_g, w_t, *, tm=1024, tn=2048):
    L, D = x.shape
    n_cols = w_t.shape[0]
    nj = n_cols // tn
    n_tail = n_cols - nj * tn
    assert 0 < n_tail <= LANES and (nj * tn) % LANES == 0
    tm = min(tm, L)
    return pl.pallas_call(
        functools.partial(_in_proj_kernel, n_tail=n_tail),
        grid=(L // tm, nj),
        in_specs=[
            pl.BlockSpec((tm, D), lambda i, j: (i, 0)),
            pl.BlockSpec((1, D), lambda i, j: (0, 0)),
            pl.BlockSpec((tn, D), lambda i, j: (j, 0)),
            pl.BlockSpec((LANES, D), lambda i, j: (nj * tn // LANES, 0)),
        ],
        out_specs=[pl.BlockSpec((tm, tn), lambda i, j: (i, j)),
                   pl.BlockSpec((tm, LANES), lambda i, j: (i, 0))],
        out_shape=[jax.ShapeDtypeStruct((L, nj * tn), F32),
                   jax.ShapeDtypeStruct((L, LANES), F32)],
        scratch_shapes=[pltpu.VMEM((tm, D), BF16)],
        compiler_params=_cparams(("parallel", "arbitrary")),
        name="in_proj",
    )(x, norm_g.reshape(1, D), w_t, w_t)


def _out_proj_kernel(x_ref, yh_ref, yg_ref, wh_ref, wg_ref, o_ref):
    o_ref[...] = (x_ref[...]
                  + jnp.dot(yh_ref[...], wh_ref[...], preferred_element_type=F32)
                  + jnp.dot(yg_ref[...], wg_ref[...], preferred_element_type=F32))


def _out_proj(x, y_hy, y_gla, w_out, *, tm=512):
    L, D = x.shape
    WH = y_hy.shape[1]
    WG = y_gla.shape[1]
    tm = min(tm, L)
    assert WH == WG and w_out.shape[0] == WH + WG
    return pl.pallas_call(
        _out_proj_kernel,
        grid=(L // tm,),
        in_specs=[
            pl.BlockSpec((tm, D), lambda i: (i, 0)),
            pl.BlockSpec((tm, WH), lambda i: (i, 0)),
            pl.BlockSpec((tm, WG), lambda i: (i, 0)),
            _single((WH, D), lambda i: (0, 0)),
            _single((WG, D), lambda i: (1, 0)),
        ],
        out_specs=pl.BlockSpec((tm, D), lambda i: (i, 0)),
        out_shape=jax.ShapeDtypeStruct((L, D), F32),
        compiler_params=_cparams(("parallel",)),
        name="out_proj",
    )(x, y_hy, y_gla, w_out, w_out)


def _filt_mlp_kernel(z_ref, w1_ref, b1_ref, f1_ref, w2_ref, b2_ref, f2_ref, w3_ref, b3_ref, f3_ref, *rest, n_cast):
    cast_src, o_ref, cast_dst = rest[:n_cast], rest[n_cast], rest[n_cast + 1:]
    dot = functools.partial(jnp.dot, precision=HIGHEST, preferred_element_type=F32)
    hid = jnp.sin(f1_ref[...] * (dot(z_ref[...], w1_ref[...]) + b1_ref[...]))
    hid = jnp.sin(f2_ref[...] * (dot(hid, w2_ref[...]) + b2_ref[...]))
    hid = jnp.sin(f3_ref[...] * (dot(hid, w3_ref[...]) + b3_ref[...]))
    hi, lo = _split_hi_lo(hid)
    o_ref[...] = jnp.concatenate([hi, hi, lo], axis=1)
    for src, dst in zip(cast_src, cast_dst):
        dst[...] = src[...].astype(BF16)


def _cast_row_blocks(shape, n):
    rows, cols = shape
    tile = 2 * SUBLANES
    block_rows = pl.cdiv(pl.cdiv(rows, n), tile) * tile
    nblocks = pl.cdiv(rows, block_rows)
    return (block_rows, cols), (lambda i: (jnp.minimum(i, nblocks - 1), 0))


def _filt_mlp(feat2, w1, b1, f1, w2, b2, f2, w3, b3, f3, *, to_cast=(), tm=512):
    L, FP = feat2.shape
    H2 = w2.shape[0]
    tm = min(tm, L)
    steps = L // tm
    full = lambda shp: pl.BlockSpec(shp, lambda i: (0, 0))
    cast_specs = [pl.BlockSpec(*_cast_row_blocks(w.shape, steps)) for w in to_cast]
    outs = pl.pallas_call(
        functools.partial(_filt_mlp_kernel, n_cast=len(to_cast)),
        grid=(steps,),
        in_specs=[pl.BlockSpec((tm, FP), lambda i: (i, 0)),
                  full((FP, H2)), full((1, H2)), full((1, H2)),
                  full((H2, H2)), full((1, H2)), full((1, H2)),
                  full((H2, H2)), full((1, H2)), full((1, H2))] + cast_specs,
        out_specs=[pl.BlockSpec((tm, 3 * H2), lambda i: (i, 0))] + cast_specs,
        out_shape=[jax.ShapeDtypeStruct((L, 3 * H2), BF16)] + [jax.ShapeDtypeStruct(w.shape, BF16) for w in to_cast],
        compiler_params=_cparams(("arbitrary",)),
        name="filt_mlp",
    )(feat2, w1, b1, f1, w2, b2, f2, w3, b3, f3, *to_cast)
    return outs[0], outs[1:]


def _dft_tables(L):
    N = 2 * L
    RA = N // RB
    NH = RA // 2
    two_pi = 2.0 * np.pi
    k1 = np.arange(RA, dtype=np.int64)

    def step1_table(NR):
        g = np.zeros((RB // 2, 2 * RA, 2 * NR), np.float64)
        for s in range(2):
            n = RB * np.arange(NR, dtype=np.int64)[None, :] + (2 * np.arange(RB // 2, dtype=np.int64) + s)[:, None]
            ang = two_pi * ((k1[None, :, None] * n[:, None, :]) % N) / N
            g[:, :RA, s * NR:(s + 1) * NR] = np.cos(ang)
            g[:, RA:, s * NR:(s + 1) * NR] = -np.sin(ang)
        return g

    phi = two_pi * ((np.arange(KB)[:, None] * np.arange(RB)[None, :]) % RB) / RB
    f3 = np.block([[np.cos(phi), np.sin(phi)], [-np.sin(phi), np.cos(phi)]])
    i1 = np.block([[np.cos(phi.T), -np.sin(phi.T)], [np.sin(phi.T), np.cos(phi.T)]])
    cast = lambda a: jnp.asarray(a.astype(np.float32)).astype(BF16)
    return dict(g_half=cast(step1_table(NH)), g_full=cast(step1_table(RA)), f3=cast(f3), i1=cast(i1))


def _fwd_step1(u_ref, g_ref, ab_ref, *, RA, NR, CB, G=2):
    def body(t, carry):
        ms = [t * G + j for j in range(G)]
        rhs = []
        for m in ms:
            ua = u_ref[pl.ds(2 * m, NR, stride=PITCH), :]
            ub = u_ref[pl.ds(2 * m + 1, NR, stride=PITCH), :]
            zz = jnp.zeros_like(ua)
            rhs.append(jnp.concatenate([jnp.concatenate([ua, zz], axis=1),
                                        jnp.concatenate([zz, ub], axis=1)], axis=0).astype(BF16))
        res = [jnp.dot(g_ref[m], r, preferred_element_type=F32) for m, r in zip(ms, rhs)]
        for m, r in zip(ms, res):
            ab_ref[pl.ds(2 * m, RA, stride=PITCH), :] = r[:RA, :CB]
            ab_ref[pl.ds(RA * PITCH + 2 * m, RA, stride=PITCH), :] = r[RA:, :CB]
            ab_ref[pl.ds(2 * m + 1, RA, stride=PITCH), :] = r[:RA, CB:]
            ab_ref[pl.ds(RA * PITCH + 2 * m + 1, RA, stride=PITCH), :] = r[RA:, CB:]
        return carry
    lax.fori_loop(0, RB // 2 // G, body, 0)


def _fwd_step3_pair(ab_ref, f3, kp, *, RA):
    cols = []
    for s in range(2):
        k1 = 2 * kp + s
        ar = ab_ref[pl.ds(pl.multiple_of(k1 * PITCH, SUBLANES), RB), :]
        ai = ab_ref[pl.ds(pl.multiple_of((RA + k1) * PITCH, SUBLANES), RB), :]
        cols.append(jnp.concatenate([ar, ai], axis=0))
    rhs = jnp.concatenate(cols, axis=1).astype(BF16)
    return jnp.dot(f3, rhs, preferred_element_type=F32)


def _for_row_blocks(nblocks, body, carry, unroll=2):
    assert nblocks >= 2
    carry = body(0, carry, True, False)
    carry = lax.fori_loop(1, nblocks - 1, lambda n1, c: body(n1, c, False, False), carry,
                          unroll=max(1, min(unroll, nblocks - 2)))
    return body(nblocks - 1, carry, False, True)


def _alt_sign(shape):
    rows = lax.broadcasted_iota(jnp.int32, shape, 0)
    return jnp.where(rows % 2 == 0, 1.0, -1.0).astype(F32)


def _filt_fft_kernel(hid_ref, w4_ref, dl_ref, g_ref, f3_ref, kh_ref, kn_ref, u_scr, ab_scr, *, L, CB, G):
    N = 2 * L
    RA = N // RB
    NH = RA // 2
    f3 = f3_ref[...]
    inv_lm1 = 1.0 / (L - 1)

    row = lax.broadcasted_iota(jnp.int32, (RB, CB), 0)
    arg_base = -(row.astype(F32) * inv_lm1) * dl_ref[...]
    arg_step = -(RB * inv_lm1) * dl_ref[...]
    arg_end = -(L * inv_lm1) * dl_ref[...]

    def fill(n1, carry, first, last):
        s_abs, s_alt = carry
        r0 = pl.multiple_of(n1 * RB, RB)
        h2 = jnp.dot(hid_ref[pl.ds(r0, RB), :], w4_ref[...], preferred_element_type=F32)
        arg = arg_base + jnp.asarray(n1, F32) * arg_step
        hf = h2[:, :CB] * jnp.exp(arg)
        hb = h2[:, CB:] * jnp.exp(arg_end - arg)
        if first:
            hb = jnp.where(row == 0, 0.0, hb)
        u_scr[pl.ds(pl.multiple_of(n1 * PITCH, SUBLANES), RB), :] = hf
        u_scr[pl.ds(pl.multiple_of((NH + n1) * PITCH, SUBLANES), RB), :] = hb
        return s_abs + (jnp.abs(hf) + jnp.abs(hb)), s_alt + (hf + hb)
    zero = jnp.zeros((RB, CB), F32)
    s_abs, s_alt = _for_row_blocks(NH, fill, (zero, zero), unroll=8)
    inv_l1 = 1.0 / jnp.sum(s_abs, axis=0, keepdims=True)
    kn = jnp.sum(s_alt * _alt_sign((RB, CB)), axis=0, keepdims=True) * inv_l1 * (1.0 / N)
    kn_ref[...] = jnp.broadcast_to(kn, kn_ref.shape)

    _fwd_step1(u_scr, g_ref, ab_scr, RA=RA, NR=RA, CB=CB, G=G)

    scale = inv_l1 * (2.0 / N)
    scale2 = jnp.concatenate([scale, scale], axis=1)
    row2 = lax.broadcasted_iota(jnp.int32, (KB, 2 * CB), 0)
    lane2 = lax.broadcasted_iota(jnp.int32, (KB, 2 * CB), 1)
    dc_pos = jnp.logical_and(row2 == 0, lane2 < CB)

    def emit(t, carry):
        kps = [t * G + j for j in range(G)]
        xs = [_fwd_step3_pair(ab_scr, f3, kp, RA=RA) for kp in kps]
        for kp, x in zip(kps, xs):
            kr = x[:KB] * scale2
            ki = x[KB:] * scale2
            dc = jnp.logical_and(dc_pos, kp == 0)
            kr = jnp.where(dc, 0.5 * kr, kr)
            ki = jnp.where(dc, 0.5 * ki, ki)
            kh_ref[pl.ds(pl.multiple_of(kp * RB, RB), RB), :] = jnp.concatenate([kr, ki], axis=0).astype(kh_ref.dtype)
        return carry
    lax.fori_loop(0, RA // 2 // G, emit, 0)


def _filt_fft(hid2, w4bd, deltas, g_tab, f3_tab, *, L, CB=LANES, G=16):
    C = deltas.shape[1]
    H2 = hid2.shape[1]
    N = 2 * L
    RA = N // RB
    nblk = C // CB
    G = min(G, RA // 2)
    return pl.pallas_call(
        functools.partial(_filt_fft_kernel, L=L, CB=CB, G=G),
        grid=(nblk,),
        in_specs=[
            _single((L, H2), lambda c: (0, 0)),
            pl.BlockSpec((H2, 2 * CB), lambda c: (0, c)),
            pl.BlockSpec((1, CB), lambda c: (0, c)),
            _single(g_tab.shape, lambda c: (0, 0, 0)),
            _single(f3_tab.shape, lambda c: (0, 0)),
        ],
        out_specs=[pl.BlockSpec((RA // 2 * RB, 2 * CB), lambda c: (0, c)),
                   pl.BlockSpec((SUBLANES, CB), lambda c: (0, c))],
        out_shape=[jax.ShapeDtypeStruct((RA // 2 * RB, 2 * C), BF16),
                   jax.ShapeDtypeStruct((SUBLANES, C), F32)],
        scratch_shapes=[pltpu.VMEM((RA * PITCH, CB), F32),
                        pltpu.VMEM((2 * RA * PITCH, CB), F32)],
        compiler_params=_cparams(("arbitrary",)),
        name="filt_fft",
    )(hid2, w4bd, deltas, g_tab, f3_tab)


def _short_conv_rows(p_ref, w_ref, b_ref, n1, *, first, last):
    r0 = pl.multiple_of(n1 * RB, RB)
    cur = p_ref[pl.ds(r0, RB), :]
    rows = lax.broadcasted_iota(jnp.int32, cur.shape, 0)
    if first:
        up = jnp.where(rows == 0, 0.0, pltpu.roll(cur, 1, axis=0))
    else:
        up = p_ref[pl.ds(r0 - 1, RB), :]
    if last:
        dn = jnp.where(rows == RB - 1, 0.0, pltpu.roll(cur, RB - 1, axis=0))
    else:
        dn = p_ref[pl.ds(r0 + 1, RB), :]
    w = w_ref[...]
    return b_ref[...] + up * w[0:1] + cur * w[1:2] + dn * w[2:3]


def _hy_conv_kernel(p_hbm, w0_ref, w1_ref, wv_ref, b0_ref, b1_ref, bv_ref,
                    kh_ref, kn_ref, skip_ref, og_ref, g_ref, f3_ref, i1_ref,
                    o_ref, u_scr, ab_scr, z_scr, p_scr, p_sem, *, L, CB, G):
    N = 2 * L
    RA = N // RB
    NH = RA // 2
    f3 = f3_ref[...]
    i1 = i1_ref[...]
    sign = _alt_sign((RB, CB))
    c = pl.program_id(0)
    nblk = pl.num_programs(0)
    p0_ref, p1_ref, pv_ref = p_scr.at[0], p_scr.at[1], p_scr.at[2]

    def fetch(group, blk):
        col = pl.multiple_of((group * nblk + blk) * CB, CB)
        return pltpu.make_async_copy(p_hbm.at[:, pl.ds(col, CB)], p_scr.at[group], p_sem.at[group])

    @pl.when(c == 0)
    def _():
        fetch(1, c).start()
        fetch(2, c).start()

    fetch(0, c).start()
    fetch(1, c).wait()
    fetch(2, c).wait()

    def fill(n1, s_alt, first, last):
        z = (_short_conv_rows(pv_ref, wv_ref, bv_ref, n1, first=first, last=last)
             * _short_conv_rows(p1_ref, w1_ref, b1_ref, n1, first=first, last=last))
        u_scr[pl.ds(pl.multiple_of(n1 * PITCH, SUBLANES), RB), :] = z
        z_scr[pl.ds(pl.multiple_of(n1 * RB, RB), RB), :] = z
        return s_alt + z
    s_alt = _for_row_blocks(NH, fill, jnp.zeros((RB, CB), F32))

    @pl.when(c + 1 < nblk)
    def _():
        fetch(1, c + 1).start()
        fetch(2, c + 1).start()

    z_nyq = jnp.sum(s_alt * sign, axis=0, keepdims=True)
    y_nyq = z_nyq * kn_ref[0:1, :]

    _fwd_step1(u_scr, g_ref, ab_scr, RA=RA, NR=NH, CB=CB, G=G)

    GM = max(G // 2, 1)

    def mid(t, carry):
        kps = [t * GM + j for j in range(GM)]
        xs = [_fwd_step3_pair(ab_scr, f3, kp, RA=RA) for kp in kps]
        khs = [kh_ref[pl.ds(pl.multiple_of(kp * RB, RB), RB), :].astype(F32) for kp in kps]
        bs = []
        for x, kh in zip(xs, khs):
            xr, xi, kr, ki = x[:KB], x[KB:], kh[:KB], kh[KB:]
            y = jnp.concatenate([xr * kr - xi * ki, xr * ki + xi * kr], axis=0).astype(BF16)
            bs.append(jnp.dot(i1, y, preferred_element_type=F32))
        for kp, b in zip(kps, bs):
            for s in range(2):
                k1 = 2 * kp + s
                ab_scr[pl.ds(pl.multiple_of(k1 * PITCH, SUBLANES), RB), :] = b[:RB, s * CB:(s + 1) * CB]
                ab_scr[pl.ds(pl.multiple_of((RA + k1) * PITCH, SUBLANES), RB), :] = b[RB:, s * CB:(s + 1) * CB]
        return carry
    lax.fori_loop(0, RA // 2 // GM, mid, 0)

    def last(t, carry):
        ms = [t * G + j for j in range(G)]
        rhs = [jnp.concatenate(
            [jnp.concatenate([ab_scr[pl.ds(2 * m + s, RA, stride=PITCH), :],
                              ab_scr[pl.ds(RA * PITCH + 2 * m + s, RA, stride=PITCH), :]], axis=0)
             for s in range(2)], axis=1).astype(BF16) for m in ms]
        ys = [_dot_tn(g_ref[m], r) for m, r in zip(ms, rhs)]
        for m, y in zip(ms, ys):
            u_scr[pl.ds(2 * m, NH, stride=PITCH), :] = y[:NH, :CB]
            u_scr[pl.ds(2 * m + 1, NH, stride=PITCH), :] = y[NH:, CB:]
        return carry
    lax.fori_loop(0, RB // 2 // G, last, 0)

    nyq_rows = sign * y_nyq
    fetch(0, c).wait()

    def finish(n1, carry, first, last):
        z = z_scr[pl.ds(pl.multiple_of(n1 * RB, RB), RB), :]
        x0 = _short_conv_rows(p0_ref, w0_ref, b0_ref, n1, first=first, last=last)
        conv = u_scr[pl.ds(pl.multiple_of(n1 * PITCH, SUBLANES), RB), :] + nyq_rows
        y = (conv + z * skip_ref[...]) * x0
        y = y * lax.rsqrt(jnp.mean(y * y, axis=-1, keepdims=True) + EPS) * og_ref[...]
        o_ref[pl.ds(pl.multiple_of(n1 * RB, RB), RB), :] = y.astype(o_ref.dtype)
        return carry
    _for_row_blocks(NH, finish, 0, unroll=4)


def _hy_conv(p, conv_w, conv_b, khat, knyq, skip, out_g, tabs, *, L, CB=LANES, G=16):
    C = skip.shape[1]
    assert CB == C // HY_GROUPS, "one channel block must be exactly one norm group"
    N = 2 * L
    RA = N // RB
    NH = RA // 2
    nblk = C // CB
    G = min(G, RA // 2)
    g_tab, f3_tab, i1_tab = tabs["g_half"], tabs["f3"], tabs["i1"]
    col = lambda off: (lambda c: (0, off * nblk + c))
    wspec = lambda off: pl.BlockSpec((3, CB), col(off))
    bspec = lambda off: pl.BlockSpec((1, CB), col(off))
    return pl.pallas_call(
        functools.partial(_hy_conv_kernel, L=L, CB=CB, G=G),
        grid=(nblk,),
        in_specs=[pl.BlockSpec(memory_space=pl.ANY),
                  wspec(0), wspec(1), wspec(2), bspec(0), bspec(1), bspec(2),
                  pl.BlockSpec((RA // 2 * RB, 2 * CB), col(0)),
                  pl.BlockSpec((SUBLANES, CB), col(0)),
                  bspec(0), bspec(0),
                  _single(g_tab.shape, lambda c: (0, 0, 0)),
                  _single(f3_tab.shape, lambda c: (0, 0)),
                  _single(i1_tab.shape, lambda c: (0, 0))],
        out_specs=pl.BlockSpec((L, CB), col(0)),
        out_shape=jax.ShapeDtypeStruct((L, C), BF16),
        scratch_shapes=[pltpu.VMEM((NH * PITCH, CB), F32),
                        pltpu.VMEM((2 * RA * PITCH, CB), F32),
                        pltpu.VMEM((L, CB), F32),
                        pltpu.VMEM((3, L, CB), F32),
                        pltpu.SemaphoreType.DMA((3,))],
        compiler_params=_cparams(("arbitrary",)),
        name="hy_conv",
    )(p, conv_w, conv_w, conv_w, conv_b, conv_b, conv_b, khat, knyq, skip, out_g,
      g_tab, f3_tab, i1_tab)


def _log_sigmoid(x):
    return jnp.minimum(x, 0.0) - jnp.log(1.0 + jnp.exp(-jnp.abs(x)))


def _gla_scan_blocks(dirs, *, TB):
    nch = TB // CHUNK
    rows = lambda c: slice(c * CHUNK, (c + 1) * CHUNK)
    units = [(d, c) for c in range(nch) for d in dirs]
    rr = lax.broadcasted_iota(jnp.int32, (CHUNK, CHUNK), 0)
    cc = lax.broadcasted_iota(jnp.int32, (CHUNK, CHUNK), 1)
    scale = GLA_DK ** -0.5

    for d in dirs:
        lr_hi, lr_lo = _split_hi_lo(d["lr"][...])
        gate_in = jnp.dot(jnp.concatenate([lr_hi, lr_hi, lr_lo], axis=1), d["wg"],
                          preferred_element_type=F32) + d["bg"]
        g = _log_sigmoid(gate_in) * (1.0 / GATE_TEMP)
        g_hi = g.astype(BF16)
        g_mid, g_lo = _split_hi_lo(g - g_hi.astype(F32))
        d["parts"][...] = jnp.concatenate([g_hi, g_mid, g_lo], axis=1)
        d["csum"] = jnp.where((cc >= rr) if d["reverse"] else (cc <= rr), 1.0, 0.0).astype(BF16)
        d["mask"] = (cc > rr) if d["reverse"] else (cc <= rr)
        d["edge"] = 0 if d["reverse"] else CHUNK - 1

    for d, c in units:
        b3 = jnp.dot(d["csum"], d["parts"][rows(c), :], preferred_element_type=F32)
        d["b"][rows(c), :] = b3[:, :GLA_DK] + b3[:, GLA_DK:2 * GLA_DK] + b3[:, 2 * GLA_DK:]

    decay = {}
    for d, c in units:
        b = d["b"][rows(c), :]
        b_edge = b[d["edge"]:d["edge"] + 1]
        k = d["k"][rows(c), :]
        d["qks"][0, rows(c), :] = (d["q"][rows(c), :] * scale * jnp.exp(b)).astype(BF16)
        d["qks"][1, rows(c), :] = (k * jnp.exp(-b)).astype(BF16)
        d["qks"][2, rows(c), :] = (k * jnp.exp(b_edge - b)).astype(BF16)
        d["vb"][rows(c), :] = d["v"][rows(c), :].astype(BF16)
        decay[(id(d), c)] = jnp.exp(b_edge)

    for d, c in units:
        att = _dot_nt(d["qks"][0, rows(c), :], d["qks"][1, rows(c), :])
        d["att"][rows(c), :] = jnp.where(d["mask"], att, 0.0).astype(BF16)

    for d, c in units:
        d["o"][rows(c), :] = jnp.dot(d["att"][rows(c), :], d["vb"][rows(c), :], preferred_element_type=F32)

    for d, c in units:
        d["kv"][c * GLA_DV:(c + 1) * GLA_DV, :] = _dot_tn(d["vb"][rows(c), :], d["qks"][2, rows(c), :])

    state = {id(d): d["s"][...] for d in dirs}
    for step in range(nch):
        for d in dirs:
            c = nch - 1 - step if d["reverse"] else step
            s_t = state[id(d)]
            d["o"][rows(c), :] += _dot_nt(d["qks"][0, rows(c), :], s_t.astype(BF16))
            state[id(d)] = s_t * decay[(id(d), c)] + d["kv"][c * GLA_DV:(c + 1) * GLA_DV, :]
    for d in dirs:
        d["s"][...] = state[id(d)]


def _gla_kernel(qf_ref, kf_ref, vf_ref, lf_ref, rf_ref, qb_ref, kb_ref, vb_ref, lb_ref, rb_ref,
                wg_ref, bg_ref, og_ref, o_ref,
                s_scr, parts_scr, b_scr, qks_scr, vb_scr, att_scr, kv_scr, ob_scr, half_scr, *, TB):
    s = pl.program_id(1)
    nb = pl.num_programs(1)

    @pl.when(s == 0)
    def _():
        s_scr[...] = jnp.zeros_like(s_scr)

    def direction(i, q, k, v, lr):
        return dict(q=q, k=k, v=v, lr=lr, wg=wg_ref[i], bg=bg_ref[i], reverse=bool(i), s=s_scr.at[i],
                    parts=parts_scr.at[i], b=b_scr.at[i], qks=qks_scr.at[i], vb=vb_scr.at[i],
                    att=att_scr.at[i], kv=kv_scr.at[i], o=ob_scr.at[i])
    _gla_scan_blocks([direction(0, qf_ref, kf_ref, vf_ref, lf_ref),
                      direction(1, qb_ref, kb_ref, vb_ref, lb_ref)], TB=TB)

    first_half = s < nb // 2
    for d, blk, r_ref in ((0, s, rf_ref), (1, nb - 1 - s, rb_ref)):
        rows = pl.ds(pl.multiple_of(blk * TB, TB), TB)

        @pl.when(first_half)
        def _():
            half_scr[rows, :] = ob_scr[d]

        @pl.when(jnp.logical_not(first_half))
        def _():
            tot = half_scr[rows, :] + ob_scr[d]
            tot = tot * lax.rsqrt(jnp.mean(tot * tot, axis=-1, keepdims=True) + EPS) * og_ref[...]
            o_ref[rows, :] = (tot * jax.nn.silu(r_ref[...])).astype(o_ref.dtype)


def _gla(p, p_lr, w_gate, b_gate, out_g, *, L, col0, TB=1024):
    TB = min(TB, L)
    nb = L // TB
    assert nb % 2 == 0, "both scan directions must meet between two blocks"
    nch = TB // CHUNK
    qb = col0 // GLA_DK
    kb = qb + GLA_HEADS
    vb = (col0 + 2 * GLA_KW) // GLA_DV
    rb = vb + GLA_HEADS
    fwd = lambda s: s
    bwd = lambda s: nb - 1 - s

    def operands(blk):
        return [pl.BlockSpec((TB, GLA_DK), lambda h, s: (blk(s), qb + h)),
                pl.BlockSpec((TB, GLA_DK), lambda h, s: (blk(s), kb + h)),
                pl.BlockSpec((TB, GLA_DV), lambda h, s: (blk(s), vb + h)),
                pl.BlockSpec((TB, LANES), lambda h, s: (blk(s), 0)),
                pl.BlockSpec((TB, GLA_DV), lambda h, s: (blk(s), rb + h))]

    return pl.pallas_call(
        functools.partial(_gla_kernel, TB=TB),
        grid=(GLA_HEADS, nb),
        in_specs=operands(fwd) + operands(bwd) + [
            pl.BlockSpec((2, 3 * LANES, GLA_DK), lambda h, s: (0, 0, h)),
            pl.BlockSpec((2, 1, GLA_DK), lambda h, s: (0, 0, h)),
            pl.BlockSpec((1, GLA_DV), lambda h, s: (0, h)),
        ],
        out_specs=pl.BlockSpec((L, GLA_DV), lambda h, s: (0, h)),
        out_shape=jax.ShapeDtypeStruct((L, GLA_VW), BF16),
        scratch_shapes=[pltpu.VMEM((2, GLA_DV, GLA_DK), F32),
                        pltpu.VMEM((2, TB, 3 * GLA_DK), BF16),
                        pltpu.VMEM((2, TB, GLA_DK), F32),
                        pltpu.VMEM((2, 3, TB, GLA_DK), BF16),
                        pltpu.VMEM((2, TB, GLA_DV), BF16),
                        pltpu.VMEM((2, TB, CHUNK), BF16),
                        pltpu.VMEM((2, nch * GLA_DV, GLA_DK), F32),
                        pltpu.VMEM((2, TB, GLA_DV), F32),
                        pltpu.VMEM((L, GLA_DV), F32)],
        compiler_params=_cparams(("arbitrary", "arbitrary")),
        name="gla",
    )(*([p, p, p, p_lr, p] * 2), w_gate, b_gate, out_g)


def _filter_features(L):
    t = np.linspace(0.0, 1.0, L)[:, None]
    bands = (FILTER_EMB - 1) // 2
    freqs = np.linspace(1e-4, bands - 1, bands)[None, :]
    ang = (2.0 * np.pi / L) * np.arange(L)[:, None] * freqs
    feat = np.concatenate([t, np.cos(ang), -np.sin(ang)], axis=-1)
    feat_rev = np.roll(feat[::-1], 1, axis=0)
    both = np.zeros((L, LANES), np.float32)
    both[:, :FILTER_EMB] = feat
    both[:, FILTER_EMB:2 * FILTER_EMB] = feat_rev
    return jnp.asarray(both)


def _twice(w, rows_out):
    r, c = w.shape
    zz = jnp.zeros_like(w)
    both = jnp.concatenate([jnp.concatenate([w, zz], axis=1), jnp.concatenate([zz, w], axis=1)], axis=0)
    return jnp.pad(both, ((0, rows_out - 2 * r), (0, 0)))


def _filter_deltas():
    min_decay = math.log(DECAY_TARGET) / LONG_DECAY_PCT
    max_decay = math.log(DECAY_TARGET) / SHORT_DECAY_PCT
    return jnp.abs(jnp.linspace(min_decay, max_decay, HY_WIDTH, dtype=F32)).reshape(1, HY_WIDTH)


def _block_diag_w4(w4):
    H = w4.shape[0]
    nblk = HY_WIDTH // LANES
    wf = w4[:, :HY_WIDTH].reshape(H, nblk, LANES)
    wb = w4[:, HY_WIDTH:].reshape(H, nblk, LANES)
    zz = jnp.zeros_like(wf)
    top = jnp.concatenate([wf, zz], axis=2)
    bot = jnp.concatenate([zz, wb], axis=2)
    return jnp.concatenate([top, bot], axis=0).reshape(2 * H, nblk * 2 * LANES)


def kernel(x, ffn1_norm, ffn1_w_gate, ffn1_w_up, ffn1_w_down, mix_norm, w_in, hy_conv_w, hy_conv_b, flt_w1, flt_b1, flt_f1, flt_w2, flt_b2, flt_f2, flt_w3, flt_b3, flt_f3, flt_w4, hy_skip, hy_out_norm, gla_w_a2_f, gla_b_a_f, gla_w_a2_b, gla_b_a_b, gla_out_norm, w_out, ffn2_norm, ffn2_w_gate, ffn2_w_up, ffn2_w_down, final_norm):
    B, L, D = x.shape
    depth = ffn1_norm.shape[0]
    tabs = _dft_tables(L)
    feat2 = _filter_features(L)
    deltas = _filter_deltas()
    gla_col0 = 3 * HY_WIDTH
    H2 = 2 * FILTER_HIDDEN
    both = lambda a: jnp.concatenate([a, a]).reshape(1, H2)
    filt = []
    for l in range(depth):
        early = [jnp.swapaxes(w_in[l], 0, 1)]
        if l == 0:
            early += [ffn1_w_gate[0], ffn1_w_up[0], ffn1_w_down[0]]
        hid3, cast = _filt_mlp(feat2, _twice(flt_w1[l], LANES), both(flt_b1[l]), both(flt_f1[l]),
                               _twice(flt_w2[l], H2), both(flt_b2[l]), both(flt_f2[l]),
                               _twice(flt_w3[l], H2), both(flt_b3[l]), both(flt_f3[l]), to_cast=early)
        w4cat = _three_pass_rows(_block_diag_w4(flt_w4[l]))
        khat, knyq = _filt_fft(hid3, w4cat, deltas, tabs["g_full"], tabs["f3"], L=L)
        filt.append((khat, knyq, cast))

    outs = []
    for bi in range(B):
        xb = x[bi]
        w1 = filt[0][2][1:]
        for l in range(depth):
            last = l == depth - 1
            khat, knyq, (w_in_t, *_) = filt[l]
            later = [ffn2_w_gate[l], ffn2_w_up[l], ffn2_w_down[l], w_out[l]]
            xb, (w2_gate, w2_up, w2_down, w_out_bf) = _ffn(xb, ffn1_norm[l], *w1, final_norm, final_norm=False,
                                                           to_cast=later)
            p, p_lr = _in_proj(xb, mix_norm[l], w_in_t)
            y_hy = _hy_conv(p, hy_conv_w[l], hy_conv_b[l].reshape(1, -1), khat, knyq,
                            hy_skip[l].reshape(1, -1), hy_out_norm[l].reshape(1, -1), tabs, L=L)

            w_gate = jnp.stack([jnp.pad(gla_w_a2_f[l], ((0, LANES - GATE_RANK), (0, 0))),
                                jnp.pad(gla_w_a2_b[l], ((GATE_RANK, LANES - 2 * GATE_RANK), (0, 0)))])
            w_gate = _three_pass_rows(w_gate)
            b_gate = jnp.stack([gla_b_a_f[l], gla_b_a_b[l]]).reshape(2, 1, GLA_KW)
            y_gla = _gla(p, p_lr, w_gate, b_gate, gla_out_norm[l].reshape(1, -1), L=L, col0=gla_col0)

            xb = _out_proj(xb, y_hy, y_gla, w_out_bf)
            nxt = [] if last else [ffn1_w_gate[l + 1], ffn1_w_up[l + 1], ffn1_w_down[l + 1]]
            xb, w1 = _ffn(xb, ffn2_norm[l], w2_gate, w2_up, w2_down, final_norm, final_norm=last, to_cast=nxt)
        outs.append(xb)
    return jnp.stack(outs)
```

```python
import functools
import math

import numpy as np
import jax
import jax.numpy as jnp
from jax import lax
from jax.experimental import pallas as pl
from jax.experimental.pallas import tpu as pltpu

F32 = jnp.float32
BF16 = jnp.bfloat16
HIGHEST = lax.Precision.HIGHEST

EPS = 1e-6
HY_WIDTH = 1024
HY_GROUPS = 8
FILTER_EMB = 33
FILTER_HIDDEN = 64
SHORT_DECAY_PCT = 0.3
LONG_DECAY_PCT = 1.5
DECAY_TARGET = 1e-2
GLA_HEADS = 4
GLA_DK = 128
GLA_DV = 256
GLA_KW = GLA_HEADS * GLA_DK
GLA_VW = GLA_HEADS * GLA_DV
GATE_RANK = 16
GATE_TEMP = 16.0
CHUNK = 64

LANES = 128
SUBLANES = 8
VMEM_LIMIT = 60 * 1024 * 1024
FFN_SLAB = 128
RB = 128
KB = RB // 2
PITCH = RB + SUBLANES


def _cparams(sem):
    return pltpu.CompilerParams(dimension_semantics=sem, vmem_limit_bytes=VMEM_LIMIT)


def _single(block_shape, index_map):
    return pl.BlockSpec(block_shape, index_map, pipeline_mode=pl.Buffered(1))


def _split_hi_lo(x):
    hi = x.astype(BF16)
    return hi, (x - hi.astype(F32)).astype(BF16)


def _three_pass_rows(w):
    w_hi, w_lo = _split_hi_lo(w)
    return jnp.concatenate([w_hi, w_lo, w_hi], axis=-2)


def _dot_nt(a, b):
    return lax.dot_general(a, b, (((1,), (1,)), ((), ())), preferred_element_type=F32)


def _dot_tn(a, b):
    return lax.dot_general(a, b, (((0,), (0,)), ((), ())), preferred_element_type=F32)


def _ffn_kernel(x_ref, g_ref, wg_ref, wu_ref, wd_ref, fg_ref, *rest, final_norm, n_cast):
    cast_src, o_ref, cast_dst, h_scr = rest[:n_cast], rest[n_cast], rest[n_cast + 1:-1], rest[-1]
    j = pl.program_id(1)

    slab = min(FFN_SLAB, x_ref.shape[0])
    nslab = x_ref.shape[0] // slab

    @pl.when(j == 0)
    def _():
        def norm_rows(t, carry):
            rows = pl.ds(pl.multiple_of(t * slab, slab), slab)
            x = x_ref[rows, :]
            r = lax.rsqrt(jnp.mean(x * x, axis=-1, keepdims=True) + EPS)
            h_scr[rows, :] = (x * r * g_ref[...]).astype(BF16)
            o_ref[rows, :] = jnp.zeros((slab, x_ref.shape[1]), F32)
            return carry
        lax.fori_loop(0, nslab, norm_rows, 0)

    h = h_scr[...]
    gate = jnp.dot(h, wg_ref[...], preferred_element_type=F32)
    up = jnp.dot(h, wu_ref[...], preferred_element_type=F32)
    a = (jax.nn.silu(gate) * up).astype(BF16)
    o_ref[...] += jnp.dot(a, wd_ref[...], preferred_element_type=F32)

    for src, dst in zip(cast_src, cast_dst):
        dst[...] = src[...].astype(BF16)

    @pl.when(j == pl.num_programs(1) - 1)
    def _():
        def finish_rows(t, carry):
            rows = pl.ds(pl.multiple_of(t * slab, slab), slab)
            y = x_ref[rows, :] + 0.5 * o_ref[rows, :]
            if final_norm:
                r = lax.rsqrt(jnp.mean(y * y, axis=-1, keepdims=True) + EPS)
                y = y * r * fg_ref[...]
            o_ref[rows, :] = y
            return carry
        lax.fori_loop(0, nslab, finish_rows, 0)


def _cast_blocking(shape, ni, nj):
    rows, cols = shape
    assert rows % ni == 0 and (rows // ni) % SUBLANES == 0 and cols % LANES == 0
    ncol = max(d for d in range(1, nj + 1) if (cols // LANES) % d == 0)
    return (rows // ni, cols // ncol), (lambda i, j: (i, jnp.minimum(j, ncol - 1)))


def _ffn(x, norm_g, w_gate, w_up, w_down, final_g, *, final_norm, to_cast=(), tm=1024, tf=512):
    L, D = x.shape
    DF = w_gate.shape[1]
    tm = min(tm, L)
    tf = min(tf, DF)
    ni, nj = L // tm, DF // tf
    blockings = [_cast_blocking(w.shape, ni, nj) for w in to_cast]
    cast_specs = [pl.BlockSpec(blk, imap) for blk, imap in blockings]
    outs = pl.pallas_call(
        functools.partial(_ffn_kernel, final_norm=final_norm, n_cast=len(to_cast)),
        grid=(ni, nj),
        in_specs=[
            pl.BlockSpec((tm, D), lambda i, j: (i, 0)),
            pl.BlockSpec((1, D), lambda i, j: (0, 0)),
            pl.BlockSpec((D, tf), lambda i, j: (0, j)),
            pl.BlockSpec((D, tf), lambda i, j: (0, j)),
            pl.BlockSpec((tf, D), lambda i, j: (j, 0)),
            pl.BlockSpec((1, D), lambda i, j: (0, 0)),
        ] + cast_specs,
        out_specs=[pl.BlockSpec((tm, D), lambda i, j: (i, 0))] + cast_specs,
        out_shape=[jax.ShapeDtypeStruct((L, D), F32)] + [jax.ShapeDtypeStruct(w.shape, BF16) for w in to_cast],
        scratch_shapes=[pltpu.VMEM((tm, D), BF16)],
        compiler_params=_cparams(("parallel", "arbitrary")),
        name="ffn",
    )(x, norm_g.reshape(1, D), w_gate, w_up, w_down, final_g.reshape(1, D), *to_cast)
    return outs[0], outs[1:]


def _in_proj_kernel(x_ref, g_ref, wt_ref, wtail_ref, o_ref, otail_ref, h_scr, *, n_tail):
    j = pl.program_id(1)

    @pl.when(j == 0)
    def _():
        x = x_ref[...]
        r = lax.rsqrt(jnp.mean(x * x, axis=-1, keepdims=True) + EPS)
        h = (x * r * g_ref[...]).astype(BF16)
        h_scr[...] = h
        tail = _dot_nt(h, wtail_ref[...])
        col = lax.broadcasted_iota(jnp.int32, tail.shape, 1)
        otail_ref[...] = jnp.where(col < n_tail, tail, 0.0)

    o_ref[...] = _dot_nt(h_scr[...], wt_ref[...])


def _in_proj(x, norm_g, w_t, *, tm=1024, tn=2048):
    L, D = x.shape
    n_cols = w_t.shape[0]
    nj = n_cols // tn
    n_tail = n_cols - nj * tn
    assert 0 < n_tail <= LANES and (nj * tn) % LANES == 0
    tm = min(tm, L)
    return pl.pallas_call(
        functools.partial(_in_proj_kernel, n_tail=n_tail),
        grid=(L // tm, nj),
        in_specs=[
            pl.BlockSpec((tm, D), lambda i, j: (i, 0)),
            pl.BlockSpec((1, D), lambda i, j: (0, 0)),
            pl.BlockSpec((tn, D), lambda i, j: (j, 0)),
            pl.BlockSpec((LANES, D), lambda i, j: (nj * tn // LANES, 0)),
        ],
        out_specs=[pl.BlockSpec((tm, tn), lambda i, j: (i, j)),
                   pl.BlockSpec((tm, LANES), lambda i, j: (i, 0))],
        out_shape=[jax.ShapeDtypeStruct((L, nj * tn), F32),
                   jax.ShapeDtypeStruct((L, LANES), F32)],
        scratch_shapes=[pltpu.VMEM((tm, D), BF16)],
        compiler_params=_cparams(("parallel", "arbitrary")),
        name="in_proj",
    )(x, norm_g.reshape(1, D), w_t, w_t)


def _out_proj_kernel(x_ref, yh_ref, yg_ref, wh_ref, wg_ref, o_ref):
    o_ref[...] = (x_ref[...]
                  + jnp.dot(yh_ref[...], wh_ref[...], preferred_element_type=F32)
                  + jnp.dot(yg_ref[...], wg_ref[...], preferred_element_type=F32))


def _out_proj(x, y_hy, y_gla, w_out, *, tm=512):
    L, D = x.shape
    WH = y_hy.shape[1]
    WG = y_gla.shape[1]
    tm = min(tm, L)
    assert WH == WG and w_out.shape[0] == WH + WG
    return pl.pallas_call(
        _out_proj_kernel,
        grid=(L // tm,),
        in_specs=[
            pl.BlockSpec((tm, D), lambda i: (i, 0)),
            pl.BlockSpec((tm, WH), lambda i: (i, 0)),
            pl.BlockSpec((tm, WG), lambda i: (i, 0)),
            _single((WH, D), lambda i: (0, 0)),
            _single((WG, D), lambda i: (1, 0)),
        ],
        out_specs=pl.BlockSpec((tm, D), lambda i: (i, 0)),
        out_shape=jax.ShapeDtypeStruct((L, D), F32),
        compiler_params=_cparams(("parallel",)),
        name="out_proj",
    )(x, y_hy, y_gla, w_out, w_out)


def _filt_mlp_kernel(z_ref, w1_ref, b1_ref, f1_ref, w2_ref, b2_ref, f2_ref, w3_ref, b3_ref, f3_ref, *rest, n_cast):
    cast_src, o_ref, cast_dst = rest[:n_cast], rest[n_cast], rest[n_cast + 1:]
    dot = functools.partial(jnp.dot, precision=HIGHEST, preferred_element_type=F32)
    hid = jnp.sin(f1_ref[...] * (dot(z_ref[...], w1_ref[...]) + b1_ref[...]))
    hid = jnp.sin(f2_ref[...] * (dot(hid, w2_ref[...]) + b2_ref[...]))
    hid = jnp.sin(f3_ref[...] * (dot(hid, w3_ref[...]) + b3_ref[...]))
    hi, lo = _split_hi_lo(hid)
    o_ref[...] = jnp.concatenate([hi, hi, lo], axis=1)
    for src, dst in zip(cast_src, cast_dst):
        dst[...] = src[...].astype(BF16)


def _cast_row_blocks(shape, n):
    rows, cols = shape
    tile = 2 * SUBLANES
    block_rows = pl.cdiv(pl.cdiv(rows, n), tile) * tile
    nblocks = pl.cdiv(rows, block_rows)
    return (block_rows, cols), (lambda i: (jnp.minimum(i, nblocks - 1), 0))


def _filt_mlp(feat2, w1, b1, f1, w2, b2, f2, w3, b3, f3, *, to_cast=(), tm=512):
    L, FP = feat2.shape
    H2 = w2.shape[0]
    tm = min(tm, L)
    steps = L // tm
    full = lambda shp: pl.BlockSpec(shp, lambda i: (0, 0))
    cast_specs = [pl.BlockSpec(*_cast_row_blocks(w.shape, steps)) for w in to_cast]
    outs = pl.pallas_call(
        functools.partial(_filt_mlp_kernel, n_cast=len(to_cast)),
        grid=(steps,),
        in_specs=[pl.BlockSpec((tm, FP), lambda i: (i, 0)),
                  full((FP, H2)), full((1, H2)), full((1, H2)),
                  full((H2, H2)), full((1, H2)), full((1, H2)),
                  full((H2, H2)), full((1, H2)), full((1, H2))] + cast_specs,
        out_specs=[pl.BlockSpec((tm, 3 * H2), lambda i: (i, 0))] + cast_specs,
        out_shape=[jax.ShapeDtypeStruct((L, 3 * H2), BF16)] + [jax.ShapeDtypeStruct(w.shape, BF16) for w in to_cast],
        compiler_params=_cparams(("arbitrary",)),
        name="filt_mlp",
    )(feat2, w1, b1, f1, w2, b2, f2, w3, b3, f3, *to_cast)
    return outs[0], outs[1:]


def _dft_tables(L):
    N = 2 * L
    RA = N // RB
    NH = RA // 2
    two_pi = 2.0 * np.pi
    k1 = np.arange(RA, dtype=np.int64)

    def step1_table(NR):
        g = np.zeros((RB // 2, 2 * RA, 2 * NR), np.float64)
        for s in range(2):
            n = RB * np.arange(NR, dtype=np.int64)[None, :] + (2 * np.arange(RB // 2, dtype=np.int64) + s)[:, None]
            ang = two_pi * ((k1[None, :, None] * n[:, None, :]) % N) / N
            g[:, :RA, s * NR:(s + 1) * NR] = np.cos(ang)
            g[:, RA:, s * NR:(s + 1) * NR] = -np.sin(ang)
        return g

    phi = two_pi * ((np.arange(KB)[:, None] * np.arange(RB)[None, :]) % RB) / RB
    f3 = np.block([[np.cos(phi), np.sin(phi)], [-np.sin(phi), np.cos(phi)]])
    i1 = np.block([[np.cos(phi.T), -np.sin(phi.T)], [np.sin(phi.T), np.cos(phi.T)]])
    cast = lambda a: jnp.asarray(a.astype(np.float32)).astype(BF16)
    return dict(g_half=cast(step1_table(NH)), g_full=cast(step1_table(RA)), f3=cast(f3), i1=cast(i1))


def _fwd_step1(u_ref, g_ref, ab_ref, *, RA, NR, CB, G=2):
    def body(t, carry):
        ms = [t * G + j for j in range(G)]
        rhs = []
        for m in ms:
            ua = u_ref[pl.ds(2 * m, NR, stride=PITCH), :]
            ub = u_ref[pl.ds(2 * m + 1, NR, stride=PITCH), :]
            zz = jnp.zeros_like(ua)
            rhs.append(jnp.concatenate([jnp.concatenate([ua, zz], axis=1),
                                        jnp.concatenate([zz, ub], axis=1)], axis=0).astype(BF16))
        res = [jnp.dot(g_ref[m], r, preferred_element_type=F32) for m, r in zip(ms, rhs)]
        for m, r in zip(ms, res):
            ab_ref[pl.ds(2 * m, RA, stride=PITCH), :] = r[:RA, :CB]
            ab_ref[pl.ds(RA * PITCH + 2 * m, RA, stride=PITCH), :] = r[RA:, :CB]
            ab_ref[pl.ds(2 * m + 1, RA, stride=PITCH), :] = r[:RA, CB:]
            ab_ref[pl.ds(RA * PITCH + 2 * m + 1, RA, stride=PITCH), :] = r[RA:, CB:]
        return carry
    lax.fori_loop(0, RB // 2 // G, body, 0)


def _fwd_step3_pair(ab_ref, f3, kp, *, RA):
    cols = []
    for s in range(2):
        k1 = 2 * kp + s
        ar = ab_ref[pl.ds(pl.multiple_of(k1 * PITCH, SUBLANES), RB), :]
        ai = ab_ref[pl.ds(pl.multiple_of((RA + k1) * PITCH, SUBLANES), RB), :]
        cols.append(jnp.concatenate([ar, ai], axis=0))
    rhs = jnp.concatenate(cols, axis=1).astype(BF16)
    return jnp.dot(f3, rhs, preferred_element_type=F32)


def _for_row_blocks(nblocks, body, carry, unroll=2):
    assert nblocks >= 2
    carry = body(0, carry, True, False)
    carry = lax.fori_loop(1, nblocks - 1, lambda n1, c: body(n1, c, False, False), carry,
                          unroll=max(1, min(unroll, nblocks - 2)))
    return body(nblocks - 1, carry, False, True)


def _alt_sign(shape):
    rows = lax.broadcasted_iota(jnp.int32, shape, 0)
    return jnp.where(rows % 2 == 0, 1.0, -1.0).astype(F32)


def _filt_fft_kernel(hid_ref, w4_ref, dl_ref, g_ref, f3_ref, kh_ref, kn_ref, u_scr, ab_scr, *, L, CB, G):
    N = 2 * L
    RA = N // RB
    NH = RA // 2
    f3 = f3_ref[...]
    inv_lm1 = 1.0 / (L - 1)

    row = lax.broadcasted_iota(jnp.int32, (RB, CB), 0)
    arg_base = -(row.astype(F32) * inv_lm1) * dl_ref[...]
    arg_step = -(RB * inv_lm1) * dl_ref[...]
    arg_end = -(L * inv_lm1) * dl_ref[...]

    def fill(n1, carry, first, last):
        s_abs, s_alt = carry
        r0 = pl.multiple_of(n1 * RB, RB)
        h2 = jnp.dot(hid_ref[pl.ds(r0, RB), :], w4_ref[...], preferred_element_type=F32)
        arg = arg_base + jnp.asarray(n1, F32) * arg_step
        hf = h2[:, :CB] * jnp.exp(arg)
        hb = h2[:, CB:] * jnp.exp(arg_end - arg)
        if first:
            hb = jnp.where(row == 0, 0.0, hb)
        u_scr[pl.ds(pl.multiple_of(n1 * PITCH, SUBLANES), RB), :] = hf
        u_scr[pl.ds(pl.multiple_of((NH + n1) * PITCH, SUBLANES), RB), :] = hb
        return s_abs + (jnp.abs(hf) + jnp.abs(hb)), s_alt + (hf + hb)
    zero = jnp.zeros((RB, CB), F32)
    s_abs, s_alt = _for_row_blocks(NH, fill, (zero, zero), unroll=8)
    inv_l1 = 1.0 / jnp.sum(s_abs, axis=0, keepdims=True)
    kn = jnp.sum(s_alt * _alt_sign((RB, CB)), axis=0, keepdims=True) * inv_l1 * (1.0 / N)
    kn_ref[...] = jnp.broadcast_to(kn, kn_ref.shape)

    _fwd_step1(u_scr, g_ref, ab_scr, RA=RA, NR=RA, CB=CB, G=G)

    scale = inv_l1 * (2.0 / N)
    scale2 = jnp.concatenate([scale, scale], axis=1)
    row2 = lax.broadcasted_iota(jnp.int32, (KB, 2 * CB), 0)
    lane2 = lax.broadcasted_iota(jnp.int32, (KB, 2 * CB), 1)
    dc_pos = jnp.logical_and(row2 == 0, lane2 < CB)

    def emit(t, carry):
        kps = [t * G + j for j in range(G)]
        xs = [_fwd_step3_pair(ab_scr, f3, kp, RA=RA) for kp in kps]
        for kp, x in zip(kps, xs):
            kr = x[:KB] * scale2
            ki = x[KB:] * scale2
            dc = jnp.logical_and(dc_pos, kp == 0)
            kr = jnp.where(dc, 0.5 * kr, kr)
            ki = jnp.where(dc, 0.5 * ki, ki)
            kh_ref[pl.ds(pl.multiple_of(kp * RB, RB), RB), :] = jnp.concatenate([kr, ki], axis=0).astype(kh_ref.dtype)
        return carry
    lax.fori_loop(0, RA // 2 // G, emit, 0)


def _filt_fft(hid2, w4bd, deltas, g_tab, f3_tab, *, L, CB=LANES, G=16):
    C = deltas.shape[1]
    H2 = hid2.shape[1]
    N = 2 * L
    RA = N // RB
    nblk = C // CB
    G = min(G, RA // 2)
    return pl.pallas_call(
        functools.partial(_filt_fft_kernel, L=L, CB=CB, G=G),
        grid=(nblk,),
        in_specs=[
            _single((L, H2), lambda c: (0, 0)),
            pl.BlockSpec((H2, 2 * CB), lambda c: (0, c)),
            pl.BlockSpec((1, CB), lambda c: (0, c)),
            _single(g_tab.shape, lambda c: (0, 0, 0)),
            _single(f3_tab.shape, lambda c: (0, 0)),
        ],
        out_specs=[pl.BlockSpec((RA // 2 * RB, 2 * CB), lambda c: (0, c)),
                   pl.BlockSpec((SUBLANES, CB), lambda c: (0, c))],
        out_shape=[jax.ShapeDtypeStruct((RA // 2 * RB, 2 * C), BF16),
                   jax.ShapeDtypeStruct((SUBLANES, C), F32)],
        scratch_shapes=[pltpu.VMEM((RA * PITCH, CB), F32),
                        pltpu.VMEM((2 * RA * PITCH, CB), F32)],
        compiler_params=_cparams(("arbitrary",)),
        name="filt_fft",
    )(hid2, w4bd, deltas, g_tab, f3_tab)


def _short_conv_rows(p_ref, w_ref, b_ref, n1, *, first, last):
    r0 = pl.multiple_of(n1 * RB, RB)
    cur = p_ref[pl.ds(r0, RB), :]
    rows = lax.broadcasted_iota(jnp.int32, cur.shape, 0)
    if first:
        up = jnp.where(rows == 0, 0.0, pltpu.roll(cur, 1, axis=0))
    else:
        up = p_ref[pl.ds(r0 - 1, RB), :]
    if last:
        dn = jnp.where(rows == RB - 1, 0.0, pltpu.roll(cur, RB - 1, axis=0))
    else:
        dn = p_ref[pl.ds(r0 + 1, RB), :]
    w = w_ref[...]
    return b_ref[...] + up * w[0:1] + cur * w[1:2] + dn * w[2:3]


def _hy_conv_kernel(p_hbm, w0_ref, w1_ref, wv_ref, b0_ref, b1_ref, bv_ref,
                    kh_ref, kn_ref, skip_ref, og_ref, g_ref, f3_ref, i1_ref,
                    o_ref, u_scr, ab_scr, z_scr, p_scr, p_sem, *, L, CB, G):
    N = 2 * L
    RA = N // RB
    NH = RA // 2
    f3 = f3_ref[...]
    i1 = i1_ref[...]
    sign = _alt_sign((RB, CB))
    c = pl.program_id(0)
    nblk = pl.num_programs(0)
    p0_ref, p1_ref, pv_ref = p_scr.at[0], p_scr.at[1], p_scr.at[2]

    def fetch(group, blk):
        col = pl.multiple_of((group * nblk + blk) * CB, CB)
        return pltpu.make_async_copy(p_hbm.at[:, pl.ds(col, CB)], p_scr.at[group], p_sem.at[group])

    @pl.when(c == 0)
    def _():
        fetch(1, c).start()
        fetch(2, c).start()

    fetch(0, c).start()
    fetch(1, c).wait()
    fetch(2, c).wait()

    def fill(n1, s_alt, first, last):
        z = (_short_conv_rows(pv_ref, wv_ref, bv_ref, n1, first=first, last=last)
             * _short_conv_rows(p1_ref, w1_ref, b1_ref, n1, first=first, last=last))
        u_scr[pl.ds(pl.multiple_of(n1 * PITCH, SUBLANES), RB), :] = z
        z_scr[pl.ds(pl.multiple_of(n1 * RB, RB), RB), :] = z
        return s_alt + z
    s_alt = _for_row_blocks(NH, fill, jnp.zeros((RB, CB), F32))

    @pl.when(c + 1 < nblk)
    def _():
        fetch(1, c + 1).start()
        fetch(2, c + 1).start()

    z_nyq = jnp.sum(s_alt * sign, axis=0, keepdims=True)
    y_nyq = z_nyq * kn_ref[0:1, :]

    _fwd_step1(u_scr, g_ref, ab_scr, RA=RA, NR=NH, CB=CB, G=G)

    GM = max(G // 2, 1)

    def mid(t, carry):
        kps = [t * GM + j for j in range(GM)]
        xs = [_fwd_step3_pair(ab_scr, f3, kp, RA=RA) for kp in kps]
        khs = [kh_ref[pl.ds(pl.multiple_of(kp * RB, RB), RB), :].astype(F32) for kp in kps]
        bs = []
        for x, kh in zip(xs, khs):
            xr, xi, kr, ki = x[:KB], x[KB:], kh[:KB], kh[KB:]
            y = jnp.concatenate([xr * kr - xi * ki, xr * ki + xi * kr], axis=0).astype(BF16)
            bs.append(jnp.dot(i1, y, preferred_element_type=F32))
        for kp, b in zip(kps, bs):
            for s in range(2):
                k1 = 2 * kp + s
                ab_scr[pl.ds(pl.multiple_of(k1 * PITCH, SUBLANES), RB), :] = b[:RB, s * CB:(s + 1) * CB]
                ab_scr[pl.ds(pl.multiple_of((RA + k1) * PITCH, SUBLANES), RB), :] = b[RB:, s * CB:(s + 1) * CB]
        return carry
    lax.fori_loop(0, RA // 2 // GM, mid, 0)

    def last(t, carry):
        ms = [t * G + j for j in range(G)]
        rhs = [jnp.concatenate(
            [jnp.concatenate([ab_scr[pl.ds(2 * m + s, RA, stride=PITCH), :],
                              ab_scr[pl.ds(RA * PITCH + 2 * m + s, RA, stride=PITCH), :]], axis=0)
             for s in range(2)], axis=1).astype(BF16) for m in ms]
        ys = [_dot_tn(g_ref[m], r) for m, r in zip(ms, rhs)]
        for m, y in zip(ms, ys):
            u_scr[pl.ds(2 * m, NH, stride=PITCH), :] = y[:NH, :CB]
            u_scr[pl.ds(2 * m + 1, NH, stride=PITCH), :] = y[NH:, CB:]
        return carry
    lax.fori_loop(0, RB // 2 // G, last, 0)

    nyq_rows = sign * y_nyq
    fetch(0, c).wait()

    def finish(n1, carry, first, last):
        z = z_scr[pl.ds(pl.multiple_of(n1 * RB, RB), RB), :]
        x0 = _short_conv_rows(p0_ref, w0_ref, b0_ref, n1, first=first, last=last)
        conv = u_scr[pl.ds(pl.multiple_of(n1 * PITCH, SUBLANES), RB), :] + nyq_rows
        y = (conv + z * skip_ref[...]) * x0
        y = y * lax.rsqrt(jnp.mean(y * y, axis=-1, keepdims=True) + EPS) * og_ref[...]
        o_ref[pl.ds(pl.multiple_of(n1 * RB, RB), RB), :] = y.astype(o_ref.dtype)
        return carry
    _for_row_blocks(NH, finish, 0, unroll=4)


def _hy_conv(p, conv_w, conv_b, khat, knyq, skip, out_g, tabs, *, L, CB=LANES, G=16):
    C = skip.shape[1]
    assert CB == C // HY_GROUPS, "one channel block must be exactly one norm group"
    N = 2 * L
    RA = N // RB
    NH = RA // 2
    nblk = C // CB
    G = min(G, RA // 2)
    g_tab, f3_tab, i1_tab = tabs["g_half"], tabs["f3"], tabs["i1"]
    col = lambda off: (lambda c: (0, off * nblk + c))
    wspec = lambda off: pl.BlockSpec((3, CB), col(off))
    bspec = lambda off: pl.BlockSpec((1, CB), col(off))
    return pl.pallas_call(
        functools.partial(_hy_conv_kernel, L=L, CB=CB, G=G),
        grid=(nblk,),
        in_specs=[pl.BlockSpec(memory_space=pl.ANY),
                  wspec(0), wspec(1), wspec(2), bspec(0), bspec(1), bspec(2),
                  pl.BlockSpec((RA // 2 * RB, 2 * CB), col(0)),
                  pl.BlockSpec((SUBLANES, CB), col(0)),
                  bspec(0), bspec(0),
                  _single(g_tab.shape, lambda c: (0, 0, 0)),
                  _single(f3_tab.shape, lambda c: (0, 0)),
                  _single(i1_tab.shape, lambda c: (0, 0))],
        out_specs=pl.BlockSpec((L, CB), col(0)),
        out_shape=jax.ShapeDtypeStruct((L, C), BF16),
        scratch_shapes=[pltpu.VMEM((NH * PITCH, CB), F32),
                        pltpu.VMEM((2 * RA * PITCH, CB), F32),
                        pltpu.VMEM((L, CB), F32),
                        pltpu.VMEM((3, L, CB), F32),
                        pltpu.SemaphoreType.DMA((3,))],
        compiler_params=_cparams(("arbitrary",)),
        name="hy_conv",
    )(p, conv_w, conv_w, conv_w, conv_b, conv_b, conv_b, khat, knyq, skip, out_g,
      g_tab, f3_tab, i1_tab)


def _log_sigmoid(x):
    return jnp.minimum(x, 0.0) - jnp.log(1.0 + jnp.exp(-jnp.abs(x)))


def _gla_scan_blocks(dirs, *, TB):
    nch = TB // CHUNK
    rows = lambda c: slice(c * CHUNK, (c + 1) * CHUNK)
    units = [(d, c) for c in range(nch) for d in dirs]
    rr = lax.broadcasted_iota(jnp.int32, (CHUNK, CHUNK), 0)
    cc = lax.broadcasted_iota(jnp.int32, (CHUNK, CHUNK), 1)
    scale = GLA_DK ** -0.5

    for d in dirs:
        lr_hi, lr_lo = _split_hi_lo(d["lr"][...])
        gate_in = jnp.dot(jnp.concatenate([lr_hi, lr_hi, lr_lo], axis=1), d["wg"],
                          preferred_element_type=F32) + d["bg"]
        g = _log_sigmoid(gate_in) * (1.0 / GATE_TEMP)
        g_hi = g.astype(BF16)
        g_mid, g_lo = _split_hi_lo(g - g_hi.astype(F32))
        d["parts"][...] = jnp.concatenate([g_hi, g_mid, g_lo], axis=1)
        d["csum"] = jnp.where((cc >= rr) if d["reverse"] else (cc <= rr), 1.0, 0.0).astype(BF16)
        d["mask"] = (cc > rr) if d["reverse"] else (cc <= rr)
        d["edge"] = 0 if d["reverse"] else CHUNK - 1

    for d, c in units:
        b3 = jnp.dot(d["csum"], d["parts"][rows(c), :], preferred_element_type=F32)
        d["b"][rows(c), :] = b3[:, :GLA_DK] + b3[:, GLA_DK:2 * GLA_DK] + b3[:, 2 * GLA_DK:]

    decay = {}
    for d, c in units:
        b = d["b"][rows(c), :]
        b_edge = b[d["edge"]:d["edge"] + 1]
        k = d["k"][rows(c), :]
        d["qks"][0, rows(c), :] = (d["q"][rows(c), :] * scale * jnp.exp(b)).astype(BF16)
        d["qks"][1, rows(c), :] = (k * jnp.exp(-b)).astype(BF16)
        d["qks"][2, rows(c), :] = (k * jnp.exp(b_edge - b)).astype(BF16)
        d["vb"][rows(c), :] = d["v"][rows(c), :].astype(BF16)
        decay[(id(d), c)] = jnp.exp(b_edge)

    for d, c in units:
        att = _dot_nt(d["qks"][0, rows(c), :], d["qks"][1, rows(c), :])
        d["att"][rows(c), :] = jnp.where(d["mask"], att, 0.0).astype(BF16)

    for d, c in units:
        d["o"][rows(c), :] = jnp.dot(d["att"][rows(c), :], d["vb"][rows(c), :], preferred_element_type=F32)

    for d, c in units:
        d["kv"][c * GLA_DV:(c + 1) * GLA_DV, :] = _dot_tn(d["vb"][rows(c), :], d["qks"][2, rows(c), :])

    state = {id(d): d["s"][...] for d in dirs}
    for step in range(nch):
        for d in dirs:
            c = nch - 1 - step if d["reverse"] else step
            s_t = state[id(d)]
            d["o"][rows(c), :] += _dot_nt(d["qks"][0, rows(c), :], s_t.astype(BF16))
            state[id(d)] = s_t * decay[(id(d), c)] + d["kv"][c * GLA_DV:(c + 1) * GLA_DV, :]
    for d in dirs:
        d["s"][...] = state[id(d)]


def _gla_kernel(qf_ref, kf_ref, vf_ref, lf_ref, rf_ref, qb_ref, kb_ref, vb_ref, lb_ref, rb_ref,
                wg_ref, bg_ref, og_ref, o_ref,
                s_scr, parts_scr, b_scr, qks_scr, vb_scr, att_scr, kv_scr, ob_scr, half_scr, *, TB):
    s = pl.program_id(1)
    nb = pl.num_programs(1)

    @pl.when(s == 0)
    def _():
        s_scr[...] = jnp.zeros_like(s_scr)

    def direction(i, q, k, v, lr):
        return dict(q=q, k=k, v=v, lr=lr, wg=wg_ref[i], bg=bg_ref[i], reverse=bool(i), s=s_scr.at[i],
                    parts=parts_scr.at[i], b=b_scr.at[i], qks=qks_scr.at[i], vb=vb_scr.at[i],
                    att=att_scr.at[i], kv=kv_scr.at[i], o=ob_scr.at[i])
    _gla_scan_blocks([direction(0, qf_ref, kf_ref, vf_ref, lf_ref),
                      direction(1, qb_ref, kb_ref, vb_ref, lb_ref)], TB=TB)

    first_half = s < nb // 2
    for d, blk, r_ref in ((0, s, rf_ref), (1, nb - 1 - s, rb_ref)):
        rows = pl.ds(pl.multiple_of(blk * TB, TB), TB)

        @pl.when(first_half)
        def _():
            half_scr[rows, :] = ob_scr[d]

        @pl.when(jnp.logical_not(first_half))
        def _():
            tot = half_scr[rows, :] + ob_scr[d]
            tot = tot * lax.rsqrt(jnp.mean(tot * tot, axis=-1, keepdims=True) + EPS) * og_ref[...]
            o_ref[rows, :] = (tot * jax.nn.silu(r_ref[...])).astype(o_ref.dtype)


def _gla(p, p_lr, w_gate, b_gate, out_g, *, L, col0, TB=1024):
    TB = min(TB, L)
    nb = L // TB
    assert nb % 2 == 0, "both scan directions must meet between two blocks"
    nch = TB // CHUNK
    qb = col0 // GLA_DK
    kb = qb + GLA_HEADS
    vb = (col0 + 2 * GLA_KW) // GLA_DV
    rb = vb + GLA_HEADS
    fwd = lambda s: s
    bwd = lambda s: nb - 1 - s

    def operands(blk):
        return [pl.BlockSpec((TB, GLA_DK), lambda h, s: (blk(s), qb + h)),
                pl.BlockSpec((TB, GLA_DK), lambda h, s: (blk(s), kb + h)),
                pl.BlockSpec((TB, GLA_DV), lambda h, s: (blk(s), vb + h)),
                pl.BlockSpec((TB, LANES), lambda h, s: (blk(s), 0)),
                pl.BlockSpec((TB, GLA_DV), lambda h, s: (blk(s), rb + h))]

    return pl.pallas_call(
        functools.partial(_gla_kernel, TB=TB),
        grid=(GLA_HEADS, nb),
        in_specs=operands(fwd) + operands(bwd) + [
            pl.BlockSpec((2, 3 * LANES, GLA_DK), lambda h, s: (0, 0, h)),
            pl.BlockSpec((2, 1, GLA_DK), lambda h, s: (0, 0, h)),
            pl.BlockSpec((1, GLA_DV), lambda h, s: (0, h)),
        ],
        out_specs=pl.BlockSpec((L, GLA_DV), lambda h, s: (0, h)),
        out_shape=jax.ShapeDtypeStruct((L, GLA_VW), BF16),
        scratch_shapes=[pltpu.VMEM((2, GLA_DV, GLA_DK), F32),
                        pltpu.VMEM((2, TB, 3 * GLA_DK), BF16),
                        pltpu.VMEM((2, TB, GLA_DK), F32),
                        pltpu.VMEM((2, 3, TB, GLA_DK), BF16),
                        pltpu.VMEM((2, TB, GLA_DV), BF16),
                        pltpu.VMEM((2, TB, CHUNK), BF16),
                        pltpu.VMEM((2, nch * GLA_DV, GLA_DK), F32),
                        pltpu.VMEM((2, TB, GLA_DV), F32),
                        pltpu.VMEM((L, GLA_DV), F32)],
        compiler_params=_cparams(("arbitrary", "arbitrary")),
        name="gla",
    )(*([p, p, p, p_lr, p] * 2), w_gate, b_gate, out_g)


def _filter_features(L):
    t = np.linspace(0.0, 1.0, L)[:, None]
    bands = (FILTER_EMB - 1) // 2
    freqs = np.linspace(1e-4, bands - 1, bands)[None, :]
    ang = (2.0 * np.pi / L) * np.arange(L)[:, None] * freqs
    feat = np.concatenate([t, np.cos(ang), -np.sin(ang)], axis=-1)
    feat_rev = np.roll(feat[::-1], 1, axis=0)
    both = np.zeros((L, LANES), np.float32)
    both[:, :FILTER_EMB] = feat
    both[:, FILTER_EMB:2 * FILTER_EMB] = feat_rev
    return jnp.asarray(both)


def _twice(w, rows_out):
    r, c = w.shape
    zz = jnp.zeros_like(w)
    both = jnp.concatenate([jnp.concatenate([w, zz], axis=1), jnp.concatenate([zz, w], axis=1)], axis=0)
    return jnp.pad(both, ((0, rows_out - 2 * r), (0, 0)))


def _filter_deltas():
    min_decay = math.log(DECAY_TARGET) / LONG_DECAY_PCT
    max_decay = math.log(DECAY_TARGET) / SHORT_DECAY_PCT
    return jnp.abs(jnp.linspace(min_decay, max_decay, HY_WIDTH, dtype=F32)).reshape(1, HY_WIDTH)


def _block_diag_w4(w4):
    H = w4.shape[0]
    nblk = HY_WIDTH // LANES
    wf = w4[:, :HY_WIDTH].reshape(H, nblk, LANES)
    wb = w4[:, HY_WIDTH:].reshape(H, nblk, LANES)
    zz = jnp.zeros_like(wf)
    top = jnp.concatenate([wf, zz], axis=2)
    bot = jnp.concatenate([zz, wb], axis=2)
    return jnp.concatenate([top, bot], axis=0).reshape(2 * H, nblk * 2 * LANES)


def kernel(x, ffn1_norm, ffn1_w_gate, ffn1_w_up, ffn1_w_down, mix_norm, w_in, hy_conv_w, hy_conv_b, flt_w1, flt_b1, flt_f1, flt_w2, flt_b2, flt_f2, flt_w3, flt_b3, flt_f3, flt_w4, hy_skip, hy_out_norm, gla_w_a2_f, gla_b_a_f, gla_w_a2_b, gla_b_a_b, gla_out_norm, w_out, ffn2_norm, ffn2_w_gate, ffn2_w_up, ffn2_w_down, final_norm):
    B, L, D = x.shape
    depth = ffn1_norm.shape[0]
    tabs = _dft_tables(L)
    feat2 = _filter_features(L)
    deltas = _filter_deltas()
    gla_col0 = 3 * HY_WIDTH
    H2 = 2 * FILTER_HIDDEN
    both = lambda a: jnp.concatenate([a, a]).reshape(1, H2)
    filt = []
    for l in range(depth):
        early = [jnp.swapaxes(w_in[l], 0, 1)]
        if l == 0:
            early += [ffn1_w_gate[0], ffn1_w_up[0], ffn1_w_down[0]]
        hid3, cast = _filt_mlp(feat2, _twice(flt_w1[l], LANES), both(flt_b1[l]), both(flt_f1[l]),
                               _twice(flt_w2[l], H2), both(flt_b2[l]), both(flt_f2[l]),
                               _twice(flt_w3[l], H2), both(flt_b3[l]), both(flt_f3[l]), to_cast=early)
        w4cat = _three_pass_rows(_block_diag_w4(flt_w4[l]))
        khat, knyq = _filt_fft(hid3, w4cat, deltas, tabs["g_full"], tabs["f3"], L=L)
        filt.append((khat, knyq, cast))

    outs = []
    for bi in range(B):
        xb = x[bi]
        w1 = filt[0][2][1:]
        for l in range(depth):
            last = l == depth - 1
            khat, knyq, (w_in_t, *_) = filt[l]
            later = [ffn2_w_gate[l], ffn2_w_up[l], ffn2_w_down[l], w_out[l]]
            xb, (w2_gate, w2_up, w2_down, w_out_bf) = _ffn(xb, ffn1_norm[l], *w1, final_norm, final_norm=False,
                                                           to_cast=later)
            p, p_lr = _in_proj(xb, mix_norm[l], w_in_t)
            y_hy = _hy_conv(p, hy_conv_w[l], hy_conv_b[l].reshape(1, -1), khat, knyq,
                            hy_skip[l].reshape(1, -1), hy_out_norm[l].reshape(1, -1), tabs, L=L)

            w_gate = jnp.stack([jnp.pad(gla_w_a2_f[l], ((0, LANES - GATE_RANK), (0, 0))),
                                jnp.pad(gla_w_a2_b[l], ((GATE_RANK, LANES - 2 * GATE_RANK), (0, 0)))])
            w_gate = _three_pass_rows(w_gate)
            b_gate = jnp.stack([gla_b_a_f[l], gla_b_a_b[l]]).reshape(2, 1, GLA_KW)
            y_gla = _gla(p, p_lr, w_gate, b_gate, gla_out_norm[l].reshape(1, -1), L=L, col0=gla_col0)

            xb = _out_proj(xb, y_hy, y_gla, w_out_bf)
            nxt = [] if last else [ffn1_w_gate[l + 1], ffn1_w_up[l + 1], ffn1_w_down[l + 1]]
            xb, w1 = _ffn(xb, ffn2_norm[l], w2_gate, w2_up, w2_down, final_norm, final_norm=last, to_cast=nxt)
        outs.append(xb)
    return jnp.stack(outs)
```

```python
import functools
import math

import numpy as np
import jax
import jax.numpy as jnp
from jax import lax
from jax.experimental import pallas as pl
from jax.experimental.pallas import tpu as pltpu

F32 = jnp.float32
BF16 = jnp.bfloat16
HIGHEST = lax.Precision.HIGHEST

EPS = 1e-6
HY_WIDTH = 1024
HY_GROUPS = 8
FILTER_EMB = 33
FILTER_HIDDEN = 64
SHORT_DECAY_PCT = 0.3
LONG_DECAY_PCT = 1.5
DECAY_TARGET = 1e-2
GLA_HEADS = 4
GLA_DK = 128
GLA_DV = 256
GLA_KW = GLA_HEADS * GLA_DK
GLA_VW = GLA_HEADS * GLA_DV
GATE_RANK = 16
GATE_TEMP = 16.0
CHUNK = 64

LANES = 128
SUBLANES = 8
VMEM_LIMIT = 60 * 1024 * 1024
FFN_SLAB = 128
RB = 128
KB = RB // 2
PITCH = RB + SUBLANES


def _cparams(sem):
    return pltpu.CompilerParams(dimension_semantics=sem, vmem_limit_bytes=VMEM_LIMIT)


def _single(block_shape, index_map):
    return pl.BlockSpec(block_shape, index_map, pipeline_mode=pl.Buffered(1))


def _split_hi_lo(x):
    hi = x.astype(BF16)
    return hi, (x - hi.astype(F32)).astype(BF16)


def _three_pass_rows(w):
    w_hi, w_lo = _split_hi_lo(w)
    return jnp.concatenate([w_hi, w_lo, w_hi], axis=-2)


def _dot_nt(a, b):
    return lax.dot_general(a, b, (((1,), (1,)), ((), ())), preferred_element_type=F32)


def _dot_tn(a, b):
    return lax.dot_general(a, b, (((0,), (0,)), ((), ())), preferred_element_type=F32)


def _ffn_kernel(x_ref, g_ref, wg_ref, wu_ref, wd_ref, fg_ref, *rest, final_norm, n_cast):
    cast_src, o_ref, cast_dst, h_scr = rest[:n_cast], rest[n_cast], rest[n_cast + 1:-1], rest[-1]
    j = pl.program_id(1)

    slab = min(FFN_SLAB, x_ref.shape[0])
    nslab = x_ref.shape[0] // slab

    @pl.when(j == 0)
    def _():
        def norm_rows(t, carry):
            rows = pl.ds(pl.multiple_of(t * slab, slab), slab)
            x = x_ref[rows, :]
            r = lax.rsqrt(jnp.mean(x * x, axis=-1, keepdims=True) + EPS)
            h_scr[rows, :] = (x * r * g_ref[...]).astype(BF16)
            o_ref[rows, :] = jnp.zeros((slab, x_ref.shape[1]), F32)
            return carry
        lax.fori_loop(0, nslab, norm_rows, 0)

    h = h_scr[...]
    gate = jnp.dot(h, wg_ref[...], preferred_element_type=F32)
    up = jnp.dot(h, wu_ref[...], preferred_element_type=F32)
    a = (jax.nn.silu(gate) * up).astype(BF16)
    o_ref[...] += jnp.dot(a, wd_ref[...], preferred_element_type=F32)

    for src, dst in zip(cast_src, cast_dst):
        dst[...] = src[...].astype(BF16)

    @pl.when(j == pl.num_programs(1) - 1)
    def _():
        def finish_rows(t, carry):
            rows = pl.ds(pl.multiple_of(t * slab, slab), slab)
            y = x_ref[rows, :] + 0.5 * o_ref[rows, :]
            if final_norm:
                r = lax.rsqrt(jnp.mean(y * y, axis=-1, keepdims=True) + EPS)
                y = y * r * fg_ref[...]
            o_ref[rows, :] = y
            return carry
        lax.fori_loop(0, nslab, finish_rows, 0)


def _cast_blocking(shape, ni, nj):
    rows, cols = shape
    assert rows % ni == 0 and (rows // ni) % SUBLANES == 0 and cols % LANES == 0
    ncol = max(d for d in range(1, nj + 1) if (cols // LANES) % d == 0)
    return (rows // ni, cols // ncol), (lambda i, j: (i, jnp.minimum(j, ncol - 1)))


def _ffn(x, norm_g, w_gate, w_up, w_down, final_g, *, final_norm, to_cast=(), tm=1024, tf=512):
    L, D = x.shape
    DF = w_gate.shape[1]
    tm = min(tm, L)
    tf = min(tf, DF)
    ni, nj = L // tm, DF // tf
    blockings = [_cast_blocking(w.shape, ni, nj) for w in to_cast]
    cast_specs = [pl.BlockSpec(blk, imap) for blk, imap in blockings]
    outs = pl.pallas_call(
        functools.partial(_ffn_kernel, final_norm=final_norm, n_cast=len(to_cast)),
        grid=(ni, nj),
        in_specs=[
            pl.BlockSpec((tm, D), lambda i, j: (i, 0)),
            pl.BlockSpec((1, D), lambda i, j: (0, 0)),
            pl.BlockSpec((D, tf), lambda i, j: (0, j)),
            pl.BlockSpec((D, tf), lambda i, j: (0, j)),
            pl.BlockSpec((tf, D), lambda i, j: (j, 0)),
            pl.BlockSpec((1, D), lambda i, j: (0, 0)),
        ] + cast_specs,
        out_specs=[pl.BlockSpec((tm, D), lambda i, j: (i, 0))] + cast_specs,
        out_shape=[jax.ShapeDtypeStruct((L, D), F32)] + [jax.ShapeDtypeStruct(w.shape, BF16) for w in to_cast],
        scratch_shapes=[pltpu.VMEM((tm, D), BF16)],
        compiler_params=_cparams(("arbitrary", "arbitrary")),
        name="ffn",
    )(x, norm_g.reshape(1, D), w_gate, w_up, w_down, final_g.reshape(1, D), *to_cast)
    return outs[0], outs[1:]


def _ffn_stream_kernel(x_hbm, g_ref, wg_hbm, wu_hbm, wd_hbm, fg_ref, o_hbm,
                       x_buf, h_buf, acc, wg_buf, wu_buf, wd_buf, w_sem, x_sem, o_sem, *, final_norm, tm, tf, nj):
    i = pl.program_id(0)
    ni = pl.num_programs(0)
    total = ni * nj
    cur = i % 2
    nxt = 1 - cur
    slab = min(FFN_SLAB, tm)
    nslab = tm // slab
    lead = 3
    assert nj >= nslab + lead, "the next tile's slabs are spread over this tile's weight steps"

    def fetch_w(j, slot):
        col = pl.multiple_of(j * tf, tf)
        return (pltpu.make_async_copy(wg_hbm.at[:, pl.ds(col, tf)], wg_buf.at[slot], w_sem.at[0, slot]),
                pltpu.make_async_copy(wu_hbm.at[:, pl.ds(col, tf)], wu_buf.at[slot], w_sem.at[1, slot]),
                pltpu.make_async_copy(wd_hbm.at[pl.ds(col, tf), :], wd_buf.at[slot], w_sem.at[2, slot]))

    def tile_rows(row_tile):
        return pl.ds(pl.multiple_of(row_tile * tm, tm), tm)

    def fetch_x(row_tile, slot):
        return pltpu.make_async_copy(x_hbm.at[tile_rows(row_tile), :], x_buf.at[slot], x_sem.at[slot])

    def send_y(row_tile, slot):
        return pltpu.make_async_copy(x_buf.at[slot], o_hbm.at[tile_rows(row_tile), :], o_sem.at[slot])

    def norm_slab(t, src_slot, dst_slot):
        rows = pl.ds(pl.multiple_of(t * slab, slab), slab)
        x = x_buf[src_slot, rows, :]
        r = lax.rsqrt(jnp.mean(x * x, axis=-1, keepdims=True) + EPS)
        h_buf[dst_slot, rows, :] = (x * r * g_ref[...]).astype(BF16)

    @pl.when(i == 0)
    def _():
        for cp in fetch_w(0, 0):
            cp.start()
        fetch_x(0, 0).start()
        fetch_x(0, 0).wait()

        def first_rows(t, carry):
            norm_slab(t, 0, 0)
            return carry
        lax.fori_loop(0, nslab, first_rows, 0)

    def tile(j, carry):
        step = i * nj + j
        slot = step % 2
        for cp in fetch_w(j, slot):
            cp.wait()

        @pl.when(step + 1 < total)
        def _():
            for cp in fetch_w((j + 1) % nj, 1 - slot):
                cp.start()

        @pl.when(jnp.logical_and(j == 1, i > 0))
        def _():
            send_y(i - 1, nxt).wait()

        @pl.when(jnp.logical_and(j == 1, i + 1 < ni))
        def _():
            fetch_x(i + 1, nxt).start()

        @pl.when(jnp.logical_and(j == lead - 1, i + 1 < ni))
        def _():
            fetch_x(i + 1, nxt).wait()

        h = h_buf[cur]
        gate = jnp.dot(h, wg_buf[slot], preferred_element_type=F32)
        up = jnp.dot(h, wu_buf[slot], preferred_element_type=F32)
        a = (jax.nn.silu(gate) * up).astype(BF16)
        acc[...] = jnp.where(j == 0, 0.0, acc[...]) + jnp.dot(a, wd_buf[slot], preferred_element_type=F32)
        src = jnp.where(j < lead, cur, nxt)
        norm_slab(jnp.clip(j - lead, 0, nslab - 1), src, nxt)
        return carry
    lax.fori_loop(0, nj, tile, 0)

    def finish_rows(t, carry):
        rows = pl.ds(pl.multiple_of(t * slab, slab), slab)
        y = x_buf[cur, rows, :] + 0.5 * acc[rows, :]
        if final_norm:
            r = lax.rsqrt(jnp.mean(y * y, axis=-1, keepdims=True) + EPS)
            y = y * r * fg_ref[...]
        x_buf[cur, rows, :] = y
        return carry
    lax.fori_loop(0, nslab, finish_rows, 0)
    send_y(i, cur).start()

    @pl.when(i == ni - 1)
    def _():
        send_y(i, cur).wait()


def _ffn_stream(x, norm_g, w_gate, w_up, w_down, final_g, *, final_norm, tm=1024, tf=512):
    L, D = x.shape
    DF = w_gate.shape[1]
    tm = min(tm, L)
    tf = min(tf, DF)
    nj = DF // tf
    hbm = pl.BlockSpec(memory_space=pl.ANY)
    return pl.pallas_call(
        functools.partial(_ffn_stream_kernel, final_norm=final_norm, tm=tm, tf=tf, nj=nj),
        grid=(L // tm,),
        in_specs=[hbm, pl.BlockSpec((1, D), lambda i: (0, 0)), hbm, hbm, hbm, pl.BlockSpec((1, D), lambda i: (0, 0))],
        out_specs=hbm,
        out_shape=jax.ShapeDtypeStruct((L, D), F32),
        scratch_shapes=[pltpu.VMEM((2, tm, D), F32), pltpu.VMEM((2, tm, D), BF16), pltpu.VMEM((tm, D), F32),
                        pltpu.VMEM((2, D, tf), BF16), pltpu.VMEM((2, D, tf), BF16), pltpu.VMEM((2, tf, D), BF16),
                        pltpu.SemaphoreType.DMA((3, 2)), pltpu.SemaphoreType.DMA((2,)), pltpu.SemaphoreType.DMA((2,))],
        compiler_params=_cparams(("arbitrary",)),
        name="ffn_stream",
    )(x, norm_g.reshape(1, D), w_gate, w_up, w_down, final_g.reshape(1, D))


def _in_proj_kernel(x_ref, g_ref, wt_ref, wtail_ref, o_ref, otail_ref, h_scr, *, n_tail):
    j = pl.program_id(1)

    @pl.when(j == 0)
    def _():
        x = x_ref[...]
        r = lax.rsqrt(jnp.mean(x * x, axis=-1, keepdims=True) + EPS)
        h = (x * r * g_ref[...]).astype(BF16)
        h_scr[...] = h
        tail = _dot_nt(h, wtail_ref[...])
        col = lax.broadcasted_iota(jnp.int32, tail.shape, 1)
        otail_ref[...] = jnp.where(col < n_tail, tail, 0.0)

    o_ref[...] = _dot_nt(h_scr[...], wt_ref[...])


def _in_proj(x, norm_g, w_t, *, tm=1024, tn=2048):
    L, D = x.shape
    n_cols = w_t.shape[0]
    nj = n_cols // tn
    n_tail = n_cols - nj * tn
    assert 0 < n_tail <= LANES and (nj * tn) % LANES == 0
    tm = min(tm, L)
    return pl.pallas_call(
        functools.partial(_in_proj_kernel, n_tail=n_tail),
        grid=(L // tm, nj),
        in_specs=[
            pl.BlockSpec((tm, D), lambda i, j: (i, 0)),
            pl.BlockSpec((1, D), lambda i, j: (0, 0)),
            pl.BlockSpec((tn, D), lambda i, j: (j, 0)),
            pl.BlockSpec((LANES, D), lambda i, j: (nj * tn // LANES, 0)),
        ],
        out_specs=[pl.BlockSpec((tm, tn), lambda i, j: (i, j)),
                   pl.BlockSpec((tm, LANES), lambda i, j: (i, 0))],
        out_shape=[jax.ShapeDtypeStruct((L, nj * tn), F32),
                   jax.ShapeDtypeStruct((L, LANES), F32)],
        scratch_shapes=[pltpu.VMEM((tm, D), BF16)],
        compiler_params=_cparams(("parallel", "arbitrary")),
        name="in_proj",
    )(x, norm_g.reshape(1, D), w_t, w_t)


def _out_proj_kernel(x_ref, yh_ref, yg_ref, wh_ref, wg_ref, o_ref):
    o_ref[...] = (x_ref[...]
                  + jnp.dot(yh_ref[...], wh_ref[...], preferred_element_type=F32)
                  + jnp.dot(yg_ref[...], wg_ref[...], preferred_element_type=F32))


def _out_proj(x, y_hy, y_gla, w_out, *, tm=512):
    L, D = x.shape
    WH = y_hy.shape[1]
    WG = y_gla.shape[1]
    tm = min(tm, L)
    assert WH == WG and w_out.shape[0] == WH + WG
    return pl.pallas_call(
        _out_proj_kernel,
        grid=(L // tm,),
        in_specs=[
            pl.BlockSpec((tm, D), lambda i: (i, 0)),
            pl.BlockSpec((tm, WH), lambda i: (i, 0)),
            pl.BlockSpec((tm, WG), lambda i: (i, 0)),
            _single((WH, D), lambda i: (0, 0)),
            _single((WG, D), lambda i: (1, 0)),
        ],
        out_specs=pl.BlockSpec((tm, D), lambda i: (i, 0)),
        out_shape=jax.ShapeDtypeStruct((L, D), F32),
        compiler_params=_cparams(("parallel",)),
        name="out_proj",
    )(x, y_hy, y_gla, w_out, w_out)


def _filt_mlp_kernel(z_ref, w1_ref, b1_ref, f1_ref, w2_ref, b2_ref, f2_ref, w3_ref, b3_ref, f3_ref, *rest, n_cast):
    cast_src, o_ref, cast_dst = rest[:n_cast], rest[n_cast], rest[n_cast + 1:]
    dot = functools.partial(jnp.dot, precision=HIGHEST, preferred_element_type=F32)
    hid = jnp.sin(f1_ref[...] * (dot(z_ref[...], w1_ref[...]) + b1_ref[...]))
    hid = jnp.sin(f2_ref[...] * (dot(hid, w2_ref[...]) + b2_ref[...]))
    hid = jnp.sin(f3_ref[...] * (dot(hid, w3_ref[...]) + b3_ref[...]))
    hi, lo = _split_hi_lo(hid)
    o_ref[...] = jnp.concatenate([hi, hi, lo], axis=1)
    for src, dst in zip(cast_src, cast_dst):
        dst[...] = src[...].astype(BF16)


def _cast_row_blocks(shape, n):
    rows, cols = shape
    tile = 2 * SUBLANES
    block_rows = pl.cdiv(pl.cdiv(rows, n), tile) * tile
    nblocks = pl.cdiv(rows, block_rows)
    return (block_rows, cols), (lambda i: (jnp.minimum(i, nblocks - 1), 0))


def _filt_mlp(feat2, w1, b1, f1, w2, b2, f2, w3, b3, f3, *, to_cast=(), tm=512):
    L, FP = feat2.shape
    H2 = w2.shape[0]
    tm = min(tm, L)
    steps = L // tm
    full = lambda shp: pl.BlockSpec(shp, lambda i: (0, 0))
    cast_specs = [pl.BlockSpec(*_cast_row_blocks(w.shape, steps)) for w in to_cast]
    outs = pl.pallas_call(
        functools.partial(_filt_mlp_kernel, n_cast=len(to_cast)),
        grid=(steps,),
        in_specs=[pl.BlockSpec((tm, FP), lambda i: (i, 0)),
                  full((FP, H2)), full((1, H2)), full((1, H2)),
                  full((H2, H2)), full((1, H2)), full((1, H2)),
                  full((H2, H2)), full((1, H2)), full((1, H2))] + cast_specs,
        out_specs=[pl.BlockSpec((tm, 3 * H2), lambda i: (i, 0))] + cast_specs,
        out_shape=[jax.ShapeDtypeStruct((L, 3 * H2), BF16)] + [jax.ShapeDtypeStruct(w.shape, BF16) for w in to_cast],
        compiler_params=_cparams(("arbitrary",)),
        name="filt_mlp",
    )(feat2, w1, b1, f1, w2, b2, f2, w3, b3, f3, *to_cast)
    return outs[0], outs[1:]


def _dft_tables(L):
    N = 2 * L
    RA = N // RB
    NH = RA // 2
    two_pi = 2.0 * np.pi
    k1 = np.arange(RA, dtype=np.int64)

    def step1_table(NR):
        g = np.zeros((RB // 2, 2 * RA, 2 * NR), np.float64)
        for s in range(2):
            n = RB * np.arange(NR, dtype=np.int64)[None, :] + (2 * np.arange(RB // 2, dtype=np.int64) + s)[:, None]
            ang = two_pi * ((k1[None, :, None] * n[:, None, :]) % N) / N
            g[:, :RA, s * NR:(s + 1) * NR] = np.cos(ang)
            g[:, RA:, s * NR:(s + 1) * NR] = -np.sin(ang)
        return g

    phi = two_pi * ((np.arange(KB)[:, None] * np.arange(RB)[None, :]) % RB) / RB
    f3 = np.block([[np.cos(phi), np.sin(phi)], [-np.sin(phi), np.cos(phi)]])
    i1 = np.block([[np.cos(phi.T), -np.sin(phi.T)], [np.sin(phi.T), np.cos(phi.T)]])
    cast = lambda a: jnp.asarray(a.astype(np.float32)).astype(BF16)
    return dict(g_half=cast(step1_table(NH)), g_full=cast(step1_table(RA)), f3=cast(f3), i1=cast(i1))


def _fwd_step1(u_ref, g_ref, ab_ref, *, RA, NR, CB, G=2):
    def body(t, carry):
        ms = [t * G + j for j in range(G)]
        rhs = []
        for m in ms:
            ua = u_ref[pl.ds(2 * m, NR, stride=PITCH), :]
            ub = u_ref[pl.ds(2 * m + 1, NR, stride=PITCH), :]
            zz = jnp.zeros_like(ua)
            rhs.append(jnp.concatenate([jnp.concatenate([ua, zz], axis=1),
                                        jnp.concatenate([zz, ub], axis=1)], axis=0).astype(BF16))
        res = [jnp.dot(g_ref[m], r, preferred_element_type=F32) for m, r in zip(ms, rhs)]
        for m, r in zip(ms, res):
            ab_ref[pl.ds(2 * m, RA, stride=PITCH), :] = r[:RA, :CB]
            ab_ref[pl.ds(RA * PITCH + 2 * m, RA, stride=PITCH), :] = r[RA:, :CB]
            ab_ref[pl.ds(2 * m + 1, RA, stride=PITCH), :] = r[:RA, CB:]
            ab_ref[pl.ds(RA * PITCH + 2 * m + 1, RA, stride=PITCH), :] = r[RA:, CB:]
        return carry
    lax.fori_loop(0, RB // 2 // G, body, 0)


def _fwd_step3_pair(ab_ref, f3, kp, *, RA):
    cols = []
    for s in range(2):
        k1 = 2 * kp + s
        ar = ab_ref[pl.ds(pl.multiple_of(k1 * PITCH, SUBLANES), RB), :]
        ai = ab_ref[pl.ds(pl.multiple_of((RA + k1) * PITCH, SUBLANES), RB), :]
        cols.append(jnp.concatenate([ar, ai], axis=0))
    rhs = jnp.concatenate(cols, axis=1).astype(BF16)
    return jnp.dot(f3, rhs, preferred_element_type=F32)


def _for_row_blocks(nblocks, body, carry, unroll=2):
    assert nblocks >= 2
    carry = body(0, carry, True, False)
    carry = lax.fori_loop(1, nblocks - 1, lambda n1, c: body(n1, c, False, False), carry,
                          unroll=max(1, min(unroll, nblocks - 2)))
    return body(nblocks - 1, carry, False, True)


def _alt_sign(shape):
    rows = lax.broadcasted_iota(jnp.int32, shape, 0)
    return jnp.where(rows % 2 == 0, 1.0, -1.0).astype(F32)


def _filt_fft_kernel(hid_ref, w4_ref, dl_ref, g_ref, f3_ref, kh_ref, kn_ref, u_scr, ab_scr, *, L, CB, G):
    N = 2 * L
    RA = N // RB
    NH = RA // 2
    f3 = f3_ref[...]
    inv_lm1 = 1.0 / (L - 1)

    row = lax.broadcasted_iota(jnp.int32, (RB, CB), 0)
    arg_base = -(row.astype(F32) * inv_lm1) * dl_ref[...]
    arg_step = -(RB * inv_lm1) * dl_ref[...]
    arg_end = -(L * inv_lm1) * dl_ref[...]

    def fill(n1, carry, first, last):
        s_abs, s_alt = carry
        r0 = pl.multiple_of(n1 * RB, RB)
        h2 = jnp.dot(hid_ref[pl.ds(r0, RB), :], w4_ref[...], preferred_element_type=F32)
        arg = arg_base + jnp.asarray(n1, F32) * arg_step
        hf = h2[:, :CB] * jnp.exp(arg)
        hb = h2[:, CB:] * jnp.exp(arg_end - arg)
        if first:
            hb = jnp.where(row == 0, 0.0, hb)
        u_scr[pl.ds(pl.multiple_of(n1 * PITCH, SUBLANES), RB), :] = hf
        u_scr[pl.ds(pl.multiple_of((NH + n1) * PITCH, SUBLANES), RB), :] = hb
        return s_abs + (jnp.abs(hf) + jnp.abs(hb)), s_alt + (hf + hb)
    zero = jnp.zeros((RB, CB), F32)
    s_abs, s_alt = _for_row_blocks(NH, fill, (zero, zero), unroll=8)
    inv_l1 = 1.0 / jnp.sum(s_abs, axis=0, keepdims=True)
    kn = jnp.sum(s_alt * _alt_sign((RB, CB)), axis=0, keepdims=True) * inv_l1 * (1.0 / N)
    kn_ref[...] = jnp.broadcast_to(kn, kn_ref.shape)

    _fwd_step1(u_scr, g_ref, ab_scr, RA=RA, NR=RA, CB=CB, G=G)

    scale = inv_l1 * (2.0 / N)
    scale2 = jnp.concatenate([scale, scale], axis=1)
    row2 = lax.broadcasted_iota(jnp.int32, (KB, 2 * CB), 0)
    lane2 = lax.broadcasted_iota(jnp.int32, (KB, 2 * CB), 1)
    dc_pos = jnp.logical_and(row2 == 0, lane2 < CB)

    def emit(t, carry):
        kps = [t * G + j for j in range(G)]
        xs = [_fwd_step3_pair(ab_scr, f3, kp, RA=RA) for kp in kps]
        for kp, x in zip(kps, xs):
            kr = x[:KB] * scale2
            ki = x[KB:] * scale2
            dc = jnp.logical_and(dc_pos, kp == 0)
            kr = jnp.where(dc, 0.5 * kr, kr)
            ki = jnp.where(dc, 0.5 * ki, ki)
            kh_ref[pl.ds(pl.multiple_of(kp * RB, RB), RB), :] = jnp.concatenate([kr, ki], axis=0).astype(kh_ref.dtype)
        return carry
    lax.fori_loop(0, RA // 2 // G, emit, 0)


def _filt_fft(hid2, w4bd, deltas, g_tab, f3_tab, *, L, CB=LANES, G=16):
    C = deltas.shape[1]
    H2 = hid2.shape[1]
    N = 2 * L
    RA = N // RB
    nblk = C // CB
    G = min(G, RA // 2)
    return pl.pallas_call(
        functools.partial(_filt_fft_kernel, L=L, CB=CB, G=G),
        grid=(nblk,),
        in_specs=[
            _single((L, H2), lambda c: (0, 0)),
            pl.BlockSpec((H2, 2 * CB), lambda c: (0, c)),
            pl.BlockSpec((1, CB), lambda c: (0, c)),
            _single(g_tab.shape, lambda c: (0, 0, 0)),
            _single(f3_tab.shape, lambda c: (0, 0)),
        ],
        out_specs=[pl.BlockSpec((RA // 2 * RB, 2 * CB), lambda c: (0, c)),
                   pl.BlockSpec((SUBLANES, CB), lambda c: (0, c))],
        out_shape=[jax.ShapeDtypeStruct((RA // 2 * RB, 2 * C), BF16),
                   jax.ShapeDtypeStruct((SUBLANES, C), F32)],
        scratch_shapes=[pltpu.VMEM((RA * PITCH, CB), F32),
                        pltpu.VMEM((2 * RA * PITCH, CB), F32)],
        compiler_params=_cparams(("arbitrary",)),
        name="filt_fft",
    )(hid2, w4bd, deltas, g_tab, f3_tab)


def _short_conv_rows(p_ref, w_ref, b_ref, n1, *, first, last):
    r0 = pl.multiple_of(n1 * RB, RB)
    cur = p_ref[pl.ds(r0, RB), :]
    rows = lax.broadcasted_iota(jnp.int32, cur.shape, 0)
    if first:
        up = jnp.where(rows == 0, 0.0, pltpu.roll(cur, 1, axis=0))
    else:
        up = p_ref[pl.ds(r0 - 1, RB), :]
    if last:
        dn = jnp.where(rows == RB - 1, 0.0, pltpu.roll(cur, RB - 1, axis=0))
    else:
        dn = p_ref[pl.ds(r0 + 1, RB), :]
    w = w_ref[...]
    return b_ref[...] + up * w[0:1] + cur * w[1:2] + dn * w[2:3]


def _hy_conv_kernel(p_hbm, w0_ref, w1_ref, wv_ref, b0_ref, b1_ref, bv_ref,
                    kh_ref, kn_ref, skip_ref, og_ref, g_ref, f3_ref, i1_ref,
                    o_ref, u_scr, ab_scr, z_scr, p_scr, p_sem, *, L, CB, G):
    N = 2 * L
    RA = N // RB
    NH = RA // 2
    f3 = f3_ref[...]
    i1 = i1_ref[...]
    sign = _alt_sign((RB, CB))
    c = pl.program_id(0)
    nblk = pl.num_programs(0)
    p0_ref, p1_ref, pv_ref = p_scr.at[0], p_scr.at[1], p_scr.at[2]

    def fetch(group, blk):
        col = pl.multiple_of((group * nblk + blk) * CB, CB)
        return pltpu.make_async_copy(p_hbm.at[:, pl.ds(col, CB)], p_scr.at[group], p_sem.at[group])

    @pl.when(c == 0)
    def _():
        fetch(1, c).start()
        fetch(2, c).start()

    fetch(0, c).start()
    fetch(1, c).wait()
    fetch(2, c).wait()

    def fill(n1, s_alt, first, last):
        z = (_short_conv_rows(pv_ref, wv_ref, bv_ref, n1, first=first, last=last)
             * _short_conv_rows(p1_ref, w1_ref, b1_ref, n1, first=first, last=last))
        u_scr[pl.ds(pl.multiple_of(n1 * PITCH, SUBLANES), RB), :] = z
        z_scr[pl.ds(pl.multiple_of(n1 * RB, RB), RB), :] = z
        return s_alt + z
    s_alt = _for_row_blocks(NH, fill, jnp.zeros((RB, CB), F32))

    @pl.when(c + 1 < nblk)
    def _():
        fetch(1, c + 1).start()
        fetch(2, c + 1).start()

    z_nyq = jnp.sum(s_alt * sign, axis=0, keepdims=True)
    y_nyq = z_nyq * kn_ref[0:1, :]

    _fwd_step1(u_scr, g_ref, ab_scr, RA=RA, NR=NH, CB=CB, G=G)

    GM = max(G // 2, 1)

    def mid(t, carry):
        kps = [t * GM + j for j in range(GM)]
        xs = [_fwd_step3_pair(ab_scr, f3, kp, RA=RA) for kp in kps]
        khs = [kh_ref[pl.ds(pl.multiple_of(kp * RB, RB), RB), :].astype(F32) for kp in kps]
        bs = []
        for x, kh in zip(xs, khs):
            xr, xi, kr, ki = x[:KB], x[KB:], kh[:KB], kh[KB:]
            y = jnp.concatenate([xr * kr - xi * ki, xr * ki + xi * kr], axis=0).astype(BF16)
            bs.append(jnp.dot(i1, y, preferred_element_type=F32))
        for kp, b in zip(kps, bs):
            for s in range(2):
                k1 = 2 * kp + s
                ab_scr[pl.ds(pl.multiple_of(k1 * PITCH, SUBLANES), RB), :] = b[:RB, s * CB:(s + 1) * CB]
                ab_scr[pl.ds(pl.multiple_of((RA + k1) * PITCH, SUBLANES), RB), :] = b[RB:, s * CB:(s + 1) * CB]
        return carry
    lax.fori_loop(0, RA // 2 // GM, mid, 0)

    def last(t, carry):
        ms = [t * G + j for j in range(G)]
        rhs = [jnp.concatenate(
            [jnp.concatenate([ab_scr[pl.ds(2 * m + s, RA, stride=PITCH), :],
                              ab_scr[pl.ds(RA * PITCH + 2 * m + s, RA, stride=PITCH), :]], axis=0)
             for s in range(2)], axis=1).astype(BF16) for m in ms]
        ys = [_dot_tn(g_ref[m], r) for m, r in zip(ms, rhs)]
        for m, y in zip(ms, ys):
            u_scr[pl.ds(2 * m, NH, stride=PITCH), :] = y[:NH, :CB]
            u_scr[pl.ds(2 * m + 1, NH, stride=PITCH), :] = y[NH:, CB:]
        return carry
    lax.fori_loop(0, RB // 2 // G, last, 0)

    nyq_rows = sign * y_nyq
    fetch(0, c).wait()

    def finish(n1, carry, first, last):
        z = z_scr[pl.ds(pl.multiple_of(n1 * RB, RB), RB), :]
        x0 = _short_conv_rows(p0_ref, w0_ref, b0_ref, n1, first=first, last=last)
        conv = u_scr[pl.ds(pl.multiple_of(n1 * PITCH, SUBLANES), RB), :] + nyq_rows
        y = (conv + z * skip_ref[...]) * x0
        y = y * lax.rsqrt(jnp.mean(y * y, axis=-1, keepdims=True) + EPS) * og_ref[...]
        o_ref[pl.ds(pl.multiple_of(n1 * RB, RB), RB), :] = y.astype(o_ref.dtype)
        return carry
    _for_row_blocks(NH, finish, 0, unroll=4)


def _hy_conv(p, conv_w, conv_b, khat, knyq, skip, out_g, tabs, *, L, CB=LANES, G=16):
    C = skip.shape[1]
    assert CB == C // HY_GROUPS, "one channel block must be exactly one norm group"
    N = 2 * L
    RA = N // RB
    NH = RA // 2
    nblk = C // CB
    G = min(G, RA // 2)
    g_tab, f3_tab, i1_tab = tabs["g_half"], tabs["f3"], tabs["i1"]
    col = lambda off: (lambda c: (0, off * nblk + c))
    wspec = lambda off: pl.BlockSpec((3, CB), col(off))
    bspec = lambda off: pl.BlockSpec((1, CB), col(off))
    return pl.pallas_call(
        functools.partial(_hy_conv_kernel, L=L, CB=CB, G=G),
        grid=(nblk,),
        in_specs=[pl.BlockSpec(memory_space=pl.ANY),
                  wspec(0), wspec(1), wspec(2), bspec(0), bspec(1), bspec(2),
                  pl.BlockSpec((RA // 2 * RB, 2 * CB), col(0)),
                  pl.BlockSpec((SUBLANES, CB), col(0)),
                  bspec(0), bspec(0),
                  _single(g_tab.shape, lambda c: (0, 0, 0)),
                  _single(f3_tab.shape, lambda c: (0, 0)),
                  _single(i1_tab.shape, lambda c: (0, 0))],
        out_specs=pl.BlockSpec((L, CB), col(0)),
        out_shape=jax.ShapeDtypeStruct((L, C), BF16),
        scratch_shapes=[pltpu.VMEM((NH * PITCH, CB), F32),
                        pltpu.VMEM((2 * RA * PITCH, CB), F32),
                        pltpu.VMEM((L, CB), F32),
                        pltpu.VMEM((3, L, CB), F32),
                        pltpu.SemaphoreType.DMA((3,))],
        compiler_params=_cparams(("arbitrary",)),
        name="hy_conv",
    )(p, conv_w, conv_w, conv_w, conv_b, conv_b, conv_b, khat, knyq, skip, out_g,
      g_tab, f3_tab, i1_tab)


def _log_sigmoid(x):
    return jnp.minimum(x, 0.0) - jnp.log(1.0 + jnp.exp(-jnp.abs(x)))


def _gla_scan_blocks(dirs, *, TB):
    nch = TB // CHUNK
    rows = lambda c: slice(c * CHUNK, (c + 1) * CHUNK)
    units = [(d, c) for c in range(nch) for d in dirs]
    rr = lax.broadcasted_iota(jnp.int32, (CHUNK, CHUNK), 0)
    cc = lax.broadcasted_iota(jnp.int32, (CHUNK, CHUNK), 1)
    scale = GLA_DK ** -0.5

    for d in dirs:
        lr_hi, lr_lo = _split_hi_lo(d["lr"][...])
        gate_in = jnp.dot(jnp.concatenate([lr_hi, lr_hi, lr_lo], axis=1), d["wg"],
                          preferred_element_type=F32) + d["bg"]
        g = _log_sigmoid(gate_in) * (1.0 / GATE_TEMP)
        g_hi = g.astype(BF16)
        g_mid, g_lo = _split_hi_lo(g - g_hi.astype(F32))
        d["parts"][...] = jnp.concatenate([g_hi, g_mid, g_lo], axis=1)
        d["csum"] = jnp.where((cc >= rr) if d["reverse"] else (cc <= rr), 1.0, 0.0).astype(BF16)
        d["mask"] = (cc > rr) if d["reverse"] else (cc <= rr)
        d["edge"] = 0 if d["reverse"] else CHUNK - 1

    for d, c in units:
        b3 = jnp.dot(d["csum"], d["parts"][rows(c), :], preferred_element_type=F32)
        d["b"][rows(c), :] = b3[:, :GLA_DK] + b3[:, GLA_DK:2 * GLA_DK] + b3[:, 2 * GLA_DK:]

    decay = {}
    for d, c in units:
        b = d["b"][rows(c), :]
        b_edge = b[d["edge"]:d["edge"] + 1]
        k = d["k"][rows(c), :]
        d["qks"][0, rows(c), :] = (d["q"][rows(c), :] * scale * jnp.exp(b)).astype(BF16)
        d["qks"][1, rows(c), :] = (k * jnp.exp(-b)).astype(BF16)
        d["qks"][2, rows(c), :] = (k * jnp.exp(b_edge - b)).astype(BF16)
        d["vb"][rows(c), :] = d["v"][rows(c), :].astype(BF16)
        decay[(id(d), c)] = jnp.exp(b_edge)

    for d, c in units:
        att = _dot_nt(d["qks"][0, rows(c), :], d["qks"][1, rows(c), :])
        d["att"][rows(c), :] = jnp.where(d["mask"], att, 0.0).astype(BF16)

    for d, c in units:
        d["o"][rows(c), :] = jnp.dot(d["att"][rows(c), :], d["vb"][rows(c), :], preferred_element_type=F32)

    for d, c in units:
        d["kv"][c * GLA_DV:(c + 1) * GLA_DV, :] = _dot_tn(d["vb"][rows(c), :], d["qks"][2, rows(c), :])

    state = {id(d): d["s"][...] for d in dirs}
    for step in range(nch):
        for d in dirs:
            c = nch - 1 - step if d["reverse"] else step
            s_t = state[id(d)]
            d["o"][rows(c), :] += _dot_nt(d["qks"][0, rows(c), :], s_t.astype(BF16))
            state[id(d)] = s_t * decay[(id(d), c)] + d["kv"][c * GLA_DV:(c + 1) * GLA_DV, :]
    for d in dirs:
        d["s"][...] = state[id(d)]


def _gla_kernel(qf_ref, kf_ref, vf_ref, lf_ref, rf_ref, qb_ref, kb_ref, vb_ref, lb_ref, rb_ref,
                wg_ref, bg_ref, og_ref, o_ref,
                s_scr, parts_scr, b_scr, qks_scr, vb_scr, att_scr, kv_scr, ob_scr, half_scr, *, TB):
    s = pl.program_id(1)
    nb = pl.num_programs(1)

    @pl.when(s == 0)
    def _():
        s_scr[...] = jnp.zeros_like(s_scr)

    def direction(i, q, k, v, lr):
        return dict(q=q, k=k, v=v, lr=lr, wg=wg_ref[i], bg=bg_ref[i], reverse=bool(i), s=s_scr.at[i],
                    parts=parts_scr.at[i], b=b_scr.at[i], qks=qks_scr.at[i], vb=vb_scr.at[i],
                    att=att_scr.at[i], kv=kv_scr.at[i], o=ob_scr.at[i])
    _gla_scan_blocks([direction(0, qf_ref, kf_ref, vf_ref, lf_ref),
                      direction(1, qb_ref, kb_ref, vb_ref, lb_ref)], TB=TB)

    first_half = s < nb // 2
    for d, blk, r_ref in ((0, s, rf_ref), (1, nb - 1 - s, rb_ref)):
        rows = pl.ds(pl.multiple_of(blk * TB, TB), TB)

        @pl.when(first_half)
        def _():
            half_scr[rows, :] = ob_scr[d]

        @pl.when(jnp.logical_not(first_half))
        def _():
            tot = half_scr[rows, :] + ob_scr[d]
            tot = tot * lax.rsqrt(jnp.mean(tot * tot, axis=-1, keepdims=True) + EPS) * og_ref[...]
            o_ref[rows, :] = (tot * jax.nn.silu(r_ref[...])).astype(o_ref.dtype)


def _gla(p, p_lr, w_gate, b_gate, out_g, *, L, col0, TB=1024):
    TB = min(TB, L)
    nb = L // TB
    assert nb % 2 == 0, "both scan directions must meet between two blocks"
    nch = TB // CHUNK
    qb = col0 // GLA_DK
    kb = qb + GLA_HEADS
    vb = (col0 + 2 * GLA_KW) // GLA_DV
    rb = vb + GLA_HEADS
    fwd = lambda s: s
    bwd = lambda s: nb - 1 - s

    def operands(blk):
        return [pl.BlockSpec((TB, GLA_DK), lambda h, s: (blk(s), qb + h)),
                pl.BlockSpec((TB, GLA_DK), lambda h, s: (blk(s), kb + h)),
                pl.BlockSpec((TB, GLA_DV), lambda h, s: (blk(s), vb + h)),
                pl.BlockSpec((TB, LANES), lambda h, s: (blk(s), 0)),
                pl.BlockSpec((TB, GLA_DV), lambda h, s: (blk(s), rb + h))]

    return pl.pallas_call(
        functools.partial(_gla_kernel, TB=TB),
        grid=(GLA_HEADS, nb),
        in_specs=operands(fwd) + operands(bwd) + [
            pl.BlockSpec((2, 3 * LANES, GLA_DK), lambda h, s: (0, 0, h)),
            pl.BlockSpec((2, 1, GLA_DK), lambda h, s: (0, 0, h)),
            pl.BlockSpec((1, GLA_DV), lambda h, s: (0, h)),
        ],
        out_specs=pl.BlockSpec((L, GLA_DV), lambda h, s: (0, h)),
        out_shape=jax.ShapeDtypeStruct((L, GLA_VW), BF16),
        scratch_shapes=[pltpu.VMEM((2, GLA_DV, GLA_DK), F32),
                        pltpu.VMEM((2, TB, 3 * GLA_DK), BF16),
                        pltpu.VMEM((2, TB, GLA_DK), F32),
                        pltpu.VMEM((2, 3, TB, GLA_DK), BF16),
                        pltpu.VMEM((2, TB, GLA_DV), BF16),
                        pltpu.VMEM((2, TB, CHUNK), BF16),
                        pltpu.VMEM((2, nch * GLA_DV, GLA_DK), F32),
                        pltpu.VMEM((2, TB, GLA_DV), F32),
                        pltpu.VMEM((L, GLA_DV), F32)],
        compiler_params=_cparams(("arbitrary", "arbitrary")),
        name="gla",
    )(*([p, p, p, p_lr, p] * 2), w_gate, b_gate, out_g)


def _filter_features(L):
    t = np.linspace(0.0, 1.0, L)[:, None]
    bands = (FILTER_EMB - 1) // 2
    freqs = np.linspace(1e-4, bands - 1, bands)[None, :]
    ang = (2.0 * np.pi / L) * np.arange(L)[:, None] * freqs
    feat = np.concatenate([t, np.cos(ang), -np.sin(ang)], axis=-1)
    feat_rev = np.roll(feat[::-1], 1, axis=0)
    both = np.zeros((L, LANES), np.float32)
    both[:, :FILTER_EMB] = feat
    both[:, FILTER_EMB:2 * FILTER_EMB] = feat_rev
    return jnp.asarray(both)


def _twice(w, rows_out):
    r, c = w.shape
    zz = jnp.zeros_like(w)
    both = jnp.concatenate([jnp.concatenate([w, zz], axis=1), jnp.concatenate([zz, w], axis=1)], axis=0)
    return jnp.pad(both, ((0, rows_out - 2 * r), (0, 0)))


def _filter_deltas():
    min_decay = math.log(DECAY_TARGET) / LONG_DECAY_PCT
    max_decay = math.log(DECAY_TARGET) / SHORT_DECAY_PCT
    return jnp.abs(jnp.linspace(min_decay, max_decay, HY_WIDTH, dtype=F32)).reshape(1, HY_WIDTH)


def _block_diag_w4(w4):
    H = w4.shape[0]
    nblk = HY_WIDTH // LANES
    wf = w4[:, :HY_WIDTH].reshape(H, nblk, LANES)
    wb = w4[:, HY_WIDTH:].reshape(H, nblk, LANES)
    zz = jnp.zeros_like(wf)
    top = jnp.concatenate([wf, zz], axis=2)
    bot = jnp.concatenate([zz, wb], axis=2)
    return jnp.concatenate([top, bot], axis=0).reshape(2 * H, nblk * 2 * LANES)


def kernel(x, ffn1_norm, ffn1_w_gate, ffn1_w_up, ffn1_w_down, mix_norm, w_in, hy_conv_w, hy_conv_b, flt_w1, flt_b1, flt_f1, flt_w2, flt_b2, flt_f2, flt_w3, flt_b3, flt_f3, flt_w4, hy_skip, hy_out_norm, gla_w_a2_f, gla_b_a_f, gla_w_a2_b, gla_b_a_b, gla_out_norm, w_out, ffn2_norm, ffn2_w_gate, ffn2_w_up, ffn2_w_down, final_norm):
    B, L, D = x.shape
    depth = ffn1_norm.shape[0]
    tabs = _dft_tables(L)
    feat2 = _filter_features(L)
    deltas = _filter_deltas()
    gla_col0 = 3 * HY_WIDTH
    H2 = 2 * FILTER_HIDDEN
    both = lambda a: jnp.concatenate([a, a]).reshape(1, H2)
    filt = []
    for l in range(depth):
        early = [jnp.swapaxes(w_in[l], 0, 1)]
        if l == 0:
            early += [ffn1_w_gate[0], ffn1_w_up[0], ffn1_w_down[0]]
        hid3, cast = _filt_mlp(feat2, _twice(flt_w1[l], LANES), both(flt_b1[l]), both(flt_f1[l]),
                               _twice(flt_w2[l], H2), both(flt_b2[l]), both(flt_f2[l]),
                               _twice(flt_w3[l], H2), both(flt_b3[l]), both(flt_f3[l]), to_cast=early)
        w4cat = _three_pass_rows(_block_diag_w4(flt_w4[l]))
        khat, knyq = _filt_fft(hid3, w4cat, deltas, tabs["g_full"], tabs["f3"], L=L)
        filt.append((khat, knyq, cast))

    outs = []
    for bi in range(B):
        xb = x[bi]
        w1 = filt[0][2][1:]
        for l in range(depth):
            last = l == depth - 1
            khat, knyq, (w_in_t, *_) = filt[l]
            later = [ffn2_w_gate[l], ffn2_w_up[l], ffn2_w_down[l], w_out[l]]
            xb, (w2_gate, w2_up, w2_down, w_out_bf) = _ffn(xb, ffn1_norm[l], *w1, final_norm, final_norm=False,
                                                           to_cast=later)
            p, p_lr = _in_proj(xb, mix_norm[l], w_in_t)
            y_hy = _hy_conv(p, hy_conv_w[l], hy_conv_b[l].reshape(1, -1), khat, knyq,
                            hy_skip[l].reshape(1, -1), hy_out_norm[l].reshape(1, -1), tabs, L=L)

            w_gate = jnp.stack([jnp.pad(gla_w_a2_f[l], ((0, LANES - GATE_RANK), (0, 0))),
                                jnp.pad(gla_w_a2_b[l], ((GATE_RANK, LANES - 2 * GATE_RANK), (0, 0)))])
            w_gate = _three_pass_rows(w_gate)
            b_gate = jnp.stack([gla_b_a_f[l], gla_b_a_b[l]]).reshape(2, 1, GLA_KW)
            y_gla = _gla(p, p_lr, w_gate, b_gate, gla_out_norm[l].reshape(1, -1), L=L, col0=gla_col0)

            xb = _out_proj(xb, y_hy, y_gla, w_out_bf)
            nxt = [] if last else [ffn1_w_gate[l + 1], ffn1_w_up[l + 1], ffn1_w_down[l + 1]]
            if nxt:
                xb, w1 = _ffn(xb, ffn2_norm[l], w2_gate, w2_up, w2_down, final_norm, final_norm=last, to_cast=nxt)
            else:
                xb = _ffn_stream(xb, ffn2_norm[l], w2_gate, w2_up, w2_down, final_norm, final_norm=last)
        outs.append(xb)
    return jnp.stack(outs)
```

```python
import functools
import math

import numpy as np
import jax
import jax.numpy as jnp
from jax import lax
from jax.experimental import pallas as pl
from jax.experimental.pallas import tpu as pltpu

F32 = jnp.float32
BF16 = jnp.bfloat16
HIGHEST = lax.Precision.HIGHEST

EPS = 1e-6
HY_WIDTH = 1024
HY_GROUPS = 8
FILTER_EMB = 33
FILTER_HIDDEN = 64
SHORT_DECAY_PCT = 0.3
LONG_DECAY_PCT = 1.5
DECAY_TARGET = 1e-2
GLA_HEADS = 4
GLA_DK = 128
GLA_DV = 256
GLA_KW = GLA_HEADS * GLA_DK
GLA_VW = GLA_HEADS * GLA_DV
GATE_RANK = 16
GATE_TEMP = 16.0
CHUNK = 64

LANES = 128
SUBLANES = 8
VMEM_LIMIT = 60 * 1024 * 1024
FFN_SLAB = 128
RB = 128
KB = RB // 2
PITCH = RB + SUBLANES


def _cparams(sem):
    return pltpu.CompilerParams(dimension_semantics=sem, vmem_limit_bytes=VMEM_LIMIT)


def _single(block_shape, index_map):
    return pl.BlockSpec(block_shape, index_map, pipeline_mode=pl.Buffered(1))


def _split_hi_lo(x):
    hi = x.astype(BF16)
    return hi, (x - hi.astype(F32)).astype(BF16)


def _three_pass_rows(w):
    w_hi, w_lo = _split_hi_lo(w)
    return jnp.concatenate([w_hi, w_lo, w_hi], axis=-2)


def _dot_nt(a, b):
    return lax.dot_general(a, b, (((1,), (1,)), ((), ())), preferred_element_type=F32)


def _dot_tn(a, b):
    return lax.dot_general(a, b, (((0,), (0,)), ((), ())), preferred_element_type=F32)


def _ffn_kernel(x_ref, g_ref, wg_ref, wu_ref, wd_ref, fg_ref, *rest, final_norm, n_cast):
    cast_src, o_ref, cast_dst, h_scr = rest[:n_cast], rest[n_cast], rest[n_cast + 1:-1], rest[-1]
    j = pl.program_id(1)

    slab = min(FFN_SLAB, x_ref.shape[0])
    nslab = x_ref.shape[0] // slab

    @pl.when(j == 0)
    def _():
        def norm_rows(t, carry):
            rows = pl.ds(pl.multiple_of(t * slab, slab), slab)
            x = x_ref[rows, :]
            r = lax.rsqrt(jnp.mean(x * x, axis=-1, keepdims=True) + EPS)
            h_scr[rows, :] = (x * r * g_ref[...]).astype(BF16)
            o_ref[rows, :] = jnp.zeros((slab, x_ref.shape[1]), F32)
            return carry
        lax.fori_loop(0, nslab, norm_rows, 0)

    h = h_scr[...]
    half = wg_ref.shape[1] // 2
    for s in range(2):
        cols = pl.ds(s * half, half)
        gate = jnp.dot(h, wg_ref[:, cols], preferred_element_type=F32)
        up = jnp.dot(h, wu_ref[:, cols], preferred_element_type=F32)
        a = (jax.nn.silu(gate) * up).astype(BF16)
        o_ref[...] += jnp.dot(a, wd_ref[cols, :], preferred_element_type=F32)

    for src, dst in zip(cast_src, cast_dst):
        dst[...] = src[...].astype(BF16)

    @pl.when(j == pl.num_programs(1) - 1)
    def _():
        def finish_rows(t, carry):
            rows = pl.ds(pl.multiple_of(t * slab, slab), slab)
            y = x_ref[rows, :] + 0.5 * o_ref[rows, :]
            if final_norm:
                r = lax.rsqrt(jnp.mean(y * y, axis=-1, keepdims=True) + EPS)
                y = y * r * fg_ref[...]
            o_ref[rows, :] = y
            return carry
        lax.fori_loop(0, nslab, finish_rows, 0)


def _cast_blocking(shape, ni, nj):
    rows, cols = shape
    assert rows % ni == 0 and (rows // ni) % SUBLANES == 0 and cols % LANES == 0
    ncol = max(d for d in range(1, nj + 1) if (cols // LANES) % d == 0)
    return (rows // ni, cols // ncol), (lambda i, j: (i, jnp.minimum(j, ncol - 1)))


def _ffn(x, norm_g, w_gate, w_up, w_down, final_g, *, final_norm, to_cast=(), tm=1024, tf=512):
    L, D = x.shape
    DF = w_gate.shape[1]
    tm = min(tm, L)
    tf = min(tf, DF)
    ni, nj = L // tm, DF // tf
    blockings = [_cast_blocking(w.shape, ni, nj) for w in to_cast]
    cast_specs = [pl.BlockSpec(blk, imap) for blk, imap in blockings]
    outs = pl.pallas_call(
        functools.partial(_ffn_kernel, final_norm=final_norm, n_cast=len(to_cast)),
        grid=(ni, nj),
        in_specs=[
            pl.BlockSpec((tm, D), lambda i, j: (i, 0)),
            pl.BlockSpec((1, D), lambda i, j: (0, 0)),
            pl.BlockSpec((D, tf), lambda i, j: (0, j)),
            pl.BlockSpec((D, tf), lambda i, j: (0, j)),
            pl.BlockSpec((tf, D), lambda i, j: (j, 0)),
            pl.BlockSpec((1, D), lambda i, j: (0, 0)),
        ] + cast_specs,
        out_specs=[pl.BlockSpec((tm, D), lambda i, j: (i, 0))] + cast_specs,
        out_shape=[jax.ShapeDtypeStruct((L, D), F32)] + [jax.ShapeDtypeStruct(w.shape, BF16) for w in to_cast],
        scratch_shapes=[pltpu.VMEM((tm, D), BF16)],
        compiler_params=_cparams(("arbitrary", "arbitrary")),
        name="ffn",
    )(x, norm_g.reshape(1, D), w_gate, w_up, w_down, final_g.reshape(1, D), *to_cast)
    return outs[0], outs[1:]


def _ffn_stream_kernel(x_hbm, g_ref, wg_hbm, wu_hbm, wd_hbm, fg_ref, o_hbm,
                       x_buf, h_buf, acc, wg_buf, wu_buf, wd_buf, w_sem, x_sem, o_sem, *, final_norm, tm, tf, nj):
    i = pl.program_id(0)
    ni = pl.num_programs(0)
    total = ni * nj
    cur = i % 2
    nxt = 1 - cur
    slab = min(FFN_SLAB, tm)
    nslab = tm // slab
    lead = 3
    assert nj >= nslab + lead, "the next tile's slabs are spread over this tile's weight steps"

    def fetch_w(j, slot):
        col = pl.multiple_of(j * tf, tf)
        return (pltpu.make_async_copy(wg_hbm.at[:, pl.ds(col, tf)], wg_buf.at[slot], w_sem.at[0, slot]),
                pltpu.make_async_copy(wu_hbm.at[:, pl.ds(col, tf)], wu_buf.at[slot], w_sem.at[1, slot]),
                pltpu.make_async_copy(wd_hbm.at[pl.ds(col, tf), :], wd_buf.at[slot], w_sem.at[2, slot]))

    def tile_rows(row_tile):
        return pl.ds(pl.multiple_of(row_tile * tm, tm), tm)

    def fetch_x(row_tile, slot):
        return pltpu.make_async_copy(x_hbm.at[tile_rows(row_tile), :], x_buf.at[slot], x_sem.at[slot])

    def send_y(row_tile, slot):
        return pltpu.make_async_copy(x_buf.at[slot], o_hbm.at[tile_rows(row_tile), :], o_sem.at[slot])

    def norm_slab(t, src_slot, dst_slot):
        rows = pl.ds(pl.multiple_of(t * slab, slab), slab)
        x = x_buf[src_slot, rows, :]
        r = lax.rsqrt(jnp.mean(x * x, axis=-1, keepdims=True) + EPS)
        h_buf[dst_slot, rows, :] = (x * r * g_ref[...]).astype(BF16)

    @pl.when(i == 0)
    def _():
        for cp in fetch_w(0, 0):
            cp.start()
        fetch_x(0, 0).start()
        fetch_x(0, 0).wait()

        def first_rows(t, carry):
            norm_slab(t, 0, 0)
            return carry
        lax.fori_loop(0, nslab, first_rows, 0)

    def tile(j, first):
        step = i * nj + j
        slot = step % 2
        for cp in fetch_w(j, slot):
            cp.wait()

        @pl.when(step + 1 < total)
        def _():
            for cp in fetch_w((j + 1) % nj, 1 - slot):
                cp.start()

        if not first:
            @pl.when(jnp.logical_and(j == 1, i > 0))
            def _():
                send_y(i - 1, nxt).wait()

            @pl.when(jnp.logical_and(j == 1, i + 1 < ni))
            def _():
                fetch_x(i + 1, nxt).start()

            @pl.when(jnp.logical_and(j == lead - 1, i + 1 < ni))
            def _():
                fetch_x(i + 1, nxt).wait()

        h = h_buf[cur]
        for s in range(2):
            cols = pl.ds(s * (tf // 2), tf // 2)
            gate = jnp.dot(h, wg_buf[slot, :, cols], preferred_element_type=F32)
            up = jnp.dot(h, wu_buf[slot, :, cols], preferred_element_type=F32)
            a = (jax.nn.silu(gate) * up).astype(BF16)
            part = jnp.dot(a, wd_buf[slot, cols, :], preferred_element_type=F32)
            if first and s == 0:
                acc[...] = part
            else:
                acc[...] += part
        if not first:
            src = jnp.where(j < lead, cur, nxt)
            norm_slab(jnp.clip(j - lead, 0, nslab - 1), src, nxt)

    tile(0, True)

    def rest(j, carry):
        tile(j, False)
        return carry
    lax.fori_loop(1, nj, rest, 0)

    def finish_rows(t, carry):
        rows = pl.ds(pl.multiple_of(t * slab, slab), slab)
        y = x_buf[cur, rows, :] + 0.5 * acc[rows, :]
        if final_norm:
            r = lax.rsqrt(jnp.mean(y * y, axis=-1, keepdims=True) + EPS)
            y = y * r * fg_ref[...]
        x_buf[cur, rows, :] = y
        return carry
    lax.fori_loop(0, nslab, finish_rows, 0)
    send_y(i, cur).start()

    @pl.when(i == ni - 1)
    def _():
        send_y(i, cur).wait()


def _ffn_stream(x, norm_g, w_gate, w_up, w_down, final_g, *, final_norm, tm=1024, tf=512):
    L, D = x.shape
    DF = w_gate.shape[1]
    tm = min(tm, L)
    tf = min(tf, DF)
    nj = DF // tf
    hbm = pl.BlockSpec(memory_space=pl.ANY)
    return pl.pallas_call(
        functools.partial(_ffn_stream_kernel, final_norm=final_norm, tm=tm, tf=tf, nj=nj),
        grid=(L // tm,),
        in_specs=[hbm, pl.BlockSpec((1, D), lambda i: (0, 0)), hbm, hbm, hbm, pl.BlockSpec((1, D), lambda i: (0, 0))],
        out_specs=hbm,
        out_shape=jax.ShapeDtypeStruct((L, D), F32),
        scratch_shapes=[pltpu.VMEM((2, tm, D), F32), pltpu.VMEM((2, tm, D), BF16), pltpu.VMEM((tm, D), F32),
                        pltpu.VMEM((2, D, tf), BF16), pltpu.VMEM((2, D, tf), BF16), pltpu.VMEM((2, tf, D), BF16),
                        pltpu.SemaphoreType.DMA((3, 2)), pltpu.SemaphoreType.DMA((2,)), pltpu.SemaphoreType.DMA((2,))],
        compiler_params=_cparams(("arbitrary",)),
        name="ffn_stream",
    )(x, norm_g.reshape(1, D), w_gate, w_up, w_down, final_g.reshape(1, D))


def _in_proj_kernel(x_ref, g_ref, wt_ref, wtail_ref, o_ref, otail_ref, h_scr, *, n_tail):
    j = pl.program_id(1)

    @pl.when(j == 0)
    def _():
        x = x_ref[...]
        r = lax.rsqrt(jnp.mean(x * x, axis=-1, keepdims=True) + EPS)
        h = (x * r * g_ref[...]).astype(BF16)
        h_scr[...] = h
        tail = _dot_nt(h, wtail_ref[...])
        col = lax.broadcasted_iota(jnp.int32, tail.shape, 1)
        otail_ref[...] = jnp.where(col < n_tail, tail, 0.0)

    o_ref[...] = _dot_nt(h_scr[...], wt_ref[...])


def _in_proj(x, norm_g, w_t, *, tm=1024, tn=2048):
    L, D = x.shape
    n_cols = w_t.shape[0]
    nj = n_cols // tn
    n_tail = n_cols - nj * tn
    assert 0 < n_tail <= LANES and (nj * tn) % LANES == 0
    tm = min(tm, L)
    return pl.pallas_call(
        functools.partial(_in_proj_kernel, n_tail=n_tail),
        grid=(L // tm, nj),
        in_specs=[
            pl.BlockSpec((tm, D), lambda i, j: (i, 0)),
            pl.BlockSpec((1, D), lambda i, j: (0, 0)),
            pl.BlockSpec((tn, D), lambda i, j: (j, 0)),
            pl.BlockSpec((LANES, D), lambda i, j: (nj * tn // LANES, 0)),
        ],
        out_specs=[pl.BlockSpec((tm, tn), lambda i, j: (i, j)),
                   pl.BlockSpec((tm, LANES), lambda i, j: (i, 0))],
        out_shape=[jax.ShapeDtypeStruct((L, nj * tn), F32),
                   jax.ShapeDtypeStruct((L, LANES), F32)],
        scratch_shapes=[pltpu.VMEM((tm, D), BF16)],
        compiler_params=_cparams(("parallel", "arbitrary")),
        name="in_proj",
    )(x, norm_g.reshape(1, D), w_t, w_t)


def _out_proj_kernel(x_ref, yh_ref, yg_ref, wh_ref, wg_ref, o_ref):
    o_ref[...] = (x_ref[...]
                  + jnp.dot(yh_ref[...], wh_ref[...], preferred_element_type=F32)
                  + jnp.dot(yg_ref[...], wg_ref[...], preferred_element_type=F32))


def _out_proj(x, y_hy, y_gla, w_out, *, tm=512):
    L, D = x.shape
    WH = y_hy.shape[1]
    WG = y_gla.shape[1]
    tm = min(tm, L)
    assert WH == WG and w_out.shape[0] == WH + WG
    return pl.pallas_call(
        _out_proj_kernel,
        grid=(L // tm,),
        in_specs=[
            pl.BlockSpec((tm, D), lambda i: (i, 0)),
            pl.BlockSpec((tm, WH), lambda i: (i, 0)),
            pl.BlockSpec((tm, WG), lambda i: (i, 0)),
            _single((WH, D), lambda i: (0, 0)),
            _single((WG, D), lambda i: (1, 0)),
        ],
        out_specs=pl.BlockSpec((tm, D), lambda i: (i, 0)),
        out_shape=jax.ShapeDtypeStruct((L, D), F32),
        compiler_params=_cparams(("parallel",)),
        name="out_proj",
    )(x, y_hy, y_gla, w_out, w_out)


def _filt_mlp_kernel(z_ref, w1_ref, b1_ref, f1_ref, w2_ref, b2_ref, f2_ref, w3_ref, b3_ref, f3_ref, *rest, n_cast):
    cast_src, o_ref, cast_dst = rest[:n_cast], rest[n_cast], rest[n_cast + 1:]
    dot = functools.partial(jnp.dot, precision=HIGHEST, preferred_element_type=F32)
    hid = jnp.sin(f1_ref[...] * (dot(z_ref[...], w1_ref[...]) + b1_ref[...]))
    hid = jnp.sin(f2_ref[...] * (dot(hid, w2_ref[...]) + b2_ref[...]))
    hid = jnp.sin(f3_ref[...] * (dot(hid, w3_ref[...]) + b3_ref[...]))
    hi, lo = _split_hi_lo(hid)
    o_ref[...] = jnp.concatenate([hi, hi, lo], axis=1)
    for src, dst in zip(cast_src, cast_dst):
        dst[...] = src[...].astype(BF16)


def _cast_row_blocks(shape, n):
    rows, cols = shape
    tile = 2 * SUBLANES
    block_rows = pl.cdiv(pl.cdiv(rows, n), tile) * tile
    nblocks = pl.cdiv(rows, block_rows)
    return (block_rows, cols), (lambda i: (jnp.minimum(i, nblocks - 1), 0))


def _filt_mlp(feat2, w1, b1, f1, w2, b2, f2, w3, b3, f3, *, to_cast=(), tm=512):
    L, FP = feat2.shape
    H2 = w2.shape[0]
    tm = min(tm, L)
    steps = L // tm
    full = lambda shp: pl.BlockSpec(shp, lambda i: (0, 0))
    cast_specs = [pl.BlockSpec(*_cast_row_blocks(w.shape, steps)) for w in to_cast]
    outs = pl.pallas_call(
        functools.partial(_filt_mlp_kernel, n_cast=len(to_cast)),
        grid=(steps,),
        in_specs=[pl.BlockSpec((tm, FP), lambda i: (i, 0)),
                  full((FP, H2)), full((1, H2)), full((1, H2)),
                  full((H2, H2)), full((1, H2)), full((1, H2)),
                  full((H2, H2)), full((1, H2)), full((1, H2))] + cast_specs,
        out_specs=[pl.BlockSpec((tm, 3 * H2), lambda i: (i, 0))] + cast_specs,
        out_shape=[jax.ShapeDtypeStruct((L, 3 * H2), BF16)] + [jax.ShapeDtypeStruct(w.shape, BF16) for w in to_cast],
        compiler_params=_cparams(("arbitrary",)),
        name="filt_mlp",
    )(feat2, w1, b1, f1, w2, b2, f2, w3, b3, f3, *to_cast)
    return outs[0], outs[1:]


def _dft_tables(L):
    N = 2 * L
    RA = N // RB
    NH = RA // 2
    two_pi = 2.0 * np.pi
    k1 = np.arange(RA, dtype=np.int64)

    def step1_table(NR):
        g = np.zeros((RB // 2, 2 * RA, 2 * NR), np.float64)
        for s in range(2):
            n = RB * np.arange(NR, dtype=np.int64)[None, :] + (2 * np.arange(RB // 2, dtype=np.int64) + s)[:, None]
            ang = two_pi * ((k1[None, :, None] * n[:, None, :]) % N) / N
            g[:, :RA, s * NR:(s + 1) * NR] = np.cos(ang)
            g[:, RA:, s * NR:(s + 1) * NR] = -np.sin(ang)
        return g

    phi = two_pi * ((np.arange(KB)[:, None] * np.arange(RB)[None, :]) % RB) / RB
    f3 = np.block([[np.cos(phi), np.sin(phi)], [-np.sin(phi), np.cos(phi)]])
    i1 = np.block([[np.cos(phi.T), -np.sin(phi.T)], [np.sin(phi.T), np.cos(phi.T)]])
    cast = lambda a: jnp.asarray(a.astype(np.float32)).astype(BF16)
    return dict(g_half=cast(step1_table(NH)), g_full=cast(step1_table(RA)), f3=cast(f3), i1=cast(i1))


def _fwd_step1(u_ref, g_ref, ab_ref, *, RA, NR, CB, G=2):
    def body(t, carry):
        ms = [t * G + j for j in range(G)]
        rhs = []
        for m in ms:
            ua = u_ref[pl.ds(2 * m, NR, stride=PITCH), :]
            ub = u_ref[pl.ds(2 * m + 1, NR, stride=PITCH), :]
            zz = jnp.zeros_like(ua)
            rhs.append(jnp.concatenate([jnp.concatenate([ua, zz], axis=1),
                                        jnp.concatenate([zz, ub], axis=1)], axis=0).astype(BF16))
        res = [jnp.dot(g_ref[m], r, preferred_element_type=F32) for m, r in zip(ms, rhs)]
        for m, r in zip(ms, res):
            ab_ref[pl.ds(2 * m, RA, stride=PITCH), :] = r[:RA, :CB]
            ab_ref[pl.ds(RA * PITCH + 2 * m, RA, stride=PITCH), :] = r[RA:, :CB]
            ab_ref[pl.ds(2 * m + 1, RA, stride=PITCH), :] = r[:RA, CB:]
            ab_ref[pl.ds(RA * PITCH + 2 * m + 1, RA, stride=PITCH), :] = r[RA:, CB:]
        return carry
    lax.fori_loop(0, RB // 2 // G, body, 0)


def _fwd_step3_pair(ab_ref, f3, kp, *, RA):
    cols = []
    for s in range(2):
        k1 = 2 * kp + s
        ar = ab_ref[pl.ds(pl.multiple_of(k1 * PITCH, SUBLANES), RB), :]
        ai = ab_ref[pl.ds(pl.multiple_of((RA + k1) * PITCH, SUBLANES), RB), :]
        cols.append(jnp.concatenate([ar, ai], axis=0))
    rhs = jnp.concatenate(cols, axis=1).astype(BF16)
    return jnp.dot(f3, rhs, preferred_element_type=F32)


def _for_row_blocks(nblocks, body, carry, unroll=2):
    assert nblocks >= 2
    carry = body(0, carry, True, False)
    carry = lax.fori_loop(1, nblocks - 1, lambda n1, c: body(n1, c, False, False), carry,
                          unroll=max(1, min(unroll, nblocks - 2)))
    return body(nblocks - 1, carry, False, True)


def _alt_sign(shape):
    rows = lax.broadcasted_iota(jnp.int32, shape, 0)
    return jnp.where(rows % 2 == 0, 1.0, -1.0).astype(F32)


def _filt_fft_kernel(hid_ref, w4_ref, dl_ref, g_ref, f3_ref, kh_ref, kn_ref, u_scr, ab_scr, *, L, CB, G):
    N = 2 * L
    RA = N // RB
    NH = RA // 2
    f3 = f3_ref[...]
    inv_lm1 = 1.0 / (L - 1)

    row = lax.broadcasted_iota(jnp.int32, (RB, CB), 0)
    arg_base = -(row.astype(F32) * inv_lm1) * dl_ref[...]
    arg_step = -(RB * inv_lm1) * dl_ref[...]
    arg_end = -(L * inv_lm1) * dl_ref[...]

    def fill(n1, carry, first, last):
        s_abs, s_alt = carry
        r0 = pl.multiple_of(n1 * RB, RB)
        h2 = jnp.dot(hid_ref[pl.ds(r0, RB), :], w4_ref[...], preferred_element_type=F32)
        arg = arg_base + jnp.asarray(n1, F32) * arg_step
        hf = h2[:, :CB] * jnp.exp(arg)
        hb = h2[:, CB:] * jnp.exp(arg_end - arg)
        if first:
            hb = jnp.where(row == 0, 0.0, hb)
        u_scr[pl.ds(pl.multiple_of(n1 * PITCH, SUBLANES), RB), :] = hf
        u_scr[pl.ds(pl.multiple_of((NH + n1) * PITCH, SUBLANES), RB), :] = hb
        return s_abs + (jnp.abs(hf) + jnp.abs(hb)), s_alt + (hf + hb)
    zero = jnp.zeros((RB, CB), F32)
    s_abs, s_alt = _for_row_blocks(NH, fill, (zero, zero), unroll=8)
    inv_l1 = 1.0 / jnp.sum(s_abs, axis=0, keepdims=True)
    kn = jnp.sum(s_alt * _alt_sign((RB, CB)), axis=0, keepdims=True) * inv_l1 * (1.0 / N)
    kn_ref[...] = jnp.broadcast_to(kn, kn_ref.shape)

    _fwd_step1(u_scr, g_ref, ab_scr, RA=RA, NR=RA, CB=CB, G=G)

    scale = inv_l1 * (2.0 / N)
    scale2 = jnp.concatenate([scale, scale], axis=1)
    row2 = lax.broadcasted_iota(jnp.int32, (KB, 2 * CB), 0)
    lane2 = lax.broadcasted_iota(jnp.int32, (KB, 2 * CB), 1)
    dc_pos = jnp.logical_and(row2 == 0, lane2 < CB)

    def emit(t, carry):
        kps = [t * G + j for j in range(G)]
        xs = [_fwd_step3_pair(ab_scr, f3, kp, RA=RA) for kp in kps]
        for kp, x in zip(kps, xs):
            kr = x[:KB] * scale2
            ki = x[KB:] * scale2
            dc = jnp.logical_and(dc_pos, kp == 0)
            kr = jnp.where(dc, 0.5 * kr, kr)
            ki = jnp.where(dc, 0.5 * ki, ki)
            kh_ref[pl.ds(pl.multiple_of(kp * RB, RB), RB), :] = jnp.concatenate([kr, ki], axis=0).astype(kh_ref.dtype)
        return carry
    lax.fori_loop(0, RA // 2 // G, emit, 0)


def _filt_fft(hid2, w4bd, deltas, g_tab, f3_tab, *, L, CB=LANES, G=16):
    C = deltas.shape[1]
    H2 = hid2.shape[1]
    N = 2 * L
    RA = N // RB
    nblk = C // CB
    G = min(G, RA // 2)
    return pl.pallas_call(
        functools.partial(_filt_fft_kernel, L=L, CB=CB, G=G),
        grid=(nblk,),
        in_specs=[
            _single((L, H2), lambda c: (0, 0)),
            pl.BlockSpec((H2, 2 * CB), lambda c: (0, c)),
            pl.BlockSpec((1, CB), lambda c: (0, c)),
            _single(g_tab.shape, lambda c: (0, 0, 0)),
            _single(f3_tab.shape, lambda c: (0, 0)),
        ],
        out_specs=[pl.BlockSpec((RA // 2 * RB, 2 * CB), lambda c: (0, c)),
                   pl.BlockSpec((SUBLANES, CB), lambda c: (0, c))],
        out_shape=[jax.ShapeDtypeStruct((RA // 2 * RB, 2 * C), BF16),
                   jax.ShapeDtypeStruct((SUBLANES, C), F32)],
        scratch_shapes=[pltpu.VMEM((RA * PITCH, CB), F32),
                        pltpu.VMEM((2 * RA * PITCH, CB), F32)],
        compiler_params=_cparams(("arbitrary",)),
        name="filt_fft",
    )(hid2, w4bd, deltas, g_tab, f3_tab)


def _short_conv_rows(p_ref, w_ref, b_ref, n1, *, first, last):
    r0 = pl.multiple_of(n1 * RB, RB)
    cur = p_ref[pl.ds(r0, RB), :]
    rows = lax.broadcasted_iota(jnp.int32, cur.shape, 0)
    if first:
        up = jnp.where(rows == 0, 0.0, pltpu.roll(cur, 1, axis=0))
    else:
        up = p_ref[pl.ds(r0 - 1, RB), :]
    if last:
        dn = jnp.where(rows == RB - 1, 0.0, pltpu.roll(cur, RB - 1, axis=0))
    else:
        dn = p_ref[pl.ds(r0 + 1, RB), :]
    w = w_ref[...]
    return b_ref[...] + up * w[0:1] + cur * w[1:2] + dn * w[2:3]


def _hy_conv_kernel(p_hbm, w0_ref, w1_ref, wv_ref, b0_ref, b1_ref, bv_ref,
                    kh_ref, kn_ref, skip_ref, og_ref, g_ref, f3_ref, i1_ref,
                    o_ref, u_scr, ab_scr, z_scr, p_scr, p_sem, *, L, CB, G):
    N = 2 * L
    RA = N // RB
    NH = RA // 2
    f3 = f3_ref[...]
    i1 = i1_ref[...]
    sign = _alt_sign((RB, CB))
    c = pl.program_id(0)
    nblk = pl.num_programs(0)
    p0_ref, p1_ref, pv_ref = p_scr.at[0], p_scr.at[1], p_scr.at[2]

    def fetch(group, blk):
        col = pl.multiple_of((group * nblk + blk) * CB, CB)
        return pltpu.make_async_copy(p_hbm.at[:, pl.ds(col, CB)], p_scr.at[group], p_sem.at[group])

    @pl.when(c == 0)
    def _():
        fetch(1, c).start()
        fetch(2, c).start()

    fetch(0, c).start()
    fetch(1, c).wait()
    fetch(2, c).wait()

    def fill(n1, s_alt, first, last):
        z = (_short_conv_rows(pv_ref, wv_ref, bv_ref, n1, first=first, last=last)
             * _short_conv_rows(p1_ref, w1_ref, b1_ref, n1, first=first, last=last))
        u_scr[pl.ds(pl.multiple_of(n1 * PITCH, SUBLANES), RB), :] = z
        z_scr[pl.ds(pl.multiple_of(n1 * RB, RB), RB), :] = z
        return s_alt + z
    s_alt = _for_row_blocks(NH, fill, jnp.zeros((RB, CB), F32))

    @pl.when(c + 1 < nblk)
    def _():
        fetch(1, c + 1).start()
        fetch(2, c + 1).start()

    z_nyq = jnp.sum(s_alt * sign, axis=0, keepdims=True)
    y_nyq = z_nyq * kn_ref[0:1, :]

    _fwd_step1(u_scr, g_ref, ab_scr, RA=RA, NR=NH, CB=CB, G=G)

    GM = max(G // 2, 1)

    def mid(t, carry):
        kps = [t * GM + j for j in range(GM)]
        xs = [_fwd_step3_pair(ab_scr, f3, kp, RA=RA) for kp in kps]
        khs = [kh_ref[pl.ds(pl.multiple_of(kp * RB, RB), RB), :].astype(F32) for kp in kps]
        bs = []
        for x, kh in zip(xs, khs):
            xr, xi, kr, ki = x[:KB], x[KB:], kh[:KB], kh[KB:]
            y = jnp.concatenate([xr * kr - xi * ki, xr * ki + xi * kr], axis=0).astype(BF16)
            bs.append(jnp.dot(i1, y, preferred_element_type=F32))
        for kp, b in zip(kps, bs):
            for s in range(2):
                k1 = 2 * kp + s
                ab_scr[pl.ds(pl.multiple_of(k1 * PITCH, SUBLANES), RB), :] = b[:RB, s * CB:(s + 1) * CB]
                ab_scr[pl.ds(pl.multiple_of((RA + k1) * PITCH, SUBLANES), RB), :] = b[RB:, s * CB:(s + 1) * CB]
        return carry
    lax.fori_loop(0, RA // 2 // GM, mid, 0)

    def last(t, carry):
        ms = [t * G + j for j in range(G)]
        rhs = [jnp.concatenate(
            [jnp.concatenate([ab_scr[pl.ds(2 * m + s, RA, stride=PITCH), :],
                              ab_scr[pl.ds(RA * PITCH + 2 * m + s, RA, stride=PITCH), :]], axis=0)
             for s in range(2)], axis=1).astype(BF16) for m in ms]
        ys = [_dot_tn(g_ref[m], r) for m, r in zip(ms, rhs)]
        for m, y in zip(ms, ys):
            u_scr[pl.ds(2 * m, NH, stride=PITCH), :] = y[:NH, :CB]
            u_scr[pl.ds(2 * m + 1, NH, stride=PITCH), :] = y[NH:, CB:]
        return carry
    lax.fori_loop(0, RB // 2 // G, last, 0)

    nyq_rows = sign * y_nyq
    fetch(0, c).wait()

    def finish(n1, carry, first, last):
        z = z_scr[pl.ds(pl.multiple_of(n1 * RB, RB), RB), :]
        x0 = _short_conv_rows(p0_ref, w0_ref, b0_ref, n1, first=first, last=last)
        conv = u_scr[pl.ds(pl.multiple_of(n1 * PITCH, SUBLANES), RB), :] + nyq_rows
        y = (conv + z * skip_ref[...]) * x0
        y = y * lax.rsqrt(jnp.mean(y * y, axis=-1, keepdims=True) + EPS) * og_ref[...]
        o_ref[pl.ds(pl.multiple_of(n1 * RB, RB), RB), :] = y.astype(o_ref.dtype)
        return carry
    _for_row_blocks(NH, finish, 0, unroll=4)


def _hy_conv(p, conv_w, conv_b, khat, knyq, skip, out_g, tabs, *, L, CB=LANES, G=16):
    C = skip.shape[1]
    assert CB == C // HY_GROUPS, "one channel block must be exactly one norm group"
    N = 2 * L
    RA = N // RB
    NH = RA // 2
    nblk = C // CB
    G = min(G, RA // 2)
    g_tab, f3_tab, i1_tab = tabs["g_half"], tabs["f3"], tabs["i1"]
    col = lambda off: (lambda c: (0, off * nblk + c))
    wspec = lambda off: pl.BlockSpec((3, CB), col(off))
    bspec = lambda off: pl.BlockSpec((1, CB), col(off))
    return pl.pallas_call(
        functools.partial(_hy_conv_kernel, L=L, CB=CB, G=G),
        grid=(nblk,),
        in_specs=[pl.BlockSpec(memory_space=pl.ANY),
                  wspec(0), wspec(1), wspec(2), bspec(0), bspec(1), bspec(2),
                  pl.BlockSpec((RA // 2 * RB, 2 * CB), col(0)),
                  pl.BlockSpec((SUBLANES, CB), col(0)),
                  bspec(0), bspec(0),
                  _single(g_tab.shape, lambda c: (0, 0, 0)),
                  _single(f3_tab.shape, lambda c: (0, 0)),
                  _single(i1_tab.shape, lambda c: (0, 0))],
        out_specs=pl.BlockSpec((L, CB), col(0)),
        out_shape=jax.ShapeDtypeStruct((L, C), BF16),
        scratch_shapes=[pltpu.VMEM((NH * PITCH, CB), F32),
                        pltpu.VMEM((2 * RA * PITCH, CB), F32),
                        pltpu.VMEM((L, CB), F32),
                        pltpu.VMEM((3, L, CB), F32),
                        pltpu.SemaphoreType.DMA((3,))],
        compiler_params=_cparams(("arbitrary",)),
        name="hy_conv",
    )(p, conv_w, conv_w, conv_w, conv_b, conv_b, conv_b, khat, knyq, skip, out_g,
      g_tab, f3_tab, i1_tab)


def _log_sigmoid(x):
    return jnp.minimum(x, 0.0) - jnp.log(1.0 + jnp.exp(-jnp.abs(x)))


def _gla_scan_blocks(dirs, *, TB):
    nch = TB // CHUNK
    rows = lambda c: slice(c * CHUNK, (c + 1) * CHUNK)
    units = [(d, c) for c in range(nch) for d in dirs]
    rr = lax.broadcasted_iota(jnp.int32, (CHUNK, CHUNK), 0)
    cc = lax.broadcasted_iota(jnp.int32, (CHUNK, CHUNK), 1)
    scale = GLA_DK ** -0.5

    for d in dirs:
        lr_hi, lr_lo = _split_hi_lo(d["lr"][...])
        gate_in = jnp.dot(jnp.concatenate([lr_hi, lr_hi, lr_lo], axis=1), d["wg"],
                          preferred_element_type=F32) + d["bg"]
        g = _log_sigmoid(gate_in) * (1.0 / GATE_TEMP)
        g_hi = g.astype(BF16)
        g_mid, g_lo = _split_hi_lo(g - g_hi.astype(F32))
        d["parts"][...] = jnp.concatenate([g_hi, g_mid, g_lo], axis=1)
        d["csum"] = jnp.where((cc >= rr) if d["reverse"] else (cc <= rr), 1.0, 0.0).astype(BF16)
        d["mask"] = (cc > rr) if d["reverse"] else (cc <= rr)
        d["edge"] = 0 if d["reverse"] else CHUNK - 1

    for d, c in units:
        b3 = jnp.dot(d["csum"], d["parts"][rows(c), :], preferred_element_type=F32)
        d["b"][rows(c), :] = b3[:, :GLA_DK] + b3[:, GLA_DK:2 * GLA_DK] + b3[:, 2 * GLA_DK:]

    decay = {}
    for d, c in units:
        b = d["b"][rows(c), :]
        b_edge = b[d["edge"]:d["edge"] + 1]
        k = d["k"][rows(c), :]
        d["qks"][0, rows(c), :] = (d["q"][rows(c), :] * scale * jnp.exp(b)).astype(BF16)
        d["qks"][1, rows(c), :] = (k * jnp.exp(-b)).astype(BF16)
        d["qks"][2, rows(c), :] = (k * jnp.exp(b_edge - b)).astype(BF16)
        d["vb"][rows(c), :] = d["v"][rows(c), :].astype(BF16)
        decay[(id(d), c)] = jnp.exp(b_edge)

    for d, c in units:
        att = _dot_nt(d["qks"][0, rows(c), :], d["qks"][1, rows(c), :])
        d["att"][rows(c), :] = jnp.where(d["mask"], att, 0.0).astype(BF16)

    for d, c in units:
        d["o"][rows(c), :] = jnp.dot(d["att"][rows(c), :], d["vb"][rows(c), :], preferred_element_type=F32)

    for d, c in units:
        d["kv"][c * GLA_DV:(c + 1) * GLA_DV, :] = _dot_tn(d["vb"][rows(c), :], d["qks"][2, rows(c), :])

    state = {id(d): d["s"][...] for d in dirs}
    for step in range(nch):
        for d in dirs:
            c = nch - 1 - step if d["reverse"] else step
            s_t = state[id(d)]
            d["o"][rows(c), :] += _dot_nt(d["qks"][0, rows(c), :], s_t.astype(BF16))
            state[id(d)] = s_t * decay[(id(d), c)] + d["kv"][c * GLA_DV:(c + 1) * GLA_DV, :]
    for d in dirs:
        d["s"][...] = state[id(d)]


def _gla_kernel(qf_ref, kf_ref, vf_ref, lf_ref, rf_ref, qb_ref, kb_ref, vb_ref, lb_ref, rb_ref,
                wg_ref, bg_ref, og_ref, o_ref,
                s_scr, parts_scr, b_scr, qks_scr, vb_scr, att_scr, kv_scr, ob_scr, half_scr, *, TB):
    s = pl.program_id(1)
    nb = pl.num_programs(1)

    @pl.when(s == 0)
    def _():
        s_scr[...] = jnp.zeros_like(s_scr)

    def direction(i, q, k, v, lr):
        return dict(q=q, k=k, v=v, lr=lr, wg=wg_ref[i], bg=bg_ref[i], reverse=bool(i), s=s_scr.at[i],
                    parts=parts_scr.at[i], b=b_scr.at[i], qks=qks_scr.at[i], vb=vb_scr.at[i],
                    att=att_scr.at[i], kv=kv_scr.at[i], o=ob_scr.at[i])
    _gla_scan_blocks([direction(0, qf_ref, kf_ref, vf_ref, lf_ref),
                      direction(1, qb_ref, kb_ref, vb_ref, lb_ref)], TB=TB)

    first_half = s < nb // 2
    for d, blk, r_ref in ((0, s, rf_ref), (1, nb - 1 - s, rb_ref)):
        rows = pl.ds(pl.multiple_of(blk * TB, TB), TB)

        @pl.when(first_half)
        def _():
            half_scr[rows, :] = ob_scr[d]

        @pl.when(jnp.logical_not(first_half))
        def _():
            tot = half_scr[rows, :] + ob_scr[d]
            tot = tot * lax.rsqrt(jnp.mean(tot * tot, axis=-1, keepdims=True) + EPS) * og_ref[...]
            o_ref[rows, :] = (tot * jax.nn.silu(r_ref[...])).astype(o_ref.dtype)


def _gla(p, p_lr, w_gate, b_gate, out_g, *, L, col0, TB=1024):
    TB = min(TB, L)
    nb = L // TB
    assert nb % 2 == 0, "both scan directions must meet between two blocks"
    nch = TB // CHUNK
    qb = col0 // GLA_DK
    kb = qb + GLA_HEADS
    vb = (col0 + 2 * GLA_KW) // GLA_DV
    rb = vb + GLA_HEADS
    fwd = lambda s: s
    bwd = lambda s: nb - 1 - s

    def operands(blk):
        return [pl.BlockSpec((TB, GLA_DK), lambda h, s: (blk(s), qb + h)),
                pl.BlockSpec((TB, GLA_DK), lambda h, s: (blk(s), kb + h)),
                pl.BlockSpec((TB, GLA_DV), lambda h, s: (blk(s), vb + h)),
                pl.BlockSpec((TB, LANES), lambda h, s: (blk(s), 0)),
                pl.BlockSpec((TB, GLA_DV), lambda h, s: (blk(s), rb + h))]

    return pl.pallas_call(
        functools.partial(_gla_kernel, TB=TB),
        grid=(GLA_HEADS, nb),
        in_specs=operands(fwd) + operands(bwd) + [
            pl.BlockSpec((2, 3 * LANES, GLA_DK), lambda h, s: (0, 0, h)),
            pl.BlockSpec((2, 1, GLA_DK), lambda h, s: (0, 0, h)),
            pl.BlockSpec((1, GLA_DV), lambda h, s: (0, h)),
        ],
        out_specs=pl.BlockSpec((L, GLA_DV), lambda h, s: (0, h)),
        out_shape=jax.ShapeDtypeStruct((L, GLA_VW), BF16),
        scratch_shapes=[pltpu.VMEM((2, GLA_DV, GLA_DK), F32),
                        pltpu.VMEM((2, TB, 3 * GLA_DK), BF16),
                        pltpu.VMEM((2, TB, GLA_DK), F32),
                        pltpu.VMEM((2, 3, TB, GLA_DK), BF16),
                        pltpu.VMEM((2, TB, GLA_DV), BF16),
                        pltpu.VMEM((2, TB, CHUNK), BF16),
                        pltpu.VMEM((2, nch * GLA_DV, GLA_DK), F32),
                        pltpu.VMEM((2, TB, GLA_DV), F32),
                        pltpu.VMEM((L, GLA_DV), F32)],
        compiler_params=_cparams(("arbitrary", "arbitrary")),
        name="gla",
    )(*([p, p, p, p_lr, p] * 2), w_gate, b_gate, out_g)


def _filter_features(L):
    t = np.linspace(0.0, 1.0, L)[:, None]
    bands = (FILTER_EMB - 1) // 2
    freqs = np.linspace(1e-4, bands - 1, bands)[None, :]
    ang = (2.0 * np.pi / L) * np.arange(L)[:, None] * freqs
    feat = np.concatenate([t, np.cos(ang), -np.sin(ang)], axis=-1)
    feat_rev = np.roll(feat[::-1], 1, axis=0)
    both = np.zeros((L, LANES), np.float32)
    both[:, :FILTER_EMB] = feat
    both[:, FILTER_EMB:2 * FILTER_EMB] = feat_rev
    return jnp.asarray(both)


def _twice(w, rows_out):
    r, c = w.shape
    zz = jnp.zeros_like(w)
    both = jnp.concatenate([jnp.concatenate([w, zz], axis=1), jnp.concatenate([zz, w], axis=1)], axis=0)
    return jnp.pad(both, ((0, rows_out - 2 * r), (0, 0)))


def _filter_deltas():
    min_decay = math.log(DECAY_TARGET) / LONG_DECAY_PCT
    max_decay = math.log(DECAY_TARGET) / SHORT_DECAY_PCT
    return jnp.abs(jnp.linspace(min_decay, max_decay, HY_WIDTH, dtype=F32)).reshape(1, HY_WIDTH)


def _block_diag_w4(w4):
    H = w4.shape[0]
    nblk = HY_WIDTH // LANES
    wf = w4[:, :HY_WIDTH].reshape(H, nblk, LANES)
    wb = w4[:, HY_WIDTH:].reshape(H, nblk, LANES)
    zz = jnp.zeros_like(wf)
    top = jnp.concatenate([wf, zz], axis=2)
    bot = jnp.concatenate([zz, wb], axis=2)
    return jnp.concatenate([top, bot], axis=0).reshape(2 * H, nblk * 2 * LANES)


def kernel(x, ffn1_norm, ffn1_w_gate, ffn1_w_up, ffn1_w_down, mix_norm, w_in, hy_conv_w, hy_conv_b, flt_w1, flt_b1, flt_f1, flt_w2, flt_b2, flt_f2, flt_w3, flt_b3, flt_f3, flt_w4, hy_skip, hy_out_norm, gla_w_a2_f, gla_b_a_f, gla_w_a2_b, gla_b_a_b, gla_out_norm, w_out, ffn2_norm, ffn2_w_gate, ffn2_w_up, ffn2_w_down, final_norm):
    B, L, D = x.shape
    depth = ffn1_norm.shape[0]
    tabs = _dft_tables(L)
    feat2 = _filter_features(L)
    deltas = _filter_deltas()
    gla_col0 = 3 * HY_WIDTH
    H2 = 2 * FILTER_HIDDEN
    both = lambda a: jnp.concatenate([a, a]).reshape(1, H2)
    filt = []
    for l in range(depth):
        early = [jnp.swapaxes(w_in[l], 0, 1)]
        if l == 0:
            early += [ffn1_w_gate[0], ffn1_w_up[0], ffn1_w_down[0]]
        hid3, cast = _filt_mlp(feat2, _twice(flt_w1[l], LANES), both(flt_b1[l]), both(flt_f1[l]),
                               _twice(flt_w2[l], H2), both(flt_b2[l]), both(flt_f2[l]),
                               _twice(flt_w3[l], H2), both(flt_b3[l]), both(flt_f3[l]), to_cast=early)
        w4cat = _three_pass_rows(_block_diag_w4(flt_w4[l]))
        khat, knyq = _filt_fft(hid3, w4cat, deltas, tabs["g_full"], tabs["f3"], L=L)
        filt.append((khat, knyq, cast))

    outs = []
    for bi in range(B):
        xb = x[bi]
        w1 = filt[0][2][1:]
        for l in range(depth):
            last = l == depth - 1
            khat, knyq, (w_in_t, *_) = filt[l]
            later = [ffn2_w_gate[l], ffn2_w_up[l], ffn2_w_down[l], w_out[l]]
            xb, (w2_gate, w2_up, w2_down, w_out_bf) = _ffn(xb, ffn1_norm[l], *w1, final_norm, final_norm=False,
                                                           to_cast=later)
            p, p_lr = _in_proj(xb, mix_norm[l], w_in_t)
            y_hy = _hy_conv(p, hy_conv_w[l], hy_conv_b[l].reshape(1, -1), khat, knyq,
                            hy_skip[l].reshape(1, -1), hy_out_norm[l].reshape(1, -1), tabs, L=L)

            w_gate = jnp.stack([jnp.pad(gla_w_a2_f[l], ((0, LANES - GATE_RANK), (0, 0))),
                                jnp.pad(gla_w_a2_b[l], ((GATE_RANK, LANES - 2 * GATE_RANK), (0, 0)))])
            w_gate = _three_pass_rows(w_gate)
            b_gate = jnp.stack([gla_b_a_f[l], gla_b_a_b[l]]).reshape(2, 1, GLA_KW)
            y_gla = _gla(p, p_lr, w_gate, b_gate, gla_out_norm[l].reshape(1, -1), L=L, col0=gla_col0)

            xb = _out_proj(xb, y_hy, y_gla, w_out_bf)
            nxt = [] if last else [ffn1_w_gate[l + 1], ffn1_w_up[l + 1], ffn1_w_down[l + 1]]
            if nxt:
                xb, w1 = _ffn(xb, ffn2_norm[l], w2_gate, w2_up, w2_down, final_norm, final_norm=last, to_cast=nxt)
            else:
                xb = _ffn_stream(xb, ffn2_norm[l], w2_gate, w2_up, w2_down, final_norm, final_norm=last)
        outs.append(xb)
    return jnp.stack(outs)
```

```python
import functools
import math

import numpy as np
import jax
import jax.numpy as jnp
from jax import lax
from jax.experimental import pallas as pl
from jax.experimental.pallas import tpu as pltpu

F32 = jnp.float32
BF16 = jnp.bfloat16
HIGHEST = lax.Precision.HIGHEST

EPS = 1e-6
HY_WIDTH = 1024
HY_GROUPS = 8
FILTER_EMB = 33
FILTER_HIDDEN = 64
SHORT_DECAY_PCT = 0.3
LONG_DECAY_PCT = 1.5
DECAY_TARGET = 1e-2
GLA_HEADS = 4
GLA_DK = 128
GLA_DV = 256
GLA_KW = GLA_HEADS * GLA_DK
GLA_VW = GLA_HEADS * GLA_DV
GATE_RANK = 16
GATE_TEMP = 16.0
CHUNK = 64

LANES = 128
SUBLANES = 8
VMEM_LIMIT = 60 * 1024 * 1024
FFN_SLAB = 128
RB = 128
KB = RB // 2
PITCH = RB + SUBLANES


def _cparams(sem):
    return pltpu.CompilerParams(dimension_semantics=sem, vmem_limit_bytes=VMEM_LIMIT)


def _single(block_shape, index_map):
    return pl.BlockSpec(block_shape, index_map, pipeline_mode=pl.Buffered(1))


def _split_hi_lo(x):
    hi = x.astype(BF16)
    return hi, (x - hi.astype(F32)).astype(BF16)


def _three_pass_rows(w):
    w_hi, w_lo = _split_hi_lo(w)
    return jnp.concatenate([w_hi, w_lo, w_hi], axis=-2)


def _dot_nt(a, b):
    return lax.dot_general(a, b, (((1,), (1,)), ((), ())), preferred_element_type=F32)


def _dot_tn(a, b):
    return lax.dot_general(a, b, (((0,), (0,)), ((), ())), preferred_element_type=F32)


def _ffn_kernel(x_ref, g_ref, wg_ref, wu_ref, wd_ref, fg_ref, *rest, final_norm, n_cast):
    cast_src, o_ref, cast_dst, h_scr = rest[:n_cast], rest[n_cast], rest[n_cast + 1:-1], rest[-1]
    j = pl.program_id(1)

    slab = min(FFN_SLAB, x_ref.shape[0])
    nslab = x_ref.shape[0] // slab

    @pl.when(j == 0)
    def _():
        def norm_rows(t, carry):
            rows = pl.ds(pl.multiple_of(t * slab, slab), slab)
            x = x_ref[rows, :]
            r = lax.rsqrt(jnp.mean(x * x, axis=-1, keepdims=True) + EPS)
            h_scr[rows, :] = (x * r * g_ref[...]).astype(BF16)
            o_ref[rows, :] = jnp.zeros((slab, x_ref.shape[1]), F32)
            return carry
        lax.fori_loop(0, nslab, norm_rows, 0)

    h = h_scr[...]
    half = wg_ref.shape[1] // 2
    for s in range(2):
        cols = pl.ds(s * half, half)
        gate = jnp.dot(h, wg_ref[:, cols], preferred_element_type=F32)
        up = jnp.dot(h, wu_ref[:, cols], preferred_element_type=F32)
        a = (jax.nn.silu(gate) * up).astype(BF16)
        o_ref[...] += jnp.dot(a, wd_ref[cols, :], preferred_element_type=F32)

    for src, dst in zip(cast_src, cast_dst):
        dst[...] = src[...].astype(BF16)

    @pl.when(j == pl.num_programs(1) - 1)
    def _():
        def finish_rows(t, carry):
            rows = pl.ds(pl.multiple_of(t * slab, slab), slab)
            y = x_ref[rows, :] + 0.5 * o_ref[rows, :]
            if final_norm:
                r = lax.rsqrt(jnp.mean(y * y, axis=-1, keepdims=True) + EPS)
                y = y * r * fg_ref[...]
            o_ref[rows, :] = y
            return carry
        lax.fori_loop(0, nslab, finish_rows, 0)


def _cast_blocking(shape, ni, nj):
    rows, cols = shape
    assert rows % ni == 0 and (rows // ni) % SUBLANES == 0 and cols % LANES == 0
    ncol = max(d for d in range(1, nj + 1) if (cols // LANES) % d == 0)
    return (rows // ni, cols // ncol), (lambda i, j: (i, jnp.minimum(j, ncol - 1)))


def _ffn(x, norm_g, w_gate, w_up, w_down, final_g, *, final_norm, to_cast=(), tm=1024, tf=512):
    L, D = x.shape
    DF = w_gate.shape[1]
    tm = min(tm, L)
    tf = min(tf, DF)
    ni, nj = L // tm, DF // tf
    blockings = [_cast_blocking(w.shape, ni, nj) for w in to_cast]
    cast_specs = [pl.BlockSpec(blk, imap) for blk, imap in blockings]
    outs = pl.pallas_call(
        functools.partial(_ffn_kernel, final_norm=final_norm, n_cast=len(to_cast)),
        grid=(ni, nj),
        in_specs=[
            pl.BlockSpec((tm, D), lambda i, j: (i, 0)),
            pl.BlockSpec((1, D), lambda i, j: (0, 0)),
            pl.BlockSpec((D, tf), lambda i, j: (0, j)),
            pl.BlockSpec((D, tf), lambda i, j: (0, j)),
            pl.BlockSpec((tf, D), lambda i, j: (j, 0)),
            pl.BlockSpec((1, D), lambda i, j: (0, 0)),
        ] + cast_specs,
        out_specs=[pl.BlockSpec((tm, D), lambda i, j: (i, 0))] + cast_specs,
        out_shape=[jax.ShapeDtypeStruct((L, D), F32)] + [jax.ShapeDtypeStruct(w.shape, BF16) for w in to_cast],
        scratch_shapes=[pltpu.VMEM((tm, D), BF16)],
        compiler_params=_cparams(("arbitrary", "arbitrary")),
        name="ffn",
    )(x, norm_g.reshape(1, D), w_gate, w_up, w_down, final_g.reshape(1, D), *to_cast)
    return outs[0], outs[1:]


def _ffn_stream_kernel(x_hbm, g_ref, wg_hbm, wu_hbm, wd_hbm, fg_ref, o_hbm,
                       x_buf, h_buf, acc, wg_buf, wu_buf, wd_buf, w_sem, x_sem, o_sem, *, final_norm, tm, tf, nj):
    i = pl.program_id(0)
    ni = pl.num_programs(0)
    total = ni * nj
    cur = i % 2
    nxt = 1 - cur
    slab = min(FFN_SLAB, tm)
    nslab = tm // slab
    lead = 3
    assert nj >= nslab + lead, "the next tile's slabs are spread over this tile's weight steps"

    def fetch_w(j, slot):
        col = pl.multiple_of(j * tf, tf)
        return (pltpu.make_async_copy(wg_hbm.at[:, pl.ds(col, tf)], wg_buf.at[slot], w_sem.at[0, slot]),
                pltpu.make_async_copy(wu_hbm.at[:, pl.ds(col, tf)], wu_buf.at[slot], w_sem.at[1, slot]),
                pltpu.make_async_copy(wd_hbm.at[pl.ds(col, tf), :], wd_buf.at[slot], w_sem.at[2, slot]))

    def tile_rows(row_tile):
        return pl.ds(pl.multiple_of(row_tile * tm, tm), tm)

    def fetch_x(row_tile, slot):
        return pltpu.make_async_copy(x_hbm.at[tile_rows(row_tile), :], x_buf.at[slot], x_sem.at[slot])

    def send_y(row_tile, slot):
        return pltpu.make_async_copy(x_buf.at[slot], o_hbm.at[tile_rows(row_tile), :], o_sem.at[slot])

    def norm_slab(t, src_slot, dst_slot):
        rows = pl.ds(pl.multiple_of(t * slab, slab), slab)
        x = x_buf[src_slot, rows, :]
        r = lax.rsqrt(jnp.mean(x * x, axis=-1, keepdims=True) + EPS)
        h_buf[dst_slot, rows, :] = (x * r * g_ref[...]).astype(BF16)

    @pl.when(i == 0)
    def _():
        for cp in fetch_w(0, 0):
            cp.start()
        fetch_x(0, 0).start()
        fetch_x(0, 0).wait()

        def first_rows(t, carry):
            norm_slab(t, 0, 0)
            return carry
        lax.fori_loop(0, nslab, first_rows, 0)

    def tile(j, first):
        step = i * nj + j
        slot = step % 2
        for cp in fetch_w(j, slot):
            cp.wait()

        @pl.when(step + 1 < total)
        def _():
            for cp in fetch_w((j + 1) % nj, 1 - slot):
                cp.start()

        if not first:
            @pl.when(jnp.logical_and(j == 1, i > 0))
            def _():
                send_y(i - 1, nxt).wait()

            @pl.when(jnp.logical_and(j == 1, i + 1 < ni))
            def _():
                fetch_x(i + 1, nxt).start()

            @pl.when(jnp.logical_and(j == lead - 1, i + 1 < ni))
            def _():
                fetch_x(i + 1, nxt).wait()

        h = h_buf[cur]
        for s in range(2):
            cols = pl.ds(s * (tf // 2), tf // 2)
            gate = jnp.dot(h, wg_buf[slot, :, cols], preferred_element_type=F32)
            up = jnp.dot(h, wu_buf[slot, :, cols], preferred_element_type=F32)
            a = (jax.nn.silu(gate) * up).astype(BF16)
            part = jnp.dot(a, wd_buf[slot, cols, :], preferred_element_type=F32)
            if first and s == 0:
                acc[...] = part
            else:
                acc[...] += part
        if not first:
            src = jnp.where(j < lead, cur, nxt)
            norm_slab(jnp.clip(j - lead, 0, nslab - 1), src, nxt)

    tile(0, True)

    def rest(j, carry):
        tile(j, False)
        return carry
    lax.fori_loop(1, nj, rest, 0)

    def finish_rows(t, carry):
        rows = pl.ds(pl.multiple_of(t * slab, slab), slab)
        y = x_buf[cur, rows, :] + 0.5 * acc[rows, :]
        if final_norm:
            r = lax.rsqrt(jnp.mean(y * y, axis=-1, keepdims=True) + EPS)
            y = y * r * fg_ref[...]
        x_buf[cur, rows, :] = y
        return carry
    lax.fori_loop(0, nslab, finish_rows, 0)
    send_y(i, cur).start()

    @pl.when(i == ni - 1)
    def _():
        send_y(i, cur).wait()


def _ffn_stream(x, norm_g, w_gate, w_up, w_down, final_g, *, final_norm, tm=1024, tf=512):
    L, D = x.shape
    DF = w_gate.shape[1]
    tm = min(tm, L)
    tf = min(tf, DF)
    nj = DF // tf
    hbm = pl.BlockSpec(memory_space=pl.ANY)
    return pl.pallas_call(
        functools.partial(_ffn_stream_kernel, final_norm=final_norm, tm=tm, tf=tf, nj=nj),
        grid=(L // tm,),
        in_specs=[hbm, pl.BlockSpec((1, D), lambda i: (0, 0)), hbm, hbm, hbm, pl.BlockSpec((1, D), lambda i: (0, 0))],
        out_specs=hbm,
        out_shape=jax.ShapeDtypeStruct((L, D), F32),
        scratch_shapes=[pltpu.VMEM((2, tm, D), F32), pltpu.VMEM((2, tm, D), BF16), pltpu.VMEM((tm, D), F32),
                        pltpu.VMEM((2, D, tf), BF16), pltpu.VMEM((2, D, tf), BF16), pltpu.VMEM((2, tf, D), BF16),
                        pltpu.SemaphoreType.DMA((3, 2)), pltpu.SemaphoreType.DMA((2,)), pltpu.SemaphoreType.DMA((2,))],
        compiler_params=_cparams(("arbitrary",)),
        name="ffn_stream",
    )(x, norm_g.reshape(1, D), w_gate, w_up, w_down, final_g.reshape(1, D))


def _in_proj_kernel(x_ref, g_ref, wt_ref, wtail_ref, o_ref, otail_ref, h_scr, *, n_tail):
    j = pl.program_id(1)

    @pl.when(j == 0)
    def _():
        x = x_ref[...]
        r = lax.rsqrt(jnp.mean(x * x, axis=-1, keepdims=True) + EPS)
        h = (x * r * g_ref[...]).astype(BF16)
        h_scr[...] = h
        tail = _dot_nt(h, wtail_ref[...])
        col = lax.broadcasted_iota(jnp.int32, tail.shape, 1)
        otail_ref[...] = jnp.where(col < n_tail, tail, 0.0)

    o_ref[...] = _dot_nt(h_scr[...], wt_ref[...])


def _in_proj(x, norm_g, w_t, *, tm=1024, tn=2048):
    L, D = x.shape
    n_cols = w_t.shape[0]
    nj = n_cols // tn
    n_tail = n_cols - nj * tn
    assert 0 < n_tail <= LANES and (nj * tn) % LANES == 0
    tm = min(tm, L)
    return pl.pallas_call(
        functools.partial(_in_proj_kernel, n_tail=n_tail),
        grid=(L // tm, nj),
        in_specs=[
            pl.BlockSpec((tm, D), lambda i, j: (i, 0)),
            pl.BlockSpec((1, D), lambda i, j: (0, 0)),
            pl.BlockSpec((tn, D), lambda i, j: (j, 0)),
            pl.BlockSpec((LANES, D), lambda i, j: (nj * tn // LANES, 0)),
        ],
        out_specs=[pl.BlockSpec((tm, tn), lambda i, j: (i, j)),
                   pl.BlockSpec((tm, LANES), lambda i, j: (i, 0))],
        out_shape=[jax.ShapeDtypeStruct((L, nj * tn), F32),
                   jax.ShapeDtypeStruct((L, LANES), F32)],
        scratch_shapes=[pltpu.VMEM((tm, D), BF16)],
        compiler_params=_cparams(("parallel", "arbitrary")),
        name="in_proj",
    )(x, norm_g.reshape(1, D), w_t, w_t)


def _out_proj_kernel(x_ref, yh_ref, yg_ref, wh_ref, wg_ref, o_ref):
    half = o_ref.shape[1] // 2
    for s in range(2):
        cols = pl.ds(s * half, half)
        o_ref[:, cols] = (x_ref[:, cols]
                          + jnp.dot(yh_ref[...], wh_ref[:, cols], preferred_element_type=F32)
                          + jnp.dot(yg_ref[...], wg_ref[:, cols], preferred_element_type=F32))


def _out_proj(x, y_hy, y_gla, w_out, *, tm=512):
    L, D = x.shape
    WH = y_hy.shape[1]
    WG = y_gla.shape[1]
    tm = min(tm, L)
    assert WH == WG and w_out.shape[0] == WH + WG
    return pl.pallas_call(
        _out_proj_kernel,
        grid=(L // tm,),
        in_specs=[
            pl.BlockSpec((tm, D), lambda i: (i, 0)),
            pl.BlockSpec((tm, WH), lambda i: (i, 0)),
            pl.BlockSpec((tm, WG), lambda i: (i, 0)),
            _single((WH, D), lambda i: (0, 0)),
            _single((WG, D), lambda i: (1, 0)),
        ],
        out_specs=pl.BlockSpec((tm, D), lambda i: (i, 0)),
        out_shape=jax.ShapeDtypeStruct((L, D), F32),
        compiler_params=_cparams(("parallel",)),
        name="out_proj",
    )(x, y_hy, y_gla, w_out, w_out)


def _filt_mlp_kernel(z_ref, w1_ref, b1_ref, f1_ref, w2_ref, b2_ref, f2_ref, w3_ref, b3_ref, f3_ref, *rest, n_cast):
    cast_src, o_ref, cast_dst = rest[:n_cast], rest[n_cast], rest[n_cast + 1:]
    dot = functools.partial(jnp.dot, precision=HIGHEST, preferred_element_type=F32)
    hid = jnp.sin(f1_ref[...] * (dot(z_ref[...], w1_ref[...]) + b1_ref[...]))
    hid = jnp.sin(f2_ref[...] * (dot(hid, w2_ref[...]) + b2_ref[...]))
    hid = jnp.sin(f3_ref[...] * (dot(hid, w3_ref[...]) + b3_ref[...]))
    hi, lo = _split_hi_lo(hid)
    o_ref[...] = jnp.concatenate([hi, hi, lo], axis=1)
    for src, dst in zip(cast_src, cast_dst):
        dst[...] = src[...].astype(BF16)


def _cast_row_blocks(shape, n):
    rows, cols = shape
    tile = 2 * SUBLANES
    block_rows = pl.cdiv(pl.cdiv(rows, n), tile) * tile
    nblocks = pl.cdiv(rows, block_rows)
    return (block_rows, cols), (lambda i: (jnp.minimum(i, nblocks - 1), 0))


def _filt_mlp(feat2, w1, b1, f1, w2, b2, f2, w3, b3, f3, *, to_cast=(), tm=512):
    L, FP = feat2.shape
    H2 = w2.shape[0]
    tm = min(tm, L)
    steps = L // tm
    full = lambda shp: pl.BlockSpec(shp, lambda i: (0, 0))
    cast_specs = [pl.BlockSpec(*_cast_row_blocks(w.shape, steps)) for w in to_cast]
    outs = pl.pallas_call(
        functools.partial(_filt_mlp_kernel, n_cast=len(to_cast)),
        grid=(steps,),
        in_specs=[pl.BlockSpec((tm, FP), lambda i: (i, 0)),
                  full((FP, H2)), full((1, H2)), full((1, H2)),
                  full((H2, H2)), full((1, H2)), full((1, H2)),
                  full((H2, H2)), full((1, H2)), full((1, H2))] + cast_specs,
        out_specs=[pl.BlockSpec((tm, 3 * H2), lambda i: (i, 0))] + cast_specs,
        out_shape=[jax.ShapeDtypeStruct((L, 3 * H2), BF16)] + [jax.ShapeDtypeStruct(w.shape, BF16) for w in to_cast],
        compiler_params=_cparams(("arbitrary",)),
        name="filt_mlp",
    )(feat2, w1, b1, f1, w2, b2, f2, w3, b3, f3, *to_cast)
    return outs[0], outs[1:]


def _dft_tables(L):
    N = 2 * L
    RA = N // RB
    NH = RA // 2
    two_pi = 2.0 * np.pi
    k1 = np.arange(RA, dtype=np.int64)

    def step1_table(NR):
        g = np.zeros((RB // 2, 2 * RA, 2 * NR), np.float64)
        for s in range(2):
            n = RB * np.arange(NR, dtype=np.int64)[None, :] + (2 * np.arange(RB // 2, dtype=np.int64) + s)[:, None]
            ang = two_pi * ((k1[None, :, None] * n[:, None, :]) % N) / N
            g[:, :RA, s * NR:(s + 1) * NR] = np.cos(ang)
            g[:, RA:, s * NR:(s + 1) * NR] = -np.sin(ang)
        return g

    phi = two_pi * ((np.arange(KB)[:, None] * np.arange(RB)[None, :]) % RB) / RB
    f3 = np.block([[np.cos(phi), np.sin(phi)], [-np.sin(phi), np.cos(phi)]])
    i1 = np.block([[np.cos(phi.T), -np.sin(phi.T)], [np.sin(phi.T), np.cos(phi.T)]])
    cast = lambda a: jnp.asarray(a.astype(np.float32)).astype(BF16)
    return dict(g_half=cast(step1_table(NH)), g_full=cast(step1_table(RA)), f3=cast(f3), i1=cast(i1))


def _fwd_step1(u_ref, g_ref, ab_ref, *, RA, NR, CB, G=2):
    def body(t, carry):
        ms = [t * G + j for j in range(G)]
        rhs = []
        for m in ms:
            ua = u_ref[pl.ds(2 * m, NR, stride=PITCH), :]
            ub = u_ref[pl.ds(2 * m + 1, NR, stride=PITCH), :]
            zz = jnp.zeros_like(ua)
            rhs.append(jnp.concatenate([jnp.concatenate([ua, zz], axis=1),
                                        jnp.concatenate([zz, ub], axis=1)], axis=0).astype(BF16))
        res = [jnp.dot(g_ref[m], r, preferred_element_type=F32) for m, r in zip(ms, rhs)]
        for m, r in zip(ms, res):
            ab_ref[pl.ds(2 * m, RA, stride=PITCH), :] = r[:RA, :CB]
            ab_ref[pl.ds(RA * PITCH + 2 * m, RA, stride=PITCH), :] = r[RA:, :CB]
            ab_ref[pl.ds(2 * m + 1, RA, stride=PITCH), :] = r[:RA, CB:]
            ab_ref[pl.ds(RA * PITCH + 2 * m + 1, RA, stride=PITCH), :] = r[RA:, CB:]
        return carry
    lax.fori_loop(0, RB // 2 // G, body, 0)


def _fwd_step3_pair(ab_ref, f3, kp, *, RA):
    cols = []
    for s in range(2):
        k1 = 2 * kp + s
        ar = ab_ref[pl.ds(pl.multiple_of(k1 * PITCH, SUBLANES), RB), :]
        ai = ab_ref[pl.ds(pl.multiple_of((RA + k1) * PITCH, SUBLANES), RB), :]
        cols.append(jnp.concatenate([ar, ai], axis=0))
    rhs = jnp.concatenate(cols, axis=1).astype(BF16)
    return jnp.dot(f3, rhs, preferred_element_type=F32)


def _for_row_blocks(nblocks, body, carry, unroll=2):
    assert nblocks >= 2
    carry = body(0, carry, True, False)
    carry = lax.fori_loop(1, nblocks - 1, lambda n1, c: body(n1, c, False, False), carry,
                          unroll=max(1, min(unroll, nblocks - 2)))
    return body(nblocks - 1, carry, False, True)


def _alt_sign(shape):
    rows = lax.broadcasted_iota(jnp.int32, shape, 0)
    return jnp.where(rows % 2 == 0, 1.0, -1.0).astype(F32)


def _filt_fft_kernel(hid_ref, w4_ref, dl_ref, g_ref, f3_ref, kh_ref, kn_ref, u_scr, ab_scr, *, L, CB, G):
    N = 2 * L
    RA = N // RB
    NH = RA // 2
    f3 = f3_ref[...]
    inv_lm1 = 1.0 / (L - 1)

    row = lax.broadcasted_iota(jnp.int32, (RB, CB), 0)
    arg_base = -(row.astype(F32) * inv_lm1) * dl_ref[...]
    arg_step = -(RB * inv_lm1) * dl_ref[...]
    arg_end = -(L * inv_lm1) * dl_ref[...]

    def fill(n1, carry, first, last):
        s_abs, s_alt = carry
        r0 = pl.multiple_of(n1 * RB, RB)
        h2 = jnp.dot(hid_ref[pl.ds(r0, RB), :], w4_ref[...], preferred_element_type=F32)
        arg = arg_base + jnp.asarray(n1, F32) * arg_step
        hf = h2[:, :CB] * jnp.exp(arg)
        hb = h2[:, CB:] * jnp.exp(arg_end - arg)
        if first:
            hb = jnp.where(row == 0, 0.0, hb)
        u_scr[pl.ds(pl.multiple_of(n1 * PITCH, SUBLANES), RB), :] = hf
        u_scr[pl.ds(pl.multiple_of((NH + n1) * PITCH, SUBLANES), RB), :] = hb
        return s_abs + (jnp.abs(hf) + jnp.abs(hb)), s_alt + (hf + hb)
    zero = jnp.zeros((RB, CB), F32)
    s_abs, s_alt = _for_row_blocks(NH, fill, (zero, zero), unroll=8)
    inv_l1 = 1.0 / jnp.sum(s_abs, axis=0, keepdims=True)
    kn = jnp.sum(s_alt * _alt_sign((RB, CB)), axis=0, keepdims=True) * inv_l1 * (1.0 / N)
    kn_ref[...] = jnp.broadcast_to(kn, kn_ref.shape)

    _fwd_step1(u_scr, g_ref, ab_scr, RA=RA, NR=RA, CB=CB, G=G)

    scale = inv_l1 * (2.0 / N)
    scale2 = jnp.concatenate([scale, scale], axis=1)
    row2 = lax.broadcasted_iota(jnp.int32, (KB, 2 * CB), 0)
    lane2 = lax.broadcasted_iota(jnp.int32, (KB, 2 * CB), 1)
    dc_pos = jnp.logical_and(row2 == 0, lane2 < CB)

    def emit(t, carry):
        kps = [t * G + j for j in range(G)]
        xs = [_fwd_step3_pair(ab_scr, f3, kp, RA=RA) for kp in kps]
        for kp, x in zip(kps, xs):
            kr = x[:KB] * scale2
            ki = x[KB:] * scale2
            dc = jnp.logical_and(dc_pos, kp == 0)
            kr = jnp.where(dc, 0.5 * kr, kr)
            ki = jnp.where(dc, 0.5 * ki, ki)
            kh_ref[pl.ds(pl.multiple_of(kp * RB, RB), RB), :] = jnp.concatenate([kr, ki], axis=0).astype(kh_ref.dtype)
        return carry
    lax.fori_loop(0, RA // 2 // G, emit, 0)


def _filt_fft(hid2, w4bd, deltas, g_tab, f3_tab, *, L, CB=LANES, G=16):
    C = deltas.shape[1]
    H2 = hid2.shape[1]
    N = 2 * L
    RA = N // RB
    nblk = C // CB
    G = min(G, RA // 2)
    return pl.pallas_call(
        functools.partial(_filt_fft_kernel, L=L, CB=CB, G=G),
        grid=(nblk,),
        in_specs=[
            _single((L, H2), lambda c: (0, 0)),
            pl.BlockSpec((H2, 2 * CB), lambda c: (0, c)),
            pl.BlockSpec((1, CB), lambda c: (0, c)),
            _single(g_tab.shape, lambda c: (0, 0, 0)),
            _single(f3_tab.shape, lambda c: (0, 0)),
        ],
        out_specs=[pl.BlockSpec((RA // 2 * RB, 2 * CB), lambda c: (0, c)),
                   pl.BlockSpec((SUBLANES, CB), lambda c: (0, c))],
        out_shape=[jax.ShapeDtypeStruct((RA // 2 * RB, 2 * C), BF16),
                   jax.ShapeDtypeStruct((SUBLANES, C), F32)],
        scratch_shapes=[pltpu.VMEM((RA * PITCH, CB), F32),
                        pltpu.VMEM((2 * RA * PITCH, CB), F32)],
        compiler_params=_cparams(("arbitrary",)),
        name="filt_fft",
    )(hid2, w4bd, deltas, g_tab, f3_tab)


def _short_conv_rows(p_ref, w_ref, b_ref, n1, *, first, last):
    r0 = pl.multiple_of(n1 * RB, RB)
    cur = p_ref[pl.ds(r0, RB), :]
    rows = lax.broadcasted_iota(jnp.int32, cur.shape, 0)
    if first:
        up = jnp.where(rows == 0, 0.0, pltpu.roll(cur, 1, axis=0))
    else:
        up = p_ref[pl.ds(r0 - 1, RB), :]
    if last:
        dn = jnp.where(rows == RB - 1, 0.0, pltpu.roll(cur, RB - 1, axis=0))
    else:
        dn = p_ref[pl.ds(r0 + 1, RB), :]
    w = w_ref[...]
    return b_ref[...] + up * w[0:1] + cur * w[1:2] + dn * w[2:3]


def _hy_conv_kernel(p_hbm, w0_ref, w1_ref, wv_ref, b0_ref, b1_ref, bv_ref,
                    kh_ref, kn_ref, skip_ref, og_ref, g_ref, f3_ref, i1_ref,
                    o_ref, u_scr, ab_scr, z_scr, p_scr, p_sem, *, L, CB, G):
    N = 2 * L
    RA = N // RB
    NH = RA // 2
    f3 = f3_ref[...]
    i1 = i1_ref[...]
    sign = _alt_sign((RB, CB))
    c = pl.program_id(0)
    nblk = pl.num_programs(0)
    p0_ref, p1_ref, pv_ref = p_scr.at[0], p_scr.at[1], p_scr.at[2]

    def fetch(group, blk):
        col = pl.multiple_of((group * nblk + blk) * CB, CB)
        return pltpu.make_async_copy(p_hbm.at[:, pl.ds(col, CB)], p_scr.at[group], p_sem.at[group])

    @pl.when(c == 0)
    def _():
        fetch(1, c).start()
        fetch(2, c).start()

    fetch(0, c).start()
    fetch(1, c).wait()
    fetch(2, c).wait()

    def fill(n1, s_alt, first, last):
        z = (_short_conv_rows(pv_ref, wv_ref, bv_ref, n1, first=first, last=last)
             * _short_conv_rows(p1_ref, w1_ref, b1_ref, n1, first=first, last=last))
        u_scr[pl.ds(pl.multiple_of(n1 * PITCH, SUBLANES), RB), :] = z
        z_scr[pl.ds(pl.multiple_of(n1 * RB, RB), RB), :] = z
        return s_alt + z
    s_alt = _for_row_blocks(NH, fill, jnp.zeros((RB, CB), F32))

    @pl.when(c + 1 < nblk)
    def _():
        fetch(1, c + 1).start()
        fetch(2, c + 1).start()

    z_nyq = jnp.sum(s_alt * sign, axis=0, keepdims=True)
    y_nyq = z_nyq * kn_ref[0:1, :]

    _fwd_step1(u_scr, g_ref, ab_scr, RA=RA, NR=NH, CB=CB, G=G)

    GM = max(G // 2, 1)

    def mid(t, carry):
        kps = [t * GM + j for j in range(GM)]
        xs = [_fwd_step3_pair(ab_scr, f3, kp, RA=RA) for kp in kps]
        khs = [kh_ref[pl.ds(pl.multiple_of(kp * RB, RB), RB), :].astype(F32) for kp in kps]
        bs = []
        for x, kh in zip(xs, khs):
            xr, xi, kr, ki = x[:KB], x[KB:], kh[:KB], kh[KB:]
            y = jnp.concatenate([xr * kr - xi * ki, xr * ki + xi * kr], axis=0).astype(BF16)
            bs.append(jnp.dot(i1, y, preferred_element_type=F32))
        for kp, b in zip(kps, bs):
            for s in range(2):
                k1 = 2 * kp + s
                ab_scr[pl.ds(pl.multiple_of(k1 * PITCH, SUBLANES), RB), :] = b[:RB, s * CB:(s + 1) * CB]
                ab_scr[pl.ds(pl.multiple_of((RA + k1) * PITCH, SUBLANES), RB), :] = b[RB:, s * CB:(s + 1) * CB]
        return carry
    lax.fori_loop(0, RA // 2 // GM, mid, 0)

    def last(t, carry):
        ms = [t * G + j for j in range(G)]
        rhs = [jnp.concatenate(
            [jnp.concatenate([ab_scr[pl.ds(2 * m + s, RA, stride=PITCH), :],
                              ab_scr[pl.ds(RA * PITCH + 2 * m + s, RA, stride=PITCH), :]], axis=0)
             for s in range(2)], axis=1).astype(BF16) for m in ms]
        ys = [_dot_tn(g_ref[m], r) for m, r in zip(ms, rhs)]
        for m, y in zip(ms, ys):
            u_scr[pl.ds(2 * m, NH, stride=PITCH), :] = y[:NH, :CB]
            u_scr[pl.ds(2 * m + 1, NH, stride=PITCH), :] = y[NH:, CB:]
        return carry
    lax.fori_loop(0, RB // 2 // G, last, 0)

    nyq_rows = sign * y_nyq
    fetch(0, c).wait()

    def finish(n1, carry, first, last):
        z = z_scr[pl.ds(pl.multiple_of(n1 * RB, RB), RB), :]
        x0 = _short_conv_rows(p0_ref, w0_ref, b0_ref, n1, first=first, last=last)
        conv = u_scr[pl.ds(pl.multiple_of(n1 * PITCH, SUBLANES), RB), :] + nyq_rows
        y = (conv + z * skip_ref[...]) * x0
        y = y * lax.rsqrt(jnp.mean(y * y, axis=-1, keepdims=True) + EPS) * og_ref[...]
        o_ref[pl.ds(pl.multiple_of(n1 * RB, RB), RB), :] = y.astype(o_ref.dtype)
        return carry
    _for_row_blocks(NH, finish, 0, unroll=4)


def _hy_conv(p, conv_w, conv_b, khat, knyq, skip, out_g, tabs, *, L, CB=LANES, G=16):
    C = skip.shape[1]
    assert CB == C // HY_GROUPS, "one channel block must be exactly one norm group"
    N = 2 * L
    RA = N // RB
    NH = RA // 2
    nblk = C // CB
    G = min(G, RA // 2)
    g_tab, f3_tab, i1_tab = tabs["g_half"], tabs["f3"], tabs["i1"]
    col = lambda off: (lambda c: (0, off * nblk + c))
    wspec = lambda off: pl.BlockSpec((3, CB), col(off))
    bspec = lambda off: pl.BlockSpec((1, CB), col(off))
    return pl.pallas_call(
        functools.partial(_hy_conv_kernel, L=L, CB=CB, G=G),
        grid=(nblk,),
        in_specs=[pl.BlockSpec(memory_space=pl.ANY),
                  wspec(0), wspec(1), wspec(2), bspec(0), bspec(1), bspec(2),
                  pl.BlockSpec((RA // 2 * RB, 2 * CB), col(0)),
                  pl.BlockSpec((SUBLANES, CB), col(0)),
                  bspec(0), bspec(0),
                  _single(g_tab.shape, lambda c: (0, 0, 0)),
                  _single(f3_tab.shape, lambda c: (0, 0)),
                  _single(i1_tab.shape, lambda c: (0, 0))],
        out_specs=pl.BlockSpec((L, CB), col(0)),
        out_shape=jax.ShapeDtypeStruct((L, C), BF16),
        scratch_shapes=[pltpu.VMEM((NH * PITCH, CB), F32),
                        pltpu.VMEM((2 * RA * PITCH, CB), F32),
                        pltpu.VMEM((L, CB), F32),
                        pltpu.VMEM((3, L, CB), F32),
                        pltpu.SemaphoreType.DMA((3,))],
        compiler_params=_cparams(("arbitrary",)),
        name="hy_conv",
    )(p, conv_w, conv_w, conv_w, conv_b, conv_b, conv_b, khat, knyq, skip, out_g,
      g_tab, f3_tab, i1_tab)


def _log_sigmoid(x):
    return jnp.minimum(x, 0.0) - jnp.log(1.0 + jnp.exp(-jnp.abs(x)))


def _gla_scan_blocks(dirs, *, TB):
    nch = TB // CHUNK
    rows = lambda c: slice(c * CHUNK, (c + 1) * CHUNK)
    units = [(d, c) for c in range(nch) for d in dirs]
    rr = lax.broadcasted_iota(jnp.int32, (CHUNK, CHUNK), 0)
    cc = lax.broadcasted_iota(jnp.int32, (CHUNK, CHUNK), 1)
    scale = GLA_DK ** -0.5

    for d in dirs:
        lr_hi, lr_lo = _split_hi_lo(d["lr"][...])
        gate_in = jnp.dot(jnp.concatenate([lr_hi, lr_hi, lr_lo], axis=1), d["wg"],
                          preferred_element_type=F32) + d["bg"]
        g = _log_sigmoid(gate_in) * (1.0 / GATE_TEMP)
        g_hi = g.astype(BF16)
        g_mid, g_lo = _split_hi_lo(g - g_hi.astype(F32))
        d["parts"][...] = jnp.concatenate([g_hi, g_mid, g_lo], axis=1)
        d["csum"] = jnp.where((cc >= rr) if d["reverse"] else (cc <= rr), 1.0, 0.0).astype(BF16)
        d["mask"] = (cc > rr) if d["reverse"] else (cc <= rr)
        d["edge"] = 0 if d["reverse"] else CHUNK - 1

    for d, c in units:
        b3 = jnp.dot(d["csum"], d["parts"][rows(c), :], preferred_element_type=F32)
        d["b"][rows(c), :] = b3[:, :GLA_DK] + b3[:, GLA_DK:2 * GLA_DK] + b3[:, 2 * GLA_DK:]

    decay = {}
    for d, c in units:
        b = d["b"][rows(c), :]
        b_edge = b[d["edge"]:d["edge"] + 1]
        k = d["k"][rows(c), :]
        d["qks"][0, rows(c), :] = (d["q"][rows(c), :] * scale * jnp.exp(b)).astype(BF16)
        d["qks"][1, rows(c), :] = (k * jnp.exp(-b)).astype(BF16)
        d["qks"][2, rows(c), :] = (k * jnp.exp(b_edge - b)).astype(BF16)
        d["vb"][rows(c), :] = d["v"][rows(c), :].astype(BF16)
        decay[(id(d), c)] = jnp.exp(b_edge)

    for d, c in units:
        att = _dot_nt(d["qks"][0, rows(c), :], d["qks"][1, rows(c), :])
        d["att"][rows(c), :] = jnp.where(d["mask"], att, 0.0).astype(BF16)

    for d, c in units:
        d["o"][rows(c), :] = jnp.dot(d["att"][rows(c), :], d["vb"][rows(c), :], preferred_element_type=F32)

    for d, c in units:
        d["kv"][c * GLA_DV:(c + 1) * GLA_DV, :] = _dot_tn(d["vb"][rows(c), :], d["qks"][2, rows(c), :])

    state = {id(d): d["s"][...] for d in dirs}
    for step in range(nch):
        for d in dirs:
            c = nch - 1 - step if d["reverse"] else step
            s_t = state[id(d)]
            d["o"][rows(c), :] += _dot_nt(d["qks"][0, rows(c), :], s_t.astype(BF16))
            state[id(d)] = s_t * decay[(id(d), c)] + d["kv"][c * GLA_DV:(c + 1) * GLA_DV, :]
    for d in dirs:
        d["s"][...] = state[id(d)]


def _gla_kernel(qf_ref, kf_ref, vf_ref, lf_ref, rf_ref, qb_ref, kb_ref, vb_ref, lb_ref, rb_ref,
                wg_ref, bg_ref, og_ref, o_ref,
                s_scr, parts_scr, b_scr, qks_scr, vb_scr, att_scr, kv_scr, ob_scr, half_scr, *, TB):
    s = pl.program_id(1)
    nb = pl.num_programs(1)

    @pl.when(s == 0)
    def _():
        s_scr[...] = jnp.zeros_like(s_scr)

    def direction(i, q, k, v, lr):
        return dict(q=q, k=k, v=v, lr=lr, wg=wg_ref[i], bg=bg_ref[i], reverse=bool(i), s=s_scr.at[i],
                    parts=parts_scr.at[i], b=b_scr.at[i], qks=qks_scr.at[i], vb=vb_scr.at[i],
                    att=att_scr.at[i], kv=kv_scr.at[i], o=ob_scr.at[i])
    _gla_scan_blocks([direction(0, qf_ref, kf_ref, vf_ref, lf_ref),
                      direction(1, qb_ref, kb_ref, vb_ref, lb_ref)], TB=TB)

    first_half = s < nb // 2
    for d, blk, r_ref in ((0, s, rf_ref), (1, nb - 1 - s, rb_ref)):
        rows = pl.ds(pl.multiple_of(blk * TB, TB), TB)

        @pl.when(first_half)
        def _():
            half_scr[rows, :] = ob_scr[d]

        @pl.when(jnp.logical_not(first_half))
        def _():
            tot = half_scr[rows, :] + ob_scr[d]
            tot = tot * lax.rsqrt(jnp.mean(tot * tot, axis=-1, keepdims=True) + EPS) * og_ref[...]
            o_ref[rows, :] = (tot * jax.nn.silu(r_ref[...])).astype(o_ref.dtype)


def _gla(p, p_lr, w_gate, b_gate, out_g, *, L, col0, TB=1024):
    TB = min(TB, L)
    nb = L // TB
    assert nb % 2 == 0, "both scan directions must meet between two blocks"
    nch = TB // CHUNK
    qb = col0 // GLA_DK
    kb = qb + GLA_HEADS
    vb = (col0 + 2 * GLA_KW) // GLA_DV
    rb = vb + GLA_HEADS
    fwd = lambda s: s
    bwd = lambda s: nb - 1 - s

    def operands(blk):
        return [pl.BlockSpec((TB, GLA_DK), lambda h, s: (blk(s), qb + h)),
                pl.BlockSpec((TB, GLA_DK), lambda h, s: (blk(s), kb + h)),
                pl.BlockSpec((TB, GLA_DV), lambda h, s: (blk(s), vb + h)),
                pl.BlockSpec((TB, LANES), lambda h, s: (blk(s), 0)),
                pl.BlockSpec((TB, GLA_DV), lambda h, s: (blk(s), rb + h))]

    return pl.pallas_call(
        functools.partial(_gla_kernel, TB=TB),
        grid=(GLA_HEADS, nb),
        in_specs=operands(fwd) + operands(bwd) + [
            pl.BlockSpec((2, 3 * LANES, GLA_DK), lambda h, s: (0, 0, h)),
            pl.BlockSpec((2, 1, GLA_DK), lambda h, s: (0, 0, h)),
            pl.BlockSpec((1, GLA_DV), lambda h, s: (0, h)),
        ],
        out_specs=pl.BlockSpec((L, GLA_DV), lambda h, s: (0, h)),
        out_shape=jax.ShapeDtypeStruct((L, GLA_VW), BF16),
        scratch_shapes=[pltpu.VMEM((2, GLA_DV, GLA_DK), F32),
                        pltpu.VMEM((2, TB, 3 * GLA_DK), BF16),
                        pltpu.VMEM((2, TB, GLA_DK), F32),
                        pltpu.VMEM((2, 3, TB, GLA_DK), BF16),
                        pltpu.VMEM((2, TB, GLA_DV), BF16),
                        pltpu.VMEM((2, TB, CHUNK), BF16),
                        pltpu.VMEM((2, nch * GLA_DV, GLA_DK), F32),
                        pltpu.VMEM((2, TB, GLA_DV), F32),
                        pltpu.VMEM((L, GLA_DV), F32)],
        compiler_params=_cparams(("arbitrary", "arbitrary")),
        name="gla",
    )(*([p, p, p, p_lr, p] * 2), w_gate, b_gate, out_g)


def _filter_features(L):
    t = np.linspace(0.0, 1.0, L)[:, None]
    bands = (FILTER_EMB - 1) // 2
    freqs = np.linspace(1e-4, bands - 1, bands)[None, :]
    ang = (2.0 * np.pi / L) * np.arange(L)[:, None] * freqs
    feat = np.concatenate([t, np.cos(ang), -np.sin(ang)], axis=-1)
    feat_rev = np.roll(feat[::-1], 1, axis=0)
    both = np.zeros((L, LANES), np.float32)
    both[:, :FILTER_EMB] = feat
    both[:, FILTER_EMB:2 * FILTER_EMB] = feat_rev
    return jnp.asarray(both)


def _twice(w, rows_out):
    r, c = w.shape
    zz = jnp.zeros_like(w)
    both = jnp.concatenate([jnp.concatenate([w, zz], axis=1), jnp.concatenate([zz, w], axis=1)], axis=0)
    return jnp.pad(both, ((0, rows_out - 2 * r), (0, 0)))


def _filter_deltas():
    min_decay = math.log(DECAY_TARGET) / LONG_DECAY_PCT
    max_decay = math.log(DECAY_TARGET) / SHORT_DECAY_PCT
    return jnp.abs(jnp.linspace(min_decay, max_decay, HY_WIDTH, dtype=F32)).reshape(1, HY_WIDTH)


def _block_diag_w4(w4):
    H = w4.shape[0]
    nblk = HY_WIDTH // LANES
    wf = w4[:, :HY_WIDTH].reshape(H, nblk, LANES)
    wb = w4[:, HY_WIDTH:].reshape(H, nblk, LANES)
    zz = jnp.zeros_like(wf)
    top = jnp.concatenate([wf, zz], axis=2)
    bot = jnp.concatenate([zz, wb], axis=2)
    return jnp.concatenate([top, bot], axis=0).reshape(2 * H, nblk * 2 * LANES)


def kernel(x, ffn1_norm, ffn1_w_gate, ffn1_w_up, ffn1_w_down, mix_norm, w_in, hy_conv_w, hy_conv_b, flt_w1, flt_b1, flt_f1, flt_w2, flt_b2, flt_f2, flt_w3, flt_b3, flt_f3, flt_w4, hy_skip, hy_out_norm, gla_w_a2_f, gla_b_a_f, gla_w_a2_b, gla_b_a_b, gla_out_norm, w_out, ffn2_norm, ffn2_w_gate, ffn2_w_up, ffn2_w_down, final_norm):
    B, L, D = x.shape
    depth = ffn1_norm.shape[0]
    tabs = _dft_tables(L)
    feat2 = _filter_features(L)
    deltas = _filter_deltas()
    gla_col0 = 3 * HY_WIDTH
    H2 = 2 * FILTER_HIDDEN
    both = lambda a: jnp.concatenate([a, a]).reshape(1, H2)
    filt = []
    for l in range(depth):
        early = [jnp.swapaxes(w_in[l], 0, 1)]
        if l == 0:
            early += [ffn1_w_gate[0], ffn1_w_up[0], ffn1_w_down[0]]
        hid3, cast = _filt_mlp(feat2, _twice(flt_w1[l], LANES), both(flt_b1[l]), both(flt_f1[l]),
                               _twice(flt_w2[l], H2), both(flt_b2[l]), both(flt_f2[l]),
                               _twice(flt_w3[l], H2), both(flt_b3[l]), both(flt_f3[l]), to_cast=early)
        w4cat = _three_pass_rows(_block_diag_w4(flt_w4[l]))
        khat, knyq = _filt_fft(hid3, w4cat, deltas, tabs["g_full"], tabs["f3"], L=L)
        filt.append((khat, knyq, cast))

    outs = []
    for bi in range(B):
        xb = x[bi]
        w1 = filt[0][2][1:]
        for l in range(depth):
            last = l == depth - 1
            khat, knyq, (w_in_t, *_) = filt[l]
            later = [ffn2_w_gate[l], ffn2_w_up[l], ffn2_w_down[l], w_out[l]]
            xb, (w2_gate, w2_up, w2_down, w_out_bf) = _ffn(xb, ffn1_norm[l], *w1, final_norm, final_norm=False,
                                                           to_cast=later)
            p, p_lr = _in_proj(xb, mix_norm[l], w_in_t)
            y_hy = _hy_conv(p, hy_conv_w[l], hy_conv_b[l].reshape(1, -1), khat, knyq,
                            hy_skip[l].reshape(1, -1), hy_out_norm[l].reshape(1, -1), tabs, L=L)

            w_gate = jnp.stack([jnp.pad(gla_w_a2_f[l], ((0, LANES - GATE_RANK), (0, 0))),
                                jnp.pad(gla_w_a2_b[l], ((GATE_RANK, LANES - 2 * GATE_RANK), (0, 0)))])
            w_gate = _three_pass_rows(w_gate)
            b_gate = jnp.stack([gla_b_a_f[l], gla_b_a_b[l]]).reshape(2, 1, GLA_KW)
            y_gla = _gla(p, p_lr, w_gate, b_gate, gla_out_norm[l].reshape(1, -1), L=L, col0=gla_col0)

            xb = _out_proj(xb, y_hy, y_gla, w_out_bf)
            nxt = [] if last else [ffn1_w_gate[l + 1], ffn1_w_up[l + 1], ffn1_w_down[l + 1]]
            if nxt:
                xb, w1 = _ffn(xb, ffn2_norm[l], w2_gate, w2_up, w2_down, final_norm, final_norm=last, to_cast=nxt)
            else:
                xb = _ffn_stream(xb, ffn2_norm[l], w2_gate, w2_up, w2_down, final_norm, final_norm=last)
        outs.append(xb)
    return jnp.stack(outs)
```

```python
import functools
import math

import numpy as np
import jax
import jax.numpy as jnp
from jax import lax
from jax.experimental import pallas as pl
from jax.experimental.pallas import tpu as pltpu

F32 = jnp.float32
BF16 = jnp.bfloat16
HIGHEST = lax.Precision.HIGHEST

EPS = 1e-6
HY_WIDTH = 1024
HY_GROUPS = 8
FILTER_EMB = 33
FILTER_HIDDEN = 64
SHORT_DECAY_PCT = 0.3
LONG_DECAY_PCT = 1.5
DECAY_TARGET = 1e-2
GLA_HEADS = 4
GLA_DK = 128
GLA_DV = 256
GLA_KW = GLA_HEADS * GLA_DK
GLA_VW = GLA_HEADS * GLA_DV
GATE_RANK = 16
GATE_TEMP = 16.0
CHUNK = 64

LANES = 128
SUBLANES = 8
VMEM_LIMIT = 60 * 1024 * 1024
FFN_SLAB = 128
RB = 128
KB = RB // 2
PITCH = RB + SUBLANES


def _cparams(sem):
    return pltpu.CompilerParams(dimension_semantics=sem, vmem_limit_bytes=VMEM_LIMIT)


def _single(block_shape, index_map):
    return pl.BlockSpec(block_shape, index_map, pipeline_mode=pl.Buffered(1))


def _split_hi_lo(x):
    hi = x.astype(BF16)
    return hi, (x - hi.astype(F32)).astype(BF16)


def _three_pass_rows(w):
    w_hi, w_lo = _split_hi_lo(w)
    return jnp.concatenate([w_hi, w_lo, w_hi], axis=-2)


def _dot_nt(a, b):
    return lax.dot_general(a, b, (((1,), (1,)), ((), ())), preferred_element_type=F32)


def _dot_tn(a, b):
    return lax.dot_general(a, b, (((0,), (0,)), ((), ())), preferred_element_type=F32)


def _ffn_kernel(x_ref, g_ref, wg_ref, wu_ref, wd_ref, fg_ref, *rest, final_norm, n_cast):
    cast_src, o_ref, cast_dst, h_scr = rest[:n_cast], rest[n_cast], rest[n_cast + 1:-1], rest[-1]
    j = pl.program_id(1)

    slab = min(FFN_SLAB, x_ref.shape[0])
    nslab = x_ref.shape[0] // slab

    @pl.when(j == 0)
    def _():
        def norm_rows(t, carry):
            rows = pl.ds(pl.multiple_of(t * slab, slab), slab)
            x = x_ref[rows, :]
            r = lax.rsqrt(jnp.mean(x * x, axis=-1, keepdims=True) + EPS)
            h_scr[rows, :] = (x * r * g_ref[...]).astype(BF16)
            o_ref[rows, :] = jnp.zeros((slab, x_ref.shape[1]), F32)
            return carry
        lax.fori_loop(0, nslab, norm_rows, 0)

    h = h_scr[...]
    half = wg_ref.shape[1] // 2
    for s in range(2):
        cols = pl.ds(s * half, half)
        gate = jnp.dot(h, wg_ref[:, cols], preferred_element_type=F32)
        up = jnp.dot(h, wu_ref[:, cols], preferred_element_type=F32)
        a = (jax.nn.silu(gate) * up).astype(BF16)
        o_ref[...] += jnp.dot(a, wd_ref[cols, :], preferred_element_type=F32)

    for src, dst in zip(cast_src, cast_dst):
        dst[...] = src[...].astype(BF16)

    @pl.when(j == pl.num_programs(1) - 1)
    def _():
        def finish_rows(t, carry):
            rows = pl.ds(pl.multiple_of(t * slab, slab), slab)
            y = x_ref[rows, :] + 0.5 * o_ref[rows, :]
            if final_norm:
                r = lax.rsqrt(jnp.mean(y * y, axis=-1, keepdims=True) + EPS)
                y = y * r * fg_ref[...]
            o_ref[rows, :] = y
            return carry
        lax.fori_loop(0, nslab, finish_rows, 0)


def _cast_blocking(shape, ni, nj):
    rows, cols = shape
    assert rows % ni == 0 and (rows // ni) % SUBLANES == 0 and cols % LANES == 0
    ncol = max(d for d in range(1, nj + 1) if (cols // LANES) % d == 0)
    return (rows // ni, cols // ncol), (lambda i, j: (i, jnp.minimum(j, ncol - 1)))


def _ffn(x, norm_g, w_gate, w_up, w_down, final_g, *, final_norm, to_cast=(), tm=1024, tf=512):
    L, D = x.shape
    DF = w_gate.shape[1]
    tm = min(tm, L)
    tf = min(tf, DF)
    ni, nj = L // tm, DF // tf
    blockings = [_cast_blocking(w.shape, ni, nj) for w in to_cast]
    cast_specs = [pl.BlockSpec(blk, imap) for blk, imap in blockings]
    outs = pl.pallas_call(
        functools.partial(_ffn_kernel, final_norm=final_norm, n_cast=len(to_cast)),
        grid=(ni, nj),
        in_specs=[
            pl.BlockSpec((tm, D), lambda i, j: (i, 0)),
            pl.BlockSpec((1, D), lambda i, j: (0, 0)),
            pl.BlockSpec((D, tf), lambda i, j: (0, j)),
            pl.BlockSpec((D, tf), lambda i, j: (0, j)),
            pl.BlockSpec((tf, D), lambda i, j: (j, 0)),
            pl.BlockSpec((1, D), lambda i, j: (0, 0)),
        ] + cast_specs,
        out_specs=[pl.BlockSpec((tm, D), lambda i, j: (i, 0))] + cast_specs,
        out_shape=[jax.ShapeDtypeStruct((L, D), F32)] + [jax.ShapeDtypeStruct(w.shape, BF16) for w in to_cast],
        scratch_shapes=[pltpu.VMEM((tm, D), BF16)],
        compiler_params=_cparams(("arbitrary", "arbitrary")),
        name="ffn",
    )(x, norm_g.reshape(1, D), w_gate, w_up, w_down, final_g.reshape(1, D), *to_cast)
    return outs[0], outs[1:]


def _ffn_stream_kernel(x_hbm, g_ref, wg_hbm, wu_hbm, wd_hbm, fg_ref, o_hbm,
                       x_buf, h_buf, acc, wg_buf, wu_buf, wd_buf, w_sem, x_sem, o_sem, *, final_norm, tm, tf, nj):
    i = pl.program_id(0)
    ni = pl.num_programs(0)
    total = ni * nj
    cur = i % 2
    nxt = 1 - cur
    slab = min(FFN_SLAB, tm)
    nslab = tm // slab
    lead = 3
    assert nj >= nslab + lead, "the next tile's slabs are spread over this tile's weight steps"

    def fetch_w(j, slot):
        col = pl.multiple_of(j * tf, tf)
        return (pltpu.make_async_copy(wg_hbm.at[:, pl.ds(col, tf)], wg_buf.at[slot], w_sem.at[0, slot]),
                pltpu.make_async_copy(wu_hbm.at[:, pl.ds(col, tf)], wu_buf.at[slot], w_sem.at[1, slot]),
                pltpu.make_async_copy(wd_hbm.at[pl.ds(col, tf), :], wd_buf.at[slot], w_sem.at[2, slot]))

    def tile_rows(row_tile):
        return pl.ds(pl.multiple_of(row_tile * tm, tm), tm)

    def fetch_x(row_tile, slot):
        return pltpu.make_async_copy(x_hbm.at[tile_rows(row_tile), :], x_buf.at[slot], x_sem.at[slot])

    def send_y(row_tile, slot):
        return pltpu.make_async_copy(x_buf.at[slot], o_hbm.at[tile_rows(row_tile), :], o_sem.at[slot])

    def norm_slab(t, src_slot, dst_slot):
        rows = pl.ds(pl.multiple_of(t * slab, slab), slab)
        x = x_buf[src_slot, rows, :]
        r = lax.rsqrt(jnp.mean(x * x, axis=-1, keepdims=True) + EPS)
        h_buf[dst_slot, rows, :] = (x * r * g_ref[...]).astype(BF16)

    @pl.when(i == 0)
    def _():
        for cp in fetch_w(0, 0):
            cp.start()
        fetch_x(0, 0).start()
        fetch_x(0, 0).wait()

        def first_rows(t, carry):
            norm_slab(t, 0, 0)
            return carry
        lax.fori_loop(0, nslab, first_rows, 0)

    def tile(j, first):
        step = i * nj + j
        slot = step % 2
        for cp in fetch_w(j, slot):
            cp.wait()

        @pl.when(step + 1 < total)
        def _():
            for cp in fetch_w((j + 1) % nj, 1 - slot):
                cp.start()

        if not first:
            @pl.when(jnp.logical_and(j == 1, i > 0))
            def _():
                send_y(i - 1, nxt).wait()

            @pl.when(jnp.logical_and(j == 1, i + 1 < ni))
            def _():
                fetch_x(i + 1, nxt).start()

            @pl.when(jnp.logical_and(j == lead - 1, i + 1 < ni))
            def _():
                fetch_x(i + 1, nxt).wait()

        h = h_buf[cur]
        for s in range(2):
            cols = pl.ds(s * (tf // 2), tf // 2)
            gate = jnp.dot(h, wg_buf[slot, :, cols], preferred_element_type=F32)
            up = jnp.dot(h, wu_buf[slot, :, cols], preferred_element_type=F32)
            a = (jax.nn.silu(gate) * up).astype(BF16)
            part = jnp.dot(a, wd_buf[slot, cols, :], preferred_element_type=F32)
            if first and s == 0:
                acc[...] = part
            else:
                acc[...] += part
        if not first:
            src = jnp.where(j < lead, cur, nxt)
            norm_slab(jnp.clip(j - lead, 0, nslab - 1), src, nxt)

    tile(0, True)

    def rest(j, carry):
        tile(j, False)
        return carry
    lax.fori_loop(1, nj, rest, 0)

    def finish_rows(t, carry):
        rows = pl.ds(pl.multiple_of(t * slab, slab), slab)
        y = x_buf[cur, rows, :] + 0.5 * acc[rows, :]
        if final_norm:
            r = lax.rsqrt(jnp.mean(y * y, axis=-1, keepdims=True) + EPS)
            y = y * r * fg_ref[...]
        x_buf[cur, rows, :] = y
        return carry
    lax.fori_loop(0, nslab, finish_rows, 0)
    send_y(i, cur).start()

    @pl.when(i == ni - 1)
    def _():
        send_y(i, cur).wait()


def _ffn_stream(x, norm_g, w_gate, w_up, w_down, final_g, *, final_norm, tm=1024, tf=512):
    L, D = x.shape
    DF = w_gate.shape[1]
    tm = min(tm, L)
    tf = min(tf, DF)
    nj = DF // tf
    hbm = pl.BlockSpec(memory_space=pl.ANY)
    return pl.pallas_call(
        functools.partial(_ffn_stream_kernel, final_norm=final_norm, tm=tm, tf=tf, nj=nj),
        grid=(L // tm,),
        in_specs=[hbm, pl.BlockSpec((1, D), lambda i: (0, 0)), hbm, hbm, hbm, pl.BlockSpec((1, D), lambda i: (0, 0))],
        out_specs=hbm,
        out_shape=jax.ShapeDtypeStruct((L, D), F32),
        scratch_shapes=[pltpu.VMEM((2, tm, D), F32), pltpu.VMEM((2, tm, D), BF16), pltpu.VMEM((tm, D), F32),
                        pltpu.VMEM((2, D, tf), BF16), pltpu.VMEM((2, D, tf), BF16), pltpu.VMEM((2, tf, D), BF16),
                        pltpu.SemaphoreType.DMA((3, 2)), pltpu.SemaphoreType.DMA((2,)), pltpu.SemaphoreType.DMA((2,))],
        compiler_params=_cparams(("arbitrary",)),
        name="ffn_stream",
    )(x, norm_g.reshape(1, D), w_gate, w_up, w_down, final_g.reshape(1, D))


def _in_proj_kernel(x_ref, g_ref, wt_ref, wtail_ref, o_ref, otail_ref, h_scr, *, n_tail):
    j = pl.program_id(1)

    @pl.when(j == 0)
    def _():
        x = x_ref[...]
        r = lax.rsqrt(jnp.mean(x * x, axis=-1, keepdims=True) + EPS)
        h = (x * r * g_ref[...]).astype(BF16)
        h_scr[...] = h
        tail = _dot_nt(h, wtail_ref[...])
        col = lax.broadcasted_iota(jnp.int32, tail.shape, 1)
        otail_ref[...] = jnp.where(col < n_tail, tail, 0.0)

    o_ref[...] = _dot_nt(h_scr[...], wt_ref[...])


def _in_proj(x, norm_g, w_t, *, tm=1024, tn=2048):
    L, D = x.shape
    n_cols = w_t.shape[0]
    nj = n_cols // tn
    n_tail = n_cols - nj * tn
    assert 0 < n_tail <= LANES and (nj * tn) % LANES == 0
    tm = min(tm, L)
    return pl.pallas_call(
        functools.partial(_in_proj_kernel, n_tail=n_tail),
        grid=(L // tm, nj),
        in_specs=[
            pl.BlockSpec((tm, D), lambda i, j: (i, 0)),
            pl.BlockSpec((1, D), lambda i, j: (0, 0)),
            pl.BlockSpec((tn, D), lambda i, j: (j, 0)),
            pl.BlockSpec((LANES, D), lambda i, j: (nj * tn // LANES, 0)),
        ],
        out_specs=[pl.BlockSpec((tm, tn), lambda i, j: (i, j)),
                   pl.BlockSpec((tm, LANES), lambda i, j: (i, 0))],
        out_shape=[jax.ShapeDtypeStruct((L, nj * tn), F32),
                   jax.ShapeDtypeStruct((L, LANES), F32)],
        scratch_shapes=[pltpu.VMEM((tm, D), BF16)],
        compiler_params=_cparams(("parallel", "arbitrary")),
        name="in_proj",
    )(x, norm_g.reshape(1, D), w_t, w_t)


def _out_proj_kernel(x_ref, yh_ref, yg_ref, wh_ref, wg_ref, o_ref):
    half = o_ref.shape[1] // 2
    for s in range(2):
        cols = pl.ds(s * half, half)
        o_ref[:, cols] = (x_ref[:, cols]
                          + jnp.dot(yh_ref[...], wh_ref[:, cols], preferred_element_type=F32)
                          + jnp.dot(yg_ref[...], wg_ref[:, cols], preferred_element_type=F32))


def _out_proj(x, y_hy, y_gla, w_out, *, tm=512):
    L, D = x.shape
    WH = y_hy.shape[1]
    WG = y_gla.shape[1]
    tm = min(tm, L)
    assert WH == WG and w_out.shape[0] == WH + WG
    return pl.pallas_call(
        _out_proj_kernel,
        grid=(L // tm,),
        in_specs=[
            pl.BlockSpec((tm, D), lambda i: (i, 0)),
            pl.BlockSpec((tm, WH), lambda i: (i, 0)),
            pl.BlockSpec((tm, WG), lambda i: (i, 0)),
            _single((WH, D), lambda i: (0, 0)),
            _single((WG, D), lambda i: (1, 0)),
        ],
        out_specs=pl.BlockSpec((tm, D), lambda i: (i, 0)),
        out_shape=jax.ShapeDtypeStruct((L, D), F32),
        compiler_params=_cparams(("parallel",)),
        name="out_proj",
    )(x, y_hy, y_gla, w_out, w_out)


def _filt_mlp_kernel(z_ref, w1_ref, b1_ref, f1_ref, w2_ref, b2_ref, f2_ref, w3_ref, b3_ref, f3_ref, *rest, n_cast):
    cast_src, o_ref, cast_dst = rest[:n_cast], rest[n_cast], rest[n_cast + 1:]
    dot = functools.partial(jnp.dot, precision=HIGHEST, preferred_element_type=F32)
    hid = jnp.sin(f1_ref[...] * (dot(z_ref[...], w1_ref[...]) + b1_ref[...]))
    hid = jnp.sin(f2_ref[...] * (dot(hid, w2_ref[...]) + b2_ref[...]))
    hid = jnp.sin(f3_ref[...] * (dot(hid, w3_ref[...]) + b3_ref[...]))
    hi, lo = _split_hi_lo(hid)
    o_ref[...] = jnp.concatenate([hi, hi, lo], axis=1)
    for src, dst in zip(cast_src, cast_dst):
        dst[...] = src[...].astype(BF16)


def _cast_row_blocks(shape, n):
    rows, cols = shape
    tile = 2 * SUBLANES
    block_rows = pl.cdiv(pl.cdiv(rows, n), tile) * tile
    nblocks = pl.cdiv(rows, block_rows)
    return (block_rows, cols), (lambda i: (jnp.minimum(i, nblocks - 1), 0))


def _filt_mlp(feat2, w1, b1, f1, w2, b2, f2, w3, b3, f3, *, to_cast=(), tm=512):
    L, FP = feat2.shape
    H2 = w2.shape[0]
    tm = min(tm, L)
    steps = L // tm
    full = lambda shp: pl.BlockSpec(shp, lambda i: (0, 0))
    cast_specs = [pl.BlockSpec(*_cast_row_blocks(w.shape, steps)) for w in to_cast]
    outs = pl.pallas_call(
        functools.partial(_filt_mlp_kernel, n_cast=len(to_cast)),
        grid=(steps,),
        in_specs=[pl.BlockSpec((tm, FP), lambda i: (i, 0)),
                  full((FP, H2)), full((1, H2)), full((1, H2)),
                  full((H2, H2)), full((1, H2)), full((1, H2)),
                  full((H2, H2)), full((1, H2)), full((1, H2))] + cast_specs,
        out_specs=[pl.BlockSpec((tm, 3 * H2), lambda i: (i, 0))] + cast_specs,
        out_shape=[jax.ShapeDtypeStruct((L, 3 * H2), BF16)] + [jax.ShapeDtypeStruct(w.shape, BF16) for w in to_cast],
        compiler_params=_cparams(("arbitrary",)),
        name="filt_mlp",
    )(feat2, w1, b1, f1, w2, b2, f2, w3, b3, f3, *to_cast)
    return outs[0], outs[1:]


def _dft_tables(L):
    N = 2 * L
    RA = N // RB
    NH = RA // 2
    two_pi = 2.0 * np.pi
    k1 = np.arange(RA, dtype=np.int64)

    def step1_table(NR):
        g = np.zeros((RB // 2, 2 * RA, 2 * NR), np.float64)
        for s in range(2):
            n = RB * np.arange(NR, dtype=np.int64)[None, :] + (2 * np.arange(RB // 2, dtype=np.int64) + s)[:, None]
            ang = two_pi * ((k1[None, :, None] * n[:, None, :]) % N) / N
            g[:, :RA, s * NR:(s + 1) * NR] = np.cos(ang)
            g[:, RA:, s * NR:(s + 1) * NR] = -np.sin(ang)
        return g

    phi = two_pi * ((np.arange(KB)[:, None] * np.arange(RB)[None, :]) % RB) / RB
    f3 = np.block([[np.cos(phi), np.sin(phi)], [-np.sin(phi), np.cos(phi)]])
    i1 = np.block([[np.cos(phi.T), -np.sin(phi.T)], [np.sin(phi.T), np.cos(phi.T)]])
    cast = lambda a: jnp.asarray(a.astype(np.float32)).astype(BF16)
    return dict(g_half=cast(step1_table(NH)), g_full=cast(step1_table(RA)), f3=cast(f3), i1=cast(i1))


def _fwd_step1(u_ref, g_ref, ab_ref, *, RA, NR, CB, G=2):
    def body(t, carry):
        ms = [t * G + j for j in range(G)]
        rhs = []
        for m in ms:
            ua = u_ref[pl.ds(2 * m, NR, stride=PITCH), :]
            ub = u_ref[pl.ds(2 * m + 1, NR, stride=PITCH), :]
            zz = jnp.zeros_like(ua)
            rhs.append(jnp.concatenate([jnp.concatenate([ua, zz], axis=1),
                                        jnp.concatenate([zz, ub], axis=1)], axis=0).astype(BF16))
        res = [jnp.dot(g_ref[m], r, preferred_element_type=F32) for m, r in zip(ms, rhs)]
        for m, r in zip(ms, res):
            ab_ref[pl.ds(2 * m, RA, stride=PITCH), :] = r[:RA, :CB]
            ab_ref[pl.ds(RA * PITCH + 2 * m, RA, stride=PITCH), :] = r[RA:, :CB]
            ab_ref[pl.ds(2 * m + 1, RA, stride=PITCH), :] = r[:RA, CB:]
            ab_ref[pl.ds(RA * PITCH + 2 * m + 1, RA, stride=PITCH), :] = r[RA:, CB:]
        return carry
    lax.fori_loop(0, RB // 2 // G, body, 0)


def _fwd_step3_pair(ab_ref, f3, kp, *, RA):
    cols = []
    for s in range(2):
        k1 = 2 * kp + s
        ar = ab_ref[pl.ds(pl.multiple_of(k1 * PITCH, SUBLANES), RB), :]
        ai = ab_ref[pl.ds(pl.multiple_of((RA + k1) * PITCH, SUBLANES), RB), :]
        cols.append(jnp.concatenate([ar, ai], axis=0))
    rhs = jnp.concatenate(cols, axis=1).astype(BF16)
    return jnp.dot(f3, rhs, preferred_element_type=F32)


def _for_row_blocks(nblocks, body, carry, unroll=2):
    assert nblocks >= 2
    carry = body(0, carry, True, False)
    carry = lax.fori_loop(1, nblocks - 1, lambda n1, c: body(n1, c, False, False), carry,
                          unroll=max(1, min(unroll, nblocks - 2)))
    return body(nblocks - 1, carry, False, True)


def _alt_sign(shape):
    rows = lax.broadcasted_iota(jnp.int32, shape, 0)
    return jnp.where(rows % 2 == 0, 1.0, -1.0).astype(F32)


def _filt_fft_kernel(hid_ref, w4_ref, dl_ref, g_ref, f3_ref, kh_ref, kn_ref, u_scr, ab_scr, *, L, CB, G):
    N = 2 * L
    RA = N // RB
    NH = RA // 2
    f3 = f3_ref[...]
    inv_lm1 = 1.0 / (L - 1)

    row = lax.broadcasted_iota(jnp.int32, (RB, CB), 0)
    arg_base = -(row.astype(F32) * inv_lm1) * dl_ref[...]
    arg_step = -(RB * inv_lm1) * dl_ref[...]
    arg_end = -(L * inv_lm1) * dl_ref[...]

    def fill(n1, carry, first, last):
        s_abs, s_alt = carry
        r0 = pl.multiple_of(n1 * RB, RB)
        h2 = jnp.dot(hid_ref[pl.ds(r0, RB), :], w4_ref[...], preferred_element_type=F32)
        arg = arg_base + jnp.asarray(n1, F32) * arg_step
        hf = h2[:, :CB] * jnp.exp(arg)
        hb = h2[:, CB:] * jnp.exp(arg_end - arg)
        if first:
            hb = jnp.where(row == 0, 0.0, hb)
        u_scr[pl.ds(pl.multiple_of(n1 * PITCH, SUBLANES), RB), :] = hf
        u_scr[pl.ds(pl.multiple_of((NH + n1) * PITCH, SUBLANES), RB), :] = hb
        return s_abs + (jnp.abs(hf) + jnp.abs(hb)), s_alt + (hf + hb)
    zero = jnp.zeros((RB, CB), F32)
    s_abs, s_alt = _for_row_blocks(NH, fill, (zero, zero), unroll=8)
    inv_l1 = 1.0 / jnp.sum(s_abs, axis=0, keepdims=True)
    kn = jnp.sum(s_alt * _alt_sign((RB, CB)), axis=0, keepdims=True) * inv_l1 * (1.0 / N)
    kn_ref[...] = jnp.broadcast_to(kn, kn_ref.shape)

    _fwd_step1(u_scr, g_ref, ab_scr, RA=RA, NR=RA, CB=CB, G=G)

    scale = inv_l1 * (2.0 / N)
    scale2 = jnp.concatenate([scale, scale], axis=1)
    row2 = lax.broadcasted_iota(jnp.int32, (KB, 2 * CB), 0)
    lane2 = lax.broadcasted_iota(jnp.int32, (KB, 2 * CB), 1)
    dc_pos = jnp.logical_and(row2 == 0, lane2 < CB)

    def emit(t, carry):
        kps = [t * G + j for j in range(G)]
        xs = [_fwd_step3_pair(ab_scr, f3, kp, RA=RA) for kp in kps]
        for kp, x in zip(kps, xs):
            kr = x[:KB] * scale2
            ki = x[KB:] * scale2
            dc = jnp.logical_and(dc_pos, kp == 0)
            kr = jnp.where(dc, 0.5 * kr, kr)
            ki = jnp.where(dc, 0.5 * ki, ki)
            kh_ref[pl.ds(pl.multiple_of(kp * RB, RB), RB), :] = jnp.concatenate([kr, ki], axis=0).astype(kh_ref.dtype)
        return carry
    lax.fori_loop(0, RA // 2 // G, emit, 0)


def _filt_fft(hid2, w4bd, deltas, g_tab, f3_tab, *, L, CB=LANES, G=32):
    C = deltas.shape[1]
    H2 = hid2.shape[1]
    N = 2 * L
    RA = N // RB
    nblk = C // CB
    G = min(G, RA // 2)
    return pl.pallas_call(
        functools.partial(_filt_fft_kernel, L=L, CB=CB, G=G),
        grid=(nblk,),
        in_specs=[
            _single((L, H2), lambda c: (0, 0)),
            pl.BlockSpec((H2, 2 * CB), lambda c: (0, c)),
            pl.BlockSpec((1, CB), lambda c: (0, c)),
            _single(g_tab.shape, lambda c: (0, 0, 0)),
            _single(f3_tab.shape, lambda c: (0, 0)),
        ],
        out_specs=[pl.BlockSpec((RA // 2 * RB, 2 * CB), lambda c: (0, c)),
                   pl.BlockSpec((SUBLANES, CB), lambda c: (0, c))],
        out_shape=[jax.ShapeDtypeStruct((RA // 2 * RB, 2 * C), BF16),
                   jax.ShapeDtypeStruct((SUBLANES, C), F32)],
        scratch_shapes=[pltpu.VMEM((RA * PITCH, CB), F32),
                        pltpu.VMEM((2 * RA * PITCH, CB), F32)],
        compiler_params=_cparams(("arbitrary",)),
        name="filt_fft",
    )(hid2, w4bd, deltas, g_tab, f3_tab)


def _short_conv_rows(p_ref, w_ref, b_ref, n1, *, first, last):
    r0 = pl.multiple_of(n1 * RB, RB)
    cur = p_ref[pl.ds(r0, RB), :]
    rows = lax.broadcasted_iota(jnp.int32, cur.shape, 0)
    if first:
        up = jnp.where(rows == 0, 0.0, pltpu.roll(cur, 1, axis=0))
    else:
        up = p_ref[pl.ds(r0 - 1, RB), :]
    if last:
        dn = jnp.where(rows == RB - 1, 0.0, pltpu.roll(cur, RB - 1, axis=0))
    else:
        dn = p_ref[pl.ds(r0 + 1, RB), :]
    w = w_ref[...]
    return b_ref[...] + up * w[0:1] + cur * w[1:2] + dn * w[2:3]


def _hy_conv_kernel(p_hbm, w0_ref, w1_ref, wv_ref, b0_ref, b1_ref, bv_ref,
                    kh_ref, kn_ref, skip_ref, og_ref, g_ref, f3_ref, i1_ref,
                    o_ref, u_scr, ab_scr, z_scr, p_scr, p_sem, *, L, CB, G):
    N = 2 * L
    RA = N // RB
    NH = RA // 2
    f3 = f3_ref[...]
    i1 = i1_ref[...]
    sign = _alt_sign((RB, CB))
    c = pl.program_id(0)
    nblk = pl.num_programs(0)
    p0_ref, p1_ref, pv_ref = p_scr.at[0], p_scr.at[1], p_scr.at[2]

    def fetch(group, blk):
        col = pl.multiple_of((group * nblk + blk) * CB, CB)
        return pltpu.make_async_copy(p_hbm.at[:, pl.ds(col, CB)], p_scr.at[group], p_sem.at[group])

    @pl.when(c == 0)
    def _():
        fetch(1, c).start()
        fetch(2, c).start()

    fetch(0, c).start()
    fetch(1, c).wait()
    fetch(2, c).wait()

    def fill(n1, s_alt, first, last):
        z = (_short_conv_rows(pv_ref, wv_ref, bv_ref, n1, first=first, last=last)
             * _short_conv_rows(p1_ref, w1_ref, b1_ref, n1, first=first, last=last))
        u_scr[pl.ds(pl.multiple_of(n1 * PITCH, SUBLANES), RB), :] = z
        z_scr[pl.ds(pl.multiple_of(n1 * RB, RB), RB), :] = z
        return s_alt + z
    s_alt = _for_row_blocks(NH, fill, jnp.zeros((RB, CB), F32))

    @pl.when(c + 1 < nblk)
    def _():
        fetch(1, c + 1).start()
        fetch(2, c + 1).start()

    z_nyq = jnp.sum(s_alt * sign, axis=0, keepdims=True)
    y_nyq = z_nyq * kn_ref[0:1, :]

    _fwd_step1(u_scr, g_ref, ab_scr, RA=RA, NR=NH, CB=CB, G=G)

    GM = max(min(G // 2, 8), 1)

    def mid(t, carry):
        kps = [t * GM + j for j in range(GM)]
        xs = [_fwd_step3_pair(ab_scr, f3, kp, RA=RA) for kp in kps]
        khs = [kh_ref[pl.ds(pl.multiple_of(kp * RB, RB), RB), :].astype(F32) for kp in kps]
        bs = []
        for x, kh in zip(xs, khs):
            xr, xi, kr, ki = x[:KB], x[KB:], kh[:KB], kh[KB:]
            y = jnp.concatenate([xr * kr - xi * ki, xr * ki + xi * kr], axis=0).astype(BF16)
            bs.append(jnp.dot(i1, y, preferred_element_type=F32))
        for kp, b in zip(kps, bs):
            for s in range(2):
                k1 = 2 * kp + s
                ab_scr[pl.ds(pl.multiple_of(k1 * PITCH, SUBLANES), RB), :] = b[:RB, s * CB:(s + 1) * CB]
                ab_scr[pl.ds(pl.multiple_of((RA + k1) * PITCH, SUBLANES), RB), :] = b[RB:, s * CB:(s + 1) * CB]
        return carry
    lax.fori_loop(0, RA // 2 // GM, mid, 0)

    def last(t, carry):
        ms = [t * G + j for j in range(G)]
        rhs = [jnp.concatenate(
            [jnp.concatenate([ab_scr[pl.ds(2 * m + s, RA, stride=PITCH), :],
                              ab_scr[pl.ds(RA * PITCH + 2 * m + s, RA, stride=PITCH), :]], axis=0)
             for s in range(2)], axis=1).astype(BF16) for m in ms]
        ys = [_dot_tn(g_ref[m], r) for m, r in zip(ms, rhs)]
        for m, y in zip(ms, ys):
            u_scr[pl.ds(2 * m, NH, stride=PITCH), :] = y[:NH, :CB]
            u_scr[pl.ds(2 * m + 1, NH, stride=PITCH), :] = y[NH:, CB:]
        return carry
    lax.fori_loop(0, RB // 2 // G, last, 0)

    nyq_rows = sign * y_nyq
    fetch(0, c).wait()

    def finish(n1, carry, first, last):
        z = z_scr[pl.ds(pl.multiple_of(n1 * RB, RB), RB), :]
        x0 = _short_conv_rows(p0_ref, w0_ref, b0_ref, n1, first=first, last=last)
        conv = u_scr[pl.ds(pl.multiple_of(n1 * PITCH, SUBLANES), RB), :] + nyq_rows
        y = (conv + z * skip_ref[...]) * x0
        y = y * lax.rsqrt(jnp.mean(y * y, axis=-1, keepdims=True) + EPS) * og_ref[...]
        o_ref[pl.ds(pl.multiple_of(n1 * RB, RB), RB), :] = y.astype(o_ref.dtype)
        return carry
    _for_row_blocks(NH, finish, 0, unroll=4)


def _hy_conv(p, conv_w, conv_b, khat, knyq, skip, out_g, tabs, *, L, CB=LANES, G=32):
    C = skip.shape[1]
    assert CB == C // HY_GROUPS, "one channel block must be exactly one norm group"
    N = 2 * L
    RA = N // RB
    NH = RA // 2
    nblk = C // CB
    G = min(G, RA // 2)
    g_tab, f3_tab, i1_tab = tabs["g_half"], tabs["f3"], tabs["i1"]
    col = lambda off: (lambda c: (0, off * nblk + c))
    wspec = lambda off: pl.BlockSpec((3, CB), col(off))
    bspec = lambda off: pl.BlockSpec((1, CB), col(off))
    return pl.pallas_call(
        functools.partial(_hy_conv_kernel, L=L, CB=CB, G=G),
        grid=(nblk,),
        in_specs=[pl.BlockSpec(memory_space=pl.ANY),
                  wspec(0), wspec(1), wspec(2), bspec(0), bspec(1), bspec(2),
                  pl.BlockSpec((RA // 2 * RB, 2 * CB), col(0)),
                  pl.BlockSpec((SUBLANES, CB), col(0)),
                  bspec(0), bspec(0),
                  _single(g_tab.shape, lambda c: (0, 0, 0)),
                  _single(f3_tab.shape, lambda c: (0, 0)),
                  _single(i1_tab.shape, lambda c: (0, 0))],
        out_specs=pl.BlockSpec((L, CB), col(0)),
        out_shape=jax.ShapeDtypeStruct((L, C), BF16),
        scratch_shapes=[pltpu.VMEM((NH * PITCH, CB), F32),
                        pltpu.VMEM((2 * RA * PITCH, CB), F32),
                        pltpu.VMEM((L, CB), F32),
                        pltpu.VMEM((3, L, CB), F32),
                        pltpu.SemaphoreType.DMA((3,))],
        compiler_params=_cparams(("arbitrary",)),
        name="hy_conv",
    )(p, conv_w, conv_w, conv_w, conv_b, conv_b, conv_b, khat, knyq, skip, out_g,
      g_tab, f3_tab, i1_tab)


def _log_sigmoid(x):
    return jnp.minimum(x, 0.0) - jnp.log(1.0 + jnp.exp(-jnp.abs(x)))


def _gla_scan_blocks(dirs, *, TB):
    nch = TB // CHUNK
    rows = lambda c: slice(c * CHUNK, (c + 1) * CHUNK)
    units = [(d, c) for c in range(nch) for d in dirs]
    rr = lax.broadcasted_iota(jnp.int32, (CHUNK, CHUNK), 0)
    cc = lax.broadcasted_iota(jnp.int32, (CHUNK, CHUNK), 1)
    scale = GLA_DK ** -0.5

    for d in dirs:
        lr_hi, lr_lo = _split_hi_lo(d["lr"][...])
        gate_in = jnp.dot(jnp.concatenate([lr_hi, lr_hi, lr_lo], axis=1), d["wg"],
                          preferred_element_type=F32) + d["bg"]
        g = _log_sigmoid(gate_in) * (1.0 / GATE_TEMP)
        g_hi = g.astype(BF16)
        g_mid, g_lo = _split_hi_lo(g - g_hi.astype(F32))
        d["parts"][...] = jnp.concatenate([g_hi, g_mid, g_lo], axis=1)
        d["csum"] = jnp.where((cc >= rr) if d["reverse"] else (cc <= rr), 1.0, 0.0).astype(BF16)
        d["mask"] = (cc > rr) if d["reverse"] else (cc <= rr)
        d["edge"] = 0 if d["reverse"] else CHUNK - 1

    for d, c in units:
        b3 = jnp.dot(d["csum"], d["parts"][rows(c), :], preferred_element_type=F32)
        d["b"][rows(c), :] = b3[:, :GLA_DK] + b3[:, GLA_DK:2 * GLA_DK] + b3[:, 2 * GLA_DK:]

    decay = {}
    for d, c in units:
        b = d["b"][rows(c), :]
        b_edge = b[d["edge"]:d["edge"] + 1]
        k = d["k"][rows(c), :]
        d["qks"][0, rows(c), :] = (d["q"][rows(c), :] * scale * jnp.exp(b)).astype(BF16)
        d["qks"][1, rows(c), :] = (k * jnp.exp(-b)).astype(BF16)
        d["qks"][2, rows(c), :] = (k * jnp.exp(b_edge - b)).astype(BF16)
        d["vb"][rows(c), :] = d["v"][rows(c), :].astype(BF16)
        decay[(id(d), c)] = jnp.exp(b_edge)

    for d, c in units:
        att = _dot_nt(d["qks"][0, rows(c), :], d["qks"][1, rows(c), :])
        d["att"][rows(c), :] = jnp.where(d["mask"], att, 0.0).astype(BF16)

    for d, c in units:
        d["o"][rows(c), :] = jnp.dot(d["att"][rows(c), :], d["vb"][rows(c), :], preferred_element_type=F32)

    for d, c in units:
        d["kv"][c * GLA_DV:(c + 1) * GLA_DV, :] = _dot_tn(d["vb"][rows(c), :], d["qks"][2, rows(c), :])

    state = {id(d): d["s"][...] for d in dirs}
    for step in range(nch):
        for d in dirs:
            c = nch - 1 - step if d["reverse"] else step
            s_t = state[id(d)]
            d["o"][rows(c), :] += _dot_nt(d["qks"][0, rows(c), :], s_t.astype(BF16))
            state[id(d)] = s_t * decay[(id(d), c)] + d["kv"][c * GLA_DV:(c + 1) * GLA_DV, :]
    for d in dirs:
        d["s"][...] = state[id(d)]


def _gla_kernel(qf_ref, kf_ref, vf_ref, lf_ref, rf_ref, qb_ref, kb_ref, vb_ref, lb_ref, rb_ref,
                wg_ref, bg_ref, og_ref, o_ref,
                s_scr, parts_scr, b_scr, qks_scr, vb_scr, att_scr, kv_scr, ob_scr, half_scr, *, TB):
    s = pl.program_id(1)
    nb = pl.num_programs(1)

    @pl.when(s == 0)
    def _():
        s_scr[...] = jnp.zeros_like(s_scr)

    def direction(i, q, k, v, lr):
        return dict(q=q, k=k, v=v, lr=lr, wg=wg_ref[i], bg=bg_ref[i], reverse=bool(i), s=s_scr.at[i],
                    parts=parts_scr.at[i], b=b_scr.at[i], qks=qks_scr.at[i], vb=vb_scr.at[i],
                    att=att_scr.at[i], kv=kv_scr.at[i], o=ob_scr.at[i])
    _gla_scan_blocks([direction(0, qf_ref, kf_ref, vf_ref, lf_ref),
                      direction(1, qb_ref, kb_ref, vb_ref, lb_ref)], TB=TB)

    first_half = s < nb // 2
    for d, blk, r_ref in ((0, s, rf_ref), (1, nb - 1 - s, rb_ref)):
        rows = pl.ds(pl.multiple_of(blk * TB, TB), TB)

        @pl.when(first_half)
        def _():
            half_scr[rows, :] = ob_scr[d]

        @pl.when(jnp.logical_not(first_half))
        def _():
            tot = half_scr[rows, :] + ob_scr[d]
            tot = tot * lax.rsqrt(jnp.mean(tot * tot, axis=-1, keepdims=True) + EPS) * og_ref[...]
            o_ref[rows, :] = (tot * jax.nn.silu(r_ref[...])).astype(o_ref.dtype)


def _gla(p, p_lr, w_gate, b_gate, out_g, *, L, col0, TB=1024):
    TB = min(TB, L)
    nb = L // TB
    assert nb % 2 == 0, "both scan directions must meet between two blocks"
    nch = TB // CHUNK
    qb = col0 // GLA_DK
    kb = qb + GLA_HEADS
    vb = (col0 + 2 * GLA_KW) // GLA_DV
    rb = vb + GLA_HEADS
    fwd = lambda s: s
    bwd = lambda s: nb - 1 - s

    def operands(blk):
        return [pl.BlockSpec((TB, GLA_DK), lambda h, s: (blk(s), qb + h)),
                pl.BlockSpec((TB, GLA_DK), lambda h, s: (blk(s), kb + h)),
                pl.BlockSpec((TB, GLA_DV), lambda h, s: (blk(s), vb + h)),
                pl.BlockSpec((TB, LANES), lambda h, s: (blk(s), 0)),
                pl.BlockSpec((TB, GLA_DV), lambda h, s: (blk(s), rb + h))]

    return pl.pallas_call(
        functools.partial(_gla_kernel, TB=TB),
        grid=(GLA_HEADS, nb),
        in_specs=operands(fwd) + operands(bwd) + [
            pl.BlockSpec((2, 3 * LANES, GLA_DK), lambda h, s: (0, 0, h)),
            pl.BlockSpec((2, 1, GLA_DK), lambda h, s: (0, 0, h)),
            pl.BlockSpec((1, GLA_DV), lambda h, s: (0, h)),
        ],
        out_specs=pl.BlockSpec((L, GLA_DV), lambda h, s: (0, h)),
        out_shape=jax.ShapeDtypeStruct((L, GLA_VW), BF16),
        scratch_shapes=[pltpu.VMEM((2, GLA_DV, GLA_DK), F32),
                        pltpu.VMEM((2, TB, 3 * GLA_DK), BF16),
                        pltpu.VMEM((2, TB, GLA_DK), F32),
                        pltpu.VMEM((2, 3, TB, GLA_DK), BF16),
                        pltpu.VMEM((2, TB, GLA_DV), BF16),
                        pltpu.VMEM((2, TB, CHUNK), BF16),
                        pltpu.VMEM((2, nch * GLA_DV, GLA_DK), F32),
                        pltpu.VMEM((2, TB, GLA_DV), F32),
                        pltpu.VMEM((L, GLA_DV), F32)],
        compiler_params=_cparams(("arbitrary", "arbitrary")),
        name="gla",
    )(*([p, p, p, p_lr, p] * 2), w_gate, b_gate, out_g)


def _filter_features(L):
    t = np.linspace(0.0, 1.0, L)[:, None]
    bands = (FILTER_EMB - 1) // 2
    freqs = np.linspace(1e-4, bands - 1, bands)[None, :]
    ang = (2.0 * np.pi / L) * np.arange(L)[:, None] * freqs
    feat = np.concatenate([t, np.cos(ang), -np.sin(ang)], axis=-1)
    feat_rev = np.roll(feat[::-1], 1, axis=0)
    both = np.zeros((L, LANES), np.float32)
    both[:, :FILTER_EMB] = feat
    both[:, FILTER_EMB:2 * FILTER_EMB] = feat_rev
    return jnp.asarray(both)


def _twice(w, rows_out):
    r, c = w.shape
    zz = jnp.zeros_like(w)
    both = jnp.concatenate([jnp.concatenate([w, zz], axis=1), jnp.concatenate([zz, w], axis=1)], axis=0)
    return jnp.pad(both, ((0, rows_out - 2 * r), (0, 0)))


def _filter_deltas():
    min_decay = math.log(DECAY_TARGET) / LONG_DECAY_PCT
    max_decay = math.log(DECAY_TARGET) / SHORT_DECAY_PCT
    return jnp.abs(jnp.linspace(min_decay, max_decay, HY_WIDTH, dtype=F32)).reshape(1, HY_WIDTH)


def _block_diag_w4(w4):
    H = w4.shape[0]
    nblk = HY_WIDTH // LANES
    wf = w4[:, :HY_WIDTH].reshape(H, nblk, LANES)
    wb = w4[:, HY_WIDTH:].reshape(H, nblk, LANES)
    zz = jnp.zeros_like(wf)
    top = jnp.concatenate([wf, zz], axis=2)
    bot = jnp.concatenate([zz, wb], axis=2)
    return jnp.concatenate([top, bot], axis=0).reshape(2 * H, nblk * 2 * LANES)


def kernel(x, ffn1_norm, ffn1_w_gate, ffn1_w_up, ffn1_w_down, mix_norm, w_in, hy_conv_w, hy_conv_b, flt_w1, flt_b1, flt_f1, flt_w2, flt_b2, flt_f2, flt_w3, flt_b3, flt_f3, flt_w4, hy_skip, hy_out_norm, gla_w_a2_f, gla_b_a_f, gla_w_a2_b, gla_b_a_b, gla_out_norm, w_out, ffn2_norm, ffn2_w_gate, ffn2_w_up, ffn2_w_down, final_norm):
    B, L, D = x.shape
    depth = ffn1_norm.shape[0]
    tabs = _dft_tables(L)
    feat2 = _filter_features(L)
    deltas = _filter_deltas()
    gla_col0 = 3 * HY_WIDTH
    H2 = 2 * FILTER_HIDDEN
    both = lambda a: jnp.concatenate([a, a]).reshape(1, H2)
    filt = []
    for l in range(depth):
        early = [jnp.swapaxes(w_in[l], 0, 1)]
        if l == 0:
            early += [ffn1_w_gate[0], ffn1_w_up[0], ffn1_w_down[0]]
        hid3, cast = _filt_mlp(feat2, _twice(flt_w1[l], LANES), both(flt_b1[l]), both(flt_f1[l]),
                               _twice(flt_w2[l], H2), both(flt_b2[l]), both(flt_f2[l]),
                               _twice(flt_w3[l], H2), both(flt_b3[l]), both(flt_f3[l]), to_cast=early)
        w4cat = _three_pass_rows(_block_diag_w4(flt_w4[l]))
        khat, knyq = _filt_fft(hid3, w4cat, deltas, tabs["g_full"], tabs["f3"], L=L)
        filt.append((khat, knyq, cast))

    outs = []
    for bi in range(B):
        xb = x[bi]
        w1 = filt[0][2][1:]
        for l in range(depth):
            last = l == depth - 1
            khat, knyq, (w_in_t, *_) = filt[l]
            later = [ffn2_w_gate[l], ffn2_w_up[l], ffn2_w_down[l], w_out[l]]
            xb, (w2_gate, w2_up, w2_down, w_out_bf) = _ffn(xb, ffn1_norm[l], *w1, final_norm, final_norm=False,
                                                           to_cast=later)
            p, p_lr = _in_proj(xb, mix_norm[l], w_in_t)
            y_hy = _hy_conv(p, hy_conv_w[l], hy_conv_b[l].reshape(1, -1), khat, knyq,
                            hy_skip[l].reshape(1, -1), hy_out_norm[l].reshape(1, -1), tabs, L=L)

            w_gate = jnp.stack([jnp.pad(gla_w_a2_f[l], ((0, LANES - GATE_RANK), (0, 0))),
                                jnp.pad(gla_w_a2_b[l], ((GATE_RANK, LANES - 2 * GATE_RANK), (0, 0)))])
            w_gate = _three_pass_rows(w_gate)
            b_gate = jnp.stack([gla_b_a_f[l], gla_b_a_b[l]]).reshape(2, 1, GLA_KW)
            y_gla = _gla(p, p_lr, w_gate, b_gate, gla_out_norm[l].reshape(1, -1), L=L, col0=gla_col0)

            xb = _out_proj(xb, y_hy, y_gla, w_out_bf)
            nxt = [] if last else [ffn1_w_gate[l + 1], ffn1_w_up[l + 1], ffn1_w_down[l + 1]]
            if nxt:
                xb, w1 = _ffn(xb, ffn2_norm[l], w2_gate, w2_up, w2_down, final_norm, final_norm=last, to_cast=nxt)
            else:
                xb = _ffn_stream(xb, ffn2_norm[l], w2_gate, w2_up, w2_down, final_norm, final_norm=last)
        outs.append(xb)
    return jnp.stack(outs)
```
